```python
import math
import jax, jax.numpy as jnp
from jax import lax
import numpy as np

D_MODEL = 1024
BATCH = 2
SEQ = 8192
DEPTH = 1

GRID_W = 64
CTX_LEN = 256
HEAD_DIM = 64
N_Q_HEADS = 8
N_KV_HEADS = 2
GROUP = N_Q_HEADS // N_KV_HEADS
WINDOW = 128
BLOCK = 128
ROPE_THETA = 10000.0
N_FOURIER_GROUPS = 4
FOURIER_GROUP_CH = 128
FOURIER_W = N_FOURIER_GROUPS * FOURIER_GROUP_CH
Q_W = N_Q_HEADS * HEAD_DIM
KV_W = N_KV_HEADS * HEAD_DIM
IN_W = Q_W + 2 * KV_W + FOURIER_W + 2 * D_MODEL
D_FF = 2816
CONV_WIDTH = 3
EPS = 1e-6
NEG = -1e30

kernel_name = "hybrid_swa_fnet_convffn_dit_block"


def rmsnorm(x, g):
    xf = x.astype(jnp.float32)
    xf = xf * lax.rsqrt(jnp.mean(xf * xf, axis=-1, keepdims=True) + EPS)
    return xf.astype(x.dtype) * g


def modulate(h, shift, scale):
    return h * (1 + scale) + shift


def adaln(cvec, w_mod, b_mod):
    return jnp.split(jax.nn.silu(cvec) @ w_mod + b_mod, 6, axis=-1)


def axial_rope_tables(rows, dtype):
    row = jnp.repeat(jnp.arange(rows), GRID_W).astype(jnp.float32)
    col = jnp.tile(jnp.arange(GRID_W), rows).astype(jnp.float32)
    half = HEAD_DIM // 2
    inv_freq = ROPE_THETA ** (-jnp.arange(0, half, 2, dtype=jnp.float32) / half)
    ang_r = (row[:, None] * inv_freq)[:, None, :]
    ang_c = (col[:, None] * inv_freq)[:, None, :]
    return tuple(t.astype(dtype) for t in (jnp.cos(ang_r), jnp.sin(ang_r), jnp.cos(ang_c), jnp.sin(ang_c)))


def rotate(x, cos, sin):
    x1, x2 = jnp.split(x, 2, axis=-1)
    return jnp.concatenate([x1 * cos - x2 * sin, x2 * cos + x1 * sin], axis=-1)


def apply_axial_rope(x, tabs):
    cos_r, sin_r, cos_c, sin_c = tabs
    xr, xc = jnp.split(x, 2, axis=-1)
    return jnp.concatenate([rotate(xr, cos_r, sin_r), rotate(xc, cos_c, sin_c)], axis=-1)


def split_proj(p):
    o1 = Q_W
    o2 = o1 + KV_W
    o3 = o2 + KV_W
    o4 = o3 + FOURIER_W
    o5 = o4 + D_MODEL
    return p[..., :o1], p[..., o1:o2], p[..., o2:o3], p[..., o3:o4], p[..., o4:o5], p[..., o5:]


def banded_window_attention(q, k, v, k_ctx, v_ctx, sink):
    B, S = q.shape[:2]
    L = k_ctx.shape[1]
    nb = S // BLOCK
    scale = HEAD_DIM ** -0.5
    qb = q.reshape(B, nb, BLOCK, N_KV_HEADS, GROUP, HEAD_DIM)

    def band(t):
        tp = jnp.pad(t, ((0, 0), (BLOCK, BLOCK), (0, 0), (0, 0)))
        tp = tp.reshape(B, nb + 2, BLOCK, N_KV_HEADS, HEAD_DIM)
        return jnp.concatenate([tp[:, :-2], tp[:, 1:-1], tp[:, 2:]], axis=2)

    kw, vw = band(k), band(v)
    s_loc = jnp.einsum('bnqhgd,bnkhd->bnhgqk', qb, kw).astype(jnp.float32) * scale
    s_ctx = jnp.einsum('bnqhgd,blhd->bnhgql', qb, k_ctx).astype(jnp.float32) * scale
    qi = jnp.arange(BLOCK)[:, None]
    kj = jnp.arange(3 * BLOCK)[None, :]
    rel = kj - BLOCK - qi
    kpos = jnp.arange(nb)[:, None, None] * BLOCK + kj[None] - BLOCK
    valid = (jnp.abs(rel) <= WINDOW)[None] & (kpos >= 0) & (kpos < S)
    s_loc = jnp.where(valid[None, :, None, None], s_loc, NEG)
    sink_l = jnp.broadcast_to(sink.astype(jnp.float32).reshape(1, 1, N_KV_HEADS, GROUP, 1, 1),
                              s_loc.shape[:-1] + (1,))
    p = jax.nn.softmax(jnp.concatenate([s_loc, s_ctx, sink_l], axis=-1), axis=-1)
    p_loc = p[..., :3 * BLOCK].astype(v.dtype)
    p_ctx = p[..., 3 * BLOCK:3 * BLOCK + L].astype(v.dtype)
    o = (jnp.einsum('bnhgqk,bnkhd->bnqhgd', p_loc, vw)
         + jnp.einsum('bnhgql,blhd->bnqhgd', p_ctx, v_ctx))
    return o.reshape(B, S, Q_W)


def context_attention(q, k, v, sink):
    B, L = q.shape[:2]
    scale = HEAD_DIM ** -0.5
    qg = q.reshape(B, L, N_KV_HEADS, GROUP, HEAD_DIM)
    s = jnp.einsum('blhgd,bmhd->bhglm', qg, k).astype(jnp.float32) * scale
    sink_l = jnp.broadcast_to(sink.astype(jnp.float32).reshape(1, N_KV_HEADS, GROUP, 1, 1),
                              s.shape[:-1] + (1,))
    p = jax.nn.softmax(jnp.concatenate([s, sink_l], axis=-1), axis=-1)[..., :L].astype(v.dtype)
    o = jnp.einsum('bhglm,bmhd->blhgd', p, v)
    return o.reshape(B, L, Q_W)


def fourier_mix(f):
    B, S = f.shape[:2]
    fg = f.reshape(B, S, N_FOURIER_GROUPS, FOURIER_GROUP_CH).astype(jnp.float32)
    out = jnp.fft.fft2(fg, axes=(1, 3), norm='ortho').real
    return out.reshape(B, S, FOURIER_W).astype(f.dtype)


def merge_branches(attn_o, four_o, g_a, g_f, w_pa, w_pf, w_out):
    m = jax.nn.sigmoid(g_a) * (attn_o @ w_pa) + jax.nn.sigmoid(g_f) * (four_o @ w_pf)
    return m @ w_out


def conv_ffn(h, w_up, conv_w, conv_b, w_down):
    u, gate = jnp.split(h @ w_up, 2, axis=-1)
    up = jnp.pad(u, ((0, 0), (1, 1), (0, 0)))
    u = up[:, :-2] * conv_w[0] + up[:, 1:-1] * conv_w[1] + up[:, 2:] * conv_w[2] + conv_b
    return (jax.nn.silu(u) * gate) @ w_down


def setup_inputs(seed: int = 0) -> dict:
    key = jax.random.key(seed)
    ks = jax.random.split(key, 19)
    f32 = jnp.float32
    D = D_MODEL

    def nrm(k, shape, s):
        return jax.random.normal(k, shape, f32) * s

    return {
        'x': nrm(ks[0], (BATCH, SEQ, D), 1.0),
        'c': nrm(ks[1], (BATCH, D), 1.0),
        'ctx': nrm(ks[2], (BATCH, CTX_LEN, D), 1.0),
        'c_ctx': nrm(ks[3], (D,), 1.0),
        'w_mod': nrm(ks[4], (DEPTH, D, 6 * D), 0.5 * D ** -0.5),
        'b_mod': nrm(ks[5], (DEPTH, 6 * D), 0.01),
        'g_pre1': 1.0 + nrm(ks[6], (DEPTH, D), 0.05),
        'g_post1': 1.0 + nrm(ks[7], (DEPTH, D), 0.05),
        'g_pre2': 1.0 + nrm(ks[8], (DEPTH, D), 0.05),
        'g_post2': 1.0 + nrm(ks[9], (DEPTH, D), 0.05),
        'w_in': nrm(ks[10], (DEPTH, D, IN_W), D ** -0.5),
        'sink': nrm(ks[11], (DEPTH, N_Q_HEADS), 0.5),
        'w_pa': nrm(ks[12], (DEPTH, Q_W, D), Q_W ** -0.5),
        'w_pf': nrm(ks[13], (DEPTH, FOURIER_W, D), FOURIER_W ** -0.5),
        'w_out': nrm(ks[14], (DEPTH, D, D), D ** -0.5),
        'w_up': nrm(ks[15], (DEPTH, D, 2 * D_FF), D ** -0.5),
        'conv_w': nrm(ks[16], (DEPTH, CONV_WIDTH, D_FF), CONV_WIDTH ** -0.5),
        'conv_b': nrm(ks[17], (DEPTH, D_FF), 0.01),
        'w_down': nrm(ks[18], (DEPTH, D_FF, D), D_FF ** -0.5),
    }


def reference(x, c, ctx, c_ctx, w_mod, b_mod, g_pre1, g_post1, g_pre2, g_post2,
              w_in, sink, w_pa, w_pf, w_out, w_up, conv_w, conv_b, w_down):
    B, S, _ = x.shape
    L = ctx.shape[1]
    ROWS = S // GRID_W
    tabs = axial_rope_tables(ROWS, x.dtype)
    for l in range(DEPTH):
        last = l == DEPTH - 1
        sh1, sc1, gt1, sh2, sc2, gt2 = [m[:, None] for m in adaln(c, w_mod[l], b_mod[l])]
        csh1, csc1, cgt1, csh2, csc2, cgt2 = adaln(c_ctx, w_mod[l], b_mod[l])

        h = modulate(rmsnorm(x, g_pre1[l]), sh1, sc1)
        hc = modulate(rmsnorm(ctx, g_pre1[l]), csh1, csc1)
        q, k, v, f, g_a, g_f = split_proj(h @ w_in[l])
        q = apply_axial_rope(q.reshape(B, S, N_Q_HEADS, HEAD_DIM), tabs)
        k = apply_axial_rope(k.reshape(B, S, N_KV_HEADS, HEAD_DIM), tabs)
        v = v.reshape(B, S, N_KV_HEADS, HEAD_DIM)
        if last:
            pc = hc @ w_in[l, :, :Q_W + 2 * KV_W]
        else:
            pc = hc @ w_in[l]
        kc = pc[..., Q_W:Q_W + KV_W].reshape(B, L, N_KV_HEADS, HEAD_DIM)
        vc = pc[..., Q_W + KV_W:Q_W + 2 * KV_W].reshape(B, L, N_KV_HEADS, HEAD_DIM)

        attn = banded_window_attention(q, k, v, kc, vc, sink[l])
        y = merge_branches(attn, fourier_mix(f), g_a, g_f, w_pa[l], w_pf[l], w_out[l])
        x = x + gt1 * rmsnorm(y, g_post1[l])

        if not last:
            qc, _, _, fc, g_ac, g_fc = split_proj(pc)
            attn_c = context_attention(qc.reshape(B, L, N_Q_HEADS, HEAD_DIM), kc, vc, sink[l])
            yc = merge_branches(attn_c, fourier_mix(fc), g_ac, g_fc, w_pa[l], w_pf[l], w_out[l])
            ctx = ctx + cgt1 * rmsnorm(yc, g_post1[l])

        h2 = modulate(rmsnorm(x, g_pre2[l]), sh2, sc2)
        x = x + gt2 * rmsnorm(conv_ffn(h2, w_up[l], conv_w[l], conv_b[l], w_down[l]), g_post2[l])

        if not last:
            hc2 = modulate(rmsnorm(ctx, g_pre2[l]), csh2, csc2)
            ctx = ctx + cgt2 * rmsnorm(conv_ffn(hc2, w_up[l], conv_w[l], conv_b[l], w_down[l]), g_post2[l])
    return x
```

```python
import functools
import math

import numpy as np
import jax
import jax.numpy as jnp
from jax import lax
from jax.experimental import pallas as pl
from jax.experimental.pallas import tpu as pltpu

F32 = jnp.float32
BF16 = jnp.bfloat16

D_MODEL = 1024
GRID_W = 64
HEAD_DIM = 64
N_Q_HEADS = 8
N_KV_HEADS = 2
GROUP = N_Q_HEADS // N_KV_HEADS
WINDOW = 128
BLOCK = 128
ROPE_THETA = 10000.0
N_FOURIER_GROUPS = 4
FOURIER_GROUP_CH = 128
FOURIER_W = N_FOURIER_GROUPS * FOURIER_GROUP_CH
Q_W = N_Q_HEADS * HEAD_DIM
KV_W = N_KV_HEADS * HEAD_DIM
D_FF = 2816
EPS = 1e-6
NEG = -1e30

LANES = 128
SUBLANES = 8
MXU_DIM = 256
VMEM_LIMIT_BYTES = 56 * 1024 * 1024

TOKEN_TILE = 512
FFN_CHUNK = MXU_DIM
DFT_MINOR = SUBLANES


def _const_spec(shape):
    nd = len(shape)
    return pl.BlockSpec(shape, lambda *_: (0,) * nd, pipeline_mode=pl.Buffered(1))


def _params(n_axes):
    return pltpu.CompilerParams(
        dimension_semantics=("arbitrary",) * n_axes,
        vmem_limit_bytes=VMEM_LIMIT_BYTES,
    )


def _norm_modulate(x, gain, shift, scale):
    ms = jnp.mean(x * x, axis=-1, keepdims=True)
    return (x * lax.rsqrt(ms + EPS)) * (gain * (1.0 + scale)) + shift


def _post_norm(y, gain):
    ms = jnp.mean(y * y, axis=-1, keepdims=True)
    return (y * lax.rsqrt(ms + EPS)) * gain


def _adaln_kernel(c_ref, w_ref, b_ref, o_ref):
    c = c_ref[...]
    s = c * jax.nn.sigmoid(c)
    o_ref[...] = jnp.dot(s, w_ref[...], preferred_element_type=F32) + b_ref[...]


def _adaln(cvecs, w_mod, b_mod):
    n_out = w_mod.shape[1]
    tn = 1536
    return pl.pallas_call(
        _adaln_kernel,
        grid=(n_out // tn,),
        in_specs=[
            pl.BlockSpec((SUBLANES, D_MODEL), lambda j: (0, 0)),
            pl.BlockSpec((D_MODEL, tn), lambda j: (0, j)),
            pl.BlockSpec((1, tn), lambda j: (0, j)),
        ],
        out_specs=pl.BlockSpec((SUBLANES, tn), lambda j: (0, j)),
        out_shape=jax.ShapeDtypeStruct((SUBLANES, n_out), F32),
        compiler_params=_params(1),
        name="adaln",
    )(cvecs, w_mod, b_mod.reshape(1, n_out))


def _rope(xb, cos_t, sin_t, first_half):
    sw = jnp.where(first_half, pltpu.roll(xb, LANES - 16, 1), pltpu.roll(xb, 16, 1))
    return xb * cos_t + sw * sin_t


def _inproj_kernel(x_ref, mod_ref, g_ref, w_ref, cos_ref, sin_ref, q_ref, k_ref, v_ref, f_ref):
    h = _norm_modulate(x_ref[...], g_ref[...], mod_ref[0:1, :], mod_ref[1:2, :])
    p = jnp.dot(h.astype(BF16), w_ref[...], preferred_element_type=F32)
    cos_t = cos_ref[...]
    sin_t = sin_ref[...]
    lane = lax.broadcasted_iota(jnp.int32, cos_t.shape, 1)
    first_half = (lane % 32) < 16
    scale = HEAD_DIM ** -0.5
    for b in range(Q_W // LANES):
        blk = p[:, b * LANES:(b + 1) * LANES]
        q_ref[:, b * LANES:(b + 1) * LANES] = (_rope(blk, cos_t, sin_t, first_half) * scale).astype(BF16)
    k_ref[...] = _rope(p[:, Q_W:Q_W + KV_W], cos_t, sin_t, first_half).astype(BF16)
    v_ref[...] = p[:, Q_W + KV_W:Q_W + 2 * KV_W].astype(BF16)
    f_ref[...] = p[:, Q_W + 2 * KV_W:]


def _inproj(x2, mods, g_pre, w_qkvf, cos_t, sin_t, seq):
    n = x2.shape[0]
    tm = TOKEN_TILE
    tiles_per_seq = seq // tm
    n_cols = w_qkvf.shape[1]
    return pl.pallas_call(
        _inproj_kernel,
        grid=(n // tm,),
        in_specs=[
            pl.BlockSpec((tm, D_MODEL), lambda i: (i, 0)),
            pl.BlockSpec((None, 6, D_MODEL), lambda i: (i // tiles_per_seq, 0, 0)),
            _const_spec((1, D_MODEL)),
            _const_spec((D_MODEL, n_cols)),
            pl.BlockSpec((tm, LANES), lambda i: (i % tiles_per_seq, 0)),
            pl.BlockSpec((tm, LANES), lambda i: (i % tiles_per_seq, 0)),
        ],
        out_specs=[
            pl.BlockSpec((tm, Q_W), lambda i: (i, 0)),
            pl.BlockSpec((tm, KV_W), lambda i: (i, 0)),
            pl.BlockSpec((tm, KV_W), lambda i: (i, 0)),
            pl.BlockSpec((tm, FOURIER_W), lambda i: (i, 0)),
        ],
        out_shape=[
            jax.ShapeDtypeStruct((n, Q_W), BF16),
            jax.ShapeDtypeStruct((n, KV_W), BF16),
            jax.ShapeDtypeStruct((n, KV_W), BF16),
            jax.ShapeDtypeStruct((n, FOURIER_W), F32),
        ],
        compiler_params=_params(1),
        name="inproj",
    )(x2, mods, g_pre, w_qkvf, cos_t, sin_t)


def _ctxkv_kernel(x_ref, mod_ref, g_ref, w_ref, k_ref, v_ref):
    h = _norm_modulate(x_ref[...], g_ref[...], mod_ref[0:1, :], mod_ref[1:2, :])
    p = jnp.dot(h.astype(BF16), w_ref[...], preferred_element_type=F32)
    k_ref[...] = p[:, :KV_W].astype(BF16)
    v_ref[...] = p[:, KV_W:].astype(BF16)


def _ctxkv(ctx2, mods_ctx, g_pre, w_kv, ctx_len):
    n = ctx2.shape[0]
    return pl.pallas_call(
        _ctxkv_kernel,
        grid=(n // ctx_len,),
        in_specs=[
            pl.BlockSpec((ctx_len, D_MODEL), lambda i: (i, 0)),
            _const_spec((6, D_MODEL)),
            _const_spec((1, D_MODEL)),
            _const_spec((D_MODEL, 2 * KV_W)),
        ],
        out_specs=[
            pl.BlockSpec((ctx_len, KV_W), lambda i: (i, 0)),
            pl.BlockSpec((ctx_len, KV_W), lambda i: (i, 0)),
        ],
        out_shape=[
            jax.ShapeDtypeStruct((n, KV_W), BF16),
            jax.ShapeDtypeStruct((n, KV_W), BF16),
        ],
        compiler_params=_params(1),
        name="ctxkv",
    )(ctx2, mods_ctx, g_pre, w_kv)


def _attn_kernel(sink_ref, q_ref, kp_ref, kc_ref, kn_ref, vp_ref, vc_ref, vn_ref,
                 kx_ref, vx_ref, bias_ref, o_ref):
    kcat = jnp.concatenate([kp_ref[...], kc_ref[...], kn_ref[...], kx_ref[...]], axis=0)
    vcat = jnp.concatenate([vp_ref[...], vc_ref[...], vn_ref[...], vx_ref[...]], axis=0)
    n_loc = 3 * BLOCK
    lane = lax.broadcasted_iota(jnp.int32, (BLOCK, LANES), 1)
    low = lane < HEAD_DIM
    bias = bias_ref[...]
    outs = []
    for h in range(N_KV_HEADS):
        keep = low if h == 0 else jnp.logical_not(low)
        lhs = jnp.concatenate(
            [jnp.where(keep, q_ref[:, j * LANES:(j + 1) * LANES], jnp.zeros((), BF16))
             for j in range(GROUP)], axis=0)
        s = lax.dot_general(lhs, kcat, (((1,), (1,)), ((), ())),
                            preferred_element_type=F32)
        s_loc = s[:, :n_loc] + bias
        s_ctx = s[:, n_loc:]
        sink_col = jnp.concatenate(
            [jnp.full((BLOCK, 1), sink_ref[h * GROUP + j], F32) for j in range(GROUP)], axis=0)
        m = jnp.maximum(jnp.maximum(jnp.max(s_loc, axis=-1, keepdims=True),
                                    jnp.max(s_ctx, axis=-1, keepdims=True)), sink_col)
        p_loc = jnp.exp(s_loc - m)
        p_ctx = jnp.exp(s_ctx - m)
        denom = (jnp.sum(p_loc, axis=-1, keepdims=True) + jnp.sum(p_ctx, axis=-1, keepdims=True)
                 + jnp.exp(sink_col - m))
        p = jnp.concatenate([p_loc, p_ctx], axis=1).astype(BF16)
        o = jnp.dot(p, vcat, preferred_element_type=F32)
        outs.append(o / denom)
    for j in range(GROUP):
        rows = slice(j * BLOCK, (j + 1) * BLOCK)
        o_ref[:, j * LANES:(j + 1) * LANES] = jnp.where(low, outs[0][rows], outs[1][rows]).astype(BF16)


def _attention(sink, q, k, v, kx, vx, bias, batch, seq, ctx_len):
    nb = seq // BLOCK
    n = batch * seq

    def cur(b, i):
        return (b * nb + i, 0)

    def prev(b, i):
        return (b * nb + jnp.maximum(i - 1, 0), 0)

    def nxt(b, i):
        return (b * nb + jnp.minimum(i + 1, nb - 1), 0)

    def bias_idx(b, i):
        return (jnp.where(i == 0, 0, jnp.where(i == nb - 1, 2, 1)), 0, 0)

    kv_spec = lambda f: pl.BlockSpec((BLOCK, KV_W), f)
    ctx_spec = pl.BlockSpec((ctx_len, KV_W), lambda b, i: (b, 0))
    return pl.pallas_call(
        _attn_kernel,
        grid=(batch, nb),
        in_specs=[
            pl.BlockSpec(memory_space=pltpu.SMEM),
            pl.BlockSpec((BLOCK, Q_W), cur),
            kv_spec(prev), kv_spec(cur), kv_spec(nxt),
            kv_spec(prev), kv_spec(cur), kv_spec(nxt),
            ctx_spec, ctx_spec,
            pl.BlockSpec((None, GROUP * BLOCK, 3 * BLOCK), bias_idx),
        ],
        out_specs=pl.BlockSpec((BLOCK, Q_W), cur),
        out_shape=jax.ShapeDtypeStruct((n, Q_W), BF16),
        compiler_params=_params(2),
        name="attention",
    )(sink, q, k, k, k, v, v, v, kx, vx, bias)


def _band_bias():
    qi = np.arange(BLOCK)[:, None]
    kj = np.arange(3 * BLOCK)[None, :]
    rel = kj - BLOCK - qi
    in_window = np.abs(rel) <= WINDOW
    variants = []
    for has_prev, has_next in ((False, True), (True, True), (True, False)):
        ok = in_window.copy()
        if not has_prev:
            ok &= kj >= BLOCK
        if not has_next:
            ok &= kj < 2 * BLOCK
        variants.append(np.tile(np.where(ok, 0.0, NEG).astype(np.float32), (GROUP, 1)))
    return jnp.asarray(np.stack(variants))


def _fourier_kernel(f_ref, f1_ref, tr_ref, ti_ref, km_ref, o_ref):
    n_major = f_ref.shape[0] // DFT_MINOR
    y2 = jnp.concatenate(
        [f_ref[pl.ds(s2, n_major, stride=DFT_MINOR), :].astype(BF16) for s2 in range(DFT_MINOR)],
        axis=1)
    a = jnp.dot(f1_ref[...], y2, preferred_element_type=F32)
    ar = a[:n_major]
    ai = a[n_major:]
    tr = tr_ref[...]
    ti = ti_ref[...]
    ap = jnp.concatenate([(ar * tr - ai * ti).astype(BF16),
                          (ar * ti + ai * tr).astype(BF16)], axis=1)
    r = jnp.dot(ap, km_ref[...], preferred_element_type=F32)
    for k2 in range(DFT_MINOR):
        o_ref[pl.ds(k2 * n_major, n_major), :] = r[:, k2 * LANES:(k2 + 1) * LANES].astype(BF16)


def _cis(num, den):
    ang = (num % den).astype(F32) * F32(-2.0 * math.pi / den)
    return jnp.cos(ang), jnp.sin(ang)


def _dft_tables(seq):
    n_major = seq // DFT_MINOR
    ch = FOURIER_GROUP_CH
    root = int(round(math.sqrt(n_major)))
    assert root * root == n_major
    s1 = jnp.arange(n_major, dtype=jnp.int32)[None, :]
    hi = jnp.arange(root, dtype=jnp.int32)[:, None] * root
    lo = jnp.arange(root, dtype=jnp.int32)[:, None]
    hr, hi_ = _cis(hi * s1, n_major)
    lr, li = _cis(lo * s1, n_major)
    f1r = (hr[:, None, :] * lr[None, :, :] - hi_[:, None, :] * li[None, :, :]).reshape(n_major, n_major)
    f1i = (hr[:, None, :] * li[None, :, :] + hi_[:, None, :] * lr[None, :, :]).reshape(n_major, n_major)
    f1 = jnp.concatenate([f1r, f1i], axis=0).astype(BF16)
    k1 = jnp.arange(n_major, dtype=jnp.int32)[:, None]
    s2 = jnp.arange(DFT_MINOR, dtype=jnp.int32)[None, :]
    tr, ti = _cis(k1 * s2, seq)
    tr = jnp.broadcast_to(tr[:, :, None], (n_major, DFT_MINOR, ch)).reshape(n_major, DFT_MINOR * ch)
    ti = jnp.broadcast_to(ti[:, :, None], (n_major, DFT_MINOR, ch)).reshape(n_major, DFT_MINOR * ch)
    a8 = jnp.arange(DFT_MINOR, dtype=jnp.int32)
    mr, mi = _cis(a8[:, None] * a8[None, :], DFT_MINOR)
    c128 = jnp.arange(ch, dtype=jnp.int32)
    cr, ci = _cis(c128[:, None] * c128[None, :], ch)
    norm = F32(1.0 / math.sqrt(seq * ch))
    kr = (mr[:, None, :, None] * cr[None, :, None, :] - mi[:, None, :, None] * ci[None, :, None, :])
    ki = (mr[:, None, :, None] * ci[None, :, None, :] + mi[:, None, :, None] * cr[None, :, None, :])
    kr = (kr * norm).reshape(DFT_MINOR * ch, DFT_MINOR * ch)
    ki = (ki * norm).reshape(DFT_MINOR * ch, DFT_MINOR * ch)
    km = jnp.concatenate([kr, -ki], axis=0).astype(BF16)
    return f1, tr, ti, km


def _fourier(f3, tables):
    batch, seq, _ = f3.shape
    f1, tr, ti, km = tables
    ch = FOURIER_GROUP_CH
    return pl.pallas_call(
        _fourier_kernel,
        grid=(batch, N_FOURIER_GROUPS),
        in_specs=[
            pl.BlockSpec((None, seq, ch), lambda b, g: (b, 0, g)),
            _const_spec(f1.shape),
            _const_spec(tr.shape),
            _const_spec(ti.shape),
            _const_spec(km.shape),
        ],
        out_specs=pl.BlockSpec((None, seq, ch), lambda b, g: (b, 0, g)),
        out_shape=jax.ShapeDtypeStruct((batch, seq, FOURIER_W), BF16),
        compiler_params=_params(2),
        name="fourier",
    )(f3, f1, tr, ti, km)


def _merge_kernel(x_ref, a_ref, f_ref, mod_ref, gpre_ref, gpost_ref,
                  wg_ref, wpa_ref, wpf_ref, wout_ref, o_ref):
    x = x_ref[...]
    h = _norm_modulate(x, gpre_ref[...], mod_ref[0:1, :], mod_ref[1:2, :]).astype(BF16)
    gates = jnp.dot(h, wg_ref[...], preferred_element_type=F32)
    pa = jnp.dot(a_ref[...], wpa_ref[...], preferred_element_type=F32)
    pf = jnp.dot(f_ref[...], wpf_ref[...], preferred_element_type=F32)
    m = jax.nn.sigmoid(gates[:, :D_MODEL]) * pa + jax.nn.sigmoid(gates[:, D_MODEL:]) * pf
    y = jnp.dot(m.astype(BF16), wout_ref[...], preferred_element_type=F32)
    o_ref[...] = x + mod_ref[2:3, :] * _post_norm(y, gpost_ref[...])


def _merge(x2, attn, four, mods, g_pre, g_post, w_g, w_pa, w_pf, w_out, seq):
    n = x2.shape[0]
    tm = TOKEN_TILE
    tiles_per_seq = seq // tm
    row = lambda i: (i, 0)
    return pl.pallas_call(
        _merge_kernel,
        grid=(n // tm,),
        in_specs=[
            pl.BlockSpec((tm, D_MODEL), row),
            pl.BlockSpec((tm, Q_W), row),
            pl.BlockSpec((tm, FOURIER_W), row),
            pl.BlockSpec((None, 6, D_MODEL), lambda i: (i // tiles_per_seq, 0, 0)),
            _const_spec((1, D_MODEL)),
            _const_spec((1, D_MODEL)),
            _const_spec(w_g.shape),
            _const_spec(w_pa.shape),
            _const_spec(w_pf.shape),
            _const_spec(w_out.shape),
        ],
        out_specs=pl.BlockSpec((tm, D_MODEL), row),
        out_shape=jax.ShapeDtypeStruct((n, D_MODEL), F32),
        compiler_params=_params(1),
        name="merge",
    )(x2, attn, four, mods, g_pre, g_post, w_g, w_pa, w_pf, w_out)


def _convffn_kernel(x_ref, xp_ref, xn_ref, mod_ref, gpre_ref, gpost_ref,
                    wu_ref, wgate_ref, cw_ref, cb_ref, wd_ref, o_ref, act_ref, *, tiles_per_seq):
    i = pl.program_id(0)
    tm = x_ref.shape[0]
    halo = SUBLANES
    has_prev = (i % tiles_per_seq != 0).astype(F32)
    has_next = (i % tiles_per_seq != tiles_per_seq - 1).astype(F32)
    x = x_ref[...]
    shift = mod_ref[3:4, :]
    scale = mod_ref[4:5, :]
    gain = gpre_ref[...]
    h = _norm_modulate(x, gain, shift, scale)
    hp = _norm_modulate(xp_ref[...], gain, shift, scale) * has_prev
    hn = _norm_modulate(xn_ref[...], gain, shift, scale) * has_next
    h_ext = jnp.concatenate([hp, h, hn], axis=0).astype(BF16)
    h_mid = h.astype(BF16)
    n_ext = tm + 2 * halo
    for c in range(D_FF // FFN_CHUNK):
        cols = slice(c * FFN_CHUNK, (c + 1) * FFN_CHUNK)
        u = jnp.dot(h_ext, wu_ref[:, cols], preferred_element_type=F32)
        gate = jnp.dot(h_mid, wgate_ref[:, cols], preferred_element_type=F32)
        u_prev = pltpu.roll(u, 1, 0)[halo:halo + tm]
        u_next = pltpu.roll(u, n_ext - 1, 0)[halo:halo + tm]
        conv = (u_prev * cw_ref[0:1, cols] + u[halo:halo + tm] * cw_ref[1:2, cols]
                + u_next * cw_ref[2:3, cols] + cb_ref[:, cols])
        act_ref[:, cols] = (conv * jax.nn.sigmoid(conv) * gate).astype(BF16)
    y = jnp.dot(act_ref[...], wd_ref[...], preferred_element_type=F32)
    o_ref[...] = x + mod_ref[5:6, :] * _post_norm(y, gpost_ref[...])


def _convffn(x1, mods, g_pre, g_post, w_u, w_gate, conv_w, conv_b, w_down, seq):
    n = x1.shape[0]
    tm = TOKEN_TILE
    tiles_per_seq = seq // tm
    halo_blocks_per_tile = tm // SUBLANES
    n_halo_blocks = n // SUBLANES
    return pl.pallas_call(
        functools.partial(_convffn_kernel, tiles_per_seq=tiles_per_seq),
        grid=(n // tm,),
        in_specs=[
            pl.BlockSpec((tm, D_MODEL), lambda i: (i, 0)),
            pl.BlockSpec((SUBLANES, D_MODEL),
                         lambda i: (jnp.maximum(i * halo_blocks_per_tile - 1, 0), 0)),
            pl.BlockSpec((SUBLANES, D_MODEL),
                         lambda i: (jnp.minimum((i + 1) * halo_blocks_per_tile, n_halo_blocks - 1), 0)),
            pl.BlockSpec((None, 6, D_MODEL), lambda i: (i // tiles_per_seq, 0, 0)),
            _const_spec((1, D_MODEL)),
            _const_spec((1, D_MODEL)),
            _const_spec(w_u.shape),
            _const_spec(w_gate.shape),
            _const_spec(conv_w.shape),
            _const_spec((1, D_FF)),
            _const_spec(w_down.shape),
        ],
        out_specs=pl.BlockSpec((tm, D_MODEL), lambda i: (i, 0)),
        out_shape=jax.ShapeDtypeStruct((n, D_MODEL), F32),
        scratch_shapes=[pltpu.VMEM((tm, D_FF), BF16)],
        compiler_params=_params(1),
        name="convffn",
    )(x1, x1, x1, mods, g_pre, g_post, w_u, w_gate, conv_w, conv_b.reshape(1, D_FF), w_down)


def _rope_tables(seq):
    rows = seq // GRID_W
    row = jnp.repeat(jnp.arange(rows), GRID_W).astype(F32)
    col = jnp.tile(jnp.arange(GRID_W), rows).astype(F32)
    half = HEAD_DIM // 2
    inv_freq = ROPE_THETA ** (-jnp.arange(0, half, 2, dtype=F32) / half)
    ang_r = row[:, None] * inv_freq
    ang_c = col[:, None] * inv_freq
    cos_h = jnp.concatenate([jnp.cos(ang_r), jnp.cos(ang_r), jnp.cos(ang_c), jnp.cos(ang_c)], axis=-1)
    sin_h = jnp.concatenate([-jnp.sin(ang_r), jnp.sin(ang_r), -jnp.sin(ang_c), jnp.sin(ang_c)], axis=-1)
    reps = LANES // HEAD_DIM
    return jnp.tile(cos_h, (1, reps)), jnp.tile(sin_h, (1, reps))


def _q_lane_permutation():
    lanes = np.arange(Q_W)
    j = lanes // LANES
    h = (lanes % LANES) // HEAD_DIM
    d = lanes % HEAD_DIM
    return (h * GROUP + j) * HEAD_DIM + d


def kernel(x, c, ctx, c_ctx, w_mod, b_mod, g_pre1, g_post1, g_pre2, g_post2,
           w_in, sink, w_pa, w_pf, w_out, w_up, conv_w, conv_b, w_down):
    batch, seq, d = x.shape
    ctx_len = ctx.shape[1]
    depth = w_mod.shape[0]
    assert depth == 1 and d == D_MODEL and batch + 1 <= SUBLANES
    assert seq % TOKEN_TILE == 0 and seq % (DFT_MINOR * SUBLANES) == 0
    n = batch * seq
    perm = _q_lane_permutation()
    cos_t, sin_t = _rope_tables(seq)
    tables = _dft_tables(seq)
    bias = _band_bias()

    l = 0
    cvecs = jnp.zeros((SUBLANES, d), F32).at[:batch].set(c).at[batch].set(c_ctx)
    mod_all = _adaln(cvecs, w_mod[l], b_mod[l])
    mods = mod_all[:batch].reshape(batch, 6, d)
    mods_ctx = mod_all[batch].reshape(6, d)

    o_q, o_k, o_v, o_f = Q_W, Q_W + KV_W, Q_W + 2 * KV_W, Q_W + 2 * KV_W + FOURIER_W
    wl = w_in[l]
    w_qkvf = jnp.concatenate([wl[:, :o_q][:, perm], wl[:, o_q:o_f]], axis=1).astype(BF16)
    w_kv = wl[:, o_q:o_v].astype(BF16)
    w_g = wl[:, o_f:].astype(BF16)
    g_pre1_row = g_pre1[l].reshape(1, d)
    g_post1_row = g_post1[l].reshape(1, d)
    g_pre2_row = g_pre2[l].reshape(1, d)
    g_post2_row = g_post2[l].reshape(1, d)

    x2 = x.reshape(n, d)
    q, k, v, f = _inproj(x2, mods, g_pre1_row, w_qkvf, cos_t, sin_t, seq)
    kx, vx = _ctxkv(ctx.reshape(batch * ctx_len, d), mods_ctx, g_pre1_row, w_kv, ctx_len)
    attn = _attention(sink[l], q, k, v, kx, vx, bias, batch, seq, ctx_len)
    four = _fourier(f.reshape(batch, seq, FOURIER_W), tables).reshape(n, FOURIER_W)
    x1 = _merge(x2, attn, four, mods, g_pre1_row, g_post1_row, w_g,
                w_pa[l][perm, :].astype(BF16), w_pf[l].astype(BF16), w_out[l].astype(BF16), seq)
    out = _convffn(x1, mods, g_pre2_row, g_post2_row,
                   w_up[l][:, :D_FF].astype(BF16), w_up[l][:, D_FF:].astype(BF16),
                   conv_w[l], conv_b[l], w_down[l].astype(BF16), seq)
    return out.reshape(batch, seq, d)
```

```python
import functools
import math

import numpy as np
import jax
import jax.numpy as jnp
from jax import lax
from jax.experimental import pallas as pl
from jax.experimental.pallas import tpu as pltpu

F32 = jnp.float32
BF16 = jnp.bfloat16

D_MODEL = 1024
GRID_W = 64
HEAD_DIM = 64
N_Q_HEADS = 8
N_KV_HEADS = 2
GROUP = N_Q_HEADS // N_KV_HEADS
WINDOW = 128
BLOCK = 128
ROPE_THETA = 10000.0
N_FOURIER_GROUPS = 4
FOURIER_GROUP_CH = 128
FOURIER_W = N_FOURIER_GROUPS * FOURIER_GROUP_CH
Q_W = N_Q_HEADS * HEAD_DIM
KV_W = N_KV_HEADS * HEAD_DIM
D_FF = 2816
EPS = 1e-6
NEG = -1e30
LOG2E = math.log2(math.e)

LANES = 128
SUBLANES = 8
MXU_DIM = 256
VMEM_LIMIT_BYTES = 56 * 1024 * 1024

TOKEN_TILE = 512
FFN_CHUNK = MXU_DIM
DFT_MINOR = SUBLANES


def _const_spec(shape):
    nd = len(shape)
    return pl.BlockSpec(shape, lambda *_: (0,) * nd, pipeline_mode=pl.Buffered(1))


def _params(n_axes):
    return pltpu.CompilerParams(
        dimension_semantics=("arbitrary",) * n_axes,
        vmem_limit_bytes=VMEM_LIMIT_BYTES,
    )


def _norm_modulate(x, gain, shift, scale):
    ms = jnp.mean(x * x, axis=-1, keepdims=True)
    return (x * lax.rsqrt(ms + EPS)) * (gain * (1.0 + scale)) + shift


def _post_norm(y, gain):
    ms = jnp.mean(y * y, axis=-1, keepdims=True)
    return (y * lax.rsqrt(ms + EPS)) * gain


def _adaln_kernel(c_ref, w_ref, b_ref, o_ref):
    c = c_ref[...]
    s = c * jax.nn.sigmoid(c)
    o_ref[...] = jnp.dot(s, w_ref[...], preferred_element_type=F32) + b_ref[...]


def _adaln(cvecs, w_mod, b_mod):
    n_out = w_mod.shape[1]
    tn = 1536
    return pl.pallas_call(
        _adaln_kernel,
        grid=(n_out // tn,),
        in_specs=[
            pl.BlockSpec((SUBLANES, D_MODEL), lambda j: (0, 0)),
            pl.BlockSpec((D_MODEL, tn), lambda j: (0, j)),
            pl.BlockSpec((1, tn), lambda j: (0, j)),
        ],
        out_specs=pl.BlockSpec((SUBLANES, tn), lambda j: (0, j)),
        out_shape=jax.ShapeDtypeStruct((SUBLANES, n_out), F32),
        compiler_params=_params(1),
        name="adaln",
    )(cvecs, w_mod, b_mod.reshape(1, n_out))


def _rope(xb, cos_t, sin_t, first_half):
    sw = jnp.where(first_half, pltpu.roll(xb, LANES - 16, 1), pltpu.roll(xb, 16, 1))
    return xb * cos_t + sw * sin_t


def _inproj_kernel(x_ref, mod_ref, g_ref, w_ref, cos_ref, sin_ref, q_ref, k_ref, v_ref, f_ref):
    h = _norm_modulate(x_ref[...], g_ref[...], mod_ref[0:1, :], mod_ref[1:2, :])
    p = jnp.dot(h.astype(BF16), w_ref[...], preferred_element_type=F32)
    cos_t = cos_ref[...]
    sin_t = sin_ref[...]
    lane = lax.broadcasted_iota(jnp.int32, cos_t.shape, 1)
    first_half = (lane % 32) < 16
    scale = HEAD_DIM ** -0.5 * LOG2E
    for b in range(Q_W // LANES):
        blk = p[:, b * LANES:(b + 1) * LANES]
        q_ref[:, b * LANES:(b + 1) * LANES] = (_rope(blk, cos_t, sin_t, first_half) * scale).astype(BF16)
    k_ref[...] = _rope(p[:, Q_W:Q_W + KV_W], cos_t, sin_t, first_half).astype(BF16)
    v_ref[...] = p[:, Q_W + KV_W:Q_W + 2 * KV_W].T.astype(BF16)
    f_ref[...] = p[:, Q_W + 2 * KV_W:]


def _inproj(x2, mods, g_pre, w_qkvf, cos_t, sin_t, seq):
    n = x2.shape[0]
    tm = TOKEN_TILE
    tiles_per_seq = seq // tm
    n_cols = w_qkvf.shape[1]
    return pl.pallas_call(
        _inproj_kernel,
        grid=(n // tm,),
        in_specs=[
            pl.BlockSpec((tm, D_MODEL), lambda i: (i, 0)),
            pl.BlockSpec((None, 6, D_MODEL), lambda i: (i // tiles_per_seq, 0, 0)),
            _const_spec((1, D_MODEL)),
            _const_spec((D_MODEL, n_cols)),
            pl.BlockSpec((tm, LANES), lambda i: (i % tiles_per_seq, 0)),
            pl.BlockSpec((tm, LANES), lambda i: (i % tiles_per_seq, 0)),
        ],
        out_specs=[
            pl.BlockSpec((tm, Q_W), lambda i: (i, 0)),
            pl.BlockSpec((tm, KV_W), lambda i: (i, 0)),
            pl.BlockSpec((KV_W, tm), lambda i: (0, i)),
            pl.BlockSpec((tm, FOURIER_W), lambda i: (i, 0)),
        ],
        out_shape=[
            jax.ShapeDtypeStruct((n, Q_W), BF16),
            jax.ShapeDtypeStruct((n, KV_W), BF16),
            jax.ShapeDtypeStruct((KV_W, n), BF16),
            jax.ShapeDtypeStruct((n, FOURIER_W), F32),
        ],
        compiler_params=_params(1),
        name="inproj",
    )(x2, mods, g_pre, w_qkvf, cos_t, sin_t)


def _ctxkv_kernel(x_ref, mod_ref, g_ref, w_ref, k_ref, v_ref):
    h = _norm_modulate(x_ref[...], g_ref[...], mod_ref[0:1, :], mod_ref[1:2, :])
    p = jnp.dot(h.astype(BF16), w_ref[...], preferred_element_type=F32)
    k_ref[...] = p[:, :KV_W].astype(BF16)
    v_ref[...] = p[:, KV_W:].T.astype(BF16)


def _ctxkv(ctx2, mods_ctx, g_pre, w_kv, ctx_len):
    n = ctx2.shape[0]
    return pl.pallas_call(
        _ctxkv_kernel,
        grid=(n // ctx_len,),
        in_specs=[
            pl.BlockSpec((ctx_len, D_MODEL), lambda i: (i, 0)),
            _const_spec((6, D_MODEL)),
            _const_spec((1, D_MODEL)),
            _const_spec((D_MODEL, 2 * KV_W)),
        ],
        out_specs=[
            pl.BlockSpec((ctx_len, KV_W), lambda i: (i, 0)),
            pl.BlockSpec((KV_W, ctx_len), lambda i: (0, i)),
        ],
        out_shape=[
            jax.ShapeDtypeStruct((n, KV_W), BF16),
            jax.ShapeDtypeStruct((KV_W, n), BF16),
        ],
        compiler_params=_params(1),
        name="ctxkv",
    )(ctx2, mods_ctx, g_pre, w_kv)


def _attn_kernel(sink_ref, q_ref, kp_ref, kc_ref, kn_ref, kx_ref, vp_ref, vc_ref, vn_ref, vx_ref,
                 bias_ref, o_ref):
    kcat = jnp.concatenate([kp_ref[...], kc_ref[...], kn_ref[...], kx_ref[...]], axis=0)
    vcat_t = jnp.concatenate([vp_ref[...], vc_ref[...], vn_ref[...], vx_ref[...]], axis=1)
    n_keys = kcat.shape[0]
    low = lax.broadcasted_iota(jnp.int32, (BLOCK, LANES), 1) < HEAD_DIM
    top = lax.broadcasted_iota(jnp.int32, (KV_W, n_keys), 0) < HEAD_DIM
    zero = jnp.zeros((), BF16)
    one = jnp.ones((), BF16)
    q_rows = jnp.concatenate(
        [jnp.where(low if h == 0 else jnp.logical_not(low), q_ref[:, j * LANES:(j + 1) * LANES], zero)
         for h in range(N_KV_HEADS) for j in range(GROUP)], axis=0)
    s_all = lax.dot_general(kcat, q_rows, (((1,), (1,)), ((), ())),
                            preferred_element_type=F32)
    bias_prev = bias_ref[0:BLOCK, :]
    bias_next = bias_ref[BLOCK:2 * BLOCK, :]
    outs = {}
    for h in range(N_KV_HEADS):
        v_h = jnp.where(top if h == 0 else jnp.logical_not(top), vcat_t, one)
        for j0 in range(0, GROUP, 2):
            probs = []
            sinks = []
            for j in (j0, j0 + 1):
                u = h * GROUP + j
                s = s_all[:, u * LANES:(u + 1) * LANES]
                parts = [s[0:BLOCK] + bias_prev,
                         s[BLOCK:2 * BLOCK],
                         s[2 * BLOCK:3 * BLOCK] + bias_next]
                parts += [s[r:r + BLOCK] for r in range(3 * BLOCK, n_keys, BLOCK)]
                mx = parts[0]
                for part in parts[1:]:
                    mx = jnp.maximum(mx, part)
                sink = sink_ref[u] * LOG2E
                m = jnp.maximum(jnp.max(mx, axis=0, keepdims=True), sink)
                probs.append(jnp.concatenate([jnp.exp2(part - m) for part in parts], axis=0).astype(BF16))
                sinks.append(jnp.exp2(sink - m))
            o2 = jnp.dot(v_h, jnp.concatenate(probs, axis=1), preferred_element_type=F32)
            for idx, j in enumerate((j0, j0 + 1)):
                o_t = o2[:, idx * LANES:(idx + 1) * LANES]
                num = o_t[0:HEAD_DIM] if h == 0 else o_t[HEAD_DIM:]
                den = (o_t[HEAD_DIM:] if h == 0 else o_t[0:HEAD_DIM]) + sinks[idx]
                outs[(h, j)] = num / den
    for j in range(GROUP):
        o_t = jnp.concatenate([outs[(0, j)], outs[(1, j)]], axis=0)
        o_ref[:, j * LANES:(j + 1) * LANES] = o_t.T.astype(BF16)


def _attention(sink, q, k, v_t, kx, vx_t, bias, batch, seq, ctx_len):
    nb = seq // BLOCK
    n = batch * seq

    def cur(b, i):
        return (b * nb + i, 0)

    def prev(b, i):
        return (b * nb + jnp.maximum(i - 1, 0), 0)

    def nxt(b, i):
        return (b * nb + jnp.minimum(i + 1, nb - 1), 0)

    def swap(f):
        return lambda b, i: f(b, i)[::-1]

    def bias_idx(b, i):
        return (jnp.where(i == 0, 0, jnp.where(i == nb - 1, 2, 1)), 0, 0)

    k_spec = lambda f: pl.BlockSpec((BLOCK, KV_W), f)
    v_spec = lambda f: pl.BlockSpec((KV_W, BLOCK), swap(f))
    return pl.pallas_call(
        _attn_kernel,
        grid=(batch, nb),
        in_specs=[
            pl.BlockSpec(memory_space=pltpu.SMEM),
            pl.BlockSpec((BLOCK, Q_W), cur),
            k_spec(prev), k_spec(cur), k_spec(nxt),
            pl.BlockSpec((ctx_len, KV_W), lambda b, i: (b, 0)),
            v_spec(prev), v_spec(cur), v_spec(nxt),
            pl.BlockSpec((KV_W, ctx_len), lambda b, i: (0, b)),
            pl.BlockSpec((None, 2 * BLOCK, BLOCK), bias_idx),
        ],
        out_specs=pl.BlockSpec((BLOCK, Q_W), cur),
        out_shape=jax.ShapeDtypeStruct((n, Q_W), BF16),
        compiler_params=_params(2),
        name="attention",
    )(sink, q, k, k, k, kx, v_t, v_t, v_t, vx_t, bias)


def _band_bias():
    qi = np.arange(BLOCK)[None, :]
    kj = np.arange(3 * BLOCK)[:, None]
    rel = kj - BLOCK - qi
    in_window = np.abs(rel) <= WINDOW
    variants = []
    for has_prev, has_next in ((False, True), (True, True), (True, False)):
        ok = in_window.copy()
        if not has_prev:
            ok &= kj >= BLOCK
        if not has_next:
            ok &= kj < 2 * BLOCK
        mask = np.where(ok, 0.0, NEG).astype(np.float32)
        variants.append(np.concatenate([mask[:BLOCK], mask[2 * BLOCK:]], axis=0))
    return jnp.asarray(np.stack(variants))


def _fourier_kernel(f_ref, f1_ref, tr_ref, ti_ref, km_ref, o_ref):
    n_major = f_ref.shape[0] // DFT_MINOR
    y2 = jnp.concatenate(
        [f_ref[pl.ds(s2, n_major, stride=DFT_MINOR), :].astype(BF16) for s2 in range(DFT_MINOR)],
        axis=1)
    a = jnp.dot(f1_ref[...], y2, preferred_element_type=F32)
    ar = a[:n_major]
    ai = a[n_major:]
    tr = tr_ref[...]
    ti = ti_ref[...]
    ap = jnp.concatenate([(ar * tr - ai * ti).astype(BF16),
                          (ar * ti + ai * tr).astype(BF16)], axis=1)
    r = jnp.dot(ap, km_ref[...], preferred_element_type=F32)
    for k2 in range(DFT_MINOR):
        o_ref[pl.ds(k2 * n_major, n_major), :] = r[:, k2 * LANES:(k2 + 1) * LANES].astype(BF16)


def _cis(num, den):
    ang = (num % den).astype(F32) * F32(-2.0 * math.pi / den)
    return jnp.cos(ang), jnp.sin(ang)


def _dft_tables(seq):
    n_major = seq // DFT_MINOR
    ch = FOURIER_GROUP_CH
    root = int(round(math.sqrt(n_major)))
    assert root * root == n_major
    s1 = jnp.arange(n_major, dtype=jnp.int32)[None, :]
    hi = jnp.arange(root, dtype=jnp.int32)[:, None] * root
    lo = jnp.arange(root, dtype=jnp.int32)[:, None]
    hr, hi_ = _cis(hi * s1, n_major)
    lr, li = _cis(lo * s1, n_major)
    f1r = (hr[:, None, :] * lr[None, :, :] - hi_[:, None, :] * li[None, :, :]).reshape(n_major, n_major)
    f1i = (hr[:, None, :] * li[None, :, :] + hi_[:, None, :] * lr[None, :, :]).reshape(n_major, n_major)
    f1 = jnp.concatenate([f1r, f1i], axis=0).astype(BF16)
    k1 = jnp.arange(n_major, dtype=jnp.int32)[:, None]
    s2 = jnp.arange(DFT_MINOR, dtype=jnp.int32)[None, :]
    tr, ti = _cis(k1 * s2, seq)
    tr = jnp.broadcast_to(tr[:, :, None], (n_major, DFT_MINOR, ch)).reshape(n_major, DFT_MINOR * ch)
    ti = jnp.broadcast_to(ti[:, :, None], (n_major, DFT_MINOR, ch)).reshape(n_major, DFT_MINOR * ch)
    a8 = jnp.arange(DFT_MINOR, dtype=jnp.int32)
    mr, mi = _cis(a8[:, None] * a8[None, :], DFT_MINOR)
    c128 = jnp.arange(ch, dtype=jnp.int32)
    cr, ci = _cis(c128[:, None] * c128[None, :], ch)
    norm = F32(1.0 / math.sqrt(seq * ch))
    kr = (mr[:, None, :, None] * cr[None, :, None, :] - mi[:, None, :, None] * ci[None, :, None, :])
    ki = (mr[:, None, :, None] * ci[None, :, None, :] + mi[:, None, :, None] * cr[None, :, None, :])
    kr = (kr * norm).reshape(DFT_MINOR * ch, DFT_MINOR * ch)
    ki = (ki * norm).reshape(DFT_MINOR * ch, DFT_MINOR * ch)
    km = jnp.concatenate([kr, -ki], axis=0).astype(BF16)
    return f1, tr, ti, km


def _fourier(f3, tables):
    batch, seq, _ = f3.shape
    f1, tr, ti, km = tables
    ch = FOURIER_GROUP_CH
    return pl.pallas_call(
        _fourier_kernel,
        grid=(batch, N_FOURIER_GROUPS),
        in_specs=[
            pl.BlockSpec((None, seq, ch), lambda b, g: (b, 0, g)),
            _const_spec(f1.shape),
            _const_spec(tr.shape),
            _const_spec(ti.shape),
            _const_spec(km.shape),
        ],
        out_specs=pl.BlockSpec((None, seq, ch), lambda b, g: (b, 0, g)),
        out_shape=jax.ShapeDtypeStruct((batch, seq, FOURIER_W), BF16),
        compiler_params=_params(2),
        name="fourier",
    )(f3, f1, tr, ti, km)


def _merge_kernel(x_ref, a_ref, f_ref, mod_ref, gpre_ref, gpost_ref,
                  wg_ref, wpa_ref, wpf_ref, wout_ref, o_ref):
    x = x_ref[...]
    h = _norm_modulate(x, gpre_ref[...], mod_ref[0:1, :], mod_ref[1:2, :]).astype(BF16)
    gates = jnp.dot(h, wg_ref[...], preferred_element_type=F32)
    pa = jnp.dot(a_ref[...], wpa_ref[...], preferred_element_type=F32)
    pf = jnp.dot(f_ref[...], wpf_ref[...], preferred_element_type=F32)
    m = jax.nn.sigmoid(gates[:, :D_MODEL]) * pa + jax.nn.sigmoid(gates[:, D_MODEL:]) * pf
    y = jnp.dot(m.astype(BF16), wout_ref[...], preferred_element_type=F32)
    o_ref[...] = x + mod_ref[2:3, :] * _post_norm(y, gpost_ref[...])


def _merge(x2, attn, four, mods, g_pre, g_post, w_g, w_pa, w_pf, w_out, seq):
    n = x2.shape[0]
    tm = TOKEN_TILE
    tiles_per_seq = seq // tm
    row = lambda i: (i, 0)
    return pl.pallas_call(
        _merge_kernel,
        grid=(n // tm,),
        in_specs=[
            pl.BlockSpec((tm, D_MODEL), row),
            pl.BlockSpec((tm, Q_W), row),
            pl.BlockSpec((tm, FOURIER_W), row),
            pl.BlockSpec((None, 6, D_MODEL), lambda i: (i // tiles_per_seq, 0, 0)),
            _const_spec((1, D_MODEL)),
            _const_spec((1, D_MODEL)),
            _const_spec(w_g.shape),
            _const_spec(w_pa.shape),
            _const_spec(w_pf.shape),
            _const_spec(w_out.shape),
        ],
        out_specs=pl.BlockSpec((tm, D_MODEL), row),
        out_shape=jax.ShapeDtypeStruct((n, D_MODEL), F32),
        compiler_params=_params(1),
        name="merge",
    )(x2, attn, four, mods, g_pre, g_post, w_g, w_pa, w_pf, w_out)


def _convffn_kernel(x_ref, xp_ref, xn_ref, mod_ref, gpre_ref, gpost_ref,
                    wu_ref, wgate_ref, cw_ref, cb_ref, wd_ref, o_ref, act_ref, *, tiles_per_seq):
    i = pl.program_id(0)
    tm = x_ref.shape[0]
    halo = SUBLANES
    has_prev = (i % tiles_per_seq != 0).astype(F32)
    has_next = (i % tiles_per_seq != tiles_per_seq - 1).astype(F32)
    x = x_ref[...]
    shift = mod_ref[3:4, :]
    scale = mod_ref[4:5, :]
    gain = gpre_ref[...]
    h = _norm_modulate(x, gain, shift, scale)
    hp = _norm_modulate(xp_ref[...], gain, shift, scale) * has_prev
    hn = _norm_modulate(xn_ref[...], gain, shift, scale) * has_next
    h_ext = jnp.concatenate([hp, h, hn], axis=0).astype(BF16)
    h_mid = h.astype(BF16)
    n_ext = tm + 2 * halo
    for c in range(D_FF // FFN_CHUNK):
        cols = slice(c * FFN_CHUNK, (c + 1) * FFN_CHUNK)
        u = jnp.dot(h_ext, wu_ref[:, cols], preferred_element_type=F32)
        gate = jnp.dot(h_mid, wgate_ref[:, cols], preferred_element_type=F32)
        u_prev = pltpu.roll(u, 1, 0)[halo:halo + tm]
        u_next = pltpu.roll(u, n_ext - 1, 0)[halo:halo + tm]
        conv = (u_prev * cw_ref[0:1, cols] + u[halo:halo + tm] * cw_ref[1:2, cols]
                + u_next * cw_ref[2:3, cols] + cb_ref[:, cols])
        act_ref[:, cols] = (conv * jax.nn.sigmoid(conv) * gate).astype(BF16)
    y = jnp.dot(act_ref[...], wd_ref[...], preferred_element_type=F32)
    o_ref[...] = x + mod_ref[5:6, :] * _post_norm(y, gpost_ref[...])


def _convffn(x1, mods, g_pre, g_post, w_u, w_gate, conv_w, conv_b, w_down, seq):
    n = x1.shape[0]
    tm = TOKEN_TILE
    tiles_per_seq = seq // tm
    halo_blocks_per_tile = tm // SUBLANES
    n_halo_blocks = n // SUBLANES
    return pl.pallas_call(
        functools.partial(_convffn_kernel, tiles_per_seq=tiles_per_seq),
        grid=(n // tm,),
        in_specs=[
            pl.BlockSpec((tm, D_MODEL), lambda i: (i, 0)),
            pl.BlockSpec((SUBLANES, D_MODEL),
                         lambda i: (jnp.maximum(i * halo_blocks_per_tile - 1, 0), 0)),
            pl.BlockSpec((SUBLANES, D_MODEL),
                         lambda i: (jnp.minimum((i + 1) * halo_blocks_per_tile, n_halo_blocks - 1), 0)),
            pl.BlockSpec((None, 6, D_MODEL), lambda i: (i // tiles_per_seq, 0, 0)),
            _const_spec((1, D_MODEL)),
            _const_spec((1, D_MODEL)),
            _const_spec(w_u.shape),
            _const_spec(w_gate.shape),
            _const_spec(conv_w.shape),
            _const_spec((1, D_FF)),
            _const_spec(w_down.shape),
        ],
        out_specs=pl.BlockSpec((tm, D_MODEL), lambda i: (i, 0)),
        out_shape=jax.ShapeDtypeStruct((n, D_MODEL), F32),
        scratch_shapes=[pltpu.VMEM((tm, D_FF), BF16)],
        compiler_params=_params(1),
        name="convffn",
    )(x1, x1, x1, mods, g_pre, g_post, w_u, w_gate, conv_w, conv_b.reshape(1, D_FF), w_down)


def _rope_tables(seq):
    rows = seq // GRID_W
    row = jnp.repeat(jnp.arange(rows), GRID_W).astype(F32)
    col = jnp.tile(jnp.arange(GRID_W), rows).astype(F32)
    half = HEAD_DIM // 2
    inv_freq = ROPE_THETA ** (-jnp.arange(0, half, 2, dtype=F32) / half)
    ang_r = row[:, None] * inv_freq
    ang_c = col[:, None] * inv_freq
    cos_h = jnp.concatenate([jnp.cos(ang_r), jnp.cos(ang_r), jnp.cos(ang_c), jnp.cos(ang_c)], axis=-1)
    sin_h = jnp.concatenate([-jnp.sin(ang_r), jnp.sin(ang_r), -jnp.sin(ang_c), jnp.sin(ang_c)], axis=-1)
    reps = LANES // HEAD_DIM
    return jnp.tile(cos_h, (1, reps)), jnp.tile(sin_h, (1, reps))


def _q_lane_permutation():
    lanes = np.arange(Q_W)
    j = lanes // LANES
    h = (lanes % LANES) // HEAD_DIM
    d = lanes % HEAD_DIM
    return (h * GROUP + j) * HEAD_DIM + d


def kernel(x, c, ctx, c_ctx, w_mod, b_mod, g_pre1, g_post1, g_pre2, g_post2,
           w_in, sink, w_pa, w_pf, w_out, w_up, conv_w, conv_b, w_down):
    batch, seq, d = x.shape
    ctx_len = ctx.shape[1]
    depth = w_mod.shape[0]
    assert depth == 1 and d == D_MODEL and batch + 1 <= SUBLANES
    assert seq % TOKEN_TILE == 0 and seq % (DFT_MINOR * SUBLANES) == 0
    n = batch * seq
    perm = _q_lane_permutation()
    cos_t, sin_t = _rope_tables(seq)
    tables = _dft_tables(seq)
    bias = _band_bias()

    l = 0
    cvecs = jnp.zeros((SUBLANES, d), F32).at[:batch].set(c).at[batch].set(c_ctx)
    mod_all = _adaln(cvecs, w_mod[l], b_mod[l])
    mods = mod_all[:batch].reshape(batch, 6, d)
    mods_ctx = mod_all[batch].reshape(6, d)

    o_q, o_k, o_v, o_f = Q_W, Q_W + KV_W, Q_W + 2 * KV_W, Q_W + 2 * KV_W + FOURIER_W
    wl = w_in[l]
    w_qkvf = jnp.concatenate([wl[:, :o_q][:, perm], wl[:, o_q:o_f]], axis=1).astype(BF16)
    w_kv = wl[:, o_q:o_v].astype(BF16)
    w_g = wl[:, o_f:].astype(BF16)
    g_pre1_row = g_pre1[l].reshape(1, d)
    g_post1_row = g_post1[l].reshape(1, d)
    g_pre2_row = g_pre2[l].reshape(1, d)
    g_post2_row = g_post2[l].reshape(1, d)

    x2 = x.reshape(n, d)
    q, k, v, f = _inproj(x2, mods, g_pre1_row, w_qkvf, cos_t, sin_t, seq)
    kx, vx = _ctxkv(ctx.reshape(batch * ctx_len, d), mods_ctx, g_pre1_row, w_kv, ctx_len)
    attn = _attention(sink[l], q, k, v, kx, vx, bias, batch, seq, ctx_len)
    four = _fourier(f.reshape(batch, seq, FOURIER_W), tables).reshape(n, FOURIER_W)
    x1 = _merge(x2, attn, four, mods, g_pre1_row, g_post1_row, w_g,
                w_pa[l][perm, :].astype(BF16), w_pf[l].astype(BF16), w_out[l].astype(BF16), seq)
    out = _convffn(x1, mods, g_pre2_row, g_post2_row,
                   w_up[l][:, :D_FF].astype(BF16), w_up[l][:, D_FF:].astype(BF16),
                   conv_w[l], conv_b[l], w_down[l].astype(BF16), seq)
    return out.reshape(batch, seq, d)
```

```python
import functools
import math

import numpy as np
import jax
import jax.numpy as jnp
from jax import lax
from jax.experimental import pallas as pl
from jax.experimental.pallas import tpu as pltpu

F32 = jnp.float32
BF16 = jnp.bfloat16

D_MODEL = 1024
GRID_W = 64
HEAD_DIM = 64
N_Q_HEADS = 8
N_KV_HEADS = 2
GROUP = N_Q_HEADS // N_KV_HEADS
WINDOW = 128
BLOCK = 128
ROPE_THETA = 10000.0
N_FOURIER_GROUPS = 4
FOURIER_GROUP_CH = 128
FOURIER_W = N_FOURIER_GROUPS * FOURIER_GROUP_CH
Q_W = N_Q_HEADS * HEAD_DIM
KV_W = N_KV_HEADS * HEAD_DIM
D_FF = 2816
EPS = 1e-6
NEG = -1e30
LOG2E = math.log2(math.e)

LANES = 128
SUBLANES = 8
MXU_DIM = 256
VMEM_LIMIT_BYTES = 56 * 1024 * 1024

TOKEN_TILE = 512
FFN_CHUNK = MXU_DIM
DFT_MINOR = SUBLANES


def _const_spec(shape):
    nd = len(shape)
    return pl.BlockSpec(shape, lambda *_: (0,) * nd, pipeline_mode=pl.Buffered(1))


def _params(n_axes):
    return pltpu.CompilerParams(
        dimension_semantics=("arbitrary",) * n_axes,
        vmem_limit_bytes=VMEM_LIMIT_BYTES,
    )


def _norm_modulate(x, gain, shift, scale):
    ms = jnp.mean(x * x, axis=-1, keepdims=True)
    return (x * lax.rsqrt(ms + EPS)) * (gain * (1.0 + scale)) + shift


def _post_norm(y, gain):
    ms = jnp.mean(y * y, axis=-1, keepdims=True)
    return (y * lax.rsqrt(ms + EPS)) * gain


def _adaln_kernel(c_ref, w_ref, b_ref, o_ref):
    c = c_ref[...]
    s = c * jax.nn.sigmoid(c)
    o_ref[...] = jnp.dot(s, w_ref[...], preferred_element_type=F32) + b_ref[...]


def _adaln(cvecs, w_mod, b_mod):
    n_out = w_mod.shape[1]
    tn = 1536
    return pl.pallas_call(
        _adaln_kernel,
        grid=(n_out // tn,),
        in_specs=[
            pl.BlockSpec((SUBLANES, D_MODEL), lambda j: (0, 0)),
            pl.BlockSpec((D_MODEL, tn), lambda j: (0, j)),
            pl.BlockSpec((1, tn), lambda j: (0, j)),
        ],
        out_specs=pl.BlockSpec((SUBLANES, tn), lambda j: (0, j)),
        out_shape=jax.ShapeDtypeStruct((SUBLANES, n_out), F32),
        compiler_params=_params(1),
        name="adaln",
    )(cvecs, w_mod, b_mod.reshape(1, n_out))


def _rope(xb, cos_t, sin_t, first_half):
    sw = jnp.where(first_half, pltpu.roll(xb, LANES - 16, 1), pltpu.roll(xb, 16, 1))
    return xb * cos_t + sw * sin_t


def _token_table(row_tab, col_tab):
    n_rows = row_tab.shape[0]
    by_row = jnp.concatenate(
        [jnp.broadcast_to(row_tab[r:r + 1, :], (GRID_W, LANES)) for r in range(n_rows)], axis=0)
    return by_row + jnp.concatenate([col_tab] * n_rows, axis=0)


def _inproj_kernel(x_ref, mod_ref, g_ref, w_ref, rcos_ref, rsin_ref, ccos_ref, csin_ref,
                   q_ref, k_ref, v_ref, f_ref, wb_ref):
    @pl.when(pl.program_id(0) == 0)
    def _():
        low = lax.broadcasted_iota(jnp.int32, (D_MODEL, LANES), 1) < HEAD_DIM
        for j in range(Q_W // LANES):
            src_a = (j // 2) * LANES
            src_b = (GROUP // 2 + j // 2) * LANES
            a = w_ref[:, src_a:src_a + LANES]
            b = w_ref[:, src_b:src_b + LANES]
            if j % 2 == 1:
                a = pltpu.roll(a, HEAD_DIM, 1)
            else:
                b = pltpu.roll(b, HEAD_DIM, 1)
            wb_ref[:, j * LANES:(j + 1) * LANES] = jnp.where(low, a, b).astype(BF16)
        wb_ref[:, Q_W:] = w_ref[:, Q_W:].astype(BF16)

    h = _norm_modulate(x_ref[...], g_ref[...], mod_ref[0:1, :], mod_ref[1:2, :])
    p = jnp.dot(h.astype(BF16), wb_ref[...], preferred_element_type=F32)
    cos_t = _token_table(rcos_ref[...], ccos_ref[...])
    sin_t = _token_table(rsin_ref[...], csin_ref[...])
    lane = lax.broadcasted_iota(jnp.int32, cos_t.shape, 1)
    first_half = (lane % 32) < 16
    scale = HEAD_DIM ** -0.5 * LOG2E
    for b in range(Q_W // LANES):
        blk = p[:, b * LANES:(b + 1) * LANES]
        q_ref[:, b * LANES:(b + 1) * LANES] = (_rope(blk, cos_t, sin_t, first_half) * scale).astype(BF16)
    k_ref[...] = _rope(p[:, Q_W:Q_W + KV_W], cos_t, sin_t, first_half).astype(BF16)
    v_ref[...] = p[:, Q_W + KV_W:Q_W + 2 * KV_W].T.astype(BF16)
    f_ref[...] = p[:, Q_W + 2 * KV_W:]


def _inproj(x2, mods, g_pre, w_in, rope_tabs, seq):
    n = x2.shape[0]
    tm = TOKEN_TILE
    tiles_per_seq = seq // tm
    rows_per_tile = tm // GRID_W
    n_cols = Q_W + 2 * KV_W + FOURIER_W
    row_spec = pl.BlockSpec((rows_per_tile, LANES), lambda i: (i % tiles_per_seq, 0))
    return pl.pallas_call(
        _inproj_kernel,
        grid=(n // tm,),
        in_specs=[
            pl.BlockSpec((tm, D_MODEL), lambda i: (i, 0)),
            pl.BlockSpec((None, 6, D_MODEL), lambda i: (i // tiles_per_seq, 0, 0)),
            _const_spec((1, D_MODEL)),
            _const_spec((D_MODEL, n_cols)),
            row_spec, row_spec,
            _const_spec((GRID_W, LANES)), _const_spec((GRID_W, LANES)),
        ],
        out_specs=[
            pl.BlockSpec((tm, Q_W), lambda i: (i, 0)),
            pl.BlockSpec((tm, KV_W), lambda i: (i, 0)),
            pl.BlockSpec((KV_W, tm), lambda i: (0, i)),
            pl.BlockSpec((tm, FOURIER_W), lambda i: (i, 0)),
        ],
        out_shape=[
            jax.ShapeDtypeStruct((n, Q_W), BF16),
            jax.ShapeDtypeStruct((n, KV_W), BF16),
            jax.ShapeDtypeStruct((KV_W, n), BF16),
            jax.ShapeDtypeStruct((n, FOURIER_W), F32),
        ],
        scratch_shapes=[pltpu.VMEM((D_MODEL, n_cols), BF16)],
        compiler_params=_params(1),
        name="inproj",
    )(x2, mods, g_pre, w_in, *rope_tabs)


def _ctxkv_kernel(x_ref, mod_ref, g_ref, w_ref, k_ref, v_ref):
    h = _norm_modulate(x_ref[...], g_ref[...], mod_ref[0:1, :], mod_ref[1:2, :])
    p = jnp.dot(h.astype(BF16), w_ref[...].astype(BF16), preferred_element_type=F32)
    k_ref[...] = p[:, :KV_W].astype(BF16)
    v_ref[...] = p[:, KV_W:].T.astype(BF16)


def _ctxkv(ctx2, mods_ctx, g_pre, w_in, ctx_len):
    n = ctx2.shape[0]
    assert Q_W % (2 * KV_W) == 0
    return pl.pallas_call(
        _ctxkv_kernel,
        grid=(n // ctx_len,),
        in_specs=[
            pl.BlockSpec((ctx_len, D_MODEL), lambda i: (i, 0)),
            _const_spec((6, D_MODEL)),
            _const_spec((1, D_MODEL)),
            pl.BlockSpec((D_MODEL, 2 * KV_W), lambda i: (0, Q_W // (2 * KV_W)),
                         pipeline_mode=pl.Buffered(1)),
        ],
        out_specs=[
            pl.BlockSpec((ctx_len, KV_W), lambda i: (i, 0)),
            pl.BlockSpec((KV_W, ctx_len), lambda i: (0, i)),
        ],
        out_shape=[
            jax.ShapeDtypeStruct((n, KV_W), BF16),
            jax.ShapeDtypeStruct((KV_W, n), BF16),
        ],
        compiler_params=_params(1),
        name="ctxkv",
    )(ctx2, mods_ctx, g_pre, w_in)


def _attn_kernel(sink_ref, q_ref, kp_ref, kc_ref, kn_ref, kx_ref, vp_ref, vc_ref, vn_ref, vx_ref,
                 bias_ref, o_ref):
    kcat = jnp.concatenate([kp_ref[...], kc_ref[...], kn_ref[...], kx_ref[...]], axis=0)
    vcat_t = jnp.concatenate([vp_ref[...], vc_ref[...], vn_ref[...], vx_ref[...]], axis=1)
    n_keys = kcat.shape[0]
    low = lax.broadcasted_iota(jnp.int32, (BLOCK, LANES), 1) < HEAD_DIM
    top = lax.broadcasted_iota(jnp.int32, (KV_W, n_keys), 0) < HEAD_DIM
    zero = jnp.zeros((), BF16)
    one = jnp.ones((), BF16)
    q_rows = [jnp.where(low if h == 0 else jnp.logical_not(low), q_ref[:, g * LANES:(g + 1) * LANES], zero)
              for h in range(N_KV_HEADS) for g in range(GROUP)]
    s_all = lax.dot_general(kcat, jnp.concatenate(q_rows, axis=0), (((1,), (1,)), ((), ())),
                            preferred_element_type=F32)
    bias_prev = bias_ref[0:BLOCK, :]
    bias_next = bias_ref[BLOCK:2 * BLOCK, :]
    outs = []
    for h in range(N_KV_HEADS):
        v_h = jnp.where(top if h == 0 else jnp.logical_not(top), vcat_t, one)
        for g0 in range(0, GROUP, 2):
            probs = []
            sinks = []
            for g in (g0, g0 + 1):
                head = h * GROUP + g
                s = s_all[:, head * LANES:(head + 1) * LANES]
                parts = [s[0:BLOCK] + bias_prev,
                         s[BLOCK:2 * BLOCK],
                         s[2 * BLOCK:3 * BLOCK] + bias_next]
                parts += [s[r:r + BLOCK] for r in range(3 * BLOCK, n_keys, BLOCK)]
                mx = parts[0]
                for part in parts[1:]:
                    mx = jnp.maximum(mx, part)
                sink = sink_ref[h * GROUP + g] * LOG2E
                m = jnp.maximum(jnp.max(mx, axis=0, keepdims=True), sink)
                probs.append(jnp.concatenate([jnp.exp2(part - m) for part in parts], axis=0).astype(BF16))
                sinks.append(jnp.exp2(sink - m))
            o2 = jnp.dot(v_h, jnp.concatenate(probs, axis=1), preferred_element_type=F32)
            for idx in range(2):
                o_t = o2[:, idx * LANES:(idx + 1) * LANES]
                num = o_t[0:HEAD_DIM] if h == 0 else o_t[HEAD_DIM:]
                den = (o_t[HEAD_DIM:] if h == 0 else o_t[0:HEAD_DIM]) + sinks[idx]
                outs.append(num / den)
    for j in range(Q_W // LANES):
        o_t = jnp.concatenate([outs[2 * j], outs[2 * j + 1]], axis=0)
        o_ref[:, j * LANES:(j + 1) * LANES] = o_t.T.astype(BF16)


def _attention(sink, q, k, v_t, kx, vx_t, bias, batch, seq, ctx_len):
    nb = seq // BLOCK
    n = batch * seq

    def cur(b, i):
        return (b * nb + i, 0)

    def prev(b, i):
        return (b * nb + jnp.maximum(i - 1, 0), 0)

    def nxt(b, i):
        return (b * nb + jnp.minimum(i + 1, nb - 1), 0)

    def swap(f):
        return lambda b, i: f(b, i)[::-1]

    def bias_idx(b, i):
        return (jnp.where(i == 0, 0, jnp.where(i == nb - 1, 2, 1)), 0, 0)

    k_spec = lambda f: pl.BlockSpec((BLOCK, KV_W), f)
    v_spec = lambda f: pl.BlockSpec((KV_W, BLOCK), swap(f))
    return pl.pallas_call(
        _attn_kernel,
        grid=(batch, nb),
        in_specs=[
            pl.BlockSpec(memory_space=pltpu.SMEM),
            pl.BlockSpec((BLOCK, Q_W), cur),
            k_spec(prev), k_spec(cur), k_spec(nxt),
            pl.BlockSpec((ctx_len, KV_W), lambda b, i: (b, 0)),
            v_spec(prev), v_spec(cur), v_spec(nxt),
            pl.BlockSpec((KV_W, ctx_len), lambda b, i: (0, b)),
            pl.BlockSpec((None, 2 * BLOCK, BLOCK), bias_idx),
        ],
        out_specs=pl.BlockSpec((BLOCK, Q_W), cur),
        out_shape=jax.ShapeDtypeStruct((n, Q_W), BF16),
        compiler_params=_params(2),
        name="attention",
    )(sink, q, k, k, k, kx, v_t, v_t, v_t, vx_t, bias)


def _band_bias():
    qi = np.arange(BLOCK)[None, :]
    kj = np.arange(3 * BLOCK)[:, None]
    rel = kj - BLOCK - qi
    in_window = np.abs(rel) <= WINDOW
    variants = []
    for has_prev, has_next in ((False, True), (True, True), (True, False)):
        ok = in_window.copy()
        if not has_prev:
            ok &= kj >= BLOCK
        if not has_next:
            ok &= kj < 2 * BLOCK
        mask = np.where(ok, 0.0, NEG).astype(np.float32)
        variants.append(np.concatenate([mask[:BLOCK], mask[2 * BLOCK:]], axis=0))
    return jnp.asarray(np.stack(variants))


def _fourier_kernel(f_ref, f1_ref, tr_ref, ti_ref, km_ref, o_ref):
    n_major = f_ref.shape[0] // DFT_MINOR
    y2 = jnp.concatenate(
        [f_ref[pl.ds(s2, n_major, stride=DFT_MINOR), :].astype(BF16) for s2 in range(DFT_MINOR)],
        axis=1)
    a = jnp.dot(f1_ref[...], y2, preferred_element_type=F32)
    ar = a[:n_major]
    ai = a[n_major:]
    tr = tr_ref[...]
    ti = ti_ref[...]
    ap = jnp.concatenate([(ar * tr - ai * ti).astype(BF16),
                          (ar * ti + ai * tr).astype(BF16)], axis=1)
    r = jnp.dot(ap, km_ref[...], preferred_element_type=F32)
    for k2 in range(DFT_MINOR):
        o_ref[pl.ds(k2 * n_major, n_major), :] = r[:, k2 * LANES:(k2 + 1) * LANES].astype(BF16)


def _cis(num, den):
    ang = (num % den).astype(F32) * F32(-2.0 * math.pi / den)
    return jnp.cos(ang), jnp.sin(ang)


def _dft_tables(seq):
    n_major = seq // DFT_MINOR
    ch = FOURIER_GROUP_CH
    root = int(round(math.sqrt(n_major)))
    assert root * root == n_major
    s1 = jnp.arange(n_major, dtype=jnp.int32)[None, :]
    hi = jnp.arange(root, dtype=jnp.int32)[:, None] * root
    lo = jnp.arange(root, dtype=jnp.int32)[:, None]
    hr, hi_ = _cis(hi * s1, n_major)
    lr, li = _cis(lo * s1, n_major)
    f1r = (hr[:, None, :] * lr[None, :, :] - hi_[:, None, :] * li[None, :, :]).reshape(n_major, n_major)
    f1i = (hr[:, None, :] * li[None, :, :] + hi_[:, None, :] * lr[None, :, :]).reshape(n_major, n_major)
    f1 = jnp.concatenate([f1r, f1i], axis=0).astype(BF16)
    k1 = jnp.arange(n_major, dtype=jnp.int32)[:, None]
    s2 = jnp.arange(DFT_MINOR, dtype=jnp.int32)[None, :]
    tr, ti = _cis(k1 * s2, seq)
    tr = jnp.broadcast_to(tr[:, :, None], (n_major, DFT_MINOR, ch)).reshape(n_major, DFT_MINOR * ch)
    ti = jnp.broadcast_to(ti[:, :, None], (n_major, DFT_MINOR, ch)).reshape(n_major, DFT_MINOR * ch)
    a8 = jnp.arange(DFT_MINOR, dtype=jnp.int32)
    mr, mi = _cis(a8[:, None] * a8[None, :], DFT_MINOR)
    c128 = jnp.arange(ch, dtype=jnp.int32)
    cr, ci = _cis(c128[:, None] * c128[None, :], ch)
    norm = F32(1.0 / math.sqrt(seq * ch))
    kr = (mr[:, None, :, None] * cr[None, :, None, :] - mi[:, None, :, None] * ci[None, :, None, :])
    ki = (mr[:, None, :, None] * ci[None, :, None, :] + mi[:, None, :, None] * cr[None, :, None, :])
    kr = (kr * norm).reshape(DFT_MINOR * ch, DFT_MINOR * ch)
    ki = (ki * norm).reshape(DFT_MINOR * ch, DFT_MINOR * ch)
    km = jnp.concatenate([kr, -ki], axis=0).astype(BF16)
    return f1, tr, ti, km


def _fourier(f3, tables):
    batch, seq, _ = f3.shape
    f1, tr, ti, km = tables
    ch = FOURIER_GROUP_CH
    return pl.pallas_call(
        _fourier_kernel,
        grid=(batch, N_FOURIER_GROUPS),
        in_specs=[
            pl.BlockSpec((None, seq, ch), lambda b, g: (b, 0, g)),
            _const_spec(f1.shape),
            _const_spec(tr.shape),
            _const_spec(ti.shape),
            _const_spec(km.shape),
        ],
        out_specs=pl.BlockSpec((None, seq, ch), lambda b, g: (b, 0, g)),
        out_shape=jax.ShapeDtypeStruct((batch, seq, FOURIER_W), BF16),
        compiler_params=_params(2),
        name="fourier",
    )(f3, f1, tr, ti, km)


def _merge_kernel(x_ref, a_ref, f_ref, mod_ref, gpre_ref, gpost_ref,
                  wg_ref, wpa_ref, wpf_ref, wout_ref, o_ref, wg_b, wpa_b, wpf_b, wout_b):
    @pl.when(pl.program_id(0) == 0)
    def _():
        wg_b[...] = wg_ref[...].astype(BF16)
        wpa_b[...] = wpa_ref[...].astype(BF16)
        wpf_b[...] = wpf_ref[...].astype(BF16)
        wout_b[...] = wout_ref[...].astype(BF16)

    x = x_ref[...]
    h = _norm_modulate(x, gpre_ref[...], mod_ref[0:1, :], mod_ref[1:2, :]).astype(BF16)
    gates = jnp.dot(h, wg_b[...], preferred_element_type=F32)
    pa = jnp.dot(a_ref[...], wpa_b[...], preferred_element_type=F32)
    pf = jnp.dot(f_ref[...], wpf_b[...], preferred_element_type=F32)
    m = jax.nn.sigmoid(gates[:, :D_MODEL]) * pa + jax.nn.sigmoid(gates[:, D_MODEL:]) * pf
    y = jnp.dot(m.astype(BF16), wout_b[...], preferred_element_type=F32)
    o_ref[...] = x + mod_ref[2:3, :] * _post_norm(y, gpost_ref[...])


def _merge(x2, attn, four, mods, g_pre, g_post, w_in, w_pa, w_pf, w_out, seq):
    n = x2.shape[0]
    tm = TOKEN_TILE
    tiles_per_seq = seq // tm
    row = lambda i: (i, 0)
    gate_col0 = Q_W + 2 * KV_W + FOURIER_W
    return pl.pallas_call(
        _merge_kernel,
        grid=(n // tm,),
        in_specs=[
            pl.BlockSpec((tm, D_MODEL), row),
            pl.BlockSpec((tm, Q_W), row),
            pl.BlockSpec((tm, FOURIER_W), row),
            pl.BlockSpec((None, 6, D_MODEL), lambda i: (i // tiles_per_seq, 0, 0)),
            _const_spec((1, D_MODEL)),
            _const_spec((1, D_MODEL)),
            pl.BlockSpec((pl.Element(D_MODEL), pl.Element(2 * D_MODEL)),
                         lambda i: (0, gate_col0),
                         pipeline_mode=pl.Buffered(1)),
            _const_spec(w_pa.shape),
            _const_spec(w_pf.shape),
            _const_spec(w_out.shape),
        ],
        out_specs=pl.BlockSpec((tm, D_MODEL), row),
        out_shape=jax.ShapeDtypeStruct((n, D_MODEL), F32),
        scratch_shapes=[pltpu.VMEM((D_MODEL, 2 * D_MODEL), BF16), pltpu.VMEM(w_pa.shape, BF16),
                        pltpu.VMEM(w_pf.shape, BF16), pltpu.VMEM(w_out.shape, BF16)],
        compiler_params=_params(1),
        name="merge",
    )(x2, attn, four, mods, g_pre, g_post, w_in, w_pa, w_pf, w_out)


def _convffn_kernel(x_ref, xp_ref, xn_ref, mod_ref, gpre_ref, gpost_ref,
                    wu_ref, wgate_ref, cw_ref, cb_ref, wd_ref, o_ref, act_ref, *, tiles_per_seq):
    i = pl.program_id(0)
    tm = x_ref.shape[0]
    halo = SUBLANES
    has_prev = (i % tiles_per_seq != 0).astype(F32)
    has_next = (i % tiles_per_seq != tiles_per_seq - 1).astype(F32)
    x = x_ref[...]
    shift = mod_ref[3:4, :]
    scale = mod_ref[4:5, :]
    gain = gpre_ref[...]
    h = _norm_modulate(x, gain, shift, scale)
    hp = _norm_modulate(xp_ref[...], gain, shift, scale) * has_prev
    hn = _norm_modulate(xn_ref[...], gain, shift, scale) * has_next
    h_ext = jnp.concatenate([hp, h, hn], axis=0).astype(BF16)
    h_mid = h.astype(BF16)
    n_ext = tm + 2 * halo
    for c in range(D_FF // FFN_CHUNK):
        cols = slice(c * FFN_CHUNK, (c + 1) * FFN_CHUNK)
        u = jnp.dot(h_ext, wu_ref[:, cols], preferred_element_type=F32)
        gate = jnp.dot(h_mid, wgate_ref[:, cols], preferred_element_type=F32)
        u_prev = pltpu.roll(u, 1, 0)[halo:halo + tm]
        u_next = pltpu.roll(u, n_ext - 1, 0)[halo:halo + tm]
        conv = (u_prev * cw_ref[0:1, cols] + u[halo:halo + tm] * cw_ref[1:2, cols]
                + u_next * cw_ref[2:3, cols] + cb_ref[:, cols])
        act_ref[:, cols] = (conv * jax.nn.sigmoid(conv) * gate).astype(BF16)
    y = jnp.dot(act_ref[...], wd_ref[...], preferred_element_type=F32)
    o_ref[...] = x + mod_ref[5:6, :] * _post_norm(y, gpost_ref[...])


def _convffn(x1, mods, g_pre, g_post, w_u, w_gate, conv_w, conv_b, w_down, seq):
    n = x1.shape[0]
    tm = TOKEN_TILE
    tiles_per_seq = seq // tm
    halo_blocks_per_tile = tm // SUBLANES
    n_halo_blocks = n // SUBLANES
    return pl.pallas_call(
        functools.partial(_convffn_kernel, tiles_per_seq=tiles_per_seq),
        grid=(n // tm,),
        in_specs=[
            pl.BlockSpec((tm, D_MODEL), lambda i: (i, 0)),
            pl.BlockSpec((SUBLANES, D_MODEL),
                         lambda i: (jnp.maximum(i * halo_blocks_per_tile - 1, 0), 0)),
            pl.BlockSpec((SUBLANES, D_MODEL),
                         lambda i: (jnp.minimum((i + 1) * halo_blocks_per_tile, n_halo_blocks - 1), 0)),
            pl.BlockSpec((None, 6, D_MODEL), lambda i: (i // tiles_per_seq, 0, 0)),
            _const_spec((1, D_MODEL)),
            _const_spec((1, D_MODEL)),
            _const_spec(w_u.shape),
            _const_spec(w_gate.shape),
            _const_spec(conv_w.shape),
            _const_spec((1, D_FF)),
            _const_spec(w_down.shape),
        ],
        out_specs=pl.BlockSpec((tm, D_MODEL), lambda i: (i, 0)),
        out_shape=jax.ShapeDtypeStruct((n, D_MODEL), F32),
        scratch_shapes=[pltpu.VMEM((tm, D_FF), BF16)],
        compiler_params=_params(1),
        name="convffn",
    )(x1, x1, x1, mods, g_pre, g_post, w_u, w_gate, conv_w, conv_b.reshape(1, D_FF), w_down)


def _rope_tables(seq):
    half = HEAD_DIM // 2
    inv_freq = ROPE_THETA ** (-jnp.arange(0, half, 2, dtype=F32) / half)
    ang_r = jnp.arange(seq // GRID_W).astype(F32)[:, None] * inv_freq
    ang_c = jnp.arange(GRID_W).astype(F32)[:, None] * inv_freq
    reps = LANES // HEAD_DIM

    def lanes(row_part, col_part):
        return jnp.tile(jnp.concatenate([row_part, col_part], axis=-1), (1, reps))

    zr = jnp.zeros((ang_r.shape[0], half), F32)
    zc = jnp.zeros((ang_c.shape[0], half), F32)
    rcos = lanes(jnp.concatenate([jnp.cos(ang_r), jnp.cos(ang_r)], axis=-1), zr)
    rsin = lanes(jnp.concatenate([-jnp.sin(ang_r), jnp.sin(ang_r)], axis=-1), zr)
    ccos = lanes(zc, jnp.concatenate([jnp.cos(ang_c), jnp.cos(ang_c)], axis=-1))
    csin = lanes(zc, jnp.concatenate([-jnp.sin(ang_c), jnp.sin(ang_c)], axis=-1))
    return rcos, rsin, ccos, csin


def kernel(x, c, ctx, c_ctx, w_mod, b_mod, g_pre1, g_post1, g_pre2, g_post2,
           w_in, sink, w_pa, w_pf, w_out, w_up, conv_w, conv_b, w_down):
    batch, seq, d = x.shape
    ctx_len = ctx.shape[1]
    depth = w_mod.shape[0]
    assert depth == 1 and d == D_MODEL and batch + 1 <= SUBLANES
    assert seq % TOKEN_TILE == 0 and TOKEN_TILE % GRID_W == 0 and seq % (DFT_MINOR * SUBLANES) == 0
    n = batch * seq
    rope_tabs = _rope_tables(seq)
    tables = _dft_tables(seq)
    bias = _band_bias()

    l = 0
    cvecs = jnp.zeros((SUBLANES, d), F32).at[:batch].set(c).at[batch].set(c_ctx)
    mod_all = _adaln(cvecs, w_mod[l], b_mod[l])
    mods = mod_all[:batch].reshape(batch, 6, d)
    mods_ctx = mod_all[batch].reshape(6, d)

    g_pre1_row = g_pre1[l].reshape(1, d)
    g_post1_row = g_post1[l].reshape(1, d)
    g_pre2_row = g_pre2[l].reshape(1, d)
    g_post2_row = g_post2[l].reshape(1, d)

    x2 = x.reshape(n, d)
    q, k, v_t, f = _inproj(x2, mods, g_pre1_row, w_in[l], rope_tabs, seq)
    kx, vx_t = _ctxkv(ctx.reshape(batch * ctx_len, d), mods_ctx, g_pre1_row, w_in[l], ctx_len)
    attn = _attention(sink[l], q, k, v_t, kx, vx_t, bias, batch, seq, ctx_len)
    four = _fourier(f.reshape(batch, seq, FOURIER_W), tables).reshape(n, FOURIER_W)
    x1 = _merge(x2, attn, four, mods, g_pre1_row, g_post1_row, w_in[l], w_pa[l], w_pf[l], w_out[l], seq)
    out = _convffn(x1, mods, g_pre2_row, g_post2_row,
                   w_up[l][:, :D_FF].astype(BF16), w_up[l][:, D_FF:].astype(BF16),
                   conv_w[l], conv_b[l], w_down[l].astype(BF16), seq)
    return out.reshape(batch, seq, d)
```

```python
import functools
import math

import numpy as np
import jax
import jax.numpy as jnp
from jax import lax
from jax.experimental import pallas as pl
from jax.experimental.pallas import tpu as pltpu

F32 = jnp.float32
BF16 = jnp.bfloat16

D_MODEL = 1024
GRID_W = 64
HEAD_DIM = 64
N_Q_HEADS = 8
N_KV_HEADS = 2
GROUP = N_Q_HEADS // N_KV_HEADS
WINDOW = 128
BLOCK = 128
ROPE_THETA = 10000.0
N_FOURIER_GROUPS = 4
FOURIER_GROUP_CH = 128
FOURIER_W = N_FOURIER_GROUPS * FOURIER_GROUP_CH
Q_W = N_Q_HEADS * HEAD_DIM
KV_W = N_KV_HEADS * HEAD_DIM
D_FF = 2816
EPS = 1e-6
NEG = -1e30
LOG2E = math.log2(math.e)

LANES = 128
SUBLANES = 8
BF16_ROWS = 16
MXU_DIM = 256
VMEM_LIMIT_BYTES = 56 * 1024 * 1024

TOKEN_TILE = 512
FFN_CHUNK = MXU_DIM
DFT_MINOR = SUBLANES


def _const_spec(shape):
    nd = len(shape)
    return pl.BlockSpec(shape, lambda *_: (0,) * nd, pipeline_mode=pl.Buffered(1))


def _params(n_axes):
    return pltpu.CompilerParams(
        dimension_semantics=("arbitrary",) * n_axes,
        vmem_limit_bytes=VMEM_LIMIT_BYTES,
    )


def _norm_modulate(x, gain, shift, scale):
    ms = jnp.mean(x * x, axis=-1, keepdims=True)
    return (x * lax.rsqrt(ms + EPS)) * (gain * (1.0 + scale)) + shift


def _post_norm(y, gain):
    ms = jnp.mean(y * y, axis=-1, keepdims=True)
    return (y * lax.rsqrt(ms + EPS)) * gain


def _adaln_kernel(c_ref, w_ref, b_ref, o_ref):
    c = c_ref[...]
    s = c * jax.nn.sigmoid(c)
    o_ref[...] = jnp.dot(s, w_ref[...], preferred_element_type=F32) + b_ref[...]


def _adaln(cvecs, w_mod, b_mod):
    n_out = w_mod.shape[1]
    tn = 1536
    return pl.pallas_call(
        _adaln_kernel,
        grid=(n_out // tn,),
        in_specs=[
            pl.BlockSpec((SUBLANES, D_MODEL), lambda j: (0, 0)),
            pl.BlockSpec((D_MODEL, tn), lambda j: (0, j)),
            pl.BlockSpec((1, tn), lambda j: (0, j)),
        ],
        out_specs=pl.BlockSpec((SUBLANES, tn), lambda j: (0, j)),
        out_shape=jax.ShapeDtypeStruct((SUBLANES, n_out), F32),
        compiler_params=_params(1),
        name="adaln",
    )(cvecs, w_mod, b_mod.reshape(1, n_out))


def _rope(xb, cos_t, sin_t, first_half):
    sw = jnp.where(first_half, pltpu.roll(xb, LANES - 16, 1), pltpu.roll(xb, 16, 1))
    return xb * cos_t + sw * sin_t


def _token_table(row_tab, col_tab):
    n_rows = row_tab.shape[0]
    by_row = jnp.concatenate(
        [jnp.broadcast_to(row_tab[r:r + 1, :], (GRID_W, LANES)) for r in range(n_rows)], axis=0)
    return by_row + jnp.concatenate([col_tab] * n_rows, axis=0)


def _inproj_kernel(x_ref, mod_ref, g_ref, w_ref, rcos_ref, rsin_ref, ccos_ref, csin_ref,
                   q_ref, k_ref, v_ref, f_ref, wb_ref):
    @pl.when(pl.program_id(0) == 0)
    def _():
        low = lax.broadcasted_iota(jnp.int32, (D_MODEL, LANES), 1) < HEAD_DIM
        for j in range(Q_W // LANES):
            src_a = (j // 2) * LANES
            src_b = (GROUP // 2 + j // 2) * LANES
            a = w_ref[:, src_a:src_a + LANES]
            b = w_ref[:, src_b:src_b + LANES]
            if j % 2 == 1:
                a = pltpu.roll(a, HEAD_DIM, 1)
            else:
                b = pltpu.roll(b, HEAD_DIM, 1)
            wb_ref[:, j * LANES:(j + 1) * LANES] = jnp.where(low, a, b).astype(BF16)
        wb_ref[:, Q_W:] = w_ref[:, Q_W:].astype(BF16)

    h = _norm_modulate(x_ref[...], g_ref[...], mod_ref[0:1, :], mod_ref[1:2, :])
    p = jnp.dot(h.astype(BF16), wb_ref[...], preferred_element_type=F32)
    cos_t = _token_table(rcos_ref[...], ccos_ref[...])
    sin_t = _token_table(rsin_ref[...], csin_ref[...])
    lane = lax.broadcasted_iota(jnp.int32, cos_t.shape, 1)
    first_half = (lane % 32) < 16
    scale = HEAD_DIM ** -0.5 * LOG2E
    for b in range(Q_W // LANES):
        blk = p[:, b * LANES:(b + 1) * LANES]
        q_ref[:, b * LANES:(b + 1) * LANES] = (_rope(blk, cos_t, sin_t, first_half) * scale).astype(BF16)
    k_ref[...] = _rope(p[:, Q_W:Q_W + KV_W], cos_t, sin_t, first_half).astype(BF16)
    v_ref[...] = p[:, Q_W + KV_W:Q_W + 2 * KV_W].T.astype(BF16)
    f_ref[...] = p[:, Q_W + 2 * KV_W:]


def _inproj(x2, mods, g_pre, w_in, rope_tabs, seq):
    n = x2.shape[0]
    tm = TOKEN_TILE
    tiles_per_seq = seq // tm
    rows_per_tile = tm // GRID_W
    n_cols = Q_W + 2 * KV_W + FOURIER_W
    row_spec = pl.BlockSpec((rows_per_tile, LANES), lambda i: (i % tiles_per_seq, 0))
    return pl.pallas_call(
        _inproj_kernel,
        grid=(n // tm,),
        in_specs=[
            pl.BlockSpec((tm, D_MODEL), lambda i: (i, 0)),
            pl.BlockSpec((None, 6, D_MODEL), lambda i: (i // tiles_per_seq, 0, 0)),
            _const_spec((1, D_MODEL)),
            _const_spec((D_MODEL, n_cols)),
            row_spec, row_spec,
            _const_spec((GRID_W, LANES)), _const_spec((GRID_W, LANES)),
        ],
        out_specs=[
            pl.BlockSpec((tm, Q_W), lambda i: (i, 0)),
            pl.BlockSpec((tm, KV_W), lambda i: (i, 0)),
            pl.BlockSpec((KV_W, tm), lambda i: (0, i)),
            pl.BlockSpec((tm, FOURIER_W), lambda i: (i, 0)),
        ],
        out_shape=[
            jax.ShapeDtypeStruct((n, Q_W), BF16),
            jax.ShapeDtypeStruct((n, KV_W), BF16),
            jax.ShapeDtypeStruct((KV_W, n), BF16),
            jax.ShapeDtypeStruct((n, FOURIER_W), F32),
        ],
        scratch_shapes=[pltpu.VMEM((D_MODEL, n_cols), BF16)],
        compiler_params=_params(1),
        name="inproj",
    )(x2, mods, g_pre, w_in, *rope_tabs)


def _ctxkv_kernel(x_ref, mod_ref, g_ref, w_ref, k_ref, v_ref):
    h = _norm_modulate(x_ref[...], g_ref[...], mod_ref[0:1, :], mod_ref[1:2, :])
    p = jnp.dot(h.astype(BF16), w_ref[...].astype(BF16), preferred_element_type=F32)
    k_ref[...] = p[:, :KV_W].astype(BF16)
    v_ref[...] = p[:, KV_W:].T.astype(BF16)


def _ctxkv(ctx2, mods_ctx, g_pre, w_in, ctx_len):
    n = ctx2.shape[0]
    assert Q_W % (2 * KV_W) == 0
    return pl.pallas_call(
        _ctxkv_kernel,
        grid=(n // ctx_len,),
        in_specs=[
            pl.BlockSpec((ctx_len, D_MODEL), lambda i: (i, 0)),
            _const_spec((6, D_MODEL)),
            _const_spec((1, D_MODEL)),
            pl.BlockSpec((D_MODEL, 2 * KV_W), lambda i: (0, Q_W // (2 * KV_W)),
                         pipeline_mode=pl.Buffered(1)),
        ],
        out_specs=[
            pl.BlockSpec((ctx_len, KV_W), lambda i: (i, 0)),
            pl.BlockSpec((KV_W, ctx_len), lambda i: (0, i)),
        ],
        out_shape=[
            jax.ShapeDtypeStruct((n, KV_W), BF16),
            jax.ShapeDtypeStruct((KV_W, n), BF16),
        ],
        compiler_params=_params(1),
        name="ctxkv",
    )(ctx2, mods_ctx, g_pre, w_in)


def _attn_kernel(sink_ref, q_ref, kp_ref, kc_ref, kn_ref, kx_ref, vp_ref, vc_ref, vn_ref, vx_ref,
                 bias_ref, o_ref):
    kcat = jnp.concatenate([kp_ref[...], kc_ref[...], kn_ref[...], kx_ref[...]], axis=0)
    vcat_t = jnp.concatenate([vp_ref[...], vc_ref[...], vn_ref[...], vx_ref[...]], axis=1)
    n_keys = kcat.shape[0]
    low = lax.broadcasted_iota(jnp.int32, (BLOCK, LANES), 1) < HEAD_DIM
    top = lax.broadcasted_iota(jnp.int32, (KV_W, n_keys), 0) < HEAD_DIM
    zero = jnp.zeros((), BF16)
    one = jnp.ones((), BF16)
    q_rows = [jnp.where(low if h == 0 else jnp.logical_not(low), q_ref[:, g * LANES:(g + 1) * LANES], zero)
              for h in range(N_KV_HEADS) for g in range(GROUP)]
    s_all = lax.dot_general(kcat, jnp.concatenate(q_rows, axis=0), (((1,), (1,)), ((), ())),
                            preferred_element_type=F32)
    bias_prev = bias_ref[0:BLOCK, :]
    bias_next = bias_ref[BLOCK:2 * BLOCK, :]
    outs = []
    for h in range(N_KV_HEADS):
        v_h = jnp.where(top if h == 0 else jnp.logical_not(top), vcat_t, one)
        for g0 in range(0, GROUP, 2):
            probs = []
            sinks = []
            for g in (g0, g0 + 1):
                head = h * GROUP + g
                s = s_all[:, head * LANES:(head + 1) * LANES]
                parts = [s[0:BLOCK] + bias_prev,
                         s[BLOCK:2 * BLOCK],
                         s[2 * BLOCK:3 * BLOCK] + bias_next]
                parts += [s[r:r + BLOCK] for r in range(3 * BLOCK, n_keys, BLOCK)]
                mx = parts[0]
                for part in parts[1:]:
                    mx = jnp.maximum(mx, part)
                sink = sink_ref[h * GROUP + g] * LOG2E
                m = jnp.maximum(jnp.max(mx, axis=0, keepdims=True), sink)
                probs.append(jnp.concatenate([jnp.exp2(part - m) for part in parts], axis=0).astype(BF16))
                sinks.append(jnp.exp2(sink - m))
            o2 = jnp.dot(v_h, jnp.concatenate(probs, axis=1), preferred_element_type=F32)
            for idx in range(2):
                o_t = o2[:, idx * LANES:(idx + 1) * LANES]
                num = o_t[0:HEAD_DIM] if h == 0 else o_t[HEAD_DIM:]
                den = (o_t[HEAD_DIM:] if h == 0 else o_t[0:HEAD_DIM]) + sinks[idx]
                outs.append(num / den)
    for j in range(Q_W // LANES):
        o_t = jnp.concatenate([outs[2 * j], outs[2 * j + 1]], axis=0)
        o_ref[:, j * LANES:(j + 1) * LANES] = o_t.T.astype(BF16)


def _attention(sink, q, k, v_t, kx, vx_t, bias, batch, seq, ctx_len):
    nb = seq // BLOCK
    n = batch * seq

    def cur(b, i):
        return (b * nb + i, 0)

    def prev(b, i):
        return (b * nb + jnp.maximum(i - 1, 0), 0)

    def nxt(b, i):
        return (b * nb + jnp.minimum(i + 1, nb - 1), 0)

    def swap(f):
        return lambda b, i: f(b, i)[::-1]

    def bias_idx(b, i):
        return (jnp.where(i == 0, 0, jnp.where(i == nb - 1, 2, 1)), 0, 0)

    k_spec = lambda f: pl.BlockSpec((BLOCK, KV_W), f)
    v_spec = lambda f: pl.BlockSpec((KV_W, BLOCK), swap(f))
    return pl.pallas_call(
        _attn_kernel,
        grid=(batch, nb),
        in_specs=[
            pl.BlockSpec(memory_space=pltpu.SMEM),
            pl.BlockSpec((BLOCK, Q_W), cur),
            k_spec(prev), k_spec(cur), k_spec(nxt),
            pl.BlockSpec((ctx_len, KV_W), lambda b, i: (b, 0)),
            v_spec(prev), v_spec(cur), v_spec(nxt),
            pl.BlockSpec((KV_W, ctx_len), lambda b, i: (0, b)),
            pl.BlockSpec((None, 2 * BLOCK, BLOCK), bias_idx),
        ],
        out_specs=pl.BlockSpec((BLOCK, Q_W), cur),
        out_shape=jax.ShapeDtypeStruct((n, Q_W), BF16),
        compiler_params=_params(2),
        name="attention",
    )(sink, q, k, k, k, kx, v_t, v_t, v_t, vx_t, bias)


def _band_bias():
    qi = np.arange(BLOCK)[None, :]
    kj = np.arange(3 * BLOCK)[:, None]
    rel = kj - BLOCK - qi
    in_window = np.abs(rel) <= WINDOW
    variants = []
    for has_prev, has_next in ((False, True), (True, True), (True, False)):
        ok = in_window.copy()
        if not has_prev:
            ok &= kj >= BLOCK
        if not has_next:
            ok &= kj < 2 * BLOCK
        mask = np.where(ok, 0.0, NEG).astype(np.float32)
        variants.append(np.concatenate([mask[:BLOCK], mask[2 * BLOCK:]], axis=0))
    return jnp.asarray(np.stack(variants))


def _fourier_kernel(f_ref, f1_ref, tr_ref, ti_ref, km_ref, o_ref):
    n_major = f_ref.shape[0] // DFT_MINOR
    y2 = jnp.concatenate(
        [f_ref[pl.ds(s2, n_major, stride=DFT_MINOR), :].astype(BF16) for s2 in range(DFT_MINOR)],
        axis=1)
    a = jnp.dot(f1_ref[...], y2, preferred_element_type=F32)
    ar = a[:n_major]
    ai = a[n_major:]
    tr = tr_ref[...]
    ti = ti_ref[...]
    ap = jnp.concatenate([(ar * tr - ai * ti).astype(BF16),
                          (ar * ti + ai * tr).astype(BF16)], axis=1)
    r = jnp.dot(ap, km_ref[...], preferred_element_type=F32)
    for k2 in range(DFT_MINOR):
        o_ref[pl.ds(k2 * n_major, n_major), :] = r[:, k2 * LANES:(k2 + 1) * LANES].astype(BF16)


def _cis(num, den):
    ang = (num % den).astype(F32) * F32(-2.0 * math.pi / den)
    return jnp.cos(ang), jnp.sin(ang)


def _dft_tables(seq):
    n_major = seq // DFT_MINOR
    ch = FOURIER_GROUP_CH
    root = int(round(math.sqrt(n_major)))
    assert root * root == n_major
    s1 = jnp.arange(n_major, dtype=jnp.int32)[None, :]
    hi = jnp.arange(root, dtype=jnp.int32)[:, None] * root
    lo = jnp.arange(root, dtype=jnp.int32)[:, None]
    hr, hi_ = _cis(hi * s1, n_major)
    lr, li = _cis(lo * s1, n_major)
    f1r = (hr[:, None, :] * lr[None, :, :] - hi_[:, None, :] * li[None, :, :]).reshape(n_major, n_major)
    f1i = (hr[:, None, :] * li[None, :, :] + hi_[:, None, :] * lr[None, :, :]).reshape(n_major, n_major)
    f1 = jnp.concatenate([f1r, f1i], axis=0).astype(BF16)
    k1 = jnp.arange(n_major, dtype=jnp.int32)[:, None]
    s2 = jnp.arange(DFT_MINOR, dtype=jnp.int32)[None, :]
    tr, ti = _cis(k1 * s2, seq)
    tr = jnp.broadcast_to(tr[:, :, None], (n_major, DFT_MINOR, ch)).reshape(n_major, DFT_MINOR * ch)
    ti = jnp.broadcast_to(ti[:, :, None], (n_major, DFT_MINOR, ch)).reshape(n_major, DFT_MINOR * ch)
    a8 = jnp.arange(DFT_MINOR, dtype=jnp.int32)
    mr, mi = _cis(a8[:, None] * a8[None, :], DFT_MINOR)
    c128 = jnp.arange(ch, dtype=jnp.int32)
    cr, ci = _cis(c128[:, None] * c128[None, :], ch)
    norm = F32(1.0 / math.sqrt(seq * ch))
    kr = (mr[:, None, :, None] * cr[None, :, None, :] - mi[:, None, :, None] * ci[None, :, None, :])
    ki = (mr[:, None, :, None] * ci[None, :, None, :] + mi[:, None, :, None] * cr[None, :, None, :])
    kr = (kr * norm).reshape(DFT_MINOR * ch, DFT_MINOR * ch)
    ki = (ki * norm).reshape(DFT_MINOR * ch, DFT_MINOR * ch)
    km = jnp.concatenate([kr, -ki], axis=0).astype(BF16)
    return f1, tr, ti, km


def _fourier(f3, tables):
    batch, seq, _ = f3.shape
    f1, tr, ti, km = tables
    ch = FOURIER_GROUP_CH
    return pl.pallas_call(
        _fourier_kernel,
        grid=(batch, N_FOURIER_GROUPS),
        in_specs=[
            pl.BlockSpec((None, seq, ch), lambda b, g: (b, 0, g)),
            _const_spec(f1.shape),
            _const_spec(tr.shape),
            _const_spec(ti.shape),
            _const_spec(km.shape),
        ],
        out_specs=pl.BlockSpec((None, seq, ch), lambda b, g: (b, 0, g)),
        out_shape=jax.ShapeDtypeStruct((batch, seq, FOURIER_W), BF16),
        compiler_params=_params(2),
        name="fourier",
    )(f3, f1, tr, ti, km)


def _merge_kernel(x_ref, a_ref, f_ref, mod_ref, gpre_ref, gpost_ref,
                  wg_ref, wpa_ref, wpf_ref, wout_ref, wup_ref, wdown_ref,
                  o_ref, wup_o, wdown_o, wg_b, wpa_b, wpf_b, wout_b):
    wup_o[...] = wup_ref[...].astype(BF16)
    wdown_o[...] = wdown_ref[...].astype(BF16)

    @pl.when(pl.program_id(0) == 0)
    def _():
        wg_b[...] = wg_ref[...].astype(BF16)
        wpa_b[...] = wpa_ref[...].astype(BF16)
        wpf_b[...] = wpf_ref[...].astype(BF16)
        wout_b[...] = wout_ref[...].astype(BF16)

    x = x_ref[...]
    h = _norm_modulate(x, gpre_ref[...], mod_ref[0:1, :], mod_ref[1:2, :]).astype(BF16)
    gates = jnp.dot(h, wg_b[...], preferred_element_type=F32)
    pa = jnp.dot(a_ref[...], wpa_b[...], preferred_element_type=F32)
    pf = jnp.dot(f_ref[...], wpf_b[...], preferred_element_type=F32)
    m = jax.nn.sigmoid(gates[:, :D_MODEL]) * pa + jax.nn.sigmoid(gates[:, D_MODEL:]) * pf
    y = jnp.dot(m.astype(BF16), wout_b[...], preferred_element_type=F32)
    o_ref[...] = x + mod_ref[2:3, :] * _post_norm(y, gpost_ref[...])


def _merge(x2, attn, four, mods, g_pre, g_post, w_in, w_pa, w_pf, w_out, w_up, w_down, seq):
    n = x2.shape[0]
    tm = TOKEN_TILE
    n_steps = n // tm
    tiles_per_seq = seq // tm
    row = lambda i: (i, 0)
    gate_col0 = Q_W + 2 * KV_W + FOURIER_W
    up_rows = w_up.shape[0] // n_steps
    down_rows = w_down.shape[0] // BF16_ROWS
    n_down = w_down.shape[0] // down_rows
    assert up_rows * n_steps == w_up.shape[0] and up_rows % BF16_ROWS == 0
    assert down_rows % BF16_ROWS == 0 and n_down <= n_steps
    up_spec = pl.BlockSpec((up_rows, w_up.shape[1]), row)
    down_spec = pl.BlockSpec((down_rows, w_down.shape[1]), lambda i: (jnp.minimum(i, n_down - 1), 0))
    return pl.pallas_call(
        _merge_kernel,
        grid=(n // tm,),
        in_specs=[
            pl.BlockSpec((tm, D_MODEL), row),
            pl.BlockSpec((tm, Q_W), row),
            pl.BlockSpec((tm, FOURIER_W), row),
            pl.BlockSpec((None, 6, D_MODEL), lambda i: (i // tiles_per_seq, 0, 0)),
            _const_spec((1, D_MODEL)),
            _const_spec((1, D_MODEL)),
            pl.BlockSpec((pl.Element(D_MODEL), pl.Element(2 * D_MODEL)),
                         lambda i: (0, gate_col0),
                         pipeline_mode=pl.Buffered(1)),
            _const_spec(w_pa.shape),
            _const_spec(w_pf.shape),
            _const_spec(w_out.shape),
            up_spec,
            down_spec,
        ],
        out_specs=[pl.BlockSpec((tm, D_MODEL), row), up_spec, down_spec],
        out_shape=[jax.ShapeDtypeStruct((n, D_MODEL), F32),
                   jax.ShapeDtypeStruct(w_up.shape, BF16),
                   jax.ShapeDtypeStruct(w_down.shape, BF16)],
        scratch_shapes=[pltpu.VMEM((D_MODEL, 2 * D_MODEL), BF16), pltpu.VMEM(w_pa.shape, BF16),
                        pltpu.VMEM(w_pf.shape, BF16), pltpu.VMEM(w_out.shape, BF16)],
        compiler_params=_params(1),
        name="merge",
    )(x2, attn, four, mods, g_pre, g_post, w_in, w_pa, w_pf, w_out, w_up, w_down)


def _convffn_kernel(x_ref, xp_ref, xn_ref, mod_ref, gpre_ref, gpost_ref,
                    wu_ref, wgate_ref, cw_ref, cb_ref, wd_ref, o_ref, act_ref, *, tiles_per_seq):
    i = pl.program_id(0)
    tm = x_ref.shape[0]
    halo = SUBLANES
    has_prev = (i % tiles_per_seq != 0).astype(F32)
    has_next = (i % tiles_per_seq != tiles_per_seq - 1).astype(F32)
    x = x_ref[...]
    shift = mod_ref[3:4, :]
    scale = mod_ref[4:5, :]
    gain = gpre_ref[...]
    h = _norm_modulate(x, gain, shift, scale)
    hp = _norm_modulate(xp_ref[...], gain, shift, scale) * has_prev
    hn = _norm_modulate(xn_ref[...], gain, shift, scale) * has_next
    h_ext = jnp.concatenate([hp, h, hn], axis=0).astype(BF16)
    h_mid = h.astype(BF16)
    n_ext = tm + 2 * halo
    for c in range(D_FF // FFN_CHUNK):
        cols = slice(c * FFN_CHUNK, (c + 1) * FFN_CHUNK)
        u = jnp.dot(h_ext, wu_ref[:, cols], preferred_element_type=F32)
        gate = jnp.dot(h_mid, wgate_ref[:, cols], preferred_element_type=F32)
        u_prev = pltpu.roll(u, 1, 0)[halo:halo + tm]
        u_next = pltpu.roll(u, n_ext - 1, 0)[halo:halo + tm]
        conv = (u_prev * cw_ref[0:1, cols] + u[halo:halo + tm] * cw_ref[1:2, cols]
                + u_next * cw_ref[2:3, cols] + cb_ref[:, cols])
        act_ref[:, cols] = (conv * jax.nn.sigmoid(conv) * gate).astype(BF16)
    y = jnp.dot(act_ref[...], wd_ref[...], preferred_element_type=F32)
    o_ref[...] = x + mod_ref[5:6, :] * _post_norm(y, gpost_ref[...])


def _convffn(x1, mods, g_pre, g_post, w_up, conv_w, conv_b, w_down, seq):
    n = x1.shape[0]
    tm = TOKEN_TILE
    tiles_per_seq = seq // tm
    halo_blocks_per_tile = tm // SUBLANES
    n_halo_blocks = n // SUBLANES
    return pl.pallas_call(
        functools.partial(_convffn_kernel, tiles_per_seq=tiles_per_seq),
        grid=(n // tm,),
        in_specs=[
            pl.BlockSpec((tm, D_MODEL), lambda i: (i, 0)),
            pl.BlockSpec((SUBLANES, D_MODEL),
                         lambda i: (jnp.maximum(i * halo_blocks_per_tile - 1, 0), 0)),
            pl.BlockSpec((SUBLANES, D_MODEL),
                         lambda i: (jnp.minimum((i + 1) * halo_blocks_per_tile, n_halo_blocks - 1), 0)),
            pl.BlockSpec((None, 6, D_MODEL), lambda i: (i // tiles_per_seq, 0, 0)),
            _const_spec((1, D_MODEL)),
            _const_spec((1, D_MODEL)),
            pl.BlockSpec((D_MODEL, D_FF), lambda i: (0, 0), pipeline_mode=pl.Buffered(1)),
            pl.BlockSpec((D_MODEL, D_FF), lambda i: (0, 1), pipeline_mode=pl.Buffered(1)),
            _const_spec(conv_w.shape),
            _const_spec((1, D_FF)),
            _const_spec(w_down.shape),
        ],
        out_specs=pl.BlockSpec((tm, D_MODEL), lambda i: (i, 0)),
        out_shape=jax.ShapeDtypeStruct((n, D_MODEL), F32),
        scratch_shapes=[pltpu.VMEM((tm, D_FF), BF16)],
        compiler_params=_params(1),
        name="convffn",
    )(x1, x1, x1, mods, g_pre, g_post, w_up, w_up, conv_w, conv_b.reshape(1, D_FF), w_down)


def _rope_tables(seq):
    half = HEAD_DIM // 2
    inv_freq = ROPE_THETA ** (-jnp.arange(0, half, 2, dtype=F32) / half)
    ang_r = jnp.arange(seq // GRID_W).astype(F32)[:, None] * inv_freq
    ang_c = jnp.arange(GRID_W).astype(F32)[:, None] * inv_freq
    reps = LANES // HEAD_DIM

    def lanes(row_part, col_part):
        return jnp.tile(jnp.concatenate([row_part, col_part], axis=-1), (1, reps))

    zr = jnp.zeros((ang_r.shape[0], half), F32)
    zc = jnp.zeros((ang_c.shape[0], half), F32)
    rcos = lanes(jnp.concatenate([jnp.cos(ang_r), jnp.cos(ang_r)], axis=-1), zr)
    rsin = lanes(jnp.concatenate([-jnp.sin(ang_r), jnp.sin(ang_r)], axis=-1), zr)
    ccos = lanes(zc, jnp.concatenate([jnp.cos(ang_c), jnp.cos(ang_c)], axis=-1))
    csin = lanes(zc, jnp.concatenate([-jnp.sin(ang_c), jnp.sin(ang_c)], axis=-1))
    return rcos, rsin, ccos, csin


def kernel(x, c, ctx, c_ctx, w_mod, b_mod, g_pre1, g_post1, g_pre2, g_post2,
           w_in, sink, w_pa, w_pf, w_out, w_up, conv_w, conv_b, w_down):
    batch, seq, d = x.shape
    ctx_len = ctx.shape[1]
    depth = w_mod.shape[0]
    assert depth == 1 and d == D_MODEL and batch + 1 <= SUBLANES
    assert seq % TOKEN_TILE == 0 and TOKEN_TILE % GRID_W == 0 and seq % (DFT_MINOR * SUBLANES) == 0
    n = batch * seq
    rope_tabs = _rope_tables(seq)
    tables = _dft_tables(seq)
    bias = _band_bias()

    l = 0
    cvecs = jnp.zeros((SUBLANES, d), F32).at[:batch].set(c).at[batch].set(c_ctx)
    mod_all = _adaln(cvecs, w_mod[l], b_mod[l])
    mods = mod_all[:batch].reshape(batch, 6, d)
    mods_ctx = mod_all[batch].reshape(6, d)

    g_pre1_row = g_pre1[l].reshape(1, d)
    g_post1_row = g_post1[l].reshape(1, d)
    g_pre2_row = g_pre2[l].reshape(1, d)
    g_post2_row = g_post2[l].reshape(1, d)

    x2 = x.reshape(n, d)
    q, k, v_t, f = _inproj(x2, mods, g_pre1_row, w_in[l], rope_tabs, seq)
    kx, vx_t = _ctxkv(ctx.reshape(batch * ctx_len, d), mods_ctx, g_pre1_row, w_in[l], ctx_len)
    attn = _attention(sink[l], q, k, v_t, kx, vx_t, bias, batch, seq, ctx_len)
    four = _fourier(f.reshape(batch, seq, FOURIER_W), tables).reshape(n, FOURIER_W)
    x1, w_up_b, w_down_b = _merge(x2, attn, four, mods, g_pre1_row, g_post1_row, w_in[l], w_pa[l], w_pf[l],
                                  w_out[l], w_up[l], w_down[l], seq)
    out = _convffn(x1, mods, g_pre2_row, g_post2_row, w_up_b, conv_w[l], conv_b[l], w_down_b, seq)
    return out.reshape(batch, seq, d)
```

```python
import functools
import math

import numpy as np
import jax
import jax.numpy as jnp
from jax import lax
from jax.experimental import pallas as pl
from jax.experimental.pallas import tpu as pltpu

F32 = jnp.float32
BF16 = jnp.bfloat16

D_MODEL = 1024
GRID_W = 64
HEAD_DIM = 64
N_Q_HEADS = 8
N_KV_HEADS = 2
GROUP = N_Q_HEADS // N_KV_HEADS
WINDOW = 128
BLOCK = 128
ROPE_THETA = 10000.0
N_FOURIER_GROUPS = 4
FOURIER_GROUP_CH = 128
FOURIER_W = N_FOURIER_GROUPS * FOURIER_GROUP_CH
Q_W = N_Q_HEADS * HEAD_DIM
KV_W = N_KV_HEADS * HEAD_DIM
D_FF = 2816
EPS = 1e-6
NEG = -1e30
LOG2E = math.log2(math.e)

LANES = 128
SUBLANES = 8
BF16_ROWS = 16
MXU_DIM = 256
VMEM_LIMIT_BYTES = 56 * 1024 * 1024

TOKEN_TILE = 512
FFN_CHUNK = MXU_DIM
DFT_MINOR = SUBLANES
ATTN_BLOCKS_PER_STEP = 8


def _const_spec(shape):
    nd = len(shape)
    return pl.BlockSpec(shape, lambda *_: (0,) * nd, pipeline_mode=pl.Buffered(1))


def _params(n_axes):
    return pltpu.CompilerParams(
        dimension_semantics=("arbitrary",) * n_axes,
        vmem_limit_bytes=VMEM_LIMIT_BYTES,
    )


def _norm_modulate(x, gain, shift, scale):
    ms = jnp.mean(x * x, axis=-1, keepdims=True)
    return (x * lax.rsqrt(ms + EPS)) * (gain * (1.0 + scale)) + shift


def _post_norm(y, gain):
    ms = jnp.mean(y * y, axis=-1, keepdims=True)
    return (y * lax.rsqrt(ms + EPS)) * gain


def _adaln_kernel(c_ref, w_ref, b_ref, o_ref):
    c = c_ref[...]
    s = c * jax.nn.sigmoid(c)
    o_ref[...] = jnp.dot(s, w_ref[...], preferred_element_type=F32) + b_ref[...]


def _adaln(cvecs, w_mod, b_mod):
    n_out = w_mod.shape[1]
    tn = 1536
    return pl.pallas_call(
        _adaln_kernel,
        grid=(n_out // tn,),
        in_specs=[
            pl.BlockSpec((SUBLANES, D_MODEL), lambda j: (0, 0)),
            pl.BlockSpec((D_MODEL, tn), lambda j: (0, j)),
            pl.BlockSpec((1, tn), lambda j: (0, j)),
        ],
        out_specs=pl.BlockSpec((SUBLANES, tn), lambda j: (0, j)),
        out_shape=jax.ShapeDtypeStruct((SUBLANES, n_out), F32),
        compiler_params=_params(1),
        name="adaln",
    )(cvecs, w_mod, b_mod.reshape(1, n_out))


def _rope(xb, cos_t, sin_t, first_half):
    sw = jnp.where(first_half, pltpu.roll(xb, LANES - 16, 1), pltpu.roll(xb, 16, 1))
    return xb * cos_t + sw * sin_t


def _token_table(row_tab, col_tab):
    n_rows = row_tab.shape[0]
    by_row = jnp.concatenate(
        [jnp.broadcast_to(row_tab[r:r + 1, :], (GRID_W, LANES)) for r in range(n_rows)], axis=0)
    return by_row + jnp.concatenate([col_tab] * n_rows, axis=0)


def _inproj_kernel(x_ref, mod_ref, g_ref, w_ref, rcos_ref, rsin_ref, ccos_ref, csin_ref,
                   q_ref, k_ref, v_ref, f_ref, wb_ref):
    @pl.when(pl.program_id(0) == 0)
    def _():
        low = lax.broadcasted_iota(jnp.int32, (D_MODEL, LANES), 1) < HEAD_DIM
        for j in range(Q_W // LANES):
            src_a = (j // 2) * LANES
            src_b = (GROUP // 2 + j // 2) * LANES
            a = w_ref[:, src_a:src_a + LANES]
            b = w_ref[:, src_b:src_b + LANES]
            if j % 2 == 1:
                a = pltpu.roll(a, HEAD_DIM, 1)
            else:
                b = pltpu.roll(b, HEAD_DIM, 1)
            wb_ref[:, j * LANES:(j + 1) * LANES] = jnp.where(low, a, b).astype(BF16)
        wb_ref[:, Q_W:] = w_ref[:, Q_W:].astype(BF16)

    h = _norm_modulate(x_ref[...], g_ref[...], mod_ref[0:1, :], mod_ref[1:2, :])
    p = jnp.dot(h.astype(BF16), wb_ref[...], preferred_element_type=F32)
    cos_t = _token_table(rcos_ref[...], ccos_ref[...])
    sin_t = _token_table(rsin_ref[...], csin_ref[...])
    lane = lax.broadcasted_iota(jnp.int32, cos_t.shape, 1)
    first_half = (lane % 32) < 16
    scale = HEAD_DIM ** -0.5 * LOG2E
    for b in range(Q_W // LANES):
        blk = p[:, b * LANES:(b + 1) * LANES]
        q_ref[:, b * LANES:(b + 1) * LANES] = (_rope(blk, cos_t, sin_t, first_half) * scale).astype(BF16)
    k_ref[...] = _rope(p[:, Q_W:Q_W + KV_W], cos_t, sin_t, first_half).astype(BF16)
    v_ref[...] = p[:, Q_W + KV_W:Q_W + 2 * KV_W].T.astype(BF16)
    f_ref[...] = p[:, Q_W + 2 * KV_W:]


def _inproj(x2, mods, g_pre, w_in, rope_tabs, seq):
    n = x2.shape[0]
    tm = TOKEN_TILE
    tiles_per_seq = seq // tm
    rows_per_tile = tm // GRID_W
    n_cols = Q_W + 2 * KV_W + FOURIER_W
    row_spec = pl.BlockSpec((rows_per_tile, LANES), lambda i: (i % tiles_per_seq, 0))
    return pl.pallas_call(
        _inproj_kernel,
        grid=(n // tm,),
        in_specs=[
            pl.BlockSpec((tm, D_MODEL), lambda i: (i, 0)),
            pl.BlockSpec((None, 6, D_MODEL), lambda i: (i // tiles_per_seq, 0, 0)),
            _const_spec((1, D_MODEL)),
            _const_spec((D_MODEL, n_cols)),
            row_spec, row_spec,
            _const_spec((GRID_W, LANES)), _const_spec((GRID_W, LANES)),
        ],
        out_specs=[
            pl.BlockSpec((tm, Q_W), lambda i: (i, 0)),
            pl.BlockSpec((tm, KV_W), lambda i: (i, 0)),
            pl.BlockSpec((KV_W, tm), lambda i: (0, i)),
            pl.BlockSpec((tm, FOURIER_W), lambda i: (i, 0)),
        ],
        out_shape=[
            jax.ShapeDtypeStruct((n, Q_W), BF16),
            jax.ShapeDtypeStruct((n, KV_W), BF16),
            jax.ShapeDtypeStruct((KV_W, n), BF16),
            jax.ShapeDtypeStruct((n, FOURIER_W), F32),
        ],
        scratch_shapes=[pltpu.VMEM((D_MODEL, n_cols), BF16)],
        compiler_params=_params(1),
        name="inproj",
    )(x2, mods, g_pre, w_in, *rope_tabs)


def _ctxkv_kernel(x_ref, mod_ref, g_ref, w_ref, k_ref, v_ref):
    h = _norm_modulate(x_ref[...], g_ref[...], mod_ref[0:1, :], mod_ref[1:2, :])
    p = jnp.dot(h.astype(BF16), w_ref[...].astype(BF16), preferred_element_type=F32)
    k_ref[...] = p[:, :KV_W].astype(BF16)
    v_ref[...] = p[:, KV_W:].T.astype(BF16)


def _ctxkv(ctx2, mods_ctx, g_pre, w_in, ctx_len):
    n = ctx2.shape[0]
    assert Q_W % (2 * KV_W) == 0
    return pl.pallas_call(
        _ctxkv_kernel,
        grid=(n // ctx_len,),
        in_specs=[
            pl.BlockSpec((ctx_len, D_MODEL), lambda i: (i, 0)),
            _const_spec((6, D_MODEL)),
            _const_spec((1, D_MODEL)),
            pl.BlockSpec((D_MODEL, 2 * KV_W), lambda i: (0, Q_W // (2 * KV_W)),
                         pipeline_mode=pl.Buffered(1)),
        ],
        out_specs=[
            pl.BlockSpec((ctx_len, KV_W), lambda i: (i, 0)),
            pl.BlockSpec((KV_W, ctx_len), lambda i: (0, i)),
        ],
        out_shape=[
            jax.ShapeDtypeStruct((n, KV_W), BF16),
            jax.ShapeDtypeStruct((KV_W, n), BF16),
        ],
        compiler_params=_params(1),
        name="ctxkv",
    )(ctx2, mods_ctx, g_pre, w_in)


def _attend_block(sink_ref, q_blk, k_parts, v_parts, bias_prev, bias_next):
    kcat = jnp.concatenate(k_parts, axis=0)
    vcat_t = jnp.concatenate(v_parts, axis=1)
    n_keys = kcat.shape[0]
    low = lax.broadcasted_iota(jnp.int32, (BLOCK, LANES), 1) < HEAD_DIM
    top = lax.broadcasted_iota(jnp.int32, (KV_W, n_keys), 0) < HEAD_DIM
    zero = jnp.zeros((), BF16)
    one = jnp.ones((), BF16)
    q_rows = [jnp.where(low if h == 0 else jnp.logical_not(low), q_blk[:, g * LANES:(g + 1) * LANES], zero)
              for h in range(N_KV_HEADS) for g in range(GROUP)]
    s_all = lax.dot_general(kcat, jnp.concatenate(q_rows, axis=0), (((1,), (1,)), ((), ())),
                            preferred_element_type=F32)
    outs = []
    for h in range(N_KV_HEADS):
        v_h = jnp.where(top if h == 0 else jnp.logical_not(top), vcat_t, one)
        for g0 in range(0, GROUP, 2):
            probs = []
            sinks = []
            for g in (g0, g0 + 1):
                head = h * GROUP + g
                s = s_all[:, head * LANES:(head + 1) * LANES]
                parts = [s[0:BLOCK] + bias_prev,
                         s[BLOCK:2 * BLOCK],
                         s[2 * BLOCK:3 * BLOCK] + bias_next]
                parts += [s[r:r + BLOCK] for r in range(3 * BLOCK, n_keys, BLOCK)]
                mx = parts[0]
                for part in parts[1:]:
                    mx = jnp.maximum(mx, part)
                sink = sink_ref[head] * LOG2E
                m = jnp.maximum(jnp.max(mx, axis=0, keepdims=True), sink)
                probs.append(jnp.concatenate([jnp.exp2(part - m) for part in parts], axis=0).astype(BF16))
                sinks.append(jnp.exp2(sink - m))
            o2 = jnp.dot(v_h, jnp.concatenate(probs, axis=1), preferred_element_type=F32)
            for idx in range(2):
                o_t = o2[:, idx * LANES:(idx + 1) * LANES]
                num = o_t[0:HEAD_DIM] if h == 0 else o_t[HEAD_DIM:]
                den = (o_t[HEAD_DIM:] if h == 0 else o_t[0:HEAD_DIM]) + sinks[idx]
                outs.append(num / den)
    return outs


def _attn_kernel(sink_ref, q_ref, kp_ref, kc_ref, kn_ref, kx_ref, vp_ref, vc_ref, vn_ref, vx_ref,
                 bias_ref, o_ref):
    i = pl.program_id(1)
    n_sub = q_ref.shape[0] // BLOCK
    first_var = jnp.where(i == 0, 0, 1)
    last_var = jnp.where(i == pl.num_programs(1) - 1, 2, 1)
    kx = kx_ref[...]
    vx = vx_ref[...]
    for t in range(n_sub):
        own = slice(t * BLOCK, (t + 1) * BLOCK)
        before = slice((t - 1) * BLOCK, t * BLOCK)
        after = slice((t + 1) * BLOCK, (t + 2) * BLOCK)
        k_parts = [kp_ref[...] if t == 0 else kc_ref[before, :], kc_ref[own, :],
                   kn_ref[...] if t == n_sub - 1 else kc_ref[after, :], kx]
        v_parts = [vp_ref[...] if t == 0 else vc_ref[:, before], vc_ref[:, own],
                   vn_ref[...] if t == n_sub - 1 else vc_ref[:, after], vx]
        bias_prev = bias_ref[first_var if t == 0 else 1, 0:BLOCK, :]
        bias_next = bias_ref[last_var if t == n_sub - 1 else 1, BLOCK:2 * BLOCK, :]
        outs = _attend_block(sink_ref, q_ref[own, :], k_parts, v_parts, bias_prev, bias_next)
        for j in range(Q_W // LANES):
            o_t = jnp.concatenate([outs[2 * j], outs[2 * j + 1]], axis=0)
            o_ref[own, j * LANES:(j + 1) * LANES] = o_t.T.astype(BF16)


def _attention(sink, q, k, v_t, kx, vx_t, bias, batch, seq, ctx_len):
    nb = seq // BLOCK
    sub = ATTN_BLOCKS_PER_STEP
    steps = nb // sub
    n = batch * seq

    def cur(b, i):
        return (b * steps + i, 0)

    def prev(b, i):
        return (b * nb + jnp.maximum(i * sub - 1, 0), 0)

    def nxt(b, i):
        return (b * nb + jnp.minimum((i + 1) * sub, nb - 1), 0)

    def swap(f):
        return lambda b, i: f(b, i)[::-1]

    return pl.pallas_call(
        _attn_kernel,
        grid=(batch, steps),
        in_specs=[
            pl.BlockSpec(memory_space=pltpu.SMEM),
            pl.BlockSpec((sub * BLOCK, Q_W), cur),
            pl.BlockSpec((BLOCK, KV_W), prev),
            pl.BlockSpec((sub * BLOCK, KV_W), cur),
            pl.BlockSpec((BLOCK, KV_W), nxt),
            pl.BlockSpec((ctx_len, KV_W), lambda b, i: (b, 0)),
            pl.BlockSpec((KV_W, BLOCK), swap(prev)),
            pl.BlockSpec((KV_W, sub * BLOCK), swap(cur)),
            pl.BlockSpec((KV_W, BLOCK), swap(nxt)),
            pl.BlockSpec((KV_W, ctx_len), lambda b, i: (0, b)),
            _const_spec(bias.shape),
        ],
        out_specs=pl.BlockSpec((sub * BLOCK, Q_W), cur),
        out_shape=jax.ShapeDtypeStruct((n, Q_W), BF16),
        compiler_params=_params(2),
        name="attention",
    )(sink, q, k, k, k, kx, v_t, v_t, v_t, vx_t, bias)


def _band_bias():
    qi = np.arange(BLOCK)[None, :]
    kj = np.arange(3 * BLOCK)[:, None]
    rel = kj - BLOCK - qi
    in_window = np.abs(rel) <= WINDOW
    variants = []
    for has_prev, has_next in ((False, True), (True, True), (True, False)):
        ok = in_window.copy()
        if not has_prev:
            ok &= kj >= BLOCK
        if not has_next:
            ok &= kj < 2 * BLOCK
        mask = np.where(ok, 0.0, NEG).astype(np.float32)
        variants.append(np.concatenate([mask[:BLOCK], mask[2 * BLOCK:]], axis=0))
    return jnp.asarray(np.stack(variants))


def _fourier_kernel(f_ref, f1_ref, tr_ref, ti_ref, km_ref, o_ref):
    n_major = f_ref.shape[0] // DFT_MINOR
    y2 = jnp.concatenate(
        [f_ref[pl.ds(s2, n_major, stride=DFT_MINOR), :].astype(BF16) for s2 in range(DFT_MINOR)],
        axis=1)
    a = jnp.dot(f1_ref[...], y2, preferred_element_type=F32)
    ar = a[:n_major]
    ai = a[n_major:]
    tr = tr_ref[...]
    ti = ti_ref[...]
    ap = jnp.concatenate([(ar * tr - ai * ti).astype(BF16),
                          (ar * ti + ai * tr).astype(BF16)], axis=1)
    r = jnp.dot(ap, km_ref[...], preferred_element_type=F32)
    for k2 in range(DFT_MINOR):
        o_ref[pl.ds(k2 * n_major, n_major), :] = r[:, k2 * LANES:(k2 + 1) * LANES].astype(BF16)


def _cis(num, den):
    ang = (num % den).astype(F32) * F32(-2.0 * math.pi / den)
    return jnp.cos(ang), jnp.sin(ang)


def _dft_tables(seq):
    n_major = seq // DFT_MINOR
    ch = FOURIER_GROUP_CH
    root = int(round(math.sqrt(n_major)))
    assert root * root == n_major
    s1 = jnp.arange(n_major, dtype=jnp.int32)[None, :]
    hi = jnp.arange(root, dtype=jnp.int32)[:, None] * root
    lo = jnp.arange(root, dtype=jnp.int32)[:, None]
    hr, hi_ = _cis(hi * s1, n_major)
    lr, li = _cis(lo * s1, n_major)
    f1r = (hr[:, None, :] * lr[None, :, :] - hi_[:, None, :] * li[None, :, :]).reshape(n_major, n_major)
    f1i = (hr[:, None, :] * li[None, :, :] + hi_[:, None, :] * lr[None, :, :]).reshape(n_major, n_major)
    f1 = jnp.concatenate([f1r, f1i], axis=0).astype(BF16)
    k1 = jnp.arange(n_major, dtype=jnp.int32)[:, None]
    s2 = jnp.arange(DFT_MINOR, dtype=jnp.int32)[None, :]
    tr, ti = _cis(k1 * s2, seq)
    tr = jnp.broadcast_to(tr[:, :, None], (n_major, DFT_MINOR, ch)).reshape(n_major, DFT_MINOR * ch)
    ti = jnp.broadcast_to(ti[:, :, None], (n_major, DFT_MINOR, ch)).reshape(n_major, DFT_MINOR * ch)
    a8 = jnp.arange(DFT_MINOR, dtype=jnp.int32)
    mr, mi = _cis(a8[:, None] * a8[None, :], DFT_MINOR)
    c128 = jnp.arange(ch, dtype=jnp.int32)
    cr, ci = _cis(c128[:, None] * c128[None, :], ch)
    norm = F32(1.0 / math.sqrt(seq * ch))
    kr = (mr[:, None, :, None] * cr[None, :, None, :] - mi[:, None, :, None] * ci[None, :, None, :])
    ki = (mr[:, None, :, None] * ci[None, :, None, :] + mi[:, None, :, None] * cr[None, :, None, :])
    kr = (kr * norm).reshape(DFT_MINOR * ch, DFT_MINOR * ch)
    ki = (ki * norm).reshape(DFT_MINOR * ch, DFT_MINOR * ch)
    km = jnp.concatenate([kr, -ki], axis=0).astype(BF16)
    return f1, tr, ti, km


def _fourier(f3, tables):
    batch, seq, _ = f3.shape
    f1, tr, ti, km = tables
    ch = FOURIER_GROUP_CH
    return pl.pallas_call(
        _fourier_kernel,
        grid=(batch, N_FOURIER_GROUPS),
        in_specs=[
            pl.BlockSpec((None, seq, ch), lambda b, g: (b, 0, g)),
            _const_spec(f1.shape),
            _const_spec(tr.shape),
            _const_spec(ti.shape),
            _const_spec(km.shape),
        ],
        out_specs=pl.BlockSpec((None, seq, ch), lambda b, g: (b, 0, g)),
        out_shape=jax.ShapeDtypeStruct((batch, seq, FOURIER_W), BF16),
        compiler_params=_params(2),
        name="fourier",
    )(f3, f1, tr, ti, km)


def _merge_kernel(x_ref, a_ref, f_ref, mod_ref, gpre_ref, gpost_ref,
                  wg_ref, wpa_ref, wpf_ref, wout_ref, wup_ref, wdown_ref,
                  o_ref, wup_o, wdown_o, wg_b, wpa_b, wpf_b, wout_b):
    wup_o[...] = wup_ref[...].astype(BF16)
    wdown_o[...] = wdown_ref[...].astype(BF16)

    @pl.when(pl.program_id(0) == 0)
    def _():
        wg_b[...] = wg_ref[...].astype(BF16)
        wpa_b[...] = wpa_ref[...].astype(BF16)
        wpf_b[...] = wpf_ref[...].astype(BF16)
        wout_b[...] = wout_ref[...].astype(BF16)

    x = x_ref[...]
    h = _norm_modulate(x, gpre_ref[...], mod_ref[0:1, :], mod_ref[1:2, :]).astype(BF16)
    gates = jnp.dot(h, wg_b[...], preferred_element_type=F32)
    pa = jnp.dot(a_ref[...], wpa_b[...], preferred_element_type=F32)
    pf = jnp.dot(f_ref[...], wpf_b[...], preferred_element_type=F32)
    m = jax.nn.sigmoid(gates[:, :D_MODEL]) * pa + jax.nn.sigmoid(gates[:, D_MODEL:]) * pf
    y = jnp.dot(m.astype(BF16), wout_b[...], preferred_element_type=F32)
    o_ref[...] = x + mod_ref[2:3, :] * _post_norm(y, gpost_ref[...])


def _merge(x2, attn, four, mods, g_pre, g_post, w_in, w_pa, w_pf, w_out, w_up, w_down, seq):
    n = x2.shape[0]
    tm = TOKEN_TILE
    n_steps = n // tm
    tiles_per_seq = seq // tm
    row = lambda i: (i, 0)
    gate_col0 = Q_W + 2 * KV_W + FOURIER_W
    up_rows = w_up.shape[0] // n_steps
    down_rows = w_down.shape[0] // BF16_ROWS
    n_down = w_down.shape[0] // down_rows
    assert up_rows * n_steps == w_up.shape[0] and up_rows % BF16_ROWS == 0
    assert down_rows % BF16_ROWS == 0 and n_down <= n_steps
    up_spec = pl.BlockSpec((up_rows, w_up.shape[1]), row)
    down_spec = pl.BlockSpec((down_rows, w_down.shape[1]), lambda i: (jnp.minimum(i, n_down - 1), 0))
    return pl.pallas_call(
        _merge_kernel,
        grid=(n // tm,),
        in_specs=[
            pl.BlockSpec((tm, D_MODEL), row),
            pl.BlockSpec((tm, Q_W), row),
            pl.BlockSpec((tm, FOURIER_W), row),
            pl.BlockSpec((None, 6, D_MODEL), lambda i: (i // tiles_per_seq, 0, 0)),
            _const_spec((1, D_MODEL)),
            _const_spec((1, D_MODEL)),
            pl.BlockSpec((pl.Element(D_MODEL), pl.Element(2 * D_MODEL)),
                         lambda i: (0, gate_col0),
                         pipeline_mode=pl.Buffered(1)),
            _const_spec(w_pa.shape),
            _const_spec(w_pf.shape),
            _const_spec(w_out.shape),
            up_spec,
            down_spec,
        ],
        out_specs=[pl.BlockSpec((tm, D_MODEL), row), up_spec, down_spec],
        out_shape=[jax.ShapeDtypeStruct((n, D_MODEL), F32),
                   jax.ShapeDtypeStruct(w_up.shape, BF16),
                   jax.ShapeDtypeStruct(w_down.shape, BF16)],
        scratch_shapes=[pltpu.VMEM((D_MODEL, 2 * D_MODEL), BF16), pltpu.VMEM(w_pa.shape, BF16),
                        pltpu.VMEM(w_pf.shape, BF16), pltpu.VMEM(w_out.shape, BF16)],
        compiler_params=_params(1),
        name="merge",
    )(x2, attn, four, mods, g_pre, g_post, w_in, w_pa, w_pf, w_out, w_up, w_down)


def _convffn_kernel(x_ref, xp_ref, xn_ref, mod_ref, gpre_ref, gpost_ref,
                    wu_ref, wgate_ref, cw_ref, cb_ref, wd_ref, o_ref, act_ref, *, tiles_per_seq):
    i = pl.program_id(0)
    tm = x_ref.shape[0]
    halo = SUBLANES
    has_prev = (i % tiles_per_seq != 0).astype(F32)
    has_next = (i % tiles_per_seq != tiles_per_seq - 1).astype(F32)
    x = x_ref[...]
    shift = mod_ref[3:4, :]
    scale = mod_ref[4:5, :]
    gain = gpre_ref[...]
    h = _norm_modulate(x, gain, shift, scale)
    hp = _norm_modulate(xp_ref[...], gain, shift, scale) * has_prev
    hn = _norm_modulate(xn_ref[...], gain, shift, scale) * has_next
    h_ext = jnp.concatenate([hp, h, hn], axis=0).astype(BF16)
    h_mid = h.astype(BF16)
    n_ext = tm + 2 * halo
    for c in range(D_FF // FFN_CHUNK):
        cols = slice(c * FFN_CHUNK, (c + 1) * FFN_CHUNK)
        u = jnp.dot(h_ext, wu_ref[:, cols], preferred_element_type=F32)
        gate = jnp.dot(h_mid, wgate_ref[:, cols], preferred_element_type=F32)
        u_prev = pltpu.roll(u, 1, 0)[halo:halo + tm]
        u_next = pltpu.roll(u, n_ext - 1, 0)[halo:halo + tm]
        conv = (u_prev * cw_ref[0:1, cols] + u[halo:halo + tm] * cw_ref[1:2, cols]
                + u_next * cw_ref[2:3, cols] + cb_ref[:, cols])
        act_ref[:, cols] = (conv * jax.nn.sigmoid(conv) * gate).astype(BF16)
    y = jnp.dot(act_ref[...], wd_ref[...], preferred_element_type=F32)
    o_ref[...] = x + mod_ref[5:6, :] * _post_norm(y, gpost_ref[...])


def _convffn(x1, mods, g_pre, g_post, w_up, conv_w, conv_b, w_down, seq):
    n = x1.shape[0]
    tm = TOKEN_TILE
    tiles_per_seq = seq // tm
    halo_blocks_per_tile = tm // SUBLANES
    n_halo_blocks = n // SUBLANES
    return pl.pallas_call(
        functools.partial(_convffn_kernel, tiles_per_seq=tiles_per_seq),
        grid=(n // tm,),
        in_specs=[
            pl.BlockSpec((tm, D_MODEL), lambda i: (i, 0)),
            pl.BlockSpec((SUBLANES, D_MODEL),
                         lambda i: (jnp.maximum(i * halo_blocks_per_tile - 1, 0), 0)),
            pl.BlockSpec((SUBLANES, D_MODEL),
                         lambda i: (jnp.minimum((i + 1) * halo_blocks_per_tile, n_halo_blocks - 1), 0)),
            pl.BlockSpec((None, 6, D_MODEL), lambda i: (i // tiles_per_seq, 0, 0)),
            _const_spec((1, D_MODEL)),
            _const_spec((1, D_MODEL)),
            pl.BlockSpec((D_MODEL, D_FF), lambda i: (0, 0), pipeline_mode=pl.Buffered(1)),
            pl.BlockSpec((D_MODEL, D_FF), lambda i: (0, 1), pipeline_mode=pl.Buffered(1)),
            _const_spec(conv_w.shape),
            _const_spec((1, D_FF)),
            _const_spec(w_down.shape),
        ],
        out_specs=pl.BlockSpec((tm, D_MODEL), lambda i: (i, 0)),
        out_shape=jax.ShapeDtypeStruct((n, D_MODEL), F32),
        scratch_shapes=[pltpu.VMEM((tm, D_FF), BF16)],
        compiler_params=_params(1),
        name="convffn",
    )(x1, x1, x1, mods, g_pre, g_post, w_up, w_up, conv_w, conv_b.reshape(1, D_FF), w_down)


def _rope_tables(seq):
    half = HEAD_DIM // 2
    inv_freq = ROPE_THETA ** (-jnp.arange(0, half, 2, dtype=F32) / half)
    ang_r = jnp.arange(seq // GRID_W).astype(F32)[:, None] * inv_freq
    ang_c = jnp.arange(GRID_W).astype(F32)[:, None] * inv_freq
    reps = LANES // HEAD_DIM

    def lanes(row_part, col_part):
        return jnp.tile(jnp.concatenate([row_part, col_part], axis=-1), (1, reps))

    zr = jnp.zeros((ang_r.shape[0], half), F32)
    zc = jnp.zeros((ang_c.shape[0], half), F32)
    rcos = lanes(jnp.concatenate([jnp.cos(ang_r), jnp.cos(ang_r)], axis=-1), zr)
    rsin = lanes(jnp.concatenate([-jnp.sin(ang_r), jnp.sin(ang_r)], axis=-1), zr)
    ccos = lanes(zc, jnp.concatenate([jnp.cos(ang_c), jnp.cos(ang_c)], axis=-1))
    csin = lanes(zc, jnp.concatenate([-jnp.sin(ang_c), jnp.sin(ang_c)], axis=-1))
    return rcos, rsin, ccos, csin


def kernel(x, c, ctx, c_ctx, w_mod, b_mod, g_pre1, g_post1, g_pre2, g_post2,
           w_in, sink, w_pa, w_pf, w_out, w_up, conv_w, conv_b, w_down):
    batch, seq, d = x.shape
    ctx_len = ctx.shape[1]
    depth = w_mod.shape[0]
    assert depth == 1 and d == D_MODEL and batch + 1 <= SUBLANES
    assert seq % TOKEN_TILE == 0 and TOKEN_TILE % GRID_W == 0 and seq % (DFT_MINOR * SUBLANES) == 0
    assert seq % (ATTN_BLOCKS_PER_STEP * BLOCK) == 0
    n = batch * seq
    rope_tabs = _rope_tables(seq)
    tables = _dft_tables(seq)
    bias = _band_bias()

    l = 0
    cvecs = jnp.zeros((SUBLANES, d), F32).at[:batch].set(c).at[batch].set(c_ctx)
    mod_all = _adaln(cvecs, w_mod[l], b_mod[l])
    mods = mod_all[:batch].reshape(batch, 6, d)
    mods_ctx = mod_all[batch].reshape(6, d)

    g_pre1_row = g_pre1[l].reshape(1, d)
    g_post1_row = g_post1[l].reshape(1, d)
    g_pre2_row = g_pre2[l].reshape(1, d)
    g_post2_row = g_post2[l].reshape(1, d)

    x2 = x.reshape(n, d)
    q, k, v_t, f = _inproj(x2, mods, g_pre1_row, w_in[l], rope_tabs, seq)
    kx, vx_t = _ctxkv(ctx.reshape(batch * ctx_len, d), mods_ctx, g_pre1_row, w_in[l], ctx_len)
    attn = _attention(sink[l], q, k, v_t, kx, vx_t, bias, batch, seq, ctx_len)
    four = _fourier(f.reshape(batch, seq, FOURIER_W), tables).reshape(n, FOURIER_W)
    x1, w_up_b, w_down_b = _merge(x2, attn, four, mods, g_pre1_row, g_post1_row, w_in[l], w_pa[l], w_pf[l],
                                  w_out[l], w_up[l], w_down[l], seq)
    out = _convffn(x1, mods, g_pre2_row, g_post2_row, w_up_b, conv_w[l], conv_b[l], w_down_b, seq)
    return out.reshape(batch, seq, d)
```

```python
import functools
import math

import numpy as np
import jax
import jax.numpy as jnp
from jax import lax
from jax.experimental import pallas as pl
from jax.experimental.pallas import tpu as pltpu

F32 = jnp.float32
BF16 = jnp.bfloat16

D_MODEL = 1024
GRID_W = 64
HEAD_DIM = 64
N_Q_HEADS = 8
N_KV_HEADS = 2
GROUP = N_Q_HEADS // N_KV_HEADS
WINDOW = 128
BLOCK = 128
ROPE_THETA = 10000.0
N_FOURIER_GROUPS = 4
FOURIER_GROUP_CH = 128
FOURIER_W = N_FOURIER_GROUPS * FOURIER_GROUP_CH
Q_W = N_Q_HEADS * HEAD_DIM
KV_W = N_KV_HEADS * HEAD_DIM
D_FF = 2816
EPS = 1e-6
NEG = -1e30
LOG2E = math.log2(math.e)

LANES = 128
SUBLANES = 8
BF16_ROWS = 16
MXU_DIM = 256
VMEM_LIMIT_BYTES = 56 * 1024 * 1024

TOKEN_TILE = 512
FFN_CHUNK = MXU_DIM
DFT_MINOR = SUBLANES
ATTN_BLOCKS_PER_STEP = 16


def _const_spec(shape):
    nd = len(shape)
    return pl.BlockSpec(shape, lambda *_: (0,) * nd, pipeline_mode=pl.Buffered(1))


def _params(n_axes):
    return pltpu.CompilerParams(
        dimension_semantics=("arbitrary",) * n_axes,
        vmem_limit_bytes=VMEM_LIMIT_BYTES,
    )


def _norm_modulate(x, gain, shift, scale):
    ms = jnp.mean(x * x, axis=-1, keepdims=True)
    return (x * lax.rsqrt(ms + EPS)) * (gain * (1.0 + scale)) + shift


def _post_norm(y, gain):
    ms = jnp.mean(y * y, axis=-1, keepdims=True)
    return (y * lax.rsqrt(ms + EPS)) * gain


def _adaln_kernel(c_ref, w_ref, b_ref, o_ref):
    c = c_ref[...]
    s = c * jax.nn.sigmoid(c)
    o_ref[...] = jnp.dot(s, w_ref[...], preferred_element_type=F32) + b_ref[...]


def _adaln(cvecs, w_mod, b_mod):
    n_out = w_mod.shape[1]
    tn = 1536
    return pl.pallas_call(
        _adaln_kernel,
        grid=(n_out // tn,),
        in_specs=[
            pl.BlockSpec((SUBLANES, D_MODEL), lambda j: (0, 0)),
            pl.BlockSpec((D_MODEL, tn), lambda j: (0, j)),
            pl.BlockSpec((1, tn), lambda j: (0, j)),
        ],
        out_specs=pl.BlockSpec((SUBLANES, tn), lambda j: (0, j)),
        out_shape=jax.ShapeDtypeStruct((SUBLANES, n_out), F32),
        compiler_params=_params(1),
        name="adaln",
    )(cvecs, w_mod, b_mod.reshape(1, n_out))


def _rope(xb, cos_t, sin_t, first_half):
    sw = jnp.where(first_half, pltpu.roll(xb, LANES - 16, 1), pltpu.roll(xb, 16, 1))
    return xb * cos_t + sw * sin_t


def _token_table(row_tab, col_tab):
    n_rows = row_tab.shape[0]
    by_row = jnp.concatenate(
        [jnp.broadcast_to(row_tab[r:r + 1, :], (GRID_W, LANES)) for r in range(n_rows)], axis=0)
    return by_row + jnp.concatenate([col_tab] * n_rows, axis=0)


def _inproj_kernel(x_ref, mod_ref, g_ref, w_ref, rcos_ref, rsin_ref, ccos_ref, csin_ref,
                   q_ref, k_ref, v_ref, f_ref, wb_ref):
    @pl.when(pl.program_id(0) == 0)
    def _():
        low = lax.broadcasted_iota(jnp.int32, (D_MODEL, LANES), 1) < HEAD_DIM
        for j in range(Q_W // LANES):
            src_a = (j // 2) * LANES
            src_b = (GROUP // 2 + j // 2) * LANES
            a = w_ref[:, src_a:src_a + LANES]
            b = w_ref[:, src_b:src_b + LANES]
            if j % 2 == 1:
                a = pltpu.roll(a, HEAD_DIM, 1)
            else:
                b = pltpu.roll(b, HEAD_DIM, 1)
            wb_ref[:, j * LANES:(j + 1) * LANES] = jnp.where(low, a, b).astype(BF16)
        wb_ref[:, Q_W:] = w_ref[:, Q_W:].astype(BF16)

    h = _norm_modulate(x_ref[...], g_ref[...], mod_ref[0:1, :], mod_ref[1:2, :])
    p = jnp.dot(h.astype(BF16), wb_ref[...], preferred_element_type=F32)
    cos_t = _token_table(rcos_ref[...], ccos_ref[...])
    sin_t = _token_table(rsin_ref[...], csin_ref[...])
    lane = lax.broadcasted_iota(jnp.int32, cos_t.shape, 1)
    first_half = (lane % 32) < 16
    scale = HEAD_DIM ** -0.5 * LOG2E
    for b in range(Q_W // LANES):
        blk = p[:, b * LANES:(b + 1) * LANES]
        q_ref[:, b * LANES:(b + 1) * LANES] = (_rope(blk, cos_t, sin_t, first_half) * scale).astype(BF16)
    k_ref[...] = _rope(p[:, Q_W:Q_W + KV_W], cos_t, sin_t, first_half).astype(BF16)
    v_ref[...] = p[:, Q_W + KV_W:Q_W + 2 * KV_W].T.astype(BF16)
    f_ref[...] = p[:, Q_W + 2 * KV_W:]


def _inproj(x2, mods, g_pre, w_in, rope_tabs, seq):
    n = x2.shape[0]
    tm = TOKEN_TILE
    tiles_per_seq = seq // tm
    rows_per_tile = tm // GRID_W
    n_cols = Q_W + 2 * KV_W + FOURIER_W
    row_spec = pl.BlockSpec((rows_per_tile, LANES), lambda i: (i % tiles_per_seq, 0))
    return pl.pallas_call(
        _inproj_kernel,
        grid=(n // tm,),
        in_specs=[
            pl.BlockSpec((tm, D_MODEL), lambda i: (i, 0)),
            pl.BlockSpec((None, 6, D_MODEL), lambda i: (i // tiles_per_seq, 0, 0)),
            _const_spec((1, D_MODEL)),
            _const_spec((D_MODEL, n_cols)),
            row_spec, row_spec,
            _const_spec((GRID_W, LANES)), _const_spec((GRID_W, LANES)),
        ],
        out_specs=[
            pl.BlockSpec((tm, Q_W), lambda i: (i, 0)),
            pl.BlockSpec((tm, KV_W), lambda i: (i, 0)),
            pl.BlockSpec((KV_W, tm), lambda i: (0, i)),
            pl.BlockSpec((tm, FOURIER_W), lambda i: (i, 0)),
        ],
        out_shape=[
            jax.ShapeDtypeStruct((n, Q_W), BF16),
            jax.ShapeDtypeStruct((n, KV_W), BF16),
            jax.ShapeDtypeStruct((KV_W, n), BF16),
            jax.ShapeDtypeStruct((n, FOURIER_W), F32),
        ],
        scratch_shapes=[pltpu.VMEM((D_MODEL, n_cols), BF16)],
        compiler_params=_params(1),
        name="inproj",
    )(x2, mods, g_pre, w_in, *rope_tabs)


def _ctxkv_kernel(x_ref, mod_ref, g_ref, w_ref, k_ref, v_ref):
    h = _norm_modulate(x_ref[...], g_ref[...], mod_ref[0:1, :], mod_ref[1:2, :])
    p = jnp.dot(h.astype(BF16), w_ref[...].astype(BF16), preferred_element_type=F32)
    k_ref[...] = p[:, :KV_W].astype(BF16)
    v_ref[...] = p[:, KV_W:].T.astype(BF16)


def _ctxkv(ctx2, mods_ctx, g_pre, w_in, ctx_len):
    n = ctx2.shape[0]
    assert Q_W % (2 * KV_W) == 0
    return pl.pallas_call(
        _ctxkv_kernel,
        grid=(n // ctx_len,),
        in_specs=[
            pl.BlockSpec((ctx_len, D_MODEL), lambda i: (i, 0)),
            _const_spec((6, D_MODEL)),
            _const_spec((1, D_MODEL)),
            pl.BlockSpec((D_MODEL, 2 * KV_W), lambda i: (0, Q_W // (2 * KV_W)),
                         pipeline_mode=pl.Buffered(1)),
        ],
        out_specs=[
            pl.BlockSpec((ctx_len, KV_W), lambda i: (i, 0)),
            pl.BlockSpec((KV_W, ctx_len), lambda i: (0, i)),
        ],
        out_shape=[
            jax.ShapeDtypeStruct((n, KV_W), BF16),
            jax.ShapeDtypeStruct((KV_W, n), BF16),
        ],
        compiler_params=_params(1),
        name="ctxkv",
    )(ctx2, mods_ctx, g_pre, w_in)


def _attend_block(sink_ref, q_blk, k_parts, v_parts, bias_prev, bias_next):
    kcat = jnp.concatenate(k_parts, axis=0)
    vcat_t = jnp.concatenate(v_parts, axis=1)
    n_keys = kcat.shape[0]
    low = lax.broadcasted_iota(jnp.int32, (BLOCK, LANES), 1) < HEAD_DIM
    top = lax.broadcasted_iota(jnp.int32, (KV_W, n_keys), 0) < HEAD_DIM
    zero = jnp.zeros((), BF16)
    one = jnp.ones((), BF16)
    q_rows = [jnp.where(low if h == 0 else jnp.logical_not(low), q_blk[:, g * LANES:(g + 1) * LANES], zero)
              for h in range(N_KV_HEADS) for g in range(GROUP)]
    s_all = lax.dot_general(kcat, jnp.concatenate(q_rows, axis=0), (((1,), (1,)), ((), ())),
                            preferred_element_type=F32)
    outs = []
    for h in range(N_KV_HEADS):
        v_h = jnp.where(top if h == 0 else jnp.logical_not(top), vcat_t, one)
        for g0 in range(0, GROUP, 2):
            probs = []
            sinks = []
            for g in (g0, g0 + 1):
                head = h * GROUP + g
                s = s_all[:, head * LANES:(head + 1) * LANES]
                parts = [s[0:BLOCK] + bias_prev,
                         s[BLOCK:2 * BLOCK],
                         s[2 * BLOCK:3 * BLOCK] + bias_next]
                parts += [s[r:r + BLOCK] for r in range(3 * BLOCK, n_keys, BLOCK)]
                mx = parts[0]
                for part in parts[1:]:
                    mx = jnp.maximum(mx, part)
                sink = sink_ref[head] * LOG2E
                m = jnp.maximum(jnp.max(mx, axis=0, keepdims=True), sink)
                probs.append(jnp.concatenate([jnp.exp2(part - m) for part in parts], axis=0).astype(BF16))
                sinks.append(jnp.exp2(sink - m))
            o2 = jnp.dot(v_h, jnp.concatenate(probs, axis=1), preferred_element_type=F32)
            for idx in range(2):
                o_t = o2[:, idx * LANES:(idx + 1) * LANES]
                num = o_t[0:HEAD_DIM] if h == 0 else o_t[HEAD_DIM:]
                den = (o_t[HEAD_DIM:] if h == 0 else o_t[0:HEAD_DIM]) + sinks[idx]
                outs.append(num / den)
    return outs


def _attention_steps(sink_ref, q_ref, kp_ref, kc_ref, kn_ref, kx_ref, vp_ref, vc_ref, vn_ref, vx_ref,
                     bias_ref, o_ref):
    i = pl.program_id(1)
    n_sub = q_ref.shape[0] // BLOCK
    first_var = jnp.where(i == 0, 0, 1)
    last_var = jnp.where(i == pl.num_programs(1) - 1, 2, 1)

    def block_step(t):
        own = slice(t * BLOCK, (t + 1) * BLOCK)
        before = slice((t - 1) * BLOCK, t * BLOCK)
        after = slice((t + 1) * BLOCK, (t + 2) * BLOCK)
        k_parts = [kp_ref[...] if t == 0 else kc_ref[before, :], kc_ref[own, :],
                   kn_ref[...] if t == n_sub - 1 else kc_ref[after, :], kx_ref[...]]
        v_parts = [vp_ref[...] if t == 0 else vc_ref[:, before], vc_ref[:, own],
                   vn_ref[...] if t == n_sub - 1 else vc_ref[:, after], vx_ref[...]]
        bias_prev = bias_ref[first_var if t == 0 else 1, 0:BLOCK, :]
        bias_next = bias_ref[last_var if t == n_sub - 1 else 1, BLOCK:2 * BLOCK, :]
        outs = _attend_block(sink_ref, q_ref[own, :], k_parts, v_parts, bias_prev, bias_next)
        for j in range(Q_W // LANES):
            o_t = jnp.concatenate([outs[2 * j], outs[2 * j + 1]], axis=0)
            o_ref[own, j * LANES:(j + 1) * LANES] = o_t.T.astype(BF16)

    return [functools.partial(block_step, t) for t in range(n_sub)]


def _attention_specs(batch, seq, ctx_len, bias_shape):
    nb = seq // BLOCK
    sub = ATTN_BLOCKS_PER_STEP
    steps = nb // sub

    def cur(b, i):
        return (b * steps + i, 0)

    def prev(b, i):
        return (b * nb + jnp.maximum(i * sub - 1, 0), 0)

    def nxt(b, i):
        return (b * nb + jnp.minimum((i + 1) * sub, nb - 1), 0)

    def swap(f):
        return lambda b, i: f(b, i)[::-1]

    in_specs = [
        pl.BlockSpec(memory_space=pltpu.SMEM),
        pl.BlockSpec((sub * BLOCK, Q_W), cur),
        pl.BlockSpec((BLOCK, KV_W), prev),
        pl.BlockSpec((sub * BLOCK, KV_W), cur),
        pl.BlockSpec((BLOCK, KV_W), nxt),
        pl.BlockSpec((ctx_len, KV_W), lambda b, i: (b, 0)),
        pl.BlockSpec((KV_W, BLOCK), swap(prev)),
        pl.BlockSpec((KV_W, sub * BLOCK), swap(cur)),
        pl.BlockSpec((KV_W, BLOCK), swap(nxt)),
        pl.BlockSpec((KV_W, ctx_len), lambda b, i: (0, b)),
        _const_spec(bias_shape),
    ]
    return steps, in_specs, pl.BlockSpec((sub * BLOCK, Q_W), cur)


def _band_bias():
    qi = np.arange(BLOCK)[None, :]
    kj = np.arange(3 * BLOCK)[:, None]
    rel = kj - BLOCK - qi
    in_window = np.abs(rel) <= WINDOW
    variants = []
    for has_prev, has_next in ((False, True), (True, True), (True, False)):
        ok = in_window.copy()
        if not has_prev:
            ok &= kj >= BLOCK
        if not has_next:
            ok &= kj < 2 * BLOCK
        mask = np.where(ok, 0.0, NEG).astype(np.float32)
        variants.append(np.concatenate([mask[:BLOCK], mask[2 * BLOCK:]], axis=0))
    return jnp.asarray(np.stack(variants))


def _fourier_steps(f_ref, f1_ref, tr_ref, ti_ref, km_ref, o_ref):
    n_major = f_ref.shape[0] // DFT_MINOR
    pair_w = 2 * LANES
    n_pairs = DFT_MINOR // 2
    row_chunk = n_major // n_pairs
    twiddled = [None] * n_pairs

    def stage1(c):
        y = jnp.concatenate(
            [f_ref[pl.ds(s2, n_major, stride=DFT_MINOR), :].astype(BF16) for s2 in (2 * c, 2 * c + 1)],
            axis=1)
        a = jnp.dot(f1_ref[...], y, preferred_element_type=F32)
        ar = a[:n_major]
        ai = a[n_major:]
        tr = tr_ref[:, c * pair_w:(c + 1) * pair_w]
        ti = ti_ref[:, c * pair_w:(c + 1) * pair_w]
        twiddled[c] = ((ar * tr - ai * ti).astype(BF16), (ar * ti + ai * tr).astype(BF16))

    def stage2(r):
        rows = slice(r * row_chunk, (r + 1) * row_chunk)
        ap = jnp.concatenate([t[0][rows] for t in twiddled] + [t[1][rows] for t in twiddled], axis=1)
        res = jnp.dot(ap, km_ref[...], preferred_element_type=F32)
        for k2 in range(DFT_MINOR):
            o_ref[pl.ds(k2 * n_major + r * row_chunk, row_chunk), :] = (
                res[:, k2 * LANES:(k2 + 1) * LANES].astype(BF16))

    return ([functools.partial(stage1, c) for c in range(n_pairs)]
            + [functools.partial(stage2, r) for r in range(n_pairs)])


def _cis(num, den):
    ang = (num % den).astype(F32) * F32(-2.0 * math.pi / den)
    return jnp.cos(ang), jnp.sin(ang)


def _dft_tables(seq):
    n_major = seq // DFT_MINOR
    ch = FOURIER_GROUP_CH
    root = int(round(math.sqrt(n_major)))
    assert root * root == n_major
    s1 = jnp.arange(n_major, dtype=jnp.int32)[None, :]
    hi = jnp.arange(root, dtype=jnp.int32)[:, None] * root
    lo = jnp.arange(root, dtype=jnp.int32)[:, None]
    hr, hi_ = _cis(hi * s1, n_major)
    lr, li = _cis(lo * s1, n_major)
    f1r = (hr[:, None, :] * lr[None, :, :] - hi_[:, None, :] * li[None, :, :]).reshape(n_major, n_major)
    f1i = (hr[:, None, :] * li[None, :, :] + hi_[:, None, :] * lr[None, :, :]).reshape(n_major, n_major)
    f1 = jnp.concatenate([f1r, f1i], axis=0).astype(BF16)
    k1 = jnp.arange(n_major, dtype=jnp.int32)[:, None]
    s2 = jnp.arange(DFT_MINOR, dtype=jnp.int32)[None, :]
    tr, ti = _cis(k1 * s2, seq)
    tr = jnp.broadcast_to(tr[:, :, None], (n_major, DFT_MINOR, ch)).reshape(n_major, DFT_MINOR * ch)
    ti = jnp.broadcast_to(ti[:, :, None], (n_major, DFT_MINOR, ch)).reshape(n_major, DFT_MINOR * ch)
    a8 = jnp.arange(DFT_MINOR, dtype=jnp.int32)
    mr, mi = _cis(a8[:, None] * a8[None, :], DFT_MINOR)
    c128 = jnp.arange(ch, dtype=jnp.int32)
    cr, ci = _cis(c128[:, None] * c128[None, :], ch)
    norm = F32(1.0 / math.sqrt(seq * ch))
    kr = (mr[:, None, :, None] * cr[None, :, None, :] - mi[:, None, :, None] * ci[None, :, None, :])
    ki = (mr[:, None, :, None] * ci[None, :, None, :] + mi[:, None, :, None] * cr[None, :, None, :])
    kr = (kr * norm).reshape(DFT_MINOR * ch, DFT_MINOR * ch)
    ki = (ki * norm).reshape(DFT_MINOR * ch, DFT_MINOR * ch)
    km = jnp.concatenate([kr, -ki], axis=0).astype(BF16)
    return f1, tr, ti, km


def _mix_kernel(sink_ref, q_ref, kp_ref, kc_ref, kn_ref, kx_ref, vp_ref, vc_ref, vn_ref, vx_ref, bias_ref,
                f_ref, f1_ref, tr_ref, ti_ref, km_ref, attn_ref, four_ref):
    attn_steps = _attention_steps(sink_ref, q_ref, kp_ref, kc_ref, kn_ref, kx_ref,
                                  vp_ref, vc_ref, vn_ref, vx_ref, bias_ref, attn_ref)
    four_steps = _fourier_steps(f_ref, f1_ref, tr_ref, ti_ref, km_ref, four_ref)
    per = len(attn_steps) // len(four_steps)
    assert per * len(four_steps) == len(attn_steps)
    for n, four_step in enumerate(four_steps):
        for attn_step in attn_steps[n * per:(n + 1) * per]:
            attn_step()
        four_step()


def _mix(sink, q, k, v_t, kx, vx_t, bias, f3, tables, ctx_len):
    batch, seq, _ = f3.shape
    f1, tr, ti, km = tables
    ch = FOURIER_GROUP_CH
    steps, attn_in, attn_out = _attention_specs(batch, seq, ctx_len, bias.shape)
    assert steps == N_FOURIER_GROUPS
    four_spec = pl.BlockSpec((None, seq, ch), lambda b, g: (b, 0, g))
    return pl.pallas_call(
        _mix_kernel,
        grid=(batch, steps),
        in_specs=attn_in + [four_spec, _const_spec(f1.shape), _const_spec(tr.shape),
                            _const_spec(ti.shape), _const_spec(km.shape)],
        out_specs=[attn_out, four_spec],
        out_shape=[jax.ShapeDtypeStruct((batch * seq, Q_W), BF16),
                   jax.ShapeDtypeStruct((batch, seq, FOURIER_W), BF16)],
        compiler_params=_params(2),
        name="mix",
    )(sink, q, k, k, k, kx, v_t, v_t, v_t, vx_t, bias, f3, f1, tr, ti, km)


def _merge_kernel(x_ref, a_ref, f_ref, mod_ref, gpre_ref, gpost_ref,
                  wg_ref, wpa_ref, wpf_ref, wout_ref, wup_ref, wdown_ref,
                  o_ref, wup_o, wdown_o, wg_b, wpa_b, wpf_b, wout_b):
    wup_o[...] = wup_ref[...].astype(BF16)
    wdown_o[...] = wdown_ref[...].astype(BF16)

    @pl.when(pl.program_id(0) == 0)
    def _():
        wg_b[...] = wg_ref[...].astype(BF16)
        wpa_b[...] = wpa_ref[...].astype(BF16)
        wpf_b[...] = wpf_ref[...].astype(BF16)
        wout_b[...] = wout_ref[...].astype(BF16)

    x = x_ref[...]
    h = _norm_modulate(x, gpre_ref[...], mod_ref[0:1, :], mod_ref[1:2, :]).astype(BF16)
    gates = jnp.dot(h, wg_b[...], preferred_element_type=F32)
    pa = jnp.dot(a_ref[...], wpa_b[...], preferred_element_type=F32)
    pf = jnp.dot(f_ref[...], wpf_b[...], preferred_element_type=F32)
    m = jax.nn.sigmoid(gates[:, :D_MODEL]) * pa + jax.nn.sigmoid(gates[:, D_MODEL:]) * pf
    y = jnp.dot(m.astype(BF16), wout_b[...], preferred_element_type=F32)
    o_ref[...] = x + mod_ref[2:3, :] * _post_norm(y, gpost_ref[...])


def _merge(x2, attn, four, mods, g_pre, g_post, w_in, w_pa, w_pf, w_out, w_up, w_down, seq):
    n = x2.shape[0]
    tm = TOKEN_TILE
    n_steps = n // tm
    tiles_per_seq = seq // tm
    row = lambda i: (i, 0)
    gate_col0 = Q_W + 2 * KV_W + FOURIER_W
    up_rows = w_up.shape[0] // n_steps
    down_rows = w_down.shape[0] // BF16_ROWS
    n_down = w_down.shape[0] // down_rows
    assert up_rows * n_steps == w_up.shape[0] and up_rows % BF16_ROWS == 0
    assert down_rows % BF16_ROWS == 0 and n_down <= n_steps
    up_spec = pl.BlockSpec((up_rows, w_up.shape[1]), row)
    down_spec = pl.BlockSpec((down_rows, w_down.shape[1]), lambda i: (jnp.minimum(i, n_down - 1), 0))
    return pl.pallas_call(
        _merge_kernel,
        grid=(n // tm,),
        in_specs=[
            pl.BlockSpec((tm, D_MODEL), row),
            pl.BlockSpec((tm, Q_W), row),
            pl.BlockSpec((tm, FOURIER_W), row),
            pl.BlockSpec((None, 6, D_MODEL), lambda i: (i // tiles_per_seq, 0, 0)),
            _const_spec((1, D_MODEL)),
            _const_spec((1, D_MODEL)),
            pl.BlockSpec((pl.Element(D_MODEL), pl.Element(2 * D_MODEL)),
                         lambda i: (0, gate_col0),
                         pipeline_mode=pl.Buffered(1)),
            _const_spec(w_pa.shape),
            _const_spec(w_pf.shape),
            _const_spec(w_out.shape),
            up_spec,
            down_spec,
        ],
        out_specs=[pl.BlockSpec((tm, D_MODEL), row), up_spec, down_spec],
        out_shape=[jax.ShapeDtypeStruct((n, D_MODEL), F32),
                   jax.ShapeDtypeStruct(w_up.shape, BF16),
                   jax.ShapeDtypeStruct(w_down.shape, BF16)],
        scratch_shapes=[pltpu.VMEM((D_MODEL, 2 * D_MODEL), BF16), pltpu.VMEM(w_pa.shape, BF16),
                        pltpu.VMEM(w_pf.shape, BF16), pltpu.VMEM(w_out.shape, BF16)],
        compiler_params=_params(1),
        name="merge",
    )(x2, attn, four, mods, g_pre, g_post, w_in, w_pa, w_pf, w_out, w_up, w_down)


def _convffn_kernel(x_ref, xp_ref, xn_ref, mod_ref, gpre_ref, gpost_ref,
                    wu_ref, wgate_ref, cw_ref, cb_ref, wd_ref, o_ref, act_ref, *, tiles_per_seq):
    i = pl.program_id(0)
    tm = x_ref.shape[0]
    halo = SUBLANES
    has_prev = (i % tiles_per_seq != 0).astype(F32)
    has_next = (i % tiles_per_seq != tiles_per_seq - 1).astype(F32)
    x = x_ref[...]
    shift = mod_ref[3:4, :]
    scale = mod_ref[4:5, :]
    gain = gpre_ref[...]
    h = _norm_modulate(x, gain, shift, scale)
    hp = _norm_modulate(xp_ref[...], gain, shift, scale) * has_prev
    hn = _norm_modulate(xn_ref[...], gain, shift, scale) * has_next
    h_ext = jnp.concatenate([hp, h, hn], axis=0).astype(BF16)
    h_mid = h.astype(BF16)
    n_ext = tm + 2 * halo
    for c in range(D_FF // FFN_CHUNK):
        cols = slice(c * FFN_CHUNK, (c + 1) * FFN_CHUNK)
        u = jnp.dot(h_ext, wu_ref[:, cols], preferred_element_type=F32)
        gate = jnp.dot(h_mid, wgate_ref[:, cols], preferred_element_type=F32)
        u_prev = pltpu.roll(u, 1, 0)[halo:halo + tm]
        u_next = pltpu.roll(u, n_ext - 1, 0)[halo:halo + tm]
        conv = (u_prev * cw_ref[0:1, cols] + u[halo:halo + tm] * cw_ref[1:2, cols]
                + u_next * cw_ref[2:3, cols] + cb_ref[:, cols])
        act_ref[:, cols] = (conv * jax.nn.sigmoid(conv) * gate).astype(BF16)
    y = jnp.dot(act_ref[...], wd_ref[...], preferred_element_type=F32)
    o_ref[...] = x + mod_ref[5:6, :] * _post_norm(y, gpost_ref[...])


def _convffn(x1, mods, g_pre, g_post, w_up, conv_w, conv_b, w_down, seq):
    n = x1.shape[0]
    tm = TOKEN_TILE
    tiles_per_seq = seq // tm
    halo_blocks_per_tile = tm // SUBLANES
    n_halo_blocks = n // SUBLANES
    return pl.pallas_call(
        functools.partial(_convffn_kernel, tiles_per_seq=tiles_per_seq),
        grid=(n // tm,),
        in_specs=[
            pl.BlockSpec((tm, D_MODEL), lambda i: (i, 0)),
            pl.BlockSpec((SUBLANES, D_MODEL),
                         lambda i: (jnp.maximum(i * halo_blocks_per_tile - 1, 0), 0)),
            pl.BlockSpec((SUBLANES, D_MODEL),
                         lambda i: (jnp.minimum((i + 1) * halo_blocks_per_tile, n_halo_blocks - 1), 0)),
            pl.BlockSpec((None, 6, D_MODEL), lambda i: (i // tiles_per_seq, 0, 0)),
            _const_spec((1, D_MODEL)),
            _const_spec((1, D_MODEL)),
            pl.BlockSpec((D_MODEL, D_FF), lambda i: (0, 0), pipeline_mode=pl.Buffered(1)),
            pl.BlockSpec((D_MODEL, D_FF), lambda i: (0, 1), pipeline_mode=pl.Buffered(1)),
            _const_spec(conv_w.shape),
            _const_spec((1, D_FF)),
            _const_spec(w_down.shape),
        ],
        out_specs=pl.BlockSpec((tm, D_MODEL), lambda i: (i, 0)),
        out_shape=jax.ShapeDtypeStruct((n, D_MODEL), F32),
        scratch_shapes=[pltpu.VMEM((tm, D_FF), BF16)],
        compiler_params=_params(1),
        name="convffn",
    )(x1, x1, x1, mods, g_pre, g_post, w_up, w_up, conv_w, conv_b.reshape(1, D_FF), w_down)


def _rope_tables(seq):
    half = HEAD_DIM // 2
    inv_freq = ROPE_THETA ** (-jnp.arange(0, half, 2, dtype=F32) / half)
    ang_r = jnp.arange(seq // GRID_W).astype(F32)[:, None] * inv_freq
    ang_c = jnp.arange(GRID_W).astype(F32)[:, None] * inv_freq
    reps = LANES // HEAD_DIM

    def lanes(row_part, col_part):
        return jnp.tile(jnp.concatenate([row_part, col_part], axis=-1), (1, reps))

    zr = jnp.zeros((ang_r.shape[0], half), F32)
    zc = jnp.zeros((ang_c.shape[0], half), F32)
    rcos = lanes(jnp.concatenate([jnp.cos(ang_r), jnp.cos(ang_r)], axis=-1), zr)
    rsin = lanes(jnp.concatenate([-jnp.sin(ang_r), jnp.sin(ang_r)], axis=-1), zr)
    ccos = lanes(zc, jnp.concatenate([jnp.cos(ang_c), jnp.cos(ang_c)], axis=-1))
    csin = lanes(zc, jnp.concatenate([-jnp.sin(ang_c), jnp.sin(ang_c)], axis=-1))
    return rcos, rsin, ccos, csin


def kernel(x, c, ctx, c_ctx, w_mod, b_mod, g_pre1, g_post1, g_pre2, g_post2,
           w_in, sink, w_pa, w_pf, w_out, w_up, conv_w, conv_b, w_down):
    batch, seq, d = x.shape
    ctx_len = ctx.shape[1]
    depth = w_mod.shape[0]
    assert depth == 1 and d == D_MODEL and batch + 1 <= SUBLANES
    assert seq % TOKEN_TILE == 0 and TOKEN_TILE % GRID_W == 0 and seq % (DFT_MINOR * SUBLANES) == 0
    assert seq % (ATTN_BLOCKS_PER_STEP * BLOCK) == 0
    n = batch * seq
    rope_tabs = _rope_tables(seq)
    tables = _dft_tables(seq)
    bias = _band_bias()

    l = 0
    cvecs = jnp.zeros((SUBLANES, d), F32).at[:batch].set(c).at[batch].set(c_ctx)
    mod_all = _adaln(cvecs, w_mod[l], b_mod[l])
    mods = mod_all[:batch].reshape(batch, 6, d)
    mods_ctx = mod_all[batch].reshape(6, d)

    g_pre1_row = g_pre1[l].reshape(1, d)
    g_post1_row = g_post1[l].reshape(1, d)
    g_pre2_row = g_pre2[l].reshape(1, d)
    g_post2_row = g_post2[l].reshape(1, d)

    x2 = x.reshape(n, d)
    q, k, v_t, f = _inproj(x2, mods, g_pre1_row, w_in[l], rope_tabs, seq)
    kx, vx_t = _ctxkv(ctx.reshape(batch * ctx_len, d), mods_ctx, g_pre1_row, w_in[l], ctx_len)
    attn, four = _mix(sink[l], q, k, v_t, kx, vx_t, bias, f.reshape(batch, seq, FOURIER_W), tables, ctx_len)
    four = four.reshape(n, FOURIER_W)
    x1, w_up_b, w_down_b = _merge(x2, attn, four, mods, g_pre1_row, g_post1_row, w_in[l], w_pa[l], w_pf[l],
                                  w_out[l], w_up[l], w_down[l], seq)
    out = _convffn(x1, mods, g_pre2_row, g_post2_row, w_up_b, conv_w[l], conv_b[l], w_down_b, seq)
    return out.reshape(batch, seq, d)
```

```python
import functools
import math

import numpy as np
import jax
import jax.numpy as jnp
from jax import lax
from jax.experimental import pallas as pl
from jax.experimental.pallas import tpu as pltpu

F32 = jnp.float32
BF16 = jnp.bfloat16

D_MODEL = 1024
GRID_W = 64
HEAD_DIM = 64
N_Q_HEADS = 8
N_KV_HEADS = 2
GROUP = N_Q_HEADS // N_KV_HEADS
WINDOW = 128
BLOCK = 128
ROPE_THETA = 10000.0
N_FOURIER_GROUPS = 4
FOURIER_GROUP_CH = 128
FOURIER_W = N_FOURIER_GROUPS * FOURIER_GROUP_CH
Q_W = N_Q_HEADS * HEAD_DIM
KV_W = N_KV_HEADS * HEAD_DIM
D_FF = 2816
EPS = 1e-6
NEG = -1e30
LOG2E = math.log2(math.e)

LANES = 128
SUBLANES = 8
BF16_ROWS = 16
MXU_DIM = 256
VMEM_LIMIT_BYTES = 56 * 1024 * 1024

TOKEN_TILE = 512
FFN_CHUNK = MXU_DIM
DFT_MINOR = SUBLANES
ATTN_BLOCKS_PER_STEP = 16


def _const_spec(shape):
    nd = len(shape)
    return pl.BlockSpec(shape, lambda *_: (0,) * nd, pipeline_mode=pl.Buffered(1))


def _params(n_axes):
    return pltpu.CompilerParams(
        dimension_semantics=("arbitrary",) * n_axes,
        vmem_limit_bytes=VMEM_LIMIT_BYTES,
    )


def _norm_modulate(x, gain, shift, scale):
    ms = jnp.mean(x * x, axis=-1, keepdims=True)
    return (x * lax.rsqrt(ms + EPS)) * (gain * (1.0 + scale)) + shift


def _post_norm(y, gain):
    ms = jnp.mean(y * y, axis=-1, keepdims=True)
    return (y * lax.rsqrt(ms + EPS)) * gain


def _adaln_kernel(c_ref, w_ref, b_ref, o_ref):
    c = c_ref[...]
    s = c * jax.nn.sigmoid(c)
    o_ref[...] = jnp.dot(s, w_ref[...], preferred_element_type=F32) + b_ref[...]


def _adaln(cvecs, w_mod, b_mod):
    n_out = w_mod.shape[1]
    tn = 1536
    return pl.pallas_call(
        _adaln_kernel,
        grid=(n_out // tn,),
        in_specs=[
            pl.BlockSpec((SUBLANES, D_MODEL), lambda j: (0, 0)),
            pl.BlockSpec((D_MODEL, tn), lambda j: (0, j)),
            pl.BlockSpec((1, tn), lambda j: (0, j)),
        ],
        out_specs=pl.BlockSpec((SUBLANES, tn), lambda j: (0, j)),
        out_shape=jax.ShapeDtypeStruct((SUBLANES, n_out), F32),
        compiler_params=_params(1),
        name="adaln",
    )(cvecs, w_mod, b_mod.reshape(1, n_out))


def _rope(xb, cos_t, sin_t, first_half):
    sw = jnp.where(first_half, pltpu.roll(xb, LANES - 16, 1), pltpu.roll(xb, 16, 1))
    return xb * cos_t + sw * sin_t


def _token_table(row_tab, col_tab):
    n_rows = row_tab.shape[0]
    by_row = jnp.concatenate(
        [jnp.broadcast_to(row_tab[r:r + 1, :], (GRID_W, LANES)) for r in range(n_rows)], axis=0)
    return by_row + jnp.concatenate([col_tab] * n_rows, axis=0)


def _inproj_kernel(x_ref, mod_ref, g_ref, w_ref, rcos_ref, rsin_ref, ccos_ref, csin_ref,
                   q_ref, k_ref, v_ref, f_ref, wb_ref):
    @pl.when(pl.program_id(0) == 0)
    def _():
        low = lax.broadcasted_iota(jnp.int32, (D_MODEL, LANES), 1) < HEAD_DIM
        for j in range(Q_W // LANES):
            src_a = (j // 2) * LANES
            src_b = (GROUP // 2 + j // 2) * LANES
            a = w_ref[:, src_a:src_a + LANES]
            b = w_ref[:, src_b:src_b + LANES]
            if j % 2 == 1:
                a = pltpu.roll(a, HEAD_DIM, 1)
            else:
                b = pltpu.roll(b, HEAD_DIM, 1)
            wb_ref[:, j * LANES:(j + 1) * LANES] = jnp.where(low, a, b).astype(BF16)
        wb_ref[:, Q_W:] = w_ref[:, Q_W:].astype(BF16)

    h = _norm_modulate(x_ref[...], g_ref[...], mod_ref[0:1, :], mod_ref[1:2, :])
    p = jnp.dot(h.astype(BF16), wb_ref[...], preferred_element_type=F32)
    cos_t = _token_table(rcos_ref[...], ccos_ref[...])
    sin_t = _token_table(rsin_ref[...], csin_ref[...])
    lane = lax.broadcasted_iota(jnp.int32, cos_t.shape, 1)
    first_half = (lane % 32) < 16
    scale = HEAD_DIM ** -0.5 * LOG2E
    for b in range(Q_W // LANES):
        blk = p[:, b * LANES:(b + 1) * LANES]
        q_ref[:, b * LANES:(b + 1) * LANES] = (_rope(blk, cos_t, sin_t, first_half) * scale).astype(BF16)
    k_ref[...] = _rope(p[:, Q_W:Q_W + KV_W], cos_t, sin_t, first_half).astype(BF16)
    v_ref[...] = p[:, Q_W + KV_W:Q_W + 2 * KV_W].T.astype(BF16)
    f_ref[...] = p[:, Q_W + 2 * KV_W:]


def _inproj(x2, mods, g_pre, w_in, rope_tabs, seq):
    n = x2.shape[0]
    tm = TOKEN_TILE
    tiles_per_seq = seq // tm
    rows_per_tile = tm // GRID_W
    n_cols = Q_W + 2 * KV_W + FOURIER_W
    row_spec = pl.BlockSpec((rows_per_tile, LANES), lambda i: (i % tiles_per_seq, 0))
    return pl.pallas_call(
        _inproj_kernel,
        grid=(n // tm,),
        in_specs=[
            pl.BlockSpec((tm, D_MODEL), lambda i: (i, 0)),
            pl.BlockSpec((None, 6, D_MODEL), lambda i: (i // tiles_per_seq, 0, 0)),
            _const_spec((1, D_MODEL)),
            _const_spec((D_MODEL, n_cols)),
            row_spec, row_spec,
            _const_spec((GRID_W, LANES)), _const_spec((GRID_W, LANES)),
        ],
        out_specs=[
            pl.BlockSpec((tm, Q_W), lambda i: (i, 0)),
            pl.BlockSpec((tm, KV_W), lambda i: (i, 0)),
            pl.BlockSpec((KV_W, tm), lambda i: (0, i)),
            pl.BlockSpec((tm, FOURIER_W), lambda i: (i, 0)),
        ],
        out_shape=[
            jax.ShapeDtypeStruct((n, Q_W), BF16),
            jax.ShapeDtypeStruct((n, KV_W), BF16),
            jax.ShapeDtypeStruct((KV_W, n), BF16),
            jax.ShapeDtypeStruct((n, FOURIER_W), F32),
        ],
        scratch_shapes=[pltpu.VMEM((D_MODEL, n_cols), BF16)],
        compiler_params=_params(1),
        name="inproj",
    )(x2, mods, g_pre, w_in, *rope_tabs)


def _ctxkv_kernel(x_ref, mod_ref, g_ref, w_ref, k_ref, v_ref):
    h = _norm_modulate(x_ref[...], g_ref[...], mod_ref[0:1, :], mod_ref[1:2, :])
    p = jnp.dot(h.astype(BF16), w_ref[...].astype(BF16), preferred_element_type=F32)
    k_ref[...] = p[:, :KV_W].astype(BF16)
    v_ref[...] = p[:, KV_W:].T.astype(BF16)


def _ctxkv(ctx2, mods_ctx, g_pre, w_in, ctx_len):
    n = ctx2.shape[0]
    assert Q_W % (2 * KV_W) == 0
    return pl.pallas_call(
        _ctxkv_kernel,
        grid=(n // ctx_len,),
        in_specs=[
            pl.BlockSpec((ctx_len, D_MODEL), lambda i: (i, 0)),
            _const_spec((6, D_MODEL)),
            _const_spec((1, D_MODEL)),
            pl.BlockSpec((D_MODEL, 2 * KV_W), lambda i: (0, Q_W // (2 * KV_W)),
                         pipeline_mode=pl.Buffered(1)),
        ],
        out_specs=[
            pl.BlockSpec((ctx_len, KV_W), lambda i: (i, 0)),
            pl.BlockSpec((KV_W, ctx_len), lambda i: (0, i)),
        ],
        out_shape=[
            jax.ShapeDtypeStruct((n, KV_W), BF16),
            jax.ShapeDtypeStruct((KV_W, n), BF16),
        ],
        compiler_params=_params(1),
        name="ctxkv",
    )(ctx2, mods_ctx, g_pre, w_in)


def _attend_block(sink_ref, q_blk, k_parts, v_parts, bias_prev, bias_next):
    kcat = jnp.concatenate(k_parts, axis=0)
    vcat_t = jnp.concatenate(v_parts, axis=1)
    n_keys = kcat.shape[0]
    low = lax.broadcasted_iota(jnp.int32, (BLOCK, LANES), 1) < HEAD_DIM
    top = lax.broadcasted_iota(jnp.int32, (KV_W, n_keys), 0) < HEAD_DIM
    zero = jnp.zeros((), BF16)
    one = jnp.ones((), BF16)
    q_rows = [jnp.where(low if h == 0 else jnp.logical_not(low), q_blk[:, g * LANES:(g + 1) * LANES], zero)
              for h in range(N_KV_HEADS) for g in range(GROUP)]
    s_all = lax.dot_general(kcat, jnp.concatenate(q_rows, axis=0), (((1,), (1,)), ((), ())),
                            preferred_element_type=F32)
    outs = []
    for h in range(N_KV_HEADS):
        v_h = jnp.where(top if h == 0 else jnp.logical_not(top), vcat_t, one)
        for g0 in range(0, GROUP, 2):
            probs = []
            sinks = []
            for g in (g0, g0 + 1):
                head = h * GROUP + g
                s = s_all[:, head * LANES:(head + 1) * LANES]
                parts = [s[0:BLOCK] + bias_prev,
                         s[BLOCK:2 * BLOCK],
                         s[2 * BLOCK:3 * BLOCK] + bias_next]
                parts += [s[r:r + BLOCK] for r in range(3 * BLOCK, n_keys, BLOCK)]
                mx = parts[0]
                for part in parts[1:]:
                    mx = jnp.maximum(mx, part)
                sink = sink_ref[head] * LOG2E
                m = jnp.maximum(jnp.max(mx, axis=0, keepdims=True), sink)
                probs.append(jnp.concatenate([jnp.exp2(part - m) for part in parts], axis=0).astype(BF16))
                sinks.append(jnp.exp2(sink - m))
            o2 = jnp.dot(v_h, jnp.concatenate(probs, axis=1), preferred_element_type=F32)
            for idx in range(2):
                o_t = o2[:, idx * LANES:(idx + 1) * LANES]
                num = o_t[0:HEAD_DIM] if h == 0 else o_t[HEAD_DIM:]
                den = (o_t[HEAD_DIM:] if h == 0 else o_t[0:HEAD_DIM]) + sinks[idx]
                outs.append(num / den)
    return outs


def _attention_steps(sink_ref, q_ref, kp_ref, kc_ref, kn_ref, kx_ref, vp_ref, vc_ref, vn_ref, vx_ref,
                     bias_ref, o_ref):
    i = pl.program_id(1)
    n_sub = q_ref.shape[0] // BLOCK
    first_var = jnp.where(i == 0, 0, 1)
    last_var = jnp.where(i == pl.num_programs(1) - 1, 2, 1)

    def block_step(t):
        own = slice(t * BLOCK, (t + 1) * BLOCK)
        before = slice((t - 1) * BLOCK, t * BLOCK)
        after = slice((t + 1) * BLOCK, (t + 2) * BLOCK)
        k_parts = [kp_ref[...] if t == 0 else kc_ref[before, :], kc_ref[own, :],
                   kn_ref[...] if t == n_sub - 1 else kc_ref[after, :], kx_ref[...]]
        v_parts = [vp_ref[...] if t == 0 else vc_ref[:, before], vc_ref[:, own],
                   vn_ref[...] if t == n_sub - 1 else vc_ref[:, after], vx_ref[...]]
        bias_prev = bias_ref[first_var if t == 0 else 1, 0:BLOCK, :]
        bias_next = bias_ref[last_var if t == n_sub - 1 else 1, BLOCK:2 * BLOCK, :]
        outs = _attend_block(sink_ref, q_ref[own, :], k_parts, v_parts, bias_prev, bias_next)
        for j in range(Q_W // LANES):
            o_t = jnp.concatenate([outs[2 * j], outs[2 * j + 1]], axis=0)
            o_ref[own, j * LANES:(j + 1) * LANES] = o_t.T.astype(BF16)

    return [functools.partial(block_step, t) for t in range(n_sub)]


def _attention_specs(batch, seq, ctx_len, bias_shape):
    nb = seq // BLOCK
    sub = ATTN_BLOCKS_PER_STEP
    steps = nb // sub

    def cur(b, i):
        return (b * steps + i, 0)

    def prev(b, i):
        return (b * nb + jnp.maximum(i * sub - 1, 0), 0)

    def nxt(b, i):
        return (b * nb + jnp.minimum((i + 1) * sub, nb - 1), 0)

    def swap(f):
        return lambda b, i: f(b, i)[::-1]

    in_specs = [
        pl.BlockSpec(memory_space=pltpu.SMEM),
        pl.BlockSpec((sub * BLOCK, Q_W), cur),
        pl.BlockSpec((BLOCK, KV_W), prev),
        pl.BlockSpec((sub * BLOCK, KV_W), cur),
        pl.BlockSpec((BLOCK, KV_W), nxt),
        pl.BlockSpec((ctx_len, KV_W), lambda b, i: (b, 0)),
        pl.BlockSpec((KV_W, BLOCK), swap(prev)),
        pl.BlockSpec((KV_W, sub * BLOCK), swap(cur)),
        pl.BlockSpec((KV_W, BLOCK), swap(nxt)),
        pl.BlockSpec((KV_W, ctx_len), lambda b, i: (0, b)),
        _const_spec(bias_shape),
    ]
    return steps, in_specs, pl.BlockSpec((sub * BLOCK, Q_W), cur)


def _band_bias():
    qi = np.arange(BLOCK)[None, :]
    kj = np.arange(3 * BLOCK)[:, None]
    rel = kj - BLOCK - qi
    in_window = np.abs(rel) <= WINDOW
    variants = []
    for has_prev, has_next in ((False, True), (True, True), (True, False)):
        ok = in_window.copy()
        if not has_prev:
            ok &= kj >= BLOCK
        if not has_next:
            ok &= kj < 2 * BLOCK
        mask = np.where(ok, 0.0, NEG).astype(np.float32)
        variants.append(np.concatenate([mask[:BLOCK], mask[2 * BLOCK:]], axis=0))
    return jnp.asarray(np.stack(variants))


def _minor_dft_terms():
    assert DFT_MINOR == 8
    plan = []
    for k2 in range(DFT_MINOR):
        unit, half = [], []
        for s2 in range(DFT_MINOR):
            ang = -2.0 * math.pi * ((s2 * k2) % DFT_MINOR) / DFT_MINOR
            for part, coef in ((0, math.cos(ang)), (1, -math.sin(ang))):
                if abs(coef) < 1e-9:
                    continue
                target = unit if abs(abs(coef) - 1.0) < 1e-9 else half
                assert target is unit or abs(abs(coef) - math.sqrt(0.5)) < 1e-9
                target.append((s2, part, 1 if coef > 0 else -1))
        plan.append((unit, half))
    return plan


def _signed_sum(terms, pick):
    pos = [pick(s2, part) for s2, part, sign in terms if sign > 0]
    neg = [pick(s2, part) for s2, part, sign in terms if sign < 0]
    total = None
    for v in pos:
        total = v if total is None else total + v
    for v in neg:
        total = -v if total is None else total - v
    return total


def _fourier_steps(f_ref, f1_ref, tr_ref, ti_ref, cm_ref, o_ref):
    n_major = f_ref.shape[0] // DFT_MINOR
    pair_w = 2 * LANES
    n_pairs = DFT_MINOR // 2
    row_chunk = n_major // n_pairs
    spectra = [None] * DFT_MINOR
    plan = _minor_dft_terms()

    def stage1(c):
        y = jnp.concatenate(
            [f_ref[pl.ds(s2, n_major, stride=DFT_MINOR), :].astype(BF16) for s2 in (2 * c, 2 * c + 1)],
            axis=1)
        a = jnp.dot(f1_ref[...], y, preferred_element_type=F32)
        ar = a[:n_major]
        ai = a[n_major:]
        tr = tr_ref[:, c * pair_w:(c + 1) * pair_w]
        ti = ti_ref[:, c * pair_w:(c + 1) * pair_w]
        apr = (ar * tr - ai * ti).astype(BF16)
        api = (ar * ti + ai * tr).astype(BF16)
        for t in range(2):
            lanes = slice(t * LANES, (t + 1) * LANES)
            ap = jnp.concatenate([apr[:, lanes], api[:, lanes]], axis=1)
            spectra[2 * c + t] = jnp.dot(ap, cm_ref[...], preferred_element_type=F32)

    def stage2(r):
        rows = slice(r * row_chunk, (r + 1) * row_chunk)

        def pick(s2, part):
            return spectra[s2][rows, part * LANES:(part + 1) * LANES]

        for k2, (unit, half) in enumerate(plan):
            res = _signed_sum(unit, pick)
            if half:
                scaled = _signed_sum(half, pick) * math.sqrt(0.5)
                res = scaled if res is None else res + scaled
            o_ref[pl.ds(k2 * n_major + r * row_chunk, row_chunk), :] = res.astype(BF16)

    return ([functools.partial(stage1, c) for c in range(n_pairs)]
            + [functools.partial(stage2, r) for r in range(n_pairs)])


def _cis(num, den):
    ang = (num % den).astype(F32) * F32(-2.0 * math.pi / den)
    return jnp.cos(ang), jnp.sin(ang)


def _dft_tables(seq):
    n_major = seq // DFT_MINOR
    ch = FOURIER_GROUP_CH
    root = int(round(math.sqrt(n_major)))
    assert root * root == n_major
    s1 = jnp.arange(n_major, dtype=jnp.int32)[None, :]
    hi = jnp.arange(root, dtype=jnp.int32)[:, None] * root
    lo = jnp.arange(root, dtype=jnp.int32)[:, None]
    hr, hi_ = _cis(hi * s1, n_major)
    lr, li = _cis(lo * s1, n_major)
    f1r = (hr[:, None, :] * lr[None, :, :] - hi_[:, None, :] * li[None, :, :]).reshape(n_major, n_major)
    f1i = (hr[:, None, :] * li[None, :, :] + hi_[:, None, :] * lr[None, :, :]).reshape(n_major, n_major)
    f1 = jnp.concatenate([f1r, f1i], axis=0).astype(BF16)
    k1 = jnp.arange(n_major, dtype=jnp.int32)[:, None]
    s2 = jnp.arange(DFT_MINOR, dtype=jnp.int32)[None, :]
    tr, ti = _cis(k1 * s2, seq)
    tr = jnp.broadcast_to(tr[:, :, None], (n_major, DFT_MINOR, ch)).reshape(n_major, DFT_MINOR * ch)
    ti = jnp.broadcast_to(ti[:, :, None], (n_major, DFT_MINOR, ch)).reshape(n_major, DFT_MINOR * ch)
    c128 = jnp.arange(ch, dtype=jnp.int32)
    cr, ci = _cis(c128[:, None] * c128[None, :], ch)
    norm = F32(1.0 / math.sqrt(seq * ch))
    cm = (jnp.concatenate([jnp.concatenate([cr, ci], axis=1),
                           jnp.concatenate([-ci, cr], axis=1)], axis=0) * norm).astype(BF16)
    return f1, tr, ti, cm


def _mix_kernel(sink_ref, q_ref, kp_ref, kc_ref, kn_ref, kx_ref, vp_ref, vc_ref, vn_ref, vx_ref, bias_ref,
                f_ref, f1_ref, tr_ref, ti_ref, cm_ref, attn_ref, four_ref):
    attn_steps = _attention_steps(sink_ref, q_ref, kp_ref, kc_ref, kn_ref, kx_ref,
                                  vp_ref, vc_ref, vn_ref, vx_ref, bias_ref, attn_ref)
    four_steps = _fourier_steps(f_ref, f1_ref, tr_ref, ti_ref, cm_ref, four_ref)
    per = len(attn_steps) // len(four_steps)
    assert per * len(four_steps) == len(attn_steps)
    for n, four_step in enumerate(four_steps):
        for attn_step in attn_steps[n * per:(n + 1) * per]:
            attn_step()
        four_step()


def _mix(sink, q, k, v_t, kx, vx_t, bias, f3, tables, ctx_len):
    batch, seq, _ = f3.shape
    f1, tr, ti, cm = tables
    ch = FOURIER_GROUP_CH
    steps, attn_in, attn_out = _attention_specs(batch, seq, ctx_len, bias.shape)
    assert steps == N_FOURIER_GROUPS
    four_spec = pl.BlockSpec((None, seq, ch), lambda b, g: (b, 0, g))
    return pl.pallas_call(
        _mix_kernel,
        grid=(batch, steps),
        in_specs=attn_in + [four_spec, _const_spec(f1.shape), _const_spec(tr.shape),
                            _const_spec(ti.shape), _const_spec(cm.shape)],
        out_specs=[attn_out, four_spec],
        out_shape=[jax.ShapeDtypeStruct((batch * seq, Q_W), BF16),
                   jax.ShapeDtypeStruct((batch, seq, FOURIER_W), BF16)],
        compiler_params=_params(2),
        name="mix",
    )(sink, q, k, k, k, kx, v_t, v_t, v_t, vx_t, bias, f3, f1, tr, ti, cm)


def _merge_kernel(x_ref, a_ref, f_ref, mod_ref, gpre_ref, gpost_ref,
                  wg_ref, wpa_ref, wpf_ref, wout_ref, wup_ref, wdown_ref,
                  o_ref, wup_o, wdown_o, wg_b, wpa_b, wpf_b, wout_b):
    wup_o[...] = wup_ref[...].astype(BF16)
    wdown_o[...] = wdown_ref[...].astype(BF16)

    @pl.when(pl.program_id(0) == 0)
    def _():
        wg_b[...] = wg_ref[...].astype(BF16)
        wpa_b[...] = wpa_ref[...].astype(BF16)
        wpf_b[...] = wpf_ref[...].astype(BF16)
        wout_b[...] = wout_ref[...].astype(BF16)

    x = x_ref[...]
    h = _norm_modulate(x, gpre_ref[...], mod_ref[0:1, :], mod_ref[1:2, :]).astype(BF16)
    gates = jnp.dot(h, wg_b[...], preferred_element_type=F32)
    pa = jnp.dot(a_ref[...], wpa_b[...], preferred_element_type=F32)
    pf = jnp.dot(f_ref[...], wpf_b[...], preferred_element_type=F32)
    m = jax.nn.sigmoid(gates[:, :D_MODEL]) * pa + jax.nn.sigmoid(gates[:, D_MODEL:]) * pf
    y = jnp.dot(m.astype(BF16), wout_b[...], preferred_element_type=F32)
    o_ref[...] = x + mod_ref[2:3, :] * _post_norm(y, gpost_ref[...])


def _merge(x2, attn, four, mods, g_pre, g_post, w_in, w_pa, w_pf, w_out, w_up, w_down, seq):
    n = x2.shape[0]
    tm = TOKEN_TILE
    n_steps = n // tm
    tiles_per_seq = seq // tm
    row = lambda i: (i, 0)
    gate_col0 = Q_W + 2 * KV_W + FOURIER_W
    up_rows = w_up.shape[0] // n_steps
    down_rows = w_down.shape[0] // BF16_ROWS
    n_down = w_down.shape[0] // down_rows
    assert up_rows * n_steps == w_up.shape[0] and up_rows % BF16_ROWS == 0
    assert down_rows % BF16_ROWS == 0 and n_down <= n_steps
    up_spec = pl.BlockSpec((up_rows, w_up.shape[1]), row)
    down_spec = pl.BlockSpec((down_rows, w_down.shape[1]), lambda i: (jnp.minimum(i, n_down - 1), 0))
    return pl.pallas_call(
        _merge_kernel,
        grid=(n // tm,),
        in_specs=[
            pl.BlockSpec((tm, D_MODEL), row),
            pl.BlockSpec((tm, Q_W), row),
            pl.BlockSpec((tm, FOURIER_W), row),
            pl.BlockSpec((None, 6, D_MODEL), lambda i: (i // tiles_per_seq, 0, 0)),
            _const_spec((1, D_MODEL)),
            _const_spec((1, D_MODEL)),
            pl.BlockSpec((pl.Element(D_MODEL), pl.Element(2 * D_MODEL)),
                         lambda i: (0, gate_col0),
                         pipeline_mode=pl.Buffered(1)),
            _const_spec(w_pa.shape),
            _const_spec(w_pf.shape),
            _const_spec(w_out.shape),
            up_spec,
            down_spec,
        ],
        out_specs=[pl.BlockSpec((tm, D_MODEL), row), up_spec, down_spec],
        out_shape=[jax.ShapeDtypeStruct((n, D_MODEL), F32),
                   jax.ShapeDtypeStruct(w_up.shape, BF16),
                   jax.ShapeDtypeStruct(w_down.shape, BF16)],
        scratch_shapes=[pltpu.VMEM((D_MODEL, 2 * D_MODEL), BF16), pltpu.VMEM(w_pa.shape, BF16),
                        pltpu.VMEM(w_pf.shape, BF16), pltpu.VMEM(w_out.shape, BF16)],
        compiler_params=_params(1),
        name="merge",
    )(x2, attn, four, mods, g_pre, g_post, w_in, w_pa, w_pf, w_out, w_up, w_down)


def _convffn_kernel(x_ref, xp_ref, xn_ref, mod_ref, gpre_ref, gpost_ref,
                    wu_ref, wgate_ref, cw_ref, cb_ref, wd_ref, o_ref, act_ref, *, tiles_per_seq):
    i = pl.program_id(0)
    tm = x_ref.shape[0]
    halo = SUBLANES
    has_prev = (i % tiles_per_seq != 0).astype(F32)
    has_next = (i % tiles_per_seq != tiles_per_seq - 1).astype(F32)
    x = x_ref[...]
    shift = mod_ref[3:4, :]
    scale = mod_ref[4:5, :]
    gain = gpre_ref[...]
    h = _norm_modulate(x, gain, shift, scale)
    hp = _norm_modulate(xp_ref[...], gain, shift, scale) * has_prev
    hn = _norm_modulate(xn_ref[...], gain, shift, scale) * has_next
    h_ext = jnp.concatenate([hp, h, hn], axis=0).astype(BF16)
    h_mid = h.astype(BF16)
    n_ext = tm + 2 * halo
    for c in range(D_FF // FFN_CHUNK):
        cols = slice(c * FFN_CHUNK, (c + 1) * FFN_CHUNK)
        u = jnp.dot(h_ext, wu_ref[:, cols], preferred_element_type=F32)
        gate = jnp.dot(h_mid, wgate_ref[:, cols], preferred_element_type=F32)
        u_prev = pltpu.roll(u, 1, 0)[halo:halo + tm]
        u_next = pltpu.roll(u, n_ext - 1, 0)[halo:halo + tm]
        conv = (u_prev * cw_ref[0:1, cols] + u[halo:halo + tm] * cw_ref[1:2, cols]
                + u_next * cw_ref[2:3, cols] + cb_ref[:, cols])
        act_ref[:, cols] = (conv * jax.nn.sigmoid(conv) * gate).astype(BF16)
    y = jnp.dot(act_ref[...], wd_ref[...], preferred_element_type=F32)
    o_ref[...] = x + mod_ref[5:6, :] * _post_norm(y, gpost_ref[...])


def _convffn(x1, mods, g_pre, g_post, w_up, conv_w, conv_b, w_down, seq):
    n = x1.shape[0]
    tm = TOKEN_TILE
    tiles_per_seq = seq // tm
    halo_blocks_per_tile = tm // SUBLANES
    n_halo_blocks = n // SUBLANES
    return pl.pallas_call(
        functools.partial(_convffn_kernel, tiles_per_seq=tiles_per_seq),
        grid=(n // tm,),
        in_specs=[
            pl.BlockSpec((tm, D_MODEL), lambda i: (i, 0)),
            pl.BlockSpec((SUBLANES, D_MODEL),
                         lambda i: (jnp.maximum(i * halo_blocks_per_tile - 1, 0), 0)),
            pl.BlockSpec((SUBLANES, D_MODEL),
                         lambda i: (jnp.minimum((i + 1) * halo_blocks_per_tile, n_halo_blocks - 1), 0)),
            pl.BlockSpec((None, 6, D_MODEL), lambda i: (i // tiles_per_seq, 0, 0)),
            _const_spec((1, D_MODEL)),
            _const_spec((1, D_MODEL)),
            pl.BlockSpec((D_MODEL, D_FF), lambda i: (0, 0), pipeline_mode=pl.Buffered(1)),
            pl.BlockSpec((D_MODEL, D_FF), lambda i: (0, 1), pipeline_mode=pl.Buffered(1)),
            _const_spec(conv_w.shape),
            _const_spec((1, D_FF)),
            _const_spec(w_down.shape),
        ],
        out_specs=pl.BlockSpec((tm, D_MODEL), lambda i: (i, 0)),
        out_shape=jax.ShapeDtypeStruct((n, D_MODEL), F32),
        scratch_shapes=[pltpu.VMEM((tm, D_FF), BF16)],
        compiler_params=_params(1),
        name="convffn",
    )(x1, x1, x1, mods, g_pre, g_post, w_up, w_up, conv_w, conv_b.reshape(1, D_FF), w_down)


def _rope_tables(seq):
    half = HEAD_DIM // 2
    inv_freq = ROPE_THETA ** (-jnp.arange(0, half, 2, dtype=F32) / half)
    ang_r = jnp.arange(seq // GRID_W).astype(F32)[:, None] * inv_freq
    ang_c = jnp.arange(GRID_W).astype(F32)[:, None] * inv_freq
    reps = LANES // HEAD_DIM

    def lanes(row_part, col_part):
        return jnp.tile(jnp.concatenate([row_part, col_part], axis=-1), (1, reps))

    zr = jnp.zeros((ang_r.shape[0], half), F32)
    zc = jnp.zeros((ang_c.shape[0], half), F32)
    rcos = lanes(jnp.concatenate([jnp.cos(ang_r), jnp.cos(ang_r)], axis=-1), zr)
    rsin = lanes(jnp.concatenate([-jnp.sin(ang_r), jnp.sin(ang_r)], axis=-1), zr)
    ccos = lanes(zc, jnp.concatenate([jnp.cos(ang_c), jnp.cos(ang_c)], axis=-1))
    csin = lanes(zc, jnp.concatenate([-jnp.sin(ang_c), jnp.sin(ang_c)], axis=-1))
    return rcos, rsin, ccos, csin


def kernel(x, c, ctx, c_ctx, w_mod, b_mod, g_pre1, g_post1, g_pre2, g_post2,
           w_in, sink, w_pa, w_pf, w_out, w_up, conv_w, conv_b, w_down):
    batch, seq, d = x.shape
    ctx_len = ctx.shape[1]
    depth = w_mod.shape[0]
    assert depth == 1 and d == D_MODEL and batch + 1 <= SUBLANES
    assert seq % TOKEN_TILE == 0 and TOKEN_TILE % GRID_W == 0 and seq % (DFT_MINOR * SUBLANES) == 0
    assert seq % (ATTN_BLOCKS_PER_STEP * BLOCK) == 0
    n = batch * seq
    rope_tabs = _rope_tables(seq)
    tables = _dft_tables(seq)
    bias = _band_bias()

    l = 0
    cvecs = jnp.zeros((SUBLANES, d), F32).at[:batch].set(c).at[batch].set(c_ctx)
    mod_all = _adaln(cvecs, w_mod[l], b_mod[l])
    mods = mod_all[:batch].reshape(batch, 6, d)
    mods_ctx = mod_all[batch].reshape(6, d)

    g_pre1_row = g_pre1[l].reshape(1, d)
    g_post1_row = g_post1[l].reshape(1, d)
    g_pre2_row = g_pre2[l].reshape(1, d)
    g_post2_row = g_post2[l].reshape(1, d)

    x2 = x.reshape(n, d)
    q, k, v_t, f = _inproj(x2, mods, g_pre1_row, w_in[l], rope_tabs, seq)
    kx, vx_t = _ctxkv(ctx.reshape(batch * ctx_len, d), mods_ctx, g_pre1_row, w_in[l], ctx_len)
    attn, four = _mix(sink[l], q, k, v_t, kx, vx_t, bias, f.reshape(batch, seq, FOURIER_W), tables, ctx_len)
    four = four.reshape(n, FOURIER_W)
    x1, w_up_b, w_down_b = _merge(x2, attn, four, mods, g_pre1_row, g_post1_row, w_in[l], w_pa[l], w_pf[l],
                                  w_out[l], w_up[l], w_down[l], seq)
    out = _convffn(x1, mods, g_pre2_row, g_post2_row, w_up_b, conv_w[l], conv_b[l], w_down_b, seq)
    return out.reshape(batch, seq, d)
```

```python
import functools
import math

import numpy as np
import jax
import jax.numpy as jnp
from jax import lax
from jax.experimental import pallas as pl
from jax.experimental.pallas import tpu as pltpu

F32 = jnp.float32
BF16 = jnp.bfloat16

D_MODEL = 1024
GRID_W = 64
HEAD_DIM = 64
N_Q_HEADS = 8
N_KV_HEADS = 2
GROUP = N_Q_HEADS // N_KV_HEADS
WINDOW = 128
BLOCK = 128
ROPE_THETA = 10000.0
N_FOURIER_GROUPS = 4
FOURIER_GROUP_CH = 128
FOURIER_W = N_FOURIER_GROUPS * FOURIER_GROUP_CH
Q_W = N_Q_HEADS * HEAD_DIM
KV_W = N_KV_HEADS * HEAD_DIM
D_FF = 2816
EPS = 1e-6
NEG = -1e30
LOG2E = math.log2(math.e)

LANES = 128
SUBLANES = 8
BF16_ROWS = 16
MXU_DIM = 256
VMEM_LIMIT_BYTES = 56 * 1024 * 1024

TOKEN_TILE = 512
FFN_CHUNK = MXU_DIM
DFT_MINOR = SUBLANES
ATTN_BLOCKS_PER_STEP = 16


def _const_spec(shape):
    nd = len(shape)
    return pl.BlockSpec(shape, lambda *_: (0,) * nd, pipeline_mode=pl.Buffered(1))


def _params(n_axes):
    return pltpu.CompilerParams(
        dimension_semantics=("arbitrary",) * n_axes,
        vmem_limit_bytes=VMEM_LIMIT_BYTES,
    )


def _norm_modulate(x, gain, shift, scale):
    ms = jnp.mean(x * x, axis=-1, keepdims=True)
    return (x * lax.rsqrt(ms + EPS)) * (gain * (1.0 + scale)) + shift


def _post_norm(y, gain):
    ms = jnp.mean(y * y, axis=-1, keepdims=True)
    return (y * lax.rsqrt(ms + EPS)) * gain


def _adaln_kernel(c_ref, w_ref, b_ref, o_ref):
    c = c_ref[...]
    s = c * jax.nn.sigmoid(c)
    o_ref[...] = jnp.dot(s, w_ref[...], preferred_element_type=F32) + b_ref[...]


def _adaln(cvecs, w_mod, b_mod):
    n_out = w_mod.shape[1]
    tn = 1536
    return pl.pallas_call(
        _adaln_kernel,
        grid=(n_out // tn,),
        in_specs=[
            pl.BlockSpec((SUBLANES, D_MODEL), lambda j: (0, 0)),
            pl.BlockSpec((D_MODEL, tn), lambda j: (0, j)),
            pl.BlockSpec((1, tn), lambda j: (0, j)),
        ],
        out_specs=pl.BlockSpec((SUBLANES, tn), lambda j: (0, j)),
        out_shape=jax.ShapeDtypeStruct((SUBLANES, n_out), F32),
        compiler_params=_params(1),
        name="adaln",
    )(cvecs, w_mod, b_mod.reshape(1, n_out))


def _rope(xb, cos_t, sin_t, first_half):
    sw = jnp.where(first_half, pltpu.roll(xb, LANES - 16, 1), pltpu.roll(xb, 16, 1))
    return xb * cos_t + sw * sin_t


def _token_table(row_tab, col_tab):
    n_rows = row_tab.shape[0]
    by_row = jnp.concatenate(
        [jnp.broadcast_to(row_tab[r:r + 1, :], (GRID_W, LANES)) for r in range(n_rows)], axis=0)
    return by_row + jnp.concatenate([col_tab] * n_rows, axis=0)


def _inproj_kernel(x_ref, mod_ref, g_ref, w_ref, rcos_ref, rsin_ref, ccos_ref, csin_ref,
                   q_ref, k_ref, v_ref, f_ref, wb_ref):
    @pl.when(pl.program_id(0) == 0)
    def _():
        low = lax.broadcasted_iota(jnp.int32, (D_MODEL, LANES), 1) < HEAD_DIM
        for j in range(Q_W // LANES):
            src_a = (j // 2) * LANES
            src_b = (GROUP // 2 + j // 2) * LANES
            a = w_ref[:, src_a:src_a + LANES]
            b = w_ref[:, src_b:src_b + LANES]
            if j % 2 == 1:
                a = pltpu.roll(a, HEAD_DIM, 1)
            else:
                b = pltpu.roll(b, HEAD_DIM, 1)
            wb_ref[:, j * LANES:(j + 1) * LANES] = jnp.where(low, a, b).astype(BF16)
        wb_ref[:, Q_W:] = w_ref[:, Q_W:].astype(BF16)

    h = _norm_modulate(x_ref[...], g_ref[...], mod_ref[0:1, :], mod_ref[1:2, :])
    p = jnp.dot(h.astype(BF16), wb_ref[...], preferred_element_type=F32)
    cos_t = _token_table(rcos_ref[...], ccos_ref[...])
    sin_t = _token_table(rsin_ref[...], csin_ref[...])
    lane = lax.broadcasted_iota(jnp.int32, cos_t.shape, 1)
    first_half = (lane % 32) < 16
    scale = HEAD_DIM ** -0.5 * LOG2E
    for b in range(Q_W // LANES):
        blk = p[:, b * LANES:(b + 1) * LANES]
        q_ref[:, b * LANES:(b + 1) * LANES] = (_rope(blk, cos_t, sin_t, first_half) * scale).astype(BF16)
    k_ref[...] = _rope(p[:, Q_W:Q_W + KV_W], cos_t, sin_t, first_half).astype(BF16)
    v_ref[...] = p[:, Q_W + KV_W:Q_W + 2 * KV_W].T.astype(BF16)
    f_ref[...] = p[:, Q_W + 2 * KV_W:]


def _inproj(x2, mods, g_pre, w_in, rope_tabs, seq):
    n = x2.shape[0]
    tm = TOKEN_TILE
    tiles_per_seq = seq // tm
    rows_per_tile = tm // GRID_W
    n_cols = Q_W + 2 * KV_W + FOURIER_W
    row_spec = pl.BlockSpec((rows_per_tile, LANES), lambda i: (i % tiles_per_seq, 0))
    return pl.pallas_call(
        _inproj_kernel,
        grid=(n // tm,),
        in_specs=[
            pl.BlockSpec((tm, D_MODEL), lambda i: (i, 0)),
            pl.BlockSpec((None, 6, D_MODEL), lambda i: (i // tiles_per_seq, 0, 0)),
            _const_spec((1, D_MODEL)),
            _const_spec((D_MODEL, n_cols)),
            row_spec, row_spec,
            _const_spec((GRID_W, LANES)), _const_spec((GRID_W, LANES)),
        ],
        out_specs=[
            pl.BlockSpec((tm, Q_W), lambda i: (i, 0)),
            pl.BlockSpec((tm, KV_W), lambda i: (i, 0)),
            pl.BlockSpec((KV_W, tm), lambda i: (0, i)),
            pl.BlockSpec((tm, FOURIER_W), lambda i: (i, 0)),
        ],
        out_shape=[
            jax.ShapeDtypeStruct((n, Q_W), BF16),
            jax.ShapeDtypeStruct((n, KV_W), BF16),
            jax.ShapeDtypeStruct((KV_W, n), BF16),
            jax.ShapeDtypeStruct((n, FOURIER_W), F32),
        ],
        scratch_shapes=[pltpu.VMEM((D_MODEL, n_cols), BF16)],
        compiler_params=_params(1),
        name="inproj",
    )(x2, mods, g_pre, w_in, *rope_tabs)


def _ctxkv_kernel(x_ref, mod_ref, g_ref, w_ref, k_ref, v_ref):
    h = _norm_modulate(x_ref[...], g_ref[...], mod_ref[0:1, :], mod_ref[1:2, :])
    p = jnp.dot(h.astype(BF16), w_ref[...].astype(BF16), preferred_element_type=F32)
    k_ref[...] = p[:, :KV_W].astype(BF16)
    v_ref[...] = p[:, KV_W:].T.astype(BF16)


def _ctxkv(ctx2, mods_ctx, g_pre, w_in, ctx_len):
    n = ctx2.shape[0]
    assert Q_W % (2 * KV_W) == 0
    return pl.pallas_call(
        _ctxkv_kernel,
        grid=(n // ctx_len,),
        in_specs=[
            pl.BlockSpec((ctx_len, D_MODEL), lambda i: (i, 0)),
            _const_spec((6, D_MODEL)),
            _const_spec((1, D_MODEL)),
            pl.BlockSpec((D_MODEL, 2 * KV_W), lambda i: (0, Q_W // (2 * KV_W)),
                         pipeline_mode=pl.Buffered(1)),
        ],
        out_specs=[
            pl.BlockSpec((ctx_len, KV_W), lambda i: (i, 0)),
            pl.BlockSpec((KV_W, ctx_len), lambda i: (0, i)),
        ],
        out_shape=[
            jax.ShapeDtypeStruct((n, KV_W), BF16),
            jax.ShapeDtypeStruct((KV_W, n), BF16),
        ],
        compiler_params=_params(1),
        name="ctxkv",
    )(ctx2, mods_ctx, g_pre, w_in)


def _attend_block(sink_ref, q_blk, k_parts, v_parts, bias_prev, bias_next):
    kcat = jnp.concatenate(k_parts, axis=0)
    vcat_t = jnp.concatenate(v_parts, axis=1)
    n_keys = kcat.shape[0]
    low = lax.broadcasted_iota(jnp.int32, (BLOCK, LANES), 1) < HEAD_DIM
    top = lax.broadcasted_iota(jnp.int32, (KV_W, n_keys), 0) < HEAD_DIM
    zero = jnp.zeros((), BF16)
    one = jnp.ones((), BF16)
    q_rows = [jnp.where(low if h == 0 else jnp.logical_not(low), q_blk[:, g * LANES:(g + 1) * LANES], zero)
              for h in range(N_KV_HEADS) for g in range(GROUP)]
    s_all = lax.dot_general(kcat, jnp.concatenate(q_rows, axis=0), (((1,), (1,)), ((), ())),
                            preferred_element_type=F32)
    outs = []
    for h in range(N_KV_HEADS):
        v_h = jnp.where(top if h == 0 else jnp.logical_not(top), vcat_t, one)
        for g0 in range(0, GROUP, 2):
            probs = []
            sinks = []
            for g in (g0, g0 + 1):
                head = h * GROUP + g
                s = s_all[:, head * LANES:(head + 1) * LANES]
                parts = [s[0:BLOCK] + bias_prev,
                         s[BLOCK:2 * BLOCK],
                         s[2 * BLOCK:3 * BLOCK] + bias_next]
                parts += [s[r:r + BLOCK] for r in range(3 * BLOCK, n_keys, BLOCK)]
                mx = parts[0]
                for part in parts[1:]:
                    mx = jnp.maximum(mx, part)
                sink = sink_ref[head] * LOG2E
                m = jnp.maximum(jnp.max(mx, axis=0, keepdims=True), sink)
                probs.append(jnp.concatenate([jnp.exp2(part - m) for part in parts], axis=0).astype(BF16))
                sinks.append(jnp.exp2(sink - m))
            o2 = jnp.dot(v_h, jnp.concatenate(probs, axis=1), preferred_element_type=F32)
            for idx in range(2):
                o_t = o2[:, idx * LANES:(idx + 1) * LANES]
                num = o_t[0:HEAD_DIM] if h == 0 else o_t[HEAD_DIM:]
                den = (o_t[HEAD_DIM:] if h == 0 else o_t[0:HEAD_DIM]) + sinks[idx]
                outs.append(num / den)
    return outs


def _attention_steps(sink_ref, q_ref, kp_ref, kc_ref, kn_ref, kx_ref, vp_ref, vc_ref, vn_ref, vx_ref,
                     bias_ref, o_ref):
    i = pl.program_id(1)
    n_sub = q_ref.shape[0] // BLOCK
    first_var = jnp.where(i == 0, 0, 1)
    last_var = jnp.where(i == pl.num_programs(1) - 1, 2, 1)

    def block_step(t):
        own = slice(t * BLOCK, (t + 1) * BLOCK)
        before = slice((t - 1) * BLOCK, t * BLOCK)
        after = slice((t + 1) * BLOCK, (t + 2) * BLOCK)
        k_parts = [kp_ref[...] if t == 0 else kc_ref[before, :], kc_ref[own, :],
                   kn_ref[...] if t == n_sub - 1 else kc_ref[after, :], kx_ref[...]]
        v_parts = [vp_ref[...] if t == 0 else vc_ref[:, before], vc_ref[:, own],
                   vn_ref[...] if t == n_sub - 1 else vc_ref[:, after], vx_ref[...]]
        bias_prev = bias_ref[first_var if t == 0 else 1, 0:BLOCK, :]
        bias_next = bias_ref[last_var if t == n_sub - 1 else 1, BLOCK:2 * BLOCK, :]
        outs = _attend_block(sink_ref, q_ref[own, :], k_parts, v_parts, bias_prev, bias_next)
        for j in range(Q_W // LANES):
            o_t = jnp.concatenate([outs[2 * j], outs[2 * j + 1]], axis=0)
            o_ref[own, j * LANES:(j + 1) * LANES] = o_t.T.astype(BF16)

    return [functools.partial(block_step, t) for t in range(n_sub)]


def _attention_specs(batch, seq, ctx_len, bias_shape):
    nb = seq // BLOCK
    sub = ATTN_BLOCKS_PER_STEP
    steps = nb // sub

    def cur(b, i):
        return (b * steps + i, 0)

    def prev(b, i):
        return (b * nb + jnp.maximum(i * sub - 1, 0), 0)

    def nxt(b, i):
        return (b * nb + jnp.minimum((i + 1) * sub, nb - 1), 0)

    def swap(f):
        return lambda b, i: f(b, i)[::-1]

    in_specs = [
        pl.BlockSpec(memory_space=pltpu.SMEM),
        pl.BlockSpec((sub * BLOCK, Q_W), cur),
        pl.BlockSpec((BLOCK, KV_W), prev),
        pl.BlockSpec((sub * BLOCK, KV_W), cur),
        pl.BlockSpec((BLOCK, KV_W), nxt),
        pl.BlockSpec((ctx_len, KV_W), lambda b, i: (b, 0)),
        pl.BlockSpec((KV_W, BLOCK), swap(prev)),
        pl.BlockSpec((KV_W, sub * BLOCK), swap(cur)),
        pl.BlockSpec((KV_W, BLOCK), swap(nxt)),
        pl.BlockSpec((KV_W, ctx_len), lambda b, i: (0, b)),
        _const_spec(bias_shape),
    ]
    return steps, in_specs, pl.BlockSpec((sub * BLOCK, Q_W), cur)


def _band_bias():
    qi = np.arange(BLOCK)[None, :]
    kj = np.arange(3 * BLOCK)[:, None]
    rel = kj - BLOCK - qi
    in_window = np.abs(rel) <= WINDOW
    variants = []
    for has_prev, has_next in ((False, True), (True, True), (True, False)):
        ok = in_window.copy()
        if not has_prev:
            ok &= kj >= BLOCK
        if not has_next:
            ok &= kj < 2 * BLOCK
        mask = np.where(ok, 0.0, NEG).astype(np.float32)
        variants.append(np.concatenate([mask[:BLOCK], mask[2 * BLOCK:]], axis=0))
    return jnp.asarray(np.stack(variants))


def _minor_dft_terms():
    assert DFT_MINOR == 8
    plan = []
    for k2 in range(DFT_MINOR):
        unit, half = [], []
        for s2 in range(DFT_MINOR):
            ang = -2.0 * math.pi * ((s2 * k2) % DFT_MINOR) / DFT_MINOR
            for part, coef in ((0, math.cos(ang)), (1, -math.sin(ang))):
                if abs(coef) < 1e-9:
                    continue
                target = unit if abs(abs(coef) - 1.0) < 1e-9 else half
                assert target is unit or abs(abs(coef) - math.sqrt(0.5)) < 1e-9
                target.append((s2, part, 1 if coef > 0 else -1))
        plan.append((unit, half))
    return plan


def _signed_sum(terms, pick):
    pos = [pick(s2, part) for s2, part, sign in terms if sign > 0]
    neg = [pick(s2, part) for s2, part, sign in terms if sign < 0]
    total = None
    for v in pos:
        total = v if total is None else total + v
    for v in neg:
        total = -v if total is None else total - v
    return total


def _fourier_steps(f_ref, f1_ref, tr_ref, ti_ref, cm_ref, o_ref):
    n_major = f_ref.shape[0] // DFT_MINOR
    pair_w = 2 * LANES
    n_pairs = DFT_MINOR // 2
    row_chunk = n_major // n_pairs
    twiddled = [None] * n_pairs
    spectra = [None] * DFT_MINOR
    plan = _minor_dft_terms()

    def stage1(c):
        y = jnp.concatenate(
            [f_ref[pl.ds(s2, n_major, stride=DFT_MINOR), :].astype(BF16) for s2 in (2 * c, 2 * c + 1)],
            axis=1)
        a = jnp.dot(f1_ref[...], y, preferred_element_type=F32)
        ar = a[:n_major]
        ai = a[n_major:]
        tr = tr_ref[:, c * pair_w:(c + 1) * pair_w]
        ti = ti_ref[:, c * pair_w:(c + 1) * pair_w]
        twiddled[c] = ((ar * tr - ai * ti).astype(BF16), (ar * ti + ai * tr).astype(BF16))

    def channels(c):
        apr, api = twiddled[c]
        for t in range(2):
            lanes = slice(t * LANES, (t + 1) * LANES)
            ap = jnp.concatenate([apr[:, lanes], api[:, lanes]], axis=1)
            spectra[2 * c + t] = jnp.dot(ap, cm_ref[...], preferred_element_type=F32)

    def stage2(r):
        rows = slice(r * row_chunk, (r + 1) * row_chunk)

        def pick(s2, part):
            return spectra[s2][rows, part * LANES:(part + 1) * LANES]

        for k2, (unit, half) in enumerate(plan):
            res = _signed_sum(unit, pick)
            if half:
                scaled = _signed_sum(half, pick) * math.sqrt(0.5)
                res = scaled if res is None else res + scaled
            o_ref[pl.ds(k2 * n_major + r * row_chunk, row_chunk), :] = res.astype(BF16)

    steps = [functools.partial(stage1, 0)]
    for c in range(1, n_pairs):
        steps += [functools.partial(stage1, c), functools.partial(channels, c - 1)]
    steps.append(functools.partial(channels, n_pairs - 1))
    return steps + [functools.partial(stage2, r) for r in range(n_pairs)]


def _cis(num, den):
    ang = (num % den).astype(F32) * F32(-2.0 * math.pi / den)
    return jnp.cos(ang), jnp.sin(ang)


def _dft_tables(seq):
    n_major = seq // DFT_MINOR
    ch = FOURIER_GROUP_CH
    root = int(round(math.sqrt(n_major)))
    assert root * root == n_major
    s1 = jnp.arange(n_major, dtype=jnp.int32)[None, :]
    hi = jnp.arange(root, dtype=jnp.int32)[:, None] * root
    lo = jnp.arange(root, dtype=jnp.int32)[:, None]
    hr, hi_ = _cis(hi * s1, n_major)
    lr, li = _cis(lo * s1, n_major)
    f1r = (hr[:, None, :] * lr[None, :, :] - hi_[:, None, :] * li[None, :, :]).reshape(n_major, n_major)
    f1i = (hr[:, None, :] * li[None, :, :] + hi_[:, None, :] * lr[None, :, :]).reshape(n_major, n_major)
    f1 = jnp.concatenate([f1r, f1i], axis=0).astype(BF16)
    k1 = jnp.arange(n_major, dtype=jnp.int32)[:, None]
    s2 = jnp.arange(DFT_MINOR, dtype=jnp.int32)[None, :]
    tr, ti = _cis(k1 * s2, seq)
    tr = jnp.broadcast_to(tr[:, :, None], (n_major, DFT_MINOR, ch)).reshape(n_major, DFT_MINOR * ch)
    ti = jnp.broadcast_to(ti[:, :, None], (n_major, DFT_MINOR, ch)).reshape(n_major, DFT_MINOR * ch)
    c128 = jnp.arange(ch, dtype=jnp.int32)
    cr, ci = _cis(c128[:, None] * c128[None, :], ch)
    norm = F32(1.0 / math.sqrt(seq * ch))
    cm = (jnp.concatenate([jnp.concatenate([cr, ci], axis=1),
                           jnp.concatenate([-ci, cr], axis=1)], axis=0) * norm).astype(BF16)
    return f1, tr, ti, cm


def _mix_kernel(sink_ref, q_ref, kp_ref, kc_ref, kn_ref, kx_ref, vp_ref, vc_ref, vn_ref, vx_ref, bias_ref,
                f_ref, f1_ref, tr_ref, ti_ref, cm_ref, attn_ref, four_ref):
    attn_steps = _attention_steps(sink_ref, q_ref, kp_ref, kc_ref, kn_ref, kx_ref,
                                  vp_ref, vc_ref, vn_ref, vx_ref, bias_ref, attn_ref)
    four_steps = _fourier_steps(f_ref, f1_ref, tr_ref, ti_ref, cm_ref, four_ref)
    done = 0
    for n, four_step in enumerate(four_steps):
        upto = (n + 1) * len(attn_steps) // len(four_steps)
        for attn_step in attn_steps[done:upto]:
            attn_step()
        done = upto
        four_step()


def _mix(sink, q, k, v_t, kx, vx_t, bias, f3, tables, ctx_len):
    batch, seq, _ = f3.shape
    f1, tr, ti, cm = tables
    ch = FOURIER_GROUP_CH
    steps, attn_in, attn_out = _attention_specs(batch, seq, ctx_len, bias.shape)
    assert steps == N_FOURIER_GROUPS
    four_spec = pl.BlockSpec((None, seq, ch), lambda b, g: (b, 0, g))
    return pl.pallas_call(
        _mix_kernel,
        grid=(batch, steps),
        in_specs=attn_in + [four_spec, _const_spec(f1.shape), _const_spec(tr.shape),
                            _const_spec(ti.shape), _const_spec(cm.shape)],
        out_specs=[attn_out, four_spec],
        out_shape=[jax.ShapeDtypeStruct((batch * seq, Q_W), BF16),
                   jax.ShapeDtypeStruct((batch, seq, FOURIER_W), BF16)],
        compiler_params=_params(2),
        name="mix",
    )(sink, q, k, k, k, kx, v_t, v_t, v_t, vx_t, bias, f3, f1, tr, ti, cm)


def _merge_kernel(x_ref, a_ref, f_ref, mod_ref, gpre_ref, gpost_ref,
                  wg_ref, wpa_ref, wpf_ref, wout_ref, wup_ref, wdown_ref,
                  o_ref, wup_o, wdown_o, wg_b, wpa_b, wpf_b, wout_b):
    wup_o[...] = wup_ref[...].astype(BF16)
    wdown_o[...] = wdown_ref[...].astype(BF16)

    @pl.when(pl.program_id(0) == 0)
    def _():
        wg_b[...] = wg_ref[...].astype(BF16)
        wpa_b[...] = wpa_ref[...].astype(BF16)
        wpf_b[...] = wpf_ref[...].astype(BF16)
        wout_b[...] = wout_ref[...].astype(BF16)

    x = x_ref[...]
    h = _norm_modulate(x, gpre_ref[...], mod_ref[0:1, :], mod_ref[1:2, :]).astype(BF16)
    gates = jnp.dot(h, wg_b[...], preferred_element_type=F32)
    pa = jnp.dot(a_ref[...], wpa_b[...], preferred_element_type=F32)
    pf = jnp.dot(f_ref[...], wpf_b[...], preferred_element_type=F32)
    m = jax.nn.sigmoid(gates[:, :D_MODEL]) * pa + jax.nn.sigmoid(gates[:, D_MODEL:]) * pf
    y = jnp.dot(m.astype(BF16), wout_b[...], preferred_element_type=F32)
    o_ref[...] = x + mod_ref[2:3, :] * _post_norm(y, gpost_ref[...])


def _merge(x2, attn, four, mods, g_pre, g_post, w_in, w_pa, w_pf, w_out, w_up, w_down, seq):
    n = x2.shape[0]
    tm = TOKEN_TILE
    n_steps = n // tm
    tiles_per_seq = seq // tm
    row = lambda i: (i, 0)
    gate_col0 = Q_W + 2 * KV_W + FOURIER_W
    up_rows = w_up.shape[0] // n_steps
    down_rows = w_down.shape[0] // BF16_ROWS
    n_down = w_down.shape[0] // down_rows
    assert up_rows * n_steps == w_up.shape[0] and up_rows % BF16_ROWS == 0
    assert down_rows % BF16_ROWS == 0 and n_down <= n_steps
    up_spec = pl.BlockSpec((up_rows, w_up.shape[1]), row)
    down_spec = pl.BlockSpec((down_rows, w_down.shape[1]), lambda i: (jnp.minimum(i, n_down - 1), 0))
    return pl.pallas_call(
        _merge_kernel,
        grid=(n // tm,),
        in_specs=[
            pl.BlockSpec((tm, D_MODEL), row),
            pl.BlockSpec((tm, Q_W), row),
            pl.BlockSpec((tm, FOURIER_W), row),
            pl.BlockSpec((None, 6, D_MODEL), lambda i: (i // tiles_per_seq, 0, 0)),
            _const_spec((1, D_MODEL)),
            _const_spec((1, D_MODEL)),
            pl.BlockSpec((pl.Element(D_MODEL), pl.Element(2 * D_MODEL)),
                         lambda i: (0, gate_col0),
                         pipeline_mode=pl.Buffered(1)),
            _const_spec(w_pa.shape),
            _const_spec(w_pf.shape),
            _const_spec(w_out.shape),
            up_spec,
            down_spec,
        ],
        out_specs=[pl.BlockSpec((tm, D_MODEL), row), up_spec, down_spec],
        out_shape=[jax.ShapeDtypeStruct((n, D_MODEL), F32),
                   jax.ShapeDtypeStruct(w_up.shape, BF16),
                   jax.ShapeDtypeStruct(w_down.shape, BF16)],
        scratch_shapes=[pltpu.VMEM((D_MODEL, 2 * D_MODEL), BF16), pltpu.VMEM(w_pa.shape, BF16),
                        pltpu.VMEM(w_pf.shape, BF16), pltpu.VMEM(w_out.shape, BF16)],
        compiler_params=_params(1),
        name="merge",
    )(x2, attn, four, mods, g_pre, g_post, w_in, w_pa, w_pf, w_out, w_up, w_down)


def _convffn_kernel(x_ref, xp_ref, xn_ref, mod_ref, gpre_ref, gpost_ref,
                    wu_ref, wgate_ref, cw_ref, cb_ref, wd_ref, o_ref, act_ref, *, tiles_per_seq):
    i = pl.program_id(0)
    tm = x_ref.shape[0]
    halo = SUBLANES
    has_prev = (i % tiles_per_seq != 0).astype(F32)
    has_next = (i % tiles_per_seq != tiles_per_seq - 1).astype(F32)
    x = x_ref[...]
    shift = mod_ref[3:4, :]
    scale = mod_ref[4:5, :]
    gain = gpre_ref[...]
    h = _norm_modulate(x, gain, shift, scale)
    hp = _norm_modulate(xp_ref[...], gain, shift, scale) * has_prev
    hn = _norm_modulate(xn_ref[...], gain, shift, scale) * has_next
    h_ext = jnp.concatenate([hp, h, hn], axis=0).astype(BF16)
    h_mid = h.astype(BF16)
    n_ext = tm + 2 * halo
    for c in range(D_FF // FFN_CHUNK):
        cols = slice(c * FFN_CHUNK, (c + 1) * FFN_CHUNK)
        u = jnp.dot(h_ext, wu_ref[:, cols], preferred_element_type=F32)
        gate = jnp.dot(h_mid, wgate_ref[:, cols], preferred_element_type=F32)
        u_prev = pltpu.roll(u, 1, 0)[halo:halo + tm]
        u_next = pltpu.roll(u, n_ext - 1, 0)[halo:halo + tm]
        conv = (u_prev * cw_ref[0:1, cols] + u[halo:halo + tm] * cw_ref[1:2, cols]
                + u_next * cw_ref[2:3, cols] + cb_ref[:, cols])
        act_ref[:, cols] = (conv * jax.nn.sigmoid(conv) * gate).astype(BF16)
    y = jnp.dot(act_ref[...], wd_ref[...], preferred_element_type=F32)
    o_ref[...] = x + mod_ref[5:6, :] * _post_norm(y, gpost_ref[...])


def _convffn(x1, mods, g_pre, g_post, w_up, conv_w, conv_b, w_down, seq):
    n = x1.shape[0]
    tm = TOKEN_TILE
    tiles_per_seq = seq // tm
    halo_blocks_per_tile = tm // SUBLANES
    n_halo_blocks = n // SUBLANES
    return pl.pallas_call(
        functools.partial(_convffn_kernel, tiles_per_seq=tiles_per_seq),
        grid=(n // tm,),
        in_specs=[
            pl.BlockSpec((tm, D_MODEL), lambda i: (i, 0)),
            pl.BlockSpec((SUBLANES, D_MODEL),
                         lambda i: (jnp.maximum(i * halo_blocks_per_tile - 1, 0), 0)),
            pl.BlockSpec((SUBLANES, D_MODEL),
                         lambda i: (jnp.minimum((i + 1) * halo_blocks_per_tile, n_halo_blocks - 1), 0)),
            pl.BlockSpec((None, 6, D_MODEL), lambda i: (i // tiles_per_seq, 0, 0)),
            _const_spec((1, D_MODEL)),
            _const_spec((1, D_MODEL)),
            pl.BlockSpec((D_MODEL, D_FF), lambda i: (0, 0), pipeline_mode=pl.Buffered(1)),
            pl.BlockSpec((D_MODEL, D_FF), lambda i: (0, 1), pipeline_mode=pl.Buffered(1)),
            _const_spec(conv_w.shape),
            _const_spec((1, D_FF)),
            _const_spec(w_down.shape),
        ],
        out_specs=pl.BlockSpec((tm, D_MODEL), lambda i: (i, 0)),
        out_shape=jax.ShapeDtypeStruct((n, D_MODEL), F32),
        scratch_shapes=[pltpu.VMEM((tm, D_FF), BF16)],
        compiler_params=_params(1),
        name="convffn",
    )(x1, x1, x1, mods, g_pre, g_post, w_up, w_up, conv_w, conv_b.reshape(1, D_FF), w_down)


def _rope_tables(seq):
    half = HEAD_DIM // 2
    inv_freq = ROPE_THETA ** (-jnp.arange(0, half, 2, dtype=F32) / half)
    ang_r = jnp.arange(seq // GRID_W).astype(F32)[:, None] * inv_freq
    ang_c = jnp.arange(GRID_W).astype(F32)[:, None] * inv_freq
    reps = LANES // HEAD_DIM

    def lanes(row_part, col_part):
        return jnp.tile(jnp.concatenate([row_part, col_part], axis=-1), (1, reps))

    zr = jnp.zeros((ang_r.shape[0], half), F32)
    zc = jnp.zeros((ang_c.shape[0], half), F32)
    rcos = lanes(jnp.concatenate([jnp.cos(ang_r), jnp.cos(ang_r)], axis=-1), zr)
    rsin = lanes(jnp.concatenate([-jnp.sin(ang_r), jnp.sin(ang_r)], axis=-1), zr)
    ccos = lanes(zc, jnp.concatenate([jnp.cos(ang_c), jnp.cos(ang_c)], axis=-1))
    csin = lanes(zc, jnp.concatenate([-jnp.sin(ang_c), jnp.sin(ang_c)], axis=-1))
    return rcos, rsin, ccos, csin


def kernel(x, c, ctx, c_ctx, w_mod, b_mod, g_pre1, g_post1, g_pre2, g_post2,
           w_in, sink, w_pa, w_pf, w_out, w_up, conv_w, conv_b, w_down):
    batch, seq, d = x.shape
    ctx_len = ctx.shape[1]
    depth = w_mod.shape[0]
    assert depth == 1 and d == D_MODEL and batch + 1 <= SUBLANES
    assert seq % TOKEN_TILE == 0 and TOKEN_TILE % GRID_W == 0 and seq % (DFT_MINOR * SUBLANES) == 0
    assert seq % (ATTN_BLOCKS_PER_STEP * BLOCK) == 0
    n = batch * seq
    rope_tabs = _rope_tables(seq)
    tables = _dft_tables(seq)
    bias = _band_bias()

    l = 0
    cvecs = jnp.zeros((SUBLANES, d), F32).at[:batch].set(c).at[batch].set(c_ctx)
    mod_all = _adaln(cvecs, w_mod[l], b_mod[l])
    mods = mod_all[:batch].reshape(batch, 6, d)
    mods_ctx = mod_all[batch].reshape(6, d)

    g_pre1_row = g_pre1[l].reshape(1, d)
    g_post1_row = g_post1[l].reshape(1, d)
    g_pre2_row = g_pre2[l].reshape(1, d)
    g_post2_row = g_post2[l].reshape(1, d)

    x2 = x.reshape(n, d)
    q, k, v_t, f = _inproj(x2, mods, g_pre1_row, w_in[l], rope_tabs, seq)
    kx, vx_t = _ctxkv(ctx.reshape(batch * ctx_len, d), mods_ctx, g_pre1_row, w_in[l], ctx_len)
    attn, four = _mix(sink[l], q, k, v_t, kx, vx_t, bias, f.reshape(batch, seq, FOURIER_W), tables, ctx_len)
    four = four.reshape(n, FOURIER_W)
    x1, w_up_b, w_down_b = _merge(x2, attn, four, mods, g_pre1_row, g_post1_row, w_in[l], w_pa[l], w_pf[l],
                                  w_out[l], w_up[l], w_down[l], seq)
    out = _convffn(x1, mods, g_pre2_row, g_post2_row, w_up_b, conv_w[l], conv_b[l], w_down_b, seq)
    return out.reshape(batch, seq, d)
```

```python
import functools
import math

import numpy as np
import jax
import jax.numpy as jnp
from jax import lax
from jax.experimental import pallas as pl
from jax.experimental.pallas import tpu as pltpu

F32 = jnp.float32
BF16 = jnp.bfloat16

D_MODEL = 1024
GRID_W = 64
HEAD_DIM = 64
N_Q_HEADS = 8
N_KV_HEADS = 2
GROUP = N_Q_HEADS // N_KV_HEADS
WINDOW = 128
BLOCK = 128
ROPE_THETA = 10000.0
N_FOURIER_GROUPS = 4
FOURIER_GROUP_CH = 128
FOURIER_W = N_FOURIER_GROUPS * FOURIER_GROUP_CH
Q_W = N_Q_HEADS * HEAD_DIM
KV_W = N_KV_HEADS * HEAD_DIM
D_FF = 2816
EPS = 1e-6
NEG = -1e30
LOG2E = math.log2(math.e)

LANES = 128
SUBLANES = 8
BF16_ROWS = 16
MXU_DIM = 256
VMEM_LIMIT_BYTES = 56 * 1024 * 1024

TOKEN_TILE = 512
FFN_TOKEN_TILE = 1024
ROW_PARTS = 2
FFN_CHUNK = MXU_DIM
DFT_MINOR = SUBLANES
ATTN_BLOCKS_PER_STEP = 16


def _const_spec(shape):
    nd = len(shape)
    return pl.BlockSpec(shape, lambda *_: (0,) * nd, pipeline_mode=pl.Buffered(1))


def _params(n_axes):
    return pltpu.CompilerParams(
        dimension_semantics=("arbitrary",) * n_axes,
        vmem_limit_bytes=VMEM_LIMIT_BYTES,
    )


def _norm_modulate(x, gain, shift, scale):
    ms = jnp.mean(x * x, axis=-1, keepdims=True)
    return (x * lax.rsqrt(ms + EPS)) * (gain * (1.0 + scale)) + shift


def _post_norm(y, gain):
    ms = jnp.mean(y * y, axis=-1, keepdims=True)
    return (y * lax.rsqrt(ms + EPS)) * gain


def _adaln_kernel(c_ref, w_ref, b_ref, o_ref):
    c = c_ref[...]
    s = c * jax.nn.sigmoid(c)
    o_ref[...] = jnp.dot(s, w_ref[...], preferred_element_type=F32) + b_ref[...]


def _adaln(cvecs, w_mod, b_mod):
    n_out = w_mod.shape[1]
    tn = 1536
    return pl.pallas_call(
        _adaln_kernel,
        grid=(n_out // tn,),
        in_specs=[
            pl.BlockSpec((SUBLANES, D_MODEL), lambda j: (0, 0)),
            pl.BlockSpec((D_MODEL, tn), lambda j: (0, j)),
            pl.BlockSpec((1, tn), lambda j: (0, j)),
        ],
        out_specs=pl.BlockSpec((SUBLANES, tn), lambda j: (0, j)),
        out_shape=jax.ShapeDtypeStruct((SUBLANES, n_out), F32),
        compiler_params=_params(1),
        name="adaln",
    )(cvecs, w_mod, b_mod.reshape(1, n_out))


def _rope(xb, cos_t, sin_t, first_half):
    sw = jnp.where(first_half, pltpu.roll(xb, LANES - 16, 1), pltpu.roll(xb, 16, 1))
    return xb * cos_t + sw * sin_t


def _token_table(row_tab, col_tab):
    n_rows = row_tab.shape[0]
    by_row = jnp.concatenate(
        [jnp.broadcast_to(row_tab[r:r + 1, :], (GRID_W, LANES)) for r in range(n_rows)], axis=0)
    return by_row + jnp.concatenate([col_tab] * n_rows, axis=0)


def _inproj_kernel(x_ref, mod_ref, g_ref, w_ref, rcos_ref, rsin_ref, ccos_ref, csin_ref,
                   q_ref, k_ref, v_ref, f_ref, wb_ref):
    @pl.when(pl.program_id(0) == 0)
    def _():
        low = lax.broadcasted_iota(jnp.int32, (D_MODEL, LANES), 1) < HEAD_DIM
        for j in range(Q_W // LANES):
            src_a = (j // 2) * LANES
            src_b = (GROUP // 2 + j // 2) * LANES
            a = w_ref[:, src_a:src_a + LANES]
            b = w_ref[:, src_b:src_b + LANES]
            if j % 2 == 1:
                a = pltpu.roll(a, HEAD_DIM, 1)
            else:
                b = pltpu.roll(b, HEAD_DIM, 1)
            wb_ref[:, j * LANES:(j + 1) * LANES] = jnp.where(low, a, b).astype(BF16)
        wb_ref[:, Q_W:] = w_ref[:, Q_W:].astype(BF16)

    scale = HEAD_DIM ** -0.5 * LOG2E
    rows_per_part = x_ref.shape[0] // ROW_PARTS
    grid_rows_per_part = rows_per_part // GRID_W
    for part in range(ROW_PARTS):
        rows = slice(part * rows_per_part, (part + 1) * rows_per_part)
        grows = slice(part * grid_rows_per_part, (part + 1) * grid_rows_per_part)
        h = _norm_modulate(x_ref[rows, :], g_ref[...], mod_ref[0:1, :], mod_ref[1:2, :])
        p = jnp.dot(h.astype(BF16), wb_ref[...], preferred_element_type=F32)
        cos_t = _token_table(rcos_ref[grows, :], ccos_ref[...])
        sin_t = _token_table(rsin_ref[grows, :], csin_ref[...])
        lane = lax.broadcasted_iota(jnp.int32, cos_t.shape, 1)
        first_half = (lane % 32) < 16
        for b in range(Q_W // LANES):
            blk = p[:, b * LANES:(b + 1) * LANES]
            q_ref[rows, b * LANES:(b + 1) * LANES] = (_rope(blk, cos_t, sin_t, first_half) * scale).astype(BF16)
        k_ref[rows, :] = _rope(p[:, Q_W:Q_W + KV_W], cos_t, sin_t, first_half).astype(BF16)
        v_ref[:, rows] = p[:, Q_W + KV_W:Q_W + 2 * KV_W].T.astype(BF16)
        f_ref[rows, :] = p[:, Q_W + 2 * KV_W:]


def _inproj(x2, mods, g_pre, w_in, rope_tabs, seq):
    n = x2.shape[0]
    tm = TOKEN_TILE
    tiles_per_seq = seq // tm
    rows_per_tile = tm // GRID_W
    n_cols = Q_W + 2 * KV_W + FOURIER_W
    row_spec = pl.BlockSpec((rows_per_tile, LANES), lambda i: (i % tiles_per_seq, 0))
    return pl.pallas_call(
        _inproj_kernel,
        grid=(n // tm,),
        in_specs=[
            pl.BlockSpec((tm, D_MODEL), lambda i: (i, 0)),
            pl.BlockSpec((None, 6, D_MODEL), lambda i: (i // tiles_per_seq, 0, 0)),
            _const_spec((1, D_MODEL)),
            _const_spec((D_MODEL, n_cols)),
            row_spec, row_spec,
            _const_spec((GRID_W, LANES)), _const_spec((GRID_W, LANES)),
        ],
        out_specs=[
            pl.BlockSpec((tm, Q_W), lambda i: (i, 0)),
            pl.BlockSpec((tm, KV_W), lambda i: (i, 0)),
            pl.BlockSpec((KV_W, tm), lambda i: (0, i)),
            pl.BlockSpec((tm, FOURIER_W), lambda i: (i, 0)),
        ],
        out_shape=[
            jax.ShapeDtypeStruct((n, Q_W), BF16),
            jax.ShapeDtypeStruct((n, KV_W), BF16),
            jax.ShapeDtypeStruct((KV_W, n), BF16),
            jax.ShapeDtypeStruct((n, FOURIER_W), F32),
        ],
        scratch_shapes=[pltpu.VMEM((D_MODEL, n_cols), BF16)],
        compiler_params=_params(1),
        name="inproj",
    )(x2, mods, g_pre, w_in, *rope_tabs)


def _ctxkv_kernel(x_ref, mod_ref, g_ref, w_ref, k_ref, v_ref):
    h = _norm_modulate(x_ref[...], g_ref[...], mod_ref[0:1, :], mod_ref[1:2, :])
    p = jnp.dot(h.astype(BF16), w_ref[...].astype(BF16), preferred_element_type=F32)
    k_ref[...] = p[:, :KV_W].astype(BF16)
    v_ref[...] = p[:, KV_W:].T.astype(BF16)


def _ctxkv(ctx2, mods_ctx, g_pre, w_in, ctx_len):
    n = ctx2.shape[0]
    assert Q_W % (2 * KV_W) == 0
    return pl.pallas_call(
        _ctxkv_kernel,
        grid=(n // ctx_len,),
        in_specs=[
            pl.BlockSpec((ctx_len, D_MODEL), lambda i: (i, 0)),
            _const_spec((6, D_MODEL)),
            _const_spec((1, D_MODEL)),
            pl.BlockSpec((D_MODEL, 2 * KV_W), lambda i: (0, Q_W // (2 * KV_W)),
                         pipeline_mode=pl.Buffered(1)),
        ],
        out_specs=[
            pl.BlockSpec((ctx_len, KV_W), lambda i: (i, 0)),
            pl.BlockSpec((KV_W, ctx_len), lambda i: (0, i)),
        ],
        out_shape=[
            jax.ShapeDtypeStruct((n, KV_W), BF16),
            jax.ShapeDtypeStruct((KV_W, n), BF16),
        ],
        compiler_params=_params(1),
        name="ctxkv",
    )(ctx2, mods_ctx, g_pre, w_in)


def _attend_block(sink_ref, q_blk, k_parts, v_parts, bias_prev, bias_next):
    kcat = jnp.concatenate(k_parts, axis=0)
    vcat_t = jnp.concatenate(v_parts, axis=1)
    n_keys = kcat.shape[0]
    low = lax.broadcasted_iota(jnp.int32, (BLOCK, LANES), 1) < HEAD_DIM
    top = lax.broadcasted_iota(jnp.int32, (KV_W, n_keys), 0) < HEAD_DIM
    zero = jnp.zeros((), BF16)
    one = jnp.ones((), BF16)
    q_rows = [jnp.where(low if h == 0 else jnp.logical_not(low), q_blk[:, g * LANES:(g + 1) * LANES], zero)
              for h in range(N_KV_HEADS) for g in range(GROUP)]
    s_all = lax.dot_general(kcat, jnp.concatenate(q_rows, axis=0), (((1,), (1,)), ((), ())),
                            preferred_element_type=F32)
    outs = []
    for h in range(N_KV_HEADS):
        v_h = jnp.where(top if h == 0 else jnp.logical_not(top), vcat_t, one)
        for g0 in range(0, GROUP, 2):
            probs = []
            sinks = []
            for g in (g0, g0 + 1):
                head = h * GROUP + g
                s = s_all[:, head * LANES:(head + 1) * LANES]
                parts = [s[0:BLOCK] + bias_prev,
                         s[BLOCK:2 * BLOCK],
                         s[2 * BLOCK:3 * BLOCK] + bias_next]
                parts += [s[r:r + BLOCK] for r in range(3 * BLOCK, n_keys, BLOCK)]
                mx = parts[0]
                for part in parts[1:]:
                    mx = jnp.maximum(mx, part)
                sink = sink_ref[head] * LOG2E
                m = jnp.maximum(jnp.max(mx, axis=0, keepdims=True), sink)
                probs.append(jnp.concatenate([jnp.exp2(part - m) for part in parts], axis=0).astype(BF16))
                sinks.append(jnp.exp2(sink - m))
            o2 = jnp.dot(v_h, jnp.concatenate(probs, axis=1), preferred_element_type=F32)
            for idx in range(2):
                o_t = o2[:, idx * LANES:(idx + 1) * LANES]
                num = o_t[0:HEAD_DIM] if h == 0 else o_t[HEAD_DIM:]
                den = (o_t[HEAD_DIM:] if h == 0 else o_t[0:HEAD_DIM]) + sinks[idx]
                outs.append(num / den)
    return outs


def _attention_steps(sink_ref, q_ref, kp_ref, kc_ref, kn_ref, kx_ref, vp_ref, vc_ref, vn_ref, vx_ref,
                     bias_ref, o_ref):
    i = pl.program_id(1)
    n_sub = q_ref.shape[0] // BLOCK
    first_var = jnp.where(i == 0, 0, 1)
    last_var = jnp.where(i == pl.num_programs(1) - 1, 2, 1)

    def block_step(t):
        own = slice(t * BLOCK, (t + 1) * BLOCK)
        before = slice((t - 1) * BLOCK, t * BLOCK)
        after = slice((t + 1) * BLOCK, (t + 2) * BLOCK)
        k_parts = [kp_ref[...] if t == 0 else kc_ref[before, :], kc_ref[own, :],
                   kn_ref[...] if t == n_sub - 1 else kc_ref[after, :], kx_ref[...]]
        v_parts = [vp_ref[...] if t == 0 else vc_ref[:, before], vc_ref[:, own],
                   vn_ref[...] if t == n_sub - 1 else vc_ref[:, after], vx_ref[...]]
        bias_prev = bias_ref[first_var if t == 0 else 1, 0:BLOCK, :]
        bias_next = bias_ref[last_var if t == n_sub - 1 else 1, BLOCK:2 * BLOCK, :]
        outs = _attend_block(sink_ref, q_ref[own, :], k_parts, v_parts, bias_prev, bias_next)
        for j in range(Q_W // LANES):
            o_t = jnp.concatenate([outs[2 * j], outs[2 * j + 1]], axis=0)
            o_ref[own, j * LANES:(j + 1) * LANES] = o_t.T.astype(BF16)

    return [functools.partial(block_step, t) for t in range(n_sub)]


def _attention_specs(batch, seq, ctx_len, bias_shape):
    nb = seq // BLOCK
    sub = ATTN_BLOCKS_PER_STEP
    steps = nb // sub

    def cur(b, i):
        return (b * steps + i, 0)

    def prev(b, i):
        return (b * nb + jnp.maximum(i * sub - 1, 0), 0)

    def nxt(b, i):
        return (b * nb + jnp.minimum((i + 1) * sub, nb - 1), 0)

    def swap(f):
        return lambda b, i: f(b, i)[::-1]

    in_specs = [
        pl.BlockSpec(memory_space=pltpu.SMEM),
        pl.BlockSpec((sub * BLOCK, Q_W), cur),
        pl.BlockSpec((BLOCK, KV_W), prev),
        pl.BlockSpec((sub * BLOCK, KV_W), cur),
        pl.BlockSpec((BLOCK, KV_W), nxt),
        pl.BlockSpec((ctx_len, KV_W), lambda b, i: (b, 0)),
        pl.BlockSpec((KV_W, BLOCK), swap(prev)),
        pl.BlockSpec((KV_W, sub * BLOCK), swap(cur)),
        pl.BlockSpec((KV_W, BLOCK), swap(nxt)),
        pl.BlockSpec((KV_W, ctx_len), lambda b, i: (0, b)),
        _const_spec(bias_shape),
    ]
    return steps, in_specs, pl.BlockSpec((sub * BLOCK, Q_W), cur)


def _band_bias():
    qi = np.arange(BLOCK)[None, :]
    kj = np.arange(3 * BLOCK)[:, None]
    rel = kj - BLOCK - qi
    in_window = np.abs(rel) <= WINDOW
    variants = []
    for has_prev, has_next in ((False, True), (True, True), (True, False)):
        ok = in_window.copy()
        if not has_prev:
            ok &= kj >= BLOCK
        if not has_next:
            ok &= kj < 2 * BLOCK
        mask = np.where(ok, 0.0, NEG).astype(np.float32)
        variants.append(np.concatenate([mask[:BLOCK], mask[2 * BLOCK:]], axis=0))
    return jnp.asarray(np.stack(variants))


def _minor_dft_terms():
    assert DFT_MINOR == 8
    plan = []
    for k2 in range(DFT_MINOR):
        unit, half = [], []
        for s2 in range(DFT_MINOR):
            ang = -2.0 * math.pi * ((s2 * k2) % DFT_MINOR) / DFT_MINOR
            for part, coef in ((0, math.cos(ang)), (1, -math.sin(ang))):
                if abs(coef) < 1e-9:
                    continue
                target = unit if abs(abs(coef) - 1.0) < 1e-9 else half
                assert target is unit or abs(abs(coef) - math.sqrt(0.5)) < 1e-9
                target.append((s2, part, 1 if coef > 0 else -1))
        plan.append((unit, half))
    return plan


def _signed_sum(terms, pick):
    pos = [pick(s2, part) for s2, part, sign in terms if sign > 0]
    neg = [pick(s2, part) for s2, part, sign in terms if sign < 0]
    total = None
    for v in pos:
        total = v if total is None else total + v
    for v in neg:
        total = -v if total is None else total - v
    return total


def _fourier_steps(f_ref, f1_ref, tr_ref, ti_ref, cm_ref, o_ref):
    n_major = f_ref.shape[0] // DFT_MINOR
    pair_w = 2 * LANES
    n_pairs = DFT_MINOR // 2
    row_chunk = n_major // n_pairs
    twiddled = [None] * n_pairs
    spectra = [None] * DFT_MINOR
    plan = _minor_dft_terms()

    def stage1(c):
        y = jnp.concatenate(
            [f_ref[pl.ds(s2, n_major, stride=DFT_MINOR), :].astype(BF16) for s2 in (2 * c, 2 * c + 1)],
            axis=1)
        a = jnp.dot(f1_ref[...], y, preferred_element_type=F32)
        ar = a[:n_major]
        ai = a[n_major:]
        tr = tr_ref[:, c * pair_w:(c + 1) * pair_w]
        ti = ti_ref[:, c * pair_w:(c + 1) * pair_w]
        twiddled[c] = ((ar * tr - ai * ti).astype(BF16), (ar * ti + ai * tr).astype(BF16))

    def channels(c):
        apr, api = twiddled[c]
        for t in range(2):
            lanes = slice(t * LANES, (t + 1) * LANES)
            ap = jnp.concatenate([apr[:, lanes], api[:, lanes]], axis=1)
            spectra[2 * c + t] = jnp.dot(ap, cm_ref[...], preferred_element_type=F32)

    def stage2(r):
        rows = slice(r * row_chunk, (r + 1) * row_chunk)

        def pick(s2, part):
            return spectra[s2][rows, part * LANES:(part + 1) * LANES]

        for k2, (unit, half) in enumerate(plan):
            res = _signed_sum(unit, pick)
            if half:
                scaled = _signed_sum(half, pick) * math.sqrt(0.5)
                res = scaled if res is None else res + scaled
            o_ref[pl.ds(k2 * n_major + r * row_chunk, row_chunk), :] = res.astype(BF16)

    steps = [functools.partial(stage1, 0)]
    for c in range(1, n_pairs):
        steps += [functools.partial(stage1, c), functools.partial(channels, c - 1)]
    steps.append(functools.partial(channels, n_pairs - 1))
    return steps + [functools.partial(stage2, r) for r in range(n_pairs)]


def _cis(num, den):
    ang = (num % den).astype(F32) * F32(-2.0 * math.pi / den)
    return jnp.cos(ang), jnp.sin(ang)


def _dft_tables(seq):
    n_major = seq // DFT_MINOR
    ch = FOURIER_GROUP_CH
    root = int(round(math.sqrt(n_major)))
    assert root * root == n_major
    s1 = jnp.arange(n_major, dtype=jnp.int32)[None, :]
    hi = jnp.arange(root, dtype=jnp.int32)[:, None] * root
    lo = jnp.arange(root, dtype=jnp.int32)[:, None]
    hr, hi_ = _cis(hi * s1, n_major)
    lr, li = _cis(lo * s1, n_major)
    f1r = (hr[:, None, :] * lr[None, :, :] - hi_[:, None, :] * li[None, :, :]).reshape(n_major, n_major)
    f1i = (hr[:, None, :] * li[None, :, :] + hi_[:, None, :] * lr[None, :, :]).reshape(n_major, n_major)
    f1 = jnp.concatenate([f1r, f1i], axis=0).astype(BF16)
    k1 = jnp.arange(n_major, dtype=jnp.int32)[:, None]
    s2 = jnp.arange(DFT_MINOR, dtype=jnp.int32)[None, :]
    tr, ti = _cis(k1 * s2, seq)
    tr = jnp.broadcast_to(tr[:, :, None], (n_major, DFT_MINOR, ch)).reshape(n_major, DFT_MINOR * ch)
    ti = jnp.broadcast_to(ti[:, :, None], (n_major, DFT_MINOR, ch)).reshape(n_major, DFT_MINOR * ch)
    c128 = jnp.arange(ch, dtype=jnp.int32)
    cr, ci = _cis(c128[:, None] * c128[None, :], ch)
    norm = F32(1.0 / math.sqrt(seq * ch))
    cm = (jnp.concatenate([jnp.concatenate([cr, ci], axis=1),
                           jnp.concatenate([-ci, cr], axis=1)], axis=0) * norm).astype(BF16)
    return f1, tr, ti, cm


def _mix_kernel(sink_ref, q_ref, kp_ref, kc_ref, kn_ref, kx_ref, vp_ref, vc_ref, vn_ref, vx_ref, bias_ref,
                f_ref, f1_ref, tr_ref, ti_ref, cm_ref, attn_ref, four_ref):
    attn_steps = _attention_steps(sink_ref, q_ref, kp_ref, kc_ref, kn_ref, kx_ref,
                                  vp_ref, vc_ref, vn_ref, vx_ref, bias_ref, attn_ref)
    four_steps = _fourier_steps(f_ref, f1_ref, tr_ref, ti_ref, cm_ref, four_ref)
    done = 0
    for n, four_step in enumerate(four_steps):
        upto = (n + 1) * len(attn_steps) // len(four_steps)
        for attn_step in attn_steps[done:upto]:
            attn_step()
        done = upto
        four_step()


def _mix(sink, q, k, v_t, kx, vx_t, bias, f3, tables, ctx_len):
    batch, seq, _ = f3.shape
    f1, tr, ti, cm = tables
    ch = FOURIER_GROUP_CH
    steps, attn_in, attn_out = _attention_specs(batch, seq, ctx_len, bias.shape)
    assert steps == N_FOURIER_GROUPS
    four_spec = pl.BlockSpec((None, seq, ch), lambda b, g: (b, 0, g))
    return pl.pallas_call(
        _mix_kernel,
        grid=(batch, steps),
        in_specs=attn_in + [four_spec, _const_spec(f1.shape), _const_spec(tr.shape),
                            _const_spec(ti.shape), _const_spec(cm.shape)],
        out_specs=[attn_out, four_spec],
        out_shape=[jax.ShapeDtypeStruct((batch * seq, Q_W), BF16),
                   jax.ShapeDtypeStruct((batch, seq, FOURIER_W), BF16)],
        compiler_params=_params(2),
        name="mix",
    )(sink, q, k, k, k, kx, v_t, v_t, v_t, vx_t, bias, f3, f1, tr, ti, cm)


def _merge_kernel(x_ref, a_ref, f_ref, mod_ref, gpre_ref, gpost_ref,
                  wg_ref, wpa_ref, wpf_ref, wout_ref, wup_ref, wdown_ref,
                  o_ref, wup_o, wdown_o, wg_b, wpa_b, wpf_b, wout_b):
    wup_o[...] = wup_ref[...].astype(BF16)
    wdown_o[...] = wdown_ref[...].astype(BF16)

    @pl.when(pl.program_id(0) == 0)
    def _():
        wg_b[...] = wg_ref[...].astype(BF16)
        wpa_b[...] = wpa_ref[...].astype(BF16)
        wpf_b[...] = wpf_ref[...].astype(BF16)
        wout_b[...] = wout_ref[...].astype(BF16)

    rows_per_part = x_ref.shape[0] // ROW_PARTS
    for part in range(ROW_PARTS):
        rows = slice(part * rows_per_part, (part + 1) * rows_per_part)
        x = x_ref[rows, :]
        h = _norm_modulate(x, gpre_ref[...], mod_ref[0:1, :], mod_ref[1:2, :]).astype(BF16)
        gates = jnp.dot(h, wg_b[...], preferred_element_type=F32)
        pa = jnp.dot(a_ref[rows, :], wpa_b[...], preferred_element_type=F32)
        pf = jnp.dot(f_ref[rows, :], wpf_b[...], preferred_element_type=F32)
        m = jax.nn.sigmoid(gates[:, :D_MODEL]) * pa + jax.nn.sigmoid(gates[:, D_MODEL:]) * pf
        y = jnp.dot(m.astype(BF16), wout_b[...], preferred_element_type=F32)
        o_ref[rows, :] = x + mod_ref[2:3, :] * _post_norm(y, gpost_ref[...])


def _merge(x2, attn, four, mods, g_pre, g_post, w_in, w_pa, w_pf, w_out, w_up, w_down, seq):
    n = x2.shape[0]
    tm = TOKEN_TILE
    n_steps = n // tm
    tiles_per_seq = seq // tm
    row = lambda i: (i, 0)
    gate_col0 = Q_W + 2 * KV_W + FOURIER_W
    up_rows = w_up.shape[0] // n_steps
    down_rows = w_down.shape[0] // BF16_ROWS
    n_down = w_down.shape[0] // down_rows
    assert up_rows * n_steps == w_up.shape[0] and up_rows % BF16_ROWS == 0
    assert down_rows % BF16_ROWS == 0 and n_down <= n_steps
    up_spec = pl.BlockSpec((up_rows, w_up.shape[1]), row)
    down_spec = pl.BlockSpec((down_rows, w_down.shape[1]), lambda i: (jnp.minimum(i, n_down - 1), 0))
    return pl.pallas_call(
        _merge_kernel,
        grid=(n // tm,),
        in_specs=[
            pl.BlockSpec((tm, D_MODEL), row),
            pl.BlockSpec((tm, Q_W), row),
            pl.BlockSpec((tm, FOURIER_W), row),
            pl.BlockSpec((None, 6, D_MODEL), lambda i: (i // tiles_per_seq, 0, 0)),
            _const_spec((1, D_MODEL)),
            _const_spec((1, D_MODEL)),
            pl.BlockSpec((pl.Element(D_MODEL), pl.Element(2 * D_MODEL)),
                         lambda i: (0, gate_col0),
                         pipeline_mode=pl.Buffered(1)),
            _const_spec(w_pa.shape),
            _const_spec(w_pf.shape),
            _const_spec(w_out.shape),
            up_spec,
            down_spec,
        ],
        out_specs=[pl.BlockSpec((tm, D_MODEL), row), up_spec, down_spec],
        out_shape=[jax.ShapeDtypeStruct((n, D_MODEL), F32),
                   jax.ShapeDtypeStruct(w_up.shape, BF16),
                   jax.ShapeDtypeStruct(w_down.shape, BF16)],
        scratch_shapes=[pltpu.VMEM((D_MODEL, 2 * D_MODEL), BF16), pltpu.VMEM(w_pa.shape, BF16),
                        pltpu.VMEM(w_pf.shape, BF16), pltpu.VMEM(w_out.shape, BF16)],
        compiler_params=_params(1),
        name="merge",
    )(x2, attn, four, mods, g_pre, g_post, w_in, w_pa, w_pf, w_out, w_up, w_down)


def _convffn_kernel(x_ref, xp_ref, xn_ref, mod_ref, gpre_ref, gpost_ref,
                    wu_ref, wgate_ref, cw_ref, cb_ref, wd_ref, o_ref, act_ref, *, tiles_per_seq):
    i = pl.program_id(0)
    tm = x_ref.shape[0]
    halo = SUBLANES
    has_prev = (i % tiles_per_seq != 0).astype(F32)
    has_next = (i % tiles_per_seq != tiles_per_seq - 1).astype(F32)
    x = x_ref[...]
    shift = mod_ref[3:4, :]
    scale = mod_ref[4:5, :]
    gain = gpre_ref[...]
    h = _norm_modulate(x, gain, shift, scale)
    hp = _norm_modulate(xp_ref[...], gain, shift, scale) * has_prev
    hn = _norm_modulate(xn_ref[...], gain, shift, scale) * has_next
    h_ext = jnp.concatenate([hp, h, hn], axis=0).astype(BF16)
    h_mid = h.astype(BF16)
    n_ext = tm + 2 * halo
    for c in range(D_FF // FFN_CHUNK):
        cols = slice(c * FFN_CHUNK, (c + 1) * FFN_CHUNK)
        u = jnp.dot(h_ext, wu_ref[:, cols], preferred_element_type=F32)
        gate = jnp.dot(h_mid, wgate_ref[:, cols], preferred_element_type=F32)
        u_prev = pltpu.roll(u, 1, 0)[halo:halo + tm]
        u_next = pltpu.roll(u, n_ext - 1, 0)[halo:halo + tm]
        conv = (u_prev * cw_ref[0:1, cols] + u[halo:halo + tm] * cw_ref[1:2, cols]
                + u_next * cw_ref[2:3, cols] + cb_ref[:, cols])
        act_ref[:, cols] = (conv * jax.nn.sigmoid(conv) * gate).astype(BF16)
    y = jnp.dot(act_ref[...], wd_ref[...], preferred_element_type=F32)
    o_ref[...] = x + mod_ref[5:6, :] * _post_norm(y, gpost_ref[...])


def _convffn(x1, mods, g_pre, g_post, w_up, conv_w, conv_b, w_down, seq):
    n = x1.shape[0]
    tm = FFN_TOKEN_TILE
    tiles_per_seq = seq // tm
    halo_blocks_per_tile = tm // SUBLANES
    n_halo_blocks = n // SUBLANES
    return pl.pallas_call(
        functools.partial(_convffn_kernel, tiles_per_seq=tiles_per_seq),
        grid=(n // tm,),
        in_specs=[
            pl.BlockSpec((tm, D_MODEL), lambda i: (i, 0)),
            pl.BlockSpec((SUBLANES, D_MODEL),
                         lambda i: (jnp.maximum(i * halo_blocks_per_tile - 1, 0), 0)),
            pl.BlockSpec((SUBLANES, D_MODEL),
                         lambda i: (jnp.minimum((i + 1) * halo_blocks_per_tile, n_halo_blocks - 1), 0)),
            pl.BlockSpec((None, 6, D_MODEL), lambda i: (i // tiles_per_seq, 0, 0)),
            _const_spec((1, D_MODEL)),
            _const_spec((1, D_MODEL)),
            pl.BlockSpec((D_MODEL, D_FF), lambda i: (0, 0), pipeline_mode=pl.Buffered(1)),
            pl.BlockSpec((D_MODEL, D_FF), lambda i: (0, 1), pipeline_mode=pl.Buffered(1)),
            _const_spec(conv_w.shape),
            _const_spec((1, D_FF)),
            _const_spec(w_down.shape),
        ],
        out_specs=pl.BlockSpec((tm, D_MODEL), lambda i: (i, 0)),
        out_shape=jax.ShapeDtypeStruct((n, D_MODEL), F32),
        scratch_shapes=[pltpu.VMEM((tm, D_FF), BF16)],
        compiler_params=_params(1),
        name="convffn",
    )(x1, x1, x1, mods, g_pre, g_post, w_up, w_up, conv_w, conv_b.reshape(1, D_FF), w_down)


def _rope_tables(seq):
    half = HEAD_DIM // 2
    inv_freq = ROPE_THETA ** (-jnp.arange(0, half, 2, dtype=F32) / half)
    ang_r = jnp.arange(seq // GRID_W).astype(F32)[:, None] * inv_freq
    ang_c = jnp.arange(GRID_W).astype(F32)[:, None] * inv_freq
    reps = LANES // HEAD_DIM

    def lanes(row_part, col_part):
        return jnp.tile(jnp.concatenate([row_part, col_part], axis=-1), (1, reps))

    zr = jnp.zeros((ang_r.shape[0], half), F32)
    zc = jnp.zeros((ang_c.shape[0], half), F32)
    rcos = lanes(jnp.concatenate([jnp.cos(ang_r), jnp.cos(ang_r)], axis=-1), zr)
    rsin = lanes(jnp.concatenate([-jnp.sin(ang_r), jnp.sin(ang_r)], axis=-1), zr)
    ccos = lanes(zc, jnp.concatenate([jnp.cos(ang_c), jnp.cos(ang_c)], axis=-1))
    csin = lanes(zc, jnp.concatenate([-jnp.sin(ang_c), jnp.sin(ang_c)], axis=-1))
    return rcos, rsin, ccos, csin


def kernel(x, c, ctx, c_ctx, w_mod, b_mod, g_pre1, g_post1, g_pre2, g_post2,
           w_in, sink, w_pa, w_pf, w_out, w_up, conv_w, conv_b, w_down):
    batch, seq, d = x.shape
    ctx_len = ctx.shape[1]
    depth = w_mod.shape[0]
    assert depth == 1 and d == D_MODEL and batch + 1 <= SUBLANES
    assert seq % TOKEN_TILE == 0 and TOKEN_TILE % (ROW_PARTS * GRID_W) == 0 and seq % FFN_TOKEN_TILE == 0
    assert seq % (DFT_MINOR * SUBLANES) == 0
    assert seq % (ATTN_BLOCKS_PER_STEP * BLOCK) == 0
    n = batch * seq
    rope_tabs = _rope_tables(seq)
    tables = _dft_tables(seq)
    bias = _band_bias()

    l = 0
    cvecs = jnp.zeros((SUBLANES, d), F32).at[:batch].set(c).at[batch].set(c_ctx)
    mod_all = _adaln(cvecs, w_mod[l], b_mod[l])
    mods = mod_all[:batch].reshape(batch, 6, d)
    mods_ctx = mod_all[batch].reshape(6, d)

    g_pre1_row = g_pre1[l].reshape(1, d)
    g_post1_row = g_post1[l].reshape(1, d)
    g_pre2_row = g_pre2[l].reshape(1, d)
    g_post2_row = g_post2[l].reshape(1, d)

    x2 = x.reshape(n, d)
    q, k, v_t, f = _inproj(x2, mods, g_pre1_row, w_in[l], rope_tabs, seq)
    kx, vx_t = _ctxkv(ctx.reshape(batch * ctx_len, d), mods_ctx, g_pre1_row, w_in[l], ctx_len)
    attn, four = _mix(sink[l], q, k, v_t, kx, vx_t, bias, f.reshape(batch, seq, FOURIER_W), tables, ctx_len)
    four = four.reshape(n, FOURIER_W)
    x1, w_up_b, w_down_b = _merge(x2, attn, four, mods, g_pre1_row, g_post1_row, w_in[l], w_pa[l], w_pf[l],
                                  w_out[l], w_up[l], w_down[l], seq)
    out = _convffn(x1, mods, g_pre2_row, g_post2_row, w_up_b, conv_w[l], conv_b[l], w_down_b, seq)
    return out.reshape(batch, seq, d)
```

```python
import functools
import math

import numpy as np
import jax
import jax.numpy as jnp
from jax import lax
from jax.experimental import pallas as pl
from jax.experimental.pallas import tpu as pltpu

F32 = jnp.float32
BF16 = jnp.bfloat16

D_MODEL = 1024
GRID_W = 64
HEAD_DIM = 64
N_Q_HEADS = 8
N_KV_HEADS = 2
GROUP = N_Q_HEADS // N_KV_HEADS
WINDOW = 128
BLOCK = 128
ROPE_THETA = 10000.0
N_FOURIER_GROUPS = 4
FOURIER_GROUP_CH = 128
FOURIER_W = N_FOURIER_GROUPS * FOURIER_GROUP_CH
Q_W = N_Q_HEADS * HEAD_DIM
KV_W = N_KV_HEADS * HEAD_DIM
D_FF = 2816
EPS = 1e-6
NEG = -1e30
LOG2E = math.log2(math.e)

LANES = 128
SUBLANES = 8
BF16_ROWS = 16
MXU_DIM = 256
VMEM_LIMIT_BYTES = 56 * 1024 * 1024

TOKEN_TILE = 512
FFN_TOKEN_TILE = 1024
FFN_CHUNK = MXU_DIM
DFT_MINOR = SUBLANES
ATTN_BLOCKS_PER_STEP = 16


def _const_spec(shape):
    nd = len(shape)
    return pl.BlockSpec(shape, lambda *_: (0,) * nd, pipeline_mode=pl.Buffered(1))


def _params(n_axes):
    return pltpu.CompilerParams(
        dimension_semantics=("arbitrary",) * n_axes,
        vmem_limit_bytes=VMEM_LIMIT_BYTES,
    )


def _norm_modulate(x, gain, shift, scale):
    ms = jnp.mean(x * x, axis=-1, keepdims=True)
    return (x * lax.rsqrt(ms + EPS)) * (gain * (1.0 + scale)) + shift


def _post_norm(y, gain):
    ms = jnp.mean(y * y, axis=-1, keepdims=True)
    return (y * lax.rsqrt(ms + EPS)) * gain


def _adaln_kernel(c_ref, w_ref, b_ref, o_ref):
    c = c_ref[...]
    s = c * jax.nn.sigmoid(c)
    o_ref[...] = jnp.dot(s, w_ref[...], preferred_element_type=F32) + b_ref[...]


def _adaln(cvecs, w_mod, b_mod):
    n_out = w_mod.shape[1]
    tn = 1536
    return pl.pallas_call(
        _adaln_kernel,
        grid=(n_out // tn,),
        in_specs=[
            pl.BlockSpec((SUBLANES, D_MODEL), lambda j: (0, 0)),
            pl.BlockSpec((D_MODEL, tn), lambda j: (0, j)),
            pl.BlockSpec((1, tn), lambda j: (0, j)),
        ],
        out_specs=pl.BlockSpec((SUBLANES, tn), lambda j: (0, j)),
        out_shape=jax.ShapeDtypeStruct((SUBLANES, n_out), F32),
        compiler_params=_params(1),
        name="adaln",
    )(cvecs, w_mod, b_mod.reshape(1, n_out))


def _rope(xb, cos_t, sin_t, first_half):
    sw = jnp.where(first_half, pltpu.roll(xb, LANES - 16, 1), pltpu.roll(xb, 16, 1))
    return xb * cos_t + sw * sin_t


def _token_table(row_tab, col_tab):
    n_rows = row_tab.shape[0]
    by_row = jnp.concatenate(
        [jnp.broadcast_to(row_tab[r:r + 1, :], (GRID_W, LANES)) for r in range(n_rows)], axis=0)
    return by_row + jnp.concatenate([col_tab] * n_rows, axis=0)


def _inproj_kernel(x_ref, mod_ref, g_ref, w_ref, rcos_ref, rsin_ref, ccos_ref, csin_ref,
                   q_ref, k_ref, v_ref, f_ref, wb_ref):
    @pl.when(pl.program_id(0) == 0)
    def _():
        low = lax.broadcasted_iota(jnp.int32, (D_MODEL, LANES), 1) < HEAD_DIM
        for j in range(Q_W // LANES):
            src_a = (j // 2) * LANES
            src_b = (GROUP // 2 + j // 2) * LANES
            a = w_ref[:, src_a:src_a + LANES]
            b = w_ref[:, src_b:src_b + LANES]
            if j % 2 == 1:
                a = pltpu.roll(a, HEAD_DIM, 1)
            else:
                b = pltpu.roll(b, HEAD_DIM, 1)
            wb_ref[:, j * LANES:(j + 1) * LANES] = jnp.where(low, a, b).astype(BF16)
        wb_ref[:, Q_W:] = w_ref[:, Q_W:].astype(BF16)

    h = _norm_modulate(x_ref[...], g_ref[...], mod_ref[0:1, :], mod_ref[1:2, :]).astype(BF16)
    cos_t = _token_table(rcos_ref[...], ccos_ref[...])
    sin_t = _token_table(rsin_ref[...], csin_ref[...])
    lane = lax.broadcasted_iota(jnp.int32, cos_t.shape, 1)
    first_half = (lane % 32) < 16
    scale = HEAD_DIM ** -0.5 * LOG2E

    def project(col0, width):
        return jnp.dot(h, wb_ref[:, col0:col0 + width], preferred_element_type=F32)

    for c in range(0, Q_W, MXU_DIM):
        p = project(c, MXU_DIM)
        for b in range(MXU_DIM // LANES):
            blk = p[:, b * LANES:(b + 1) * LANES]
            q_ref[:, c + b * LANES:c + (b + 1) * LANES] = (
                _rope(blk, cos_t, sin_t, first_half) * scale).astype(BF16)
    p = project(Q_W, 2 * KV_W)
    k_ref[...] = _rope(p[:, :KV_W], cos_t, sin_t, first_half).astype(BF16)
    v_ref[...] = p[:, KV_W:].T.astype(BF16)
    for c in range(0, FOURIER_W, MXU_DIM):
        f_ref[:, c:c + MXU_DIM] = project(Q_W + 2 * KV_W + c, MXU_DIM)


def _inproj(x2, mods, g_pre, w_in, rope_tabs, seq):
    n = x2.shape[0]
    tm = TOKEN_TILE
    tiles_per_seq = seq // tm
    rows_per_tile = tm // GRID_W
    n_cols = Q_W + 2 * KV_W + FOURIER_W
    row_spec = pl.BlockSpec((rows_per_tile, LANES), lambda i: (i % tiles_per_seq, 0))
    return pl.pallas_call(
        _inproj_kernel,
        grid=(n // tm,),
        in_specs=[
            pl.BlockSpec((tm, D_MODEL), lambda i: (i, 0)),
            pl.BlockSpec((None, 6, D_MODEL), lambda i: (i // tiles_per_seq, 0, 0)),
            _const_spec((1, D_MODEL)),
            _const_spec((D_MODEL, n_cols)),
            row_spec, row_spec,
            _const_spec((GRID_W, LANES)), _const_spec((GRID_W, LANES)),
        ],
        out_specs=[
            pl.BlockSpec((tm, Q_W), lambda i: (i, 0)),
            pl.BlockSpec((tm, KV_W), lambda i: (i, 0)),
            pl.BlockSpec((KV_W, tm), lambda i: (0, i)),
            pl.BlockSpec((tm, FOURIER_W), lambda i: (i, 0)),
        ],
        out_shape=[
            jax.ShapeDtypeStruct((n, Q_W), BF16),
            jax.ShapeDtypeStruct((n, KV_W), BF16),
            jax.ShapeDtypeStruct((KV_W, n), BF16),
            jax.ShapeDtypeStruct((n, FOURIER_W), F32),
        ],
        scratch_shapes=[pltpu.VMEM((D_MODEL, n_cols), BF16)],
        compiler_params=_params(1),
        name="inproj",
    )(x2, mods, g_pre, w_in, *rope_tabs)


def _ctxkv_kernel(x_ref, mod_ref, g_ref, w_ref, k_ref, v_ref):
    h = _norm_modulate(x_ref[...], g_ref[...], mod_ref[0:1, :], mod_ref[1:2, :])
    p = jnp.dot(h.astype(BF16), w_ref[...].astype(BF16), preferred_element_type=F32)
    k_ref[...] = p[:, :KV_W].astype(BF16)
    v_ref[...] = p[:, KV_W:].T.astype(BF16)


def _ctxkv(ctx2, mods_ctx, g_pre, w_in, ctx_len):
    n = ctx2.shape[0]
    assert Q_W % (2 * KV_W) == 0
    return pl.pallas_call(
        _ctxkv_kernel,
        grid=(n // ctx_len,),
        in_specs=[
            pl.BlockSpec((ctx_len, D_MODEL), lambda i: (i, 0)),
            _const_spec((6, D_MODEL)),
            _const_spec((1, D_MODEL)),
            pl.BlockSpec((D_MODEL, 2 * KV_W), lambda i: (0, Q_W // (2 * KV_W)),
                         pipeline_mode=pl.Buffered(1)),
        ],
        out_specs=[
            pl.BlockSpec((ctx_len, KV_W), lambda i: (i, 0)),
            pl.BlockSpec((KV_W, ctx_len), lambda i: (0, i)),
        ],
        out_shape=[
            jax.ShapeDtypeStruct((n, KV_W), BF16),
            jax.ShapeDtypeStruct((KV_W, n), BF16),
        ],
        compiler_params=_params(1),
        name="ctxkv",
    )(ctx2, mods_ctx, g_pre, w_in)


def _attend_block(sink_ref, q_blk, k_parts, v_parts, bias_prev, bias_next):
    kcat = jnp.concatenate(k_parts, axis=0)
    vcat_t = jnp.concatenate(v_parts, axis=1)
    n_keys = kcat.shape[0]
    low = lax.broadcasted_iota(jnp.int32, (BLOCK, LANES), 1) < HEAD_DIM
    top = lax.broadcasted_iota(jnp.int32, (KV_W, n_keys), 0) < HEAD_DIM
    zero = jnp.zeros((), BF16)
    one = jnp.ones((), BF16)
    q_rows = [jnp.where(low if h == 0 else jnp.logical_not(low), q_blk[:, g * LANES:(g + 1) * LANES], zero)
              for h in range(N_KV_HEADS) for g in range(GROUP)]
    s_all = lax.dot_general(kcat, jnp.concatenate(q_rows, axis=0), (((1,), (1,)), ((), ())),
                            preferred_element_type=F32)
    outs = []
    for h in range(N_KV_HEADS):
        v_h = jnp.where(top if h == 0 else jnp.logical_not(top), vcat_t, one)
        for g0 in range(0, GROUP, 2):
            probs = []
            sinks = []
            for g in (g0, g0 + 1):
                head = h * GROUP + g
                s = s_all[:, head * LANES:(head + 1) * LANES]
                parts = [s[0:BLOCK] + bias_prev,
                         s[BLOCK:2 * BLOCK],
                         s[2 * BLOCK:3 * BLOCK] + bias_next]
                parts += [s[r:r + BLOCK] for r in range(3 * BLOCK, n_keys, BLOCK)]
                mx = parts[0]
                for part in parts[1:]:
                    mx = jnp.maximum(mx, part)
                sink = sink_ref[head] * LOG2E
                m = jnp.maximum(jnp.max(mx, axis=0, keepdims=True), sink)
                probs.append(jnp.concatenate([jnp.exp2(part - m) for part in parts], axis=0).astype(BF16))
                sinks.append(jnp.exp2(sink - m))
            o2 = jnp.dot(v_h, jnp.concatenate(probs, axis=1), preferred_element_type=F32)
            for idx in range(2):
                o_t = o2[:, idx * LANES:(idx + 1) * LANES]
                num = o_t[0:HEAD_DIM] if h == 0 else o_t[HEAD_DIM:]
                den = (o_t[HEAD_DIM:] if h == 0 else o_t[0:HEAD_DIM]) + sinks[idx]
                outs.append(num / den)
    return outs


def _attention_steps(sink_ref, q_ref, kp_ref, kc_ref, kn_ref, kx_ref, vp_ref, vc_ref, vn_ref, vx_ref,
                     bias_ref, o_ref):
    i = pl.program_id(1)
    n_sub = q_ref.shape[0] // BLOCK
    first_var = jnp.where(i == 0, 0, 1)
    last_var = jnp.where(i == pl.num_programs(1) - 1, 2, 1)

    def block_step(t):
        own = slice(t * BLOCK, (t + 1) * BLOCK)
        before = slice((t - 1) * BLOCK, t * BLOCK)
        after = slice((t + 1) * BLOCK, (t + 2) * BLOCK)
        k_parts = [kp_ref[...] if t == 0 else kc_ref[before, :], kc_ref[own, :],
                   kn_ref[...] if t == n_sub - 1 else kc_ref[after, :], kx_ref[...]]
        v_parts = [vp_ref[...] if t == 0 else vc_ref[:, before], vc_ref[:, own],
                   vn_ref[...] if t == n_sub - 1 else vc_ref[:, after], vx_ref[...]]
        bias_prev = bias_ref[first_var if t == 0 else 1, 0:BLOCK, :]
        bias_next = bias_ref[last_var if t == n_sub - 1 else 1, BLOCK:2 * BLOCK, :]
        outs = _attend_block(sink_ref, q_ref[own, :], k_parts, v_parts, bias_prev, bias_next)
        for j in range(Q_W // LANES):
            o_t = jnp.concatenate([outs[2 * j], outs[2 * j + 1]], axis=0)
            o_ref[own, j * LANES:(j + 1) * LANES] = o_t.T.astype(BF16)

    return [functools.partial(block_step, t) for t in range(n_sub)]


def _attention_specs(batch, seq, ctx_len, bias_shape):
    nb = seq // BLOCK
    sub = ATTN_BLOCKS_PER_STEP
    steps = nb // sub

    def cur(b, i):
        return (b * steps + i, 0)

    def prev(b, i):
        return (b * nb + jnp.maximum(i * sub - 1, 0), 0)

    def nxt(b, i):
        return (b * nb + jnp.minimum((i + 1) * sub, nb - 1), 0)

    def swap(f):
        return lambda b, i: f(b, i)[::-1]

    in_specs = [
        pl.BlockSpec(memory_space=pltpu.SMEM),
        pl.BlockSpec((sub * BLOCK, Q_W), cur),
        pl.BlockSpec((BLOCK, KV_W), prev),
        pl.BlockSpec((sub * BLOCK, KV_W), cur),
        pl.BlockSpec((BLOCK, KV_W), nxt),
        pl.BlockSpec((ctx_len, KV_W), lambda b, i: (b, 0)),
        pl.BlockSpec((KV_W, BLOCK), swap(prev)),
        pl.BlockSpec((KV_W, sub * BLOCK), swap(cur)),
        pl.BlockSpec((KV_W, BLOCK), swap(nxt)),
        pl.BlockSpec((KV_W, ctx_len), lambda b, i: (0, b)),
        _const_spec(bias_shape),
    ]
    return steps, in_specs, pl.BlockSpec((sub * BLOCK, Q_W), cur)


def _band_bias():
    qi = np.arange(BLOCK)[None, :]
    kj = np.arange(3 * BLOCK)[:, None]
    rel = kj - BLOCK - qi
    in_window = np.abs(rel) <= WINDOW
    variants = []
    for has_prev, has_next in ((False, True), (True, True), (True, False)):
        ok = in_window.copy()
        if not has_prev:
            ok &= kj >= BLOCK
        if not has_next:
            ok &= kj < 2 * BLOCK
        mask = np.where(ok, 0.0, NEG).astype(np.float32)
        variants.append(np.concatenate([mask[:BLOCK], mask[2 * BLOCK:]], axis=0))
    return jnp.asarray(np.stack(variants))


def _minor_dft_terms():
    assert DFT_MINOR == 8
    plan = []
    for k2 in range(DFT_MINOR):
        unit, half = [], []
        for s2 in range(DFT_MINOR):
            ang = -2.0 * math.pi * ((s2 * k2) % DFT_MINOR) / DFT_MINOR
            for part, coef in ((0, math.cos(ang)), (1, -math.sin(ang))):
                if abs(coef) < 1e-9:
                    continue
                target = unit if abs(abs(coef) - 1.0) < 1e-9 else half
                assert target is unit or abs(abs(coef) - math.sqrt(0.5)) < 1e-9
                target.append((s2, part, 1 if coef > 0 else -1))
        plan.append((unit, half))
    return plan


def _signed_sum(terms, pick):
    pos = [pick(s2, part) for s2, part, sign in terms if sign > 0]
    neg = [pick(s2, part) for s2, part, sign in terms if sign < 0]
    total = None
    for v in pos:
        total = v if total is None else total + v
    for v in neg:
        total = -v if total is None else total - v
    return total


def _fourier_steps(f_ref, f1_ref, tr_ref, ti_ref, cm_ref, o_ref):
    n_major = f_ref.shape[0] // DFT_MINOR
    pair_w = 2 * LANES
    n_pairs = DFT_MINOR // 2
    row_chunk = n_major // n_pairs
    twiddled = [None] * n_pairs
    spectra = [None] * DFT_MINOR
    plan = _minor_dft_terms()

    def stage1(c):
        y = jnp.concatenate(
            [f_ref[pl.ds(s2, n_major, stride=DFT_MINOR), :].astype(BF16) for s2 in (2 * c, 2 * c + 1)],
            axis=1)
        a = jnp.dot(f1_ref[...], y, preferred_element_type=F32)
        ar = a[:n_major]
        ai = a[n_major:]
        tr = tr_ref[:, c * pair_w:(c + 1) * pair_w]
        ti = ti_ref[:, c * pair_w:(c + 1) * pair_w]
        twiddled[c] = ((ar * tr - ai * ti).astype(BF16), (ar * ti + ai * tr).astype(BF16))

    def channels(c):
        apr, api = twiddled[c]
        for t in range(2):
            lanes = slice(t * LANES, (t + 1) * LANES)
            ap = jnp.concatenate([apr[:, lanes], api[:, lanes]], axis=1)
            spectra[2 * c + t] = jnp.dot(ap, cm_ref[...], preferred_element_type=F32)

    def stage2(r):
        rows = slice(r * row_chunk, (r + 1) * row_chunk)

        def pick(s2, part):
            return spectra[s2][rows, part * LANES:(part + 1) * LANES]

        for k2, (unit, half) in enumerate(plan):
            res = _signed_sum(unit, pick)
            if half:
                scaled = _signed_sum(half, pick) * math.sqrt(0.5)
                res = scaled if res is None else res + scaled
            o_ref[pl.ds(k2 * n_major + r * row_chunk, row_chunk), :] = res.astype(BF16)

    steps = [functools.partial(stage1, 0)]
    for c in range(1, n_pairs):
        steps += [functools.partial(stage1, c), functools.partial(channels, c - 1)]
    steps.append(functools.partial(channels, n_pairs - 1))
    return steps + [functools.partial(stage2, r) for r in range(n_pairs)]


def _cis(num, den):
    ang = (num % den).astype(F32) * F32(-2.0 * math.pi / den)
    return jnp.cos(ang), jnp.sin(ang)


def _dft_tables(seq):
    n_major = seq // DFT_MINOR
    ch = FOURIER_GROUP_CH
    root = int(round(math.sqrt(n_major)))
    assert root * root == n_major
    s1 = jnp.arange(n_major, dtype=jnp.int32)[None, :]
    hi = jnp.arange(root, dtype=jnp.int32)[:, None] * root
    lo = jnp.arange(root, dtype=jnp.int32)[:, None]
    hr, hi_ = _cis(hi * s1, n_major)
    lr, li = _cis(lo * s1, n_major)
    f1r = (hr[:, None, :] * lr[None, :, :] - hi_[:, None, :] * li[None, :, :]).reshape(n_major, n_major)
    f1i = (hr[:, None, :] * li[None, :, :] + hi_[:, None, :] * lr[None, :, :]).reshape(n_major, n_major)
    f1 = jnp.concatenate([f1r, f1i], axis=0).astype(BF16)
    k1 = jnp.arange(n_major, dtype=jnp.int32)[:, None]
    s2 = jnp.arange(DFT_MINOR, dtype=jnp.int32)[None, :]
    tr, ti = _cis(k1 * s2, seq)
    tr = jnp.broadcast_to(tr[:, :, None], (n_major, DFT_MINOR, ch)).reshape(n_major, DFT_MINOR * ch)
    ti = jnp.broadcast_to(ti[:, :, None], (n_major, DFT_MINOR, ch)).reshape(n_major, DFT_MINOR * ch)
    c128 = jnp.arange(ch, dtype=jnp.int32)
    cr, ci = _cis(c128[:, None] * c128[None, :], ch)
    norm = F32(1.0 / math.sqrt(seq * ch))
    cm = (jnp.concatenate([jnp.concatenate([cr, ci], axis=1),
                           jnp.concatenate([-ci, cr], axis=1)], axis=0) * norm).astype(BF16)
    return f1, tr, ti, cm


def _mix_kernel(sink_ref, q_ref, kp_ref, kc_ref, kn_ref, kx_ref, vp_ref, vc_ref, vn_ref, vx_ref, bias_ref,
                f_ref, f1_ref, tr_ref, ti_ref, cm_ref, attn_ref, four_ref):
    attn_steps = _attention_steps(sink_ref, q_ref, kp_ref, kc_ref, kn_ref, kx_ref,
                                  vp_ref, vc_ref, vn_ref, vx_ref, bias_ref, attn_ref)
    four_steps = _fourier_steps(f_ref, f1_ref, tr_ref, ti_ref, cm_ref, four_ref)
    done = 0
    for n, four_step in enumerate(four_steps):
        upto = (n + 1) * len(attn_steps) // len(four_steps)
        for attn_step in attn_steps[done:upto]:
            attn_step()
        done = upto
        four_step()


def _mix(sink, q, k, v_t, kx, vx_t, bias, f3, tables, ctx_len):
    batch, seq, _ = f3.shape
    f1, tr, ti, cm = tables
    ch = FOURIER_GROUP_CH
    steps, attn_in, attn_out = _attention_specs(batch, seq, ctx_len, bias.shape)
    assert steps == N_FOURIER_GROUPS
    four_spec = pl.BlockSpec((None, seq, ch), lambda b, g: (b, 0, g))
    return pl.pallas_call(
        _mix_kernel,
        grid=(batch, steps),
        in_specs=attn_in + [four_spec, _const_spec(f1.shape), _const_spec(tr.shape),
                            _const_spec(ti.shape), _const_spec(cm.shape)],
        out_specs=[attn_out, four_spec],
        out_shape=[jax.ShapeDtypeStruct((batch * seq, Q_W), BF16),
                   jax.ShapeDtypeStruct((batch, seq, FOURIER_W), BF16)],
        compiler_params=_params(2),
        name="mix",
    )(sink, q, k, k, k, kx, v_t, v_t, v_t, vx_t, bias, f3, f1, tr, ti, cm)


def _merge_kernel(x_ref, a_ref, f_ref, mod_ref, gpre_ref, gpost_ref,
                  wg_ref, wpa_ref, wpf_ref, wout_ref, wup_ref, wdown_ref,
                  o_ref, wup_o, wdown_o, wg_b, wpa_b, wpf_b, wout_b):
    wup_o[...] = wup_ref[...].astype(BF16)
    wdown_o[...] = wdown_ref[...].astype(BF16)

    @pl.when(pl.program_id(0) == 0)
    def _():
        wg_b[...] = wg_ref[...].astype(BF16)
        wpa_b[...] = wpa_ref[...].astype(BF16)
        wpf_b[...] = wpf_ref[...].astype(BF16)
        wout_b[...] = wout_ref[...].astype(BF16)

    x = x_ref[...]
    h = _norm_modulate(x, gpre_ref[...], mod_ref[0:1, :], mod_ref[1:2, :]).astype(BF16)
    a = a_ref[...]
    f = f_ref[...]
    y = None
    for c in range(0, D_MODEL, MXU_DIM):
        cols = slice(c, c + MXU_DIM)
        gate_a = jnp.dot(h, wg_b[:, cols], preferred_element_type=F32)
        gate_f = jnp.dot(h, wg_b[:, D_MODEL + c:D_MODEL + c + MXU_DIM], preferred_element_type=F32)
        pa = jnp.dot(a, wpa_b[:, cols], preferred_element_type=F32)
        pf = jnp.dot(f, wpf_b[:, cols], preferred_element_type=F32)
        m = (jax.nn.sigmoid(gate_a) * pa + jax.nn.sigmoid(gate_f) * pf).astype(BF16)
        part = jnp.dot(m, wout_b[cols, :], preferred_element_type=F32)
        y = part if y is None else y + part
    o_ref[...] = x + mod_ref[2:3, :] * _post_norm(y, gpost_ref[...])


def _merge(x2, attn, four, mods, g_pre, g_post, w_in, w_pa, w_pf, w_out, w_up, w_down, seq):
    n = x2.shape[0]
    tm = TOKEN_TILE
    n_steps = n // tm
    tiles_per_seq = seq // tm
    row = lambda i: (i, 0)
    gate_col0 = Q_W + 2 * KV_W + FOURIER_W
    up_rows = w_up.shape[0] // n_steps
    down_rows = w_down.shape[0] // BF16_ROWS
    n_down = w_down.shape[0] // down_rows
    assert up_rows * n_steps == w_up.shape[0] and up_rows % BF16_ROWS == 0
    assert down_rows % BF16_ROWS == 0 and n_down <= n_steps
    up_spec = pl.BlockSpec((up_rows, w_up.shape[1]), row)
    down_spec = pl.BlockSpec((down_rows, w_down.shape[1]), lambda i: (jnp.minimum(i, n_down - 1), 0))
    return pl.pallas_call(
        _merge_kernel,
        grid=(n // tm,),
        in_specs=[
            pl.BlockSpec((tm, D_MODEL), row),
            pl.BlockSpec((tm, Q_W), row),
            pl.BlockSpec((tm, FOURIER_W), row),
            pl.BlockSpec((None, 6, D_MODEL), lambda i: (i // tiles_per_seq, 0, 0)),
            _const_spec((1, D_MODEL)),
            _const_spec((1, D_MODEL)),
            pl.BlockSpec((pl.Element(D_MODEL), pl.Element(2 * D_MODEL)),
                         lambda i: (0, gate_col0),
                         pipeline_mode=pl.Buffered(1)),
            _const_spec(w_pa.shape),
            _const_spec(w_pf.shape),
            _const_spec(w_out.shape),
            up_spec,
            down_spec,
        ],
        out_specs=[pl.BlockSpec((tm, D_MODEL), row), up_spec, down_spec],
        out_shape=[jax.ShapeDtypeStruct((n, D_MODEL), F32),
                   jax.ShapeDtypeStruct(w_up.shape, BF16),
                   jax.ShapeDtypeStruct(w_down.shape, BF16)],
        scratch_shapes=[pltpu.VMEM((D_MODEL, 2 * D_MODEL), BF16), pltpu.VMEM(w_pa.shape, BF16),
                        pltpu.VMEM(w_pf.shape, BF16), pltpu.VMEM(w_out.shape, BF16)],
        compiler_params=_params(1),
        name="merge",
    )(x2, attn, four, mods, g_pre, g_post, w_in, w_pa, w_pf, w_out, w_up, w_down)


def _convffn_kernel(x_ref, xp_ref, xn_ref, mod_ref, gpre_ref, gpost_ref,
                    wu_ref, wgate_ref, cw_ref, cb_ref, wd_ref, o_ref, act_ref, *, tiles_per_seq):
    i = pl.program_id(0)
    tm = x_ref.shape[0]
    halo = SUBLANES
    has_prev = (i % tiles_per_seq != 0).astype(F32)
    has_next = (i % tiles_per_seq != tiles_per_seq - 1).astype(F32)
    x = x_ref[...]
    shift = mod_ref[3:4, :]
    scale = mod_ref[4:5, :]
    gain = gpre_ref[...]
    h = _norm_modulate(x, gain, shift, scale)
    hp = _norm_modulate(xp_ref[...], gain, shift, scale) * has_prev
    hn = _norm_modulate(xn_ref[...], gain, shift, scale) * has_next
    h_ext = jnp.concatenate([hp, h, hn], axis=0).astype(BF16)
    h_mid = h.astype(BF16)
    n_ext = tm + 2 * halo
    for c in range(D_FF // FFN_CHUNK):
        cols = slice(c * FFN_CHUNK, (c + 1) * FFN_CHUNK)
        u = jnp.dot(h_ext, wu_ref[:, cols], preferred_element_type=F32)
        gate = jnp.dot(h_mid, wgate_ref[:, cols], preferred_element_type=F32)
        u_prev = pltpu.roll(u, 1, 0)[halo:halo + tm]
        u_next = pltpu.roll(u, n_ext - 1, 0)[halo:halo + tm]
        conv = (u_prev * cw_ref[0:1, cols] + u[halo:halo + tm] * cw_ref[1:2, cols]
                + u_next * cw_ref[2:3, cols] + cb_ref[:, cols])
        act_ref[:, cols] = (conv * jax.nn.sigmoid(conv) * gate).astype(BF16)
    y = jnp.dot(act_ref[...], wd_ref[...], preferred_element_type=F32)
    o_ref[...] = x + mod_ref[5:6, :] * _post_norm(y, gpost_ref[...])


def _convffn(x1, mods, g_pre, g_post, w_up, conv_w, conv_b, w_down, seq):
    n = x1.shape[0]
    tm = FFN_TOKEN_TILE
    tiles_per_seq = seq // tm
    halo_blocks_per_tile = tm // SUBLANES
    n_halo_blocks = n // SUBLANES
    return pl.pallas_call(
        functools.partial(_convffn_kernel, tiles_per_seq=tiles_per_seq),
        grid=(n // tm,),
        in_specs=[
            pl.BlockSpec((tm, D_MODEL), lambda i: (i, 0)),
            pl.BlockSpec((SUBLANES, D_MODEL),
                         lambda i: (jnp.maximum(i * halo_blocks_per_tile - 1, 0), 0)),
            pl.BlockSpec((SUBLANES, D_MODEL),
                         lambda i: (jnp.minimum((i + 1) * halo_blocks_per_tile, n_halo_blocks - 1), 0)),
            pl.BlockSpec((None, 6, D_MODEL), lambda i: (i // tiles_per_seq, 0, 0)),
            _const_spec((1, D_MODEL)),
            _const_spec((1, D_MODEL)),
            pl.BlockSpec((D_MODEL, D_FF), lambda i: (0, 0), pipeline_mode=pl.Buffered(1)),
            pl.BlockSpec((D_MODEL, D_FF), lambda i: (0, 1), pipeline_mode=pl.Buffered(1)),
            _const_spec(conv_w.shape),
            _const_spec((1, D_FF)),
            _const_spec(w_down.shape),
        ],
        out_specs=pl.BlockSpec((tm, D_MODEL), lambda i: (i, 0)),
        out_shape=jax.ShapeDtypeStruct((n, D_MODEL), F32),
        scratch_shapes=[pltpu.VMEM((tm, D_FF), BF16)],
        compiler_params=_params(1),
        name="convffn",
    )(x1, x1, x1, mods, g_pre, g_post, w_up, w_up, conv_w, conv_b.reshape(1, D_FF), w_down)


def _rope_tables(seq):
    half = HEAD_DIM // 2
    inv_freq = ROPE_THETA ** (-jnp.arange(0, half, 2, dtype=F32) / half)
    ang_r = jnp.arange(seq // GRID_W).astype(F32)[:, None] * inv_freq
    ang_c = jnp.arange(GRID_W).astype(F32)[:, None] * inv_freq
    reps = LANES // HEAD_DIM

    def lanes(row_part, col_part):
        return jnp.tile(jnp.concatenate([row_part, col_part], axis=-1), (1, reps))

    zr = jnp.zeros((ang_r.shape[0], half), F32)
    zc = jnp.zeros((ang_c.shape[0], half), F32)
    rcos = lanes(jnp.concatenate([jnp.cos(ang_r), jnp.cos(ang_r)], axis=-1), zr)
    rsin = lanes(jnp.concatenate([-jnp.sin(ang_r), jnp.sin(ang_r)], axis=-1), zr)
    ccos = lanes(zc, jnp.concatenate([jnp.cos(ang_c), jnp.cos(ang_c)], axis=-1))
    csin = lanes(zc, jnp.concatenate([-jnp.sin(ang_c), jnp.sin(ang_c)], axis=-1))
    return rcos, rsin, ccos, csin


def kernel(x, c, ctx, c_ctx, w_mod, b_mod, g_pre1, g_post1, g_pre2, g_post2,
           w_in, sink, w_pa, w_pf, w_out, w_up, conv_w, conv_b, w_down):
    batch, seq, d = x.shape
    ctx_len = ctx.shape[1]
    depth = w_mod.shape[0]
    assert depth == 1 and d == D_MODEL and batch + 1 <= SUBLANES
    assert seq % TOKEN_TILE == 0 and TOKEN_TILE % GRID_W == 0 and seq % FFN_TOKEN_TILE == 0
    assert seq % (DFT_MINOR * SUBLANES) == 0
    assert seq % (ATTN_BLOCKS_PER_STEP * BLOCK) == 0
    n = batch * seq
    rope_tabs = _rope_tables(seq)
    tables = _dft_tables(seq)
    bias = _band_bias()

    l = 0
    cvecs = jnp.zeros((SUBLANES, d), F32).at[:batch].set(c).at[batch].set(c_ctx)
    mod_all = _adaln(cvecs, w_mod[l], b_mod[l])
    mods = mod_all[:batch].reshape(batch, 6, d)
    mods_ctx = mod_all[batch].reshape(6, d)

    g_pre1_row = g_pre1[l].reshape(1, d)
    g_post1_row = g_post1[l].reshape(1, d)
    g_pre2_row = g_pre2[l].reshape(1, d)
    g_post2_row = g_post2[l].reshape(1, d)

    x2 = x.reshape(n, d)
    q, k, v_t, f = _inproj(x2, mods, g_pre1_row, w_in[l], rope_tabs, seq)
    kx, vx_t = _ctxkv(ctx.reshape(batch * ctx_len, d), mods_ctx, g_pre1_row, w_in[l], ctx_len)
    attn, four = _mix(sink[l], q, k, v_t, kx, vx_t, bias, f.reshape(batch, seq, FOURIER_W), tables, ctx_len)
    four = four.reshape(n, FOURIER_W)
    x1, w_up_b, w_down_b = _merge(x2, attn, four, mods, g_pre1_row, g_post1_row, w_in[l], w_pa[l], w_pf[l],
                                  w_out[l], w_up[l], w_down[l], seq)
    out = _convffn(x1, mods, g_pre2_row, g_post2_row, w_up_b, conv_w[l], conv_b[l], w_down_b, seq)
    return out.reshape(batch, seq, d)
```

```python
import functools
import math

import numpy as np
import jax
import jax.numpy as jnp
from jax import lax
from jax.experimental import pallas as pl
from jax.experimental.pallas import tpu as pltpu

F32 = jnp.float32
BF16 = jnp.bfloat16

D_MODEL = 1024
GRID_W = 64
HEAD_DIM = 64
N_Q_HEADS = 8
N_KV_HEADS = 2
GROUP = N_Q_HEADS // N_KV_HEADS
WINDOW = 128
BLOCK = 128
ROPE_THETA = 10000.0
N_FOURIER_GROUPS = 4
FOURIER_GROUP_CH = 128
FOURIER_W = N_FOURIER_GROUPS * FOURIER_GROUP_CH
Q_W = N_Q_HEADS * HEAD_DIM
KV_W = N_KV_HEADS * HEAD_DIM
D_FF = 2816
EPS = 1e-6
NEG = -1e30
LOG2E = math.log2(math.e)

LANES = 128
SUBLANES = 8
BF16_ROWS = 16
MXU_DIM = 256
VMEM_LIMIT_BYTES = 56 * 1024 * 1024

TOKEN_TILE = 512
FFN_TOKEN_TILE = 1024
FFN_CHUNK = MXU_DIM
DFT_MINOR = SUBLANES
ATTN_BLOCKS_PER_STEP = 16


def _const_spec(shape):
    nd = len(shape)
    return pl.BlockSpec(shape, lambda *_: (0,) * nd, pipeline_mode=pl.Buffered(1))


def _params(n_axes):
    return pltpu.CompilerParams(
        dimension_semantics=("arbitrary",) * n_axes,
        vmem_limit_bytes=VMEM_LIMIT_BYTES,
    )


def _norm_modulate(x, gain, shift, scale):
    ms = jnp.mean(x * x, axis=-1, keepdims=True)
    return (x * lax.rsqrt(ms + EPS)) * (gain * (1.0 + scale)) + shift


def _post_norm(y, gain):
    ms = jnp.mean(y * y, axis=-1, keepdims=True)
    return (y * lax.rsqrt(ms + EPS)) * gain


def _adaln_kernel(c_ref, w_ref, b_ref, o_ref):
    c = c_ref[...]
    s = c * jax.nn.sigmoid(c)
    o_ref[...] = jnp.dot(s, w_ref[...], preferred_element_type=F32) + b_ref[...]


def _adaln(cvecs, w_mod, b_mod):
    n_out = w_mod.shape[1]
    tn = 1536
    return pl.pallas_call(
        _adaln_kernel,
        grid=(n_out // tn,),
        in_specs=[
            pl.BlockSpec((SUBLANES, D_MODEL), lambda j: (0, 0)),
            pl.BlockSpec((D_MODEL, tn), lambda j: (0, j)),
            pl.BlockSpec((1, tn), lambda j: (0, j)),
        ],
        out_specs=pl.BlockSpec((SUBLANES, tn), lambda j: (0, j)),
        out_shape=jax.ShapeDtypeStruct((SUBLANES, n_out), F32),
        compiler_params=_params(1),
        name="adaln",
    )(cvecs, w_mod, b_mod.reshape(1, n_out))


def _rope(xb, cos_t, sin_t, first_half):
    sw = jnp.where(first_half, pltpu.roll(xb, LANES - 16, 1), pltpu.roll(xb, 16, 1))
    return xb * cos_t + sw * sin_t


def _token_table(row_tab, col_tab):
    n_rows = row_tab.shape[0]
    by_row = jnp.concatenate(
        [jnp.broadcast_to(row_tab[r:r + 1, :], (GRID_W, LANES)) for r in range(n_rows)], axis=0)
    return by_row + jnp.concatenate([col_tab] * n_rows, axis=0)


def _inproj_kernel(x_ref, mod_ref, g_ref, w_ref, rcos_ref, rsin_ref, ccos_ref, csin_ref,
                   q_ref, k_ref, v_ref, f_ref, wb_ref):
    @pl.when(pl.program_id(0) == 0)
    def _():
        low = lax.broadcasted_iota(jnp.int32, (D_MODEL, LANES), 1) < HEAD_DIM
        for j in range(Q_W // LANES):
            src_a = (j // 2) * LANES
            src_b = (GROUP // 2 + j // 2) * LANES
            a = w_ref[:, src_a:src_a + LANES]
            b = w_ref[:, src_b:src_b + LANES]
            if j % 2 == 1:
                a = pltpu.roll(a, HEAD_DIM, 1)
            else:
                b = pltpu.roll(b, HEAD_DIM, 1)
            wb_ref[:, j * LANES:(j + 1) * LANES] = jnp.where(low, a, b).astype(BF16)
        wb_ref[:, Q_W:] = w_ref[:, Q_W:].astype(BF16)

    h = _norm_modulate(x_ref[...], g_ref[...], mod_ref[0:1, :], mod_ref[1:2, :]).astype(BF16)
    cos_t = _token_table(rcos_ref[...], ccos_ref[...])
    sin_t = _token_table(rsin_ref[...], csin_ref[...])
    lane = lax.broadcasted_iota(jnp.int32, cos_t.shape, 1)
    first_half = (lane % 32) < 16
    scale = HEAD_DIM ** -0.5 * LOG2E

    def project(col0, width):
        return jnp.dot(h, wb_ref[:, col0:col0 + width], preferred_element_type=F32)

    for c in range(0, Q_W, MXU_DIM):
        p = project(c, MXU_DIM)
        for b in range(MXU_DIM // LANES):
            blk = p[:, b * LANES:(b + 1) * LANES]
            q_ref[:, c + b * LANES:c + (b + 1) * LANES] = (
                _rope(blk, cos_t, sin_t, first_half) * scale).astype(BF16)
    p = project(Q_W, 2 * KV_W)
    k_ref[...] = _rope(p[:, :KV_W], cos_t, sin_t, first_half).astype(BF16)
    v_ref[...] = p[:, KV_W:].T.astype(BF16)
    for c in range(0, FOURIER_W, MXU_DIM):
        f_ref[:, c:c + MXU_DIM] = project(Q_W + 2 * KV_W + c, MXU_DIM)


def _inproj(x2, mods, g_pre, w_in, rope_tabs, seq):
    n = x2.shape[0]
    tm = TOKEN_TILE
    tiles_per_seq = seq // tm
    rows_per_tile = tm // GRID_W
    n_cols = Q_W + 2 * KV_W + FOURIER_W
    row_spec = pl.BlockSpec((rows_per_tile, LANES), lambda i: (i % tiles_per_seq, 0))
    return pl.pallas_call(
        _inproj_kernel,
        grid=(n // tm,),
        in_specs=[
            pl.BlockSpec((tm, D_MODEL), lambda i: (i, 0)),
            pl.BlockSpec((None, 6, D_MODEL), lambda i: (i // tiles_per_seq, 0, 0)),
            _const_spec((1, D_MODEL)),
            _const_spec((D_MODEL, n_cols)),
            row_spec, row_spec,
            _const_spec((GRID_W, LANES)), _const_spec((GRID_W, LANES)),
        ],
        out_specs=[
            pl.BlockSpec((tm, Q_W), lambda i: (i, 0)),
            pl.BlockSpec((tm, KV_W), lambda i: (i, 0)),
            pl.BlockSpec((KV_W, tm), lambda i: (0, i)),
            pl.BlockSpec((tm, FOURIER_W), lambda i: (i, 0)),
        ],
        out_shape=[
            jax.ShapeDtypeStruct((n, Q_W), BF16),
            jax.ShapeDtypeStruct((n, KV_W), BF16),
            jax.ShapeDtypeStruct((KV_W, n), BF16),
            jax.ShapeDtypeStruct((n, FOURIER_W), F32),
        ],
        scratch_shapes=[pltpu.VMEM((D_MODEL, n_cols), BF16)],
        compiler_params=_params(1),
        name="inproj",
    )(x2, mods, g_pre, w_in, *rope_tabs)


def _ctxkv_kernel(x_ref, mod_ref, g_ref, w_ref, k_ref, v_ref):
    h = _norm_modulate(x_ref[...], g_ref[...], mod_ref[0:1, :], mod_ref[1:2, :])
    p = jnp.dot(h.astype(BF16), w_ref[...].astype(BF16), preferred_element_type=F32)
    k_ref[...] = p[:, :KV_W].astype(BF16)
    v_ref[...] = p[:, KV_W:].T.astype(BF16)


def _ctxkv(ctx2, mods_ctx, g_pre, w_in, ctx_len):
    n = ctx2.shape[0]
    assert Q_W % (2 * KV_W) == 0
    return pl.pallas_call(
        _ctxkv_kernel,
        grid=(n // ctx_len,),
        in_specs=[
            pl.BlockSpec((ctx_len, D_MODEL), lambda i: (i, 0)),
            _const_spec((6, D_MODEL)),
            _const_spec((1, D_MODEL)),
            pl.BlockSpec((D_MODEL, 2 * KV_W), lambda i: (0, Q_W // (2 * KV_W)),
                         pipeline_mode=pl.Buffered(1)),
        ],
        out_specs=[
            pl.BlockSpec((ctx_len, KV_W), lambda i: (i, 0)),
            pl.BlockSpec((KV_W, ctx_len), lambda i: (0, i)),
        ],
        out_shape=[
            jax.ShapeDtypeStruct((n, KV_W), BF16),
            jax.ShapeDtypeStruct((KV_W, n), BF16),
        ],
        compiler_params=_params(1),
        name="ctxkv",
    )(ctx2, mods_ctx, g_pre, w_in)


def _attend_block(sink_ref, q_blk, k_parts, v_parts, bias_prev, bias_next):
    kcat = jnp.concatenate(k_parts, axis=0)
    vcat_t = jnp.concatenate(v_parts, axis=1)
    n_keys = kcat.shape[0]
    low = lax.broadcasted_iota(jnp.int32, (BLOCK, LANES), 1) < HEAD_DIM
    top = lax.broadcasted_iota(jnp.int32, (KV_W, n_keys), 0) < HEAD_DIM
    zero = jnp.zeros((), BF16)
    one = jnp.ones((), BF16)
    q_rows = [jnp.where(low if h == 0 else jnp.logical_not(low), q_blk[:, g * LANES:(g + 1) * LANES], zero)
              for h in range(N_KV_HEADS) for g in range(GROUP)]
    s_all = lax.dot_general(kcat, jnp.concatenate(q_rows, axis=0), (((1,), (1,)), ((), ())),
                            preferred_element_type=F32).astype(BF16)
    outs = []
    for h in range(N_KV_HEADS):
        v_h = jnp.where(top if h == 0 else jnp.logical_not(top), vcat_t, one)
        for g0 in range(0, GROUP, 2):
            probs = []
            sinks = []
            for g in (g0, g0 + 1):
                head = h * GROUP + g
                s = s_all[:, head * LANES:(head + 1) * LANES]
                parts = [s[0:BLOCK] + bias_prev,
                         s[BLOCK:2 * BLOCK],
                         s[2 * BLOCK:3 * BLOCK] + bias_next]
                parts += [s[r:r + BLOCK] for r in range(3 * BLOCK, n_keys, BLOCK)]
                mx = parts[0]
                for part in parts[1:]:
                    mx = jnp.maximum(mx, part)
                sink = sink_ref[head] * LOG2E
                m = jnp.maximum(jnp.max(mx, axis=0, keepdims=True), sink.astype(BF16))
                probs.append(jnp.concatenate([jnp.exp2(part - m) for part in parts], axis=0))
                sinks.append(jnp.exp2(sink - m.astype(F32)))
            o2 = jnp.dot(v_h, jnp.concatenate(probs, axis=1), preferred_element_type=F32)
            for idx in range(2):
                o_t = o2[:, idx * LANES:(idx + 1) * LANES]
                num = o_t[0:HEAD_DIM] if h == 0 else o_t[HEAD_DIM:]
                den = (o_t[HEAD_DIM:] if h == 0 else o_t[0:HEAD_DIM]) + sinks[idx]
                outs.append(num / den)
    return outs


def _attention_steps(sink_ref, q_ref, kp_ref, kc_ref, kn_ref, kx_ref, vp_ref, vc_ref, vn_ref, vx_ref,
                     bias_ref, o_ref):
    i = pl.program_id(1)
    n_sub = q_ref.shape[0] // BLOCK
    first_var = jnp.where(i == 0, 0, 1)
    last_var = jnp.where(i == pl.num_programs(1) - 1, 2, 1)

    def block_step(t):
        own = slice(t * BLOCK, (t + 1) * BLOCK)
        before = slice((t - 1) * BLOCK, t * BLOCK)
        after = slice((t + 1) * BLOCK, (t + 2) * BLOCK)
        k_parts = [kp_ref[...] if t == 0 else kc_ref[before, :], kc_ref[own, :],
                   kn_ref[...] if t == n_sub - 1 else kc_ref[after, :], kx_ref[...]]
        v_parts = [vp_ref[...] if t == 0 else vc_ref[:, before], vc_ref[:, own],
                   vn_ref[...] if t == n_sub - 1 else vc_ref[:, after], vx_ref[...]]
        bias_prev = bias_ref[first_var if t == 0 else 1, 0:BLOCK, :]
        bias_next = bias_ref[last_var if t == n_sub - 1 else 1, BLOCK:2 * BLOCK, :]
        outs = _attend_block(sink_ref, q_ref[own, :], k_parts, v_parts, bias_prev, bias_next)
        for j in range(Q_W // LANES):
            o_t = jnp.concatenate([outs[2 * j], outs[2 * j + 1]], axis=0)
            o_ref[own, j * LANES:(j + 1) * LANES] = o_t.T.astype(BF16)

    return [functools.partial(block_step, t) for t in range(n_sub)]


def _attention_specs(batch, seq, ctx_len, bias_shape):
    nb = seq // BLOCK
    sub = ATTN_BLOCKS_PER_STEP
    steps = nb // sub

    def cur(b, i):
        return (b * steps + i, 0)

    def prev(b, i):
        return (b * nb + jnp.maximum(i * sub - 1, 0), 0)

    def nxt(b, i):
        return (b * nb + jnp.minimum((i + 1) * sub, nb - 1), 0)

    def swap(f):
        return lambda b, i: f(b, i)[::-1]

    in_specs = [
        pl.BlockSpec(memory_space=pltpu.SMEM),
        pl.BlockSpec((sub * BLOCK, Q_W), cur),
        pl.BlockSpec((BLOCK, KV_W), prev),
        pl.BlockSpec((sub * BLOCK, KV_W), cur),
        pl.BlockSpec((BLOCK, KV_W), nxt),
        pl.BlockSpec((ctx_len, KV_W), lambda b, i: (b, 0)),
        pl.BlockSpec((KV_W, BLOCK), swap(prev)),
        pl.BlockSpec((KV_W, sub * BLOCK), swap(cur)),
        pl.BlockSpec((KV_W, BLOCK), swap(nxt)),
        pl.BlockSpec((KV_W, ctx_len), lambda b, i: (0, b)),
        _const_spec(bias_shape),
    ]
    return steps, in_specs, pl.BlockSpec((sub * BLOCK, Q_W), cur)


def _band_bias():
    qi = np.arange(BLOCK)[None, :]
    kj = np.arange(3 * BLOCK)[:, None]
    rel = kj - BLOCK - qi
    in_window = np.abs(rel) <= WINDOW
    variants = []
    for has_prev, has_next in ((False, True), (True, True), (True, False)):
        ok = in_window.copy()
        if not has_prev:
            ok &= kj >= BLOCK
        if not has_next:
            ok &= kj < 2 * BLOCK
        mask = np.where(ok, 0.0, NEG).astype(np.float32)
        variants.append(np.concatenate([mask[:BLOCK], mask[2 * BLOCK:]], axis=0))
    return jnp.asarray(np.stack(variants)).astype(BF16)


def _minor_dft_terms():
    assert DFT_MINOR == 8
    plan = []
    for k2 in range(DFT_MINOR):
        unit, half = [], []
        for s2 in range(DFT_MINOR):
            ang = -2.0 * math.pi * ((s2 * k2) % DFT_MINOR) / DFT_MINOR
            for part, coef in ((0, math.cos(ang)), (1, -math.sin(ang))):
                if abs(coef) < 1e-9:
                    continue
                target = unit if abs(abs(coef) - 1.0) < 1e-9 else half
                assert target is unit or abs(abs(coef) - math.sqrt(0.5)) < 1e-9
                target.append((s2, part, 1 if coef > 0 else -1))
        plan.append((unit, half))
    return plan


def _signed_sum(terms, pick):
    pos = [pick(s2, part) for s2, part, sign in terms if sign > 0]
    neg = [pick(s2, part) for s2, part, sign in terms if sign < 0]
    total = None
    for v in pos:
        total = v if total is None else total + v
    for v in neg:
        total = -v if total is None else total - v
    return total


def _fourier_steps(f_ref, f1_ref, tr_ref, ti_ref, cm_ref, o_ref):
    n_major = f_ref.shape[0] // DFT_MINOR
    pair_w = 2 * LANES
    n_pairs = DFT_MINOR // 2
    row_chunk = n_major // n_pairs
    twiddled = [None] * n_pairs
    spectra = [None] * DFT_MINOR
    plan = _minor_dft_terms()

    def stage1(c):
        y = jnp.concatenate(
            [f_ref[pl.ds(s2, n_major, stride=DFT_MINOR), :].astype(BF16) for s2 in (2 * c, 2 * c + 1)],
            axis=1)
        a = jnp.dot(f1_ref[...], y, preferred_element_type=F32)
        ar = a[:n_major]
        ai = a[n_major:]
        tr = tr_ref[:, c * pair_w:(c + 1) * pair_w]
        ti = ti_ref[:, c * pair_w:(c + 1) * pair_w]
        twiddled[c] = ((ar * tr - ai * ti).astype(BF16), (ar * ti + ai * tr).astype(BF16))

    def channels(c):
        apr, api = twiddled[c]
        for t in range(2):
            lanes = slice(t * LANES, (t + 1) * LANES)
            ap = jnp.concatenate([apr[:, lanes], api[:, lanes]], axis=1)
            spectra[2 * c + t] = jnp.dot(ap, cm_ref[...], preferred_element_type=F32)

    def stage2(r):
        rows = slice(r * row_chunk, (r + 1) * row_chunk)

        def pick(s2, part):
            return spectra[s2][rows, part * LANES:(part + 1) * LANES]

        for k2, (unit, half) in enumerate(plan):
            res = _signed_sum(unit, pick)
            if half:
                scaled = _signed_sum(half, pick) * math.sqrt(0.5)
                res = scaled if res is None else res + scaled
            o_ref[pl.ds(k2 * n_major + r * row_chunk, row_chunk), :] = res.astype(BF16)

    steps = [functools.partial(stage1, 0)]
    for c in range(1, n_pairs):
        steps += [functools.partial(stage1, c), functools.partial(channels, c - 1)]
    steps.append(functools.partial(channels, n_pairs - 1))
    return steps + [functools.partial(stage2, r) for r in range(n_pairs)]


def _cis(num, den):
    ang = (num % den).astype(F32) * F32(-2.0 * math.pi / den)
    return jnp.cos(ang), jnp.sin(ang)


def _dft_tables(seq):
    n_major = seq // DFT_MINOR
    ch = FOURIER_GROUP_CH
    root = int(round(math.sqrt(n_major)))
    assert root * root == n_major
    s1 = jnp.arange(n_major, dtype=jnp.int32)[None, :]
    hi = jnp.arange(root, dtype=jnp.int32)[:, None] * root
    lo = jnp.arange(root, dtype=jnp.int32)[:, None]
    hr, hi_ = _cis(hi * s1, n_major)
    lr, li = _cis(lo * s1, n_major)
    f1r = (hr[:, None, :] * lr[None, :, :] - hi_[:, None, :] * li[None, :, :]).reshape(n_major, n_major)
    f1i = (hr[:, None, :] * li[None, :, :] + hi_[:, None, :] * lr[None, :, :]).reshape(n_major, n_major)
    f1 = jnp.concatenate([f1r, f1i], axis=0).astype(BF16)
    k1 = jnp.arange(n_major, dtype=jnp.int32)[:, None]
    s2 = jnp.arange(DFT_MINOR, dtype=jnp.int32)[None, :]
    tr, ti = _cis(k1 * s2, seq)
    tr = jnp.broadcast_to(tr[:, :, None], (n_major, DFT_MINOR, ch)).reshape(n_major, DFT_MINOR * ch)
    ti = jnp.broadcast_to(ti[:, :, None], (n_major, DFT_MINOR, ch)).reshape(n_major, DFT_MINOR * ch)
    c128 = jnp.arange(ch, dtype=jnp.int32)
    cr, ci = _cis(c128[:, None] * c128[None, :], ch)
    norm = F32(1.0 / math.sqrt(seq * ch))
    cm = (jnp.concatenate([jnp.concatenate([cr, ci], axis=1),
                           jnp.concatenate([-ci, cr], axis=1)], axis=0) * norm).astype(BF16)
    return f1, tr, ti, cm


def _mix_kernel(sink_ref, q_ref, kp_ref, kc_ref, kn_ref, kx_ref, vp_ref, vc_ref, vn_ref, vx_ref, bias_ref,
                f_ref, f1_ref, tr_ref, ti_ref, cm_ref, attn_ref, four_ref):
    attn_steps = _attention_steps(sink_ref, q_ref, kp_ref, kc_ref, kn_ref, kx_ref,
                                  vp_ref, vc_ref, vn_ref, vx_ref, bias_ref, attn_ref)
    four_steps = _fourier_steps(f_ref, f1_ref, tr_ref, ti_ref, cm_ref, four_ref)
    done = 0
    for n, four_step in enumerate(four_steps):
        upto = (n + 1) * len(attn_steps) // len(four_steps)
        for attn_step in attn_steps[done:upto]:
            attn_step()
        done = upto
        four_step()


def _mix(sink, q, k, v_t, kx, vx_t, bias, f3, tables, ctx_len):
    batch, seq, _ = f3.shape
    f1, tr, ti, cm = tables
    ch = FOURIER_GROUP_CH
    steps, attn_in, attn_out = _attention_specs(batch, seq, ctx_len, bias.shape)
    assert steps == N_FOURIER_GROUPS
    four_spec = pl.BlockSpec((None, seq, ch), lambda b, g: (b, 0, g))
    return pl.pallas_call(
        _mix_kernel,
        grid=(batch, steps),
        in_specs=attn_in + [four_spec, _const_spec(f1.shape), _const_spec(tr.shape),
                            _const_spec(ti.shape), _const_spec(cm.shape)],
        out_specs=[attn_out, four_spec],
        out_shape=[jax.ShapeDtypeStruct((batch * seq, Q_W), BF16),
                   jax.ShapeDtypeStruct((batch, seq, FOURIER_W), BF16)],
        compiler_params=_params(2),
        name="mix",
    )(sink, q, k, k, k, kx, v_t, v_t, v_t, vx_t, bias, f3, f1, tr, ti, cm)


def _merge_kernel(x_ref, a_ref, f_ref, mod_ref, gpre_ref, gpost_ref,
                  wg_ref, wpa_ref, wpf_ref, wout_ref, wup_ref, wdown_ref,
                  o_ref, wup_o, wdown_o, wg_b, wpa_b, wpf_b, wout_b):
    wup_o[...] = wup_ref[...].astype(BF16)
    wdown_o[...] = wdown_ref[...].astype(BF16)

    @pl.when(pl.program_id(0) == 0)
    def _():
        wg_b[...] = wg_ref[...].astype(BF16)
        wpa_b[...] = wpa_ref[...].astype(BF16)
        wpf_b[...] = wpf_ref[...].astype(BF16)
        wout_b[...] = wout_ref[...].astype(BF16)

    x = x_ref[...]
    h = _norm_modulate(x, gpre_ref[...], mod_ref[0:1, :], mod_ref[1:2, :]).astype(BF16)
    a = a_ref[...]
    f = f_ref[...]
    y = None
    for c in range(0, D_MODEL, MXU_DIM):
        cols = slice(c, c + MXU_DIM)
        gate_a = jnp.dot(h, wg_b[:, cols], preferred_element_type=F32)
        gate_f = jnp.dot(h, wg_b[:, D_MODEL + c:D_MODEL + c + MXU_DIM], preferred_element_type=F32)
        pa = jnp.dot(a, wpa_b[:, cols], preferred_element_type=F32)
        pf = jnp.dot(f, wpf_b[:, cols], preferred_element_type=F32)
        m = (jax.nn.sigmoid(gate_a) * pa + jax.nn.sigmoid(gate_f) * pf).astype(BF16)
        part = jnp.dot(m, wout_b[cols, :], preferred_element_type=F32)
        y = part if y is None else y + part
    o_ref[...] = x + mod_ref[2:3, :] * _post_norm(y, gpost_ref[...])


def _merge(x2, attn, four, mods, g_pre, g_post, w_in, w_pa, w_pf, w_out, w_up, w_down, seq):
    n = x2.shape[0]
    tm = TOKEN_TILE
    n_steps = n // tm
    tiles_per_seq = seq // tm
    row = lambda i: (i, 0)
    gate_col0 = Q_W + 2 * KV_W + FOURIER_W
    up_rows = w_up.shape[0] // n_steps
    down_rows = w_down.shape[0] // BF16_ROWS
    n_down = w_down.shape[0] // down_rows
    assert up_rows * n_steps == w_up.shape[0] and up_rows % BF16_ROWS == 0
    assert down_rows % BF16_ROWS == 0 and n_down <= n_steps
    up_spec = pl.BlockSpec((up_rows, w_up.shape[1]), row)
    down_spec = pl.BlockSpec((down_rows, w_down.shape[1]), lambda i: (jnp.minimum(i, n_down - 1), 0))
    return pl.pallas_call(
        _merge_kernel,
        grid=(n // tm,),
        in_specs=[
            pl.BlockSpec((tm, D_MODEL), row),
            pl.BlockSpec((tm, Q_W), row),
            pl.BlockSpec((tm, FOURIER_W), row),
            pl.BlockSpec((None, 6, D_MODEL), lambda i: (i // tiles_per_seq, 0, 0)),
            _const_spec((1, D_MODEL)),
            _const_spec((1, D_MODEL)),
            pl.BlockSpec((pl.Element(D_MODEL), pl.Element(2 * D_MODEL)),
                         lambda i: (0, gate_col0),
                         pipeline_mode=pl.Buffered(1)),
            _const_spec(w_pa.shape),
            _const_spec(w_pf.shape),
            _const_spec(w_out.shape),
            up_spec,
            down_spec,
        ],
        out_specs=[pl.BlockSpec((tm, D_MODEL), row), up_spec, down_spec],
        out_shape=[jax.ShapeDtypeStruct((n, D_MODEL), F32),
                   jax.ShapeDtypeStruct(w_up.shape, BF16),
                   jax.ShapeDtypeStruct(w_down.shape, BF16)],
        scratch_shapes=[pltpu.VMEM((D_MODEL, 2 * D_MODEL), BF16), pltpu.VMEM(w_pa.shape, BF16),
                        pltpu.VMEM(w_pf.shape, BF16), pltpu.VMEM(w_out.shape, BF16)],
        compiler_params=_params(1),
        name="merge",
    )(x2, attn, four, mods, g_pre, g_post, w_in, w_pa, w_pf, w_out, w_up, w_down)


def _convffn_kernel(x_ref, xp_ref, xn_ref, mod_ref, gpre_ref, gpost_ref,
                    wu_ref, wgate_ref, cw_ref, cb_ref, wd_ref, o_ref, act_ref, *, tiles_per_seq):
    i = pl.program_id(0)
    tm = x_ref.shape[0]
    halo = SUBLANES
    has_prev = (i % tiles_per_seq != 0).astype(F32)
    has_next = (i % tiles_per_seq != tiles_per_seq - 1).astype(F32)
    x = x_ref[...]
    shift = mod_ref[3:4, :]
    scale = mod_ref[4:5, :]
    gain = gpre_ref[...]
    h = _norm_modulate(x, gain, shift, scale)
    hp = _norm_modulate(xp_ref[...], gain, shift, scale) * has_prev
    hn = _norm_modulate(xn_ref[...], gain, shift, scale) * has_next
    h_ext = jnp.concatenate([hp, h, hn], axis=0).astype(BF16)
    h_mid = h.astype(BF16)
    n_ext = tm + 2 * halo
    for c in range(D_FF // FFN_CHUNK):
        cols = slice(c * FFN_CHUNK, (c + 1) * FFN_CHUNK)
        u = jnp.dot(h_ext, wu_ref[:, cols], preferred_element_type=F32)
        gate = jnp.dot(h_mid, wgate_ref[:, cols], preferred_element_type=F32)
        u_prev = pltpu.roll(u, 1, 0)[halo:halo + tm]
        u_next = pltpu.roll(u, n_ext - 1, 0)[halo:halo + tm]
        conv = (u_prev * cw_ref[0:1, cols] + u[halo:halo + tm] * cw_ref[1:2, cols]
                + u_next * cw_ref[2:3, cols] + cb_ref[:, cols])
        act_ref[:, cols] = (conv * jax.nn.sigmoid(conv) * gate).astype(BF16)
    y = jnp.dot(act_ref[...], wd_ref[...], preferred_element_type=F32)
    o_ref[...] = x + mod_ref[5:6, :] * _post_norm(y, gpost_ref[...])


def _convffn(x1, mods, g_pre, g_post, w_up, conv_w, conv_b, w_down, seq):
    n = x1.shape[0]
    tm = FFN_TOKEN_TILE
    tiles_per_seq = seq // tm
    halo_blocks_per_tile = tm // SUBLANES
    n_halo_blocks = n // SUBLANES
    return pl.pallas_call(
        functools.partial(_convffn_kernel, tiles_per_seq=tiles_per_seq),
        grid=(n // tm,),
        in_specs=[
            pl.BlockSpec((tm, D_MODEL), lambda i: (i, 0)),
            pl.BlockSpec((SUBLANES, D_MODEL),
                         lambda i: (jnp.maximum(i * halo_blocks_per_tile - 1, 0), 0)),
            pl.BlockSpec((SUBLANES, D_MODEL),
                         lambda i: (jnp.minimum((i + 1) * halo_blocks_per_tile, n_halo_blocks - 1), 0)),
            pl.BlockSpec((None, 6, D_MODEL), lambda i: (i // tiles_per_seq, 0, 0)),
            _const_spec((1, D_MODEL)),
            _const_spec((1, D_MODEL)),
            pl.BlockSpec((D_MODEL, D_FF), lambda i: (0, 0), pipeline_mode=pl.Buffered(1)),
            pl.BlockSpec((D_MODEL, D_FF), lambda i: (0, 1), pipeline_mode=pl.Buffered(1)),
            _const_spec(conv_w.shape),
            _const_spec((1, D_FF)),
            _const_spec(w_down.shape),
        ],
        out_specs=pl.BlockSpec((tm, D_MODEL), lambda i: (i, 0)),
        out_shape=jax.ShapeDtypeStruct((n, D_MODEL), F32),
        scratch_shapes=[pltpu.VMEM((tm, D_FF), BF16)],
        compiler_params=_params(1),
        name="convffn",
    )(x1, x1, x1, mods, g_pre, g_post, w_up, w_up, conv_w, conv_b.reshape(1, D_FF), w_down)


def _rope_tables(seq):
    half = HEAD_DIM // 2
    inv_freq = ROPE_THETA ** (-jnp.arange(0, half, 2, dtype=F32) / half)
    ang_r = jnp.arange(seq // GRID_W).astype(F32)[:, None] * inv_freq
    ang_c = jnp.arange(GRID_W).astype(F32)[:, None] * inv_freq
    reps = LANES // HEAD_DIM

    def lanes(row_part, col_part):
        return jnp.tile(jnp.concatenate([row_part, col_part], axis=-1), (1, reps))

    zr = jnp.zeros((ang_r.shape[0], half), F32)
    zc = jnp.zeros((ang_c.shape[0], half), F32)
    rcos = lanes(jnp.concatenate([jnp.cos(ang_r), jnp.cos(ang_r)], axis=-1), zr)
    rsin = lanes(jnp.concatenate([-jnp.sin(ang_r), jnp.sin(ang_r)], axis=-1), zr)
    ccos = lanes(zc, jnp.concatenate([jnp.cos(ang_c), jnp.cos(ang_c)], axis=-1))
    csin = lanes(zc, jnp.concatenate([-jnp.sin(ang_c), jnp.sin(ang_c)], axis=-1))
    return rcos, rsin, ccos, csin


def kernel(x, c, ctx, c_ctx, w_mod, b_mod, g_pre1, g_post1, g_pre2, g_post2,
           w_in, sink, w_pa, w_pf, w_out, w_up, conv_w, conv_b, w_down):
    batch, seq, d = x.shape
    ctx_len = ctx.shape[1]
    depth = w_mod.shape[0]
    assert depth == 1 and d == D_MODEL and batch + 1 <= SUBLANES
    assert seq % TOKEN_TILE == 0 and TOKEN_TILE % GRID_W == 0 and seq % FFN_TOKEN_TILE == 0
    assert seq % (DFT_MINOR * SUBLANES) == 0
    assert seq % (ATTN_BLOCKS_PER_STEP * BLOCK) == 0
    n = batch * seq
    rope_tabs = _rope_tables(seq)
    tables = _dft_tables(seq)
    bias = _band_bias()

    l = 0
    cvecs = jnp.zeros((SUBLANES, d), F32).at[:batch].set(c).at[batch].set(c_ctx)
    mod_all = _adaln(cvecs, w_mod[l], b_mod[l])
    mods = mod_all[:batch].reshape(batch, 6, d)
    mods_ctx = mod_all[batch].reshape(6, d)

    g_pre1_row = g_pre1[l].reshape(1, d)
    g_post1_row = g_post1[l].reshape(1, d)
    g_pre2_row = g_pre2[l].reshape(1, d)
    g_post2_row = g_post2[l].reshape(1, d)

    x2 = x.reshape(n, d)
    q, k, v_t, f = _inproj(x2, mods, g_pre1_row, w_in[l], rope_tabs, seq)
    kx, vx_t = _ctxkv(ctx.reshape(batch * ctx_len, d), mods_ctx, g_pre1_row, w_in[l], ctx_len)
    attn, four = _mix(sink[l], q, k, v_t, kx, vx_t, bias, f.reshape(batch, seq, FOURIER_W), tables, ctx_len)
    four = four.reshape(n, FOURIER_W)
    x1, w_up_b, w_down_b = _merge(x2, attn, four, mods, g_pre1_row, g_post1_row, w_in[l], w_pa[l], w_pf[l],
                                  w_out[l], w_up[l], w_down[l], seq)
    out = _convffn(x1, mods, g_pre2_row, g_post2_row, w_up_b, conv_w[l], conv_b[l], w_down_b, seq)
    return out.reshape(batch, seq, d)
```

```python
import functools
import math

import numpy as np
import jax
import jax.numpy as jnp
from jax import lax
from jax.experimental import pallas as pl
from jax.experimental.pallas import tpu as pltpu

F32 = jnp.float32
BF16 = jnp.bfloat16

D_MODEL = 1024
GRID_W = 64
HEAD_DIM = 64
N_Q_HEADS = 8
N_KV_HEADS = 2
GROUP = N_Q_HEADS // N_KV_HEADS
WINDOW = 128
BLOCK = 128
ROPE_THETA = 10000.0
N_FOURIER_GROUPS = 4
FOURIER_GROUP_CH = 128
FOURIER_W = N_FOURIER_GROUPS * FOURIER_GROUP_CH
Q_W = N_Q_HEADS * HEAD_DIM
KV_W = N_KV_HEADS * HEAD_DIM
D_FF = 2816
EPS = 1e-6
NEG = -1e30
LOG2E = math.log2(math.e)

LANES = 128
SUBLANES = 8
BF16_ROWS = 16
MXU_DIM = 256
VMEM_LIMIT_BYTES = 56 * 1024 * 1024

TOKEN_TILE = 1024
FFN_CHUNK = MXU_DIM
DFT_MINOR = SUBLANES
ATTN_BLOCKS_PER_STEP = 16


def _const_spec(shape):
    nd = len(shape)
    return pl.BlockSpec(shape, lambda *_: (0,) * nd, pipeline_mode=pl.Buffered(1))


def _params(n_axes):
    return pltpu.CompilerParams(
        dimension_semantics=("arbitrary",) * n_axes,
        vmem_limit_bytes=VMEM_LIMIT_BYTES,
    )


def _norm_modulate(x, gain, shift, scale):
    ms = jnp.mean(x * x, axis=-1, keepdims=True)
    return (x * lax.rsqrt(ms + EPS)) * (gain * (1.0 + scale)) + shift


def _post_norm(y, gain):
    ms = jnp.mean(y * y, axis=-1, keepdims=True)
    return (y * lax.rsqrt(ms + EPS)) * gain


def _adaln_kernel(c_ref, w_ref, b_ref, o_ref):
    c = c_ref[...]
    s = c * jax.nn.sigmoid(c)
    o_ref[...] = jnp.dot(s, w_ref[...], preferred_element_type=F32) + b_ref[...]


def _adaln(cvecs, w_mod, b_mod):
    n_out = w_mod.shape[1]
    tn = 1536
    return pl.pallas_call(
        _adaln_kernel,
        grid=(n_out // tn,),
        in_specs=[
            pl.BlockSpec((SUBLANES, D_MODEL), lambda j: (0, 0)),
            pl.BlockSpec((D_MODEL, tn), lambda j: (0, j)),
            pl.BlockSpec((1, tn), lambda j: (0, j)),
        ],
        out_specs=pl.BlockSpec((SUBLANES, tn), lambda j: (0, j)),
        out_shape=jax.ShapeDtypeStruct((SUBLANES, n_out), F32),
        compiler_params=_params(1),
        name="adaln",
    )(cvecs, w_mod, b_mod.reshape(1, n_out))


def _rope(xb, cos_t, sin_t, first_half):
    sw = jnp.where(first_half, pltpu.roll(xb, LANES - 16, 1), pltpu.roll(xb, 16, 1))
    return xb * cos_t + sw * sin_t


def _token_table(row_tab, col_tab):
    n_rows = row_tab.shape[0]
    by_row = jnp.concatenate(
        [jnp.broadcast_to(row_tab[r:r + 1, :], (GRID_W, LANES)) for r in range(n_rows)], axis=0)
    return by_row + jnp.concatenate([col_tab] * n_rows, axis=0)


def _inproj_kernel(x_ref, mod_ref, g_ref, w_ref, rcos_ref, rsin_ref, ccos_ref, csin_ref,
                   wg_ref, wpa_ref, wpf_ref, wout_ref,
                   q_ref, k_ref, v_ref, f_ref, wg_o, wpa_o, wpf_o, wout_o, wb_ref):
    wg_o[...] = wg_ref[...].astype(BF16)
    wpa_o[...] = wpa_ref[...].astype(BF16)
    wpf_o[...] = wpf_ref[...].astype(BF16)
    wout_o[...] = wout_ref[...].astype(BF16)

    @pl.when(pl.program_id(0) == 0)
    def _():
        low = lax.broadcasted_iota(jnp.int32, (D_MODEL, LANES), 1) < HEAD_DIM
        for j in range(Q_W // LANES):
            src_a = (j // 2) * LANES
            src_b = (GROUP // 2 + j // 2) * LANES
            a = w_ref[:, src_a:src_a + LANES]
            b = w_ref[:, src_b:src_b + LANES]
            if j % 2 == 1:
                a = pltpu.roll(a, HEAD_DIM, 1)
            else:
                b = pltpu.roll(b, HEAD_DIM, 1)
            wb_ref[:, j * LANES:(j + 1) * LANES] = jnp.where(low, a, b).astype(BF16)
        wb_ref[:, Q_W:] = w_ref[:, Q_W:].astype(BF16)

    h = _norm_modulate(x_ref[...], g_ref[...], mod_ref[0:1, :], mod_ref[1:2, :]).astype(BF16)
    cos_t = _token_table(rcos_ref[...], ccos_ref[...])
    sin_t = _token_table(rsin_ref[...], csin_ref[...])
    lane = lax.broadcasted_iota(jnp.int32, cos_t.shape, 1)
    first_half = (lane % 32) < 16
    scale = HEAD_DIM ** -0.5 * LOG2E

    def project(col0, width):
        return jnp.dot(h, wb_ref[:, col0:col0 + width], preferred_element_type=F32)

    for c in range(0, Q_W, MXU_DIM):
        p = project(c, MXU_DIM)
        for b in range(MXU_DIM // LANES):
            blk = p[:, b * LANES:(b + 1) * LANES]
            q_ref[:, c + b * LANES:c + (b + 1) * LANES] = (
                _rope(blk, cos_t, sin_t, first_half) * scale).astype(BF16)
    p = project(Q_W, 2 * KV_W)
    k_ref[...] = _rope(p[:, :KV_W], cos_t, sin_t, first_half).astype(BF16)
    v_ref[...] = p[:, KV_W:].T.astype(BF16)
    for c in range(0, FOURIER_W, MXU_DIM):
        f_ref[:, c:c + MXU_DIM] = project(Q_W + 2 * KV_W + c, MXU_DIM)


def _row_slab_rows(n_rows, n_steps):
    rows = n_rows // n_steps
    assert rows * n_steps == n_rows and rows % BF16_ROWS == 0
    return rows


def _inproj(x2, mods, g_pre, w_in, w_pa, w_pf, w_out, rope_tabs, seq):
    n = x2.shape[0]
    tm = TOKEN_TILE
    n_steps = n // tm
    tiles_per_seq = seq // tm
    rows_per_tile = tm // GRID_W
    n_cols = Q_W + 2 * KV_W + FOURIER_W
    row_spec = pl.BlockSpec((rows_per_tile, LANES), lambda i: (i % tiles_per_seq, 0))
    g_rows = _row_slab_rows(D_MODEL, n_steps)
    pa_rows = _row_slab_rows(w_pa.shape[0], n_steps)
    pf_rows = _row_slab_rows(w_pf.shape[0], n_steps)
    out_rows = _row_slab_rows(w_out.shape[0], n_steps)
    slab = lambda rows, cols: pl.BlockSpec((rows, cols), lambda i: (i, 0))
    return pl.pallas_call(
        _inproj_kernel,
        grid=(n // tm,),
        in_specs=[
            pl.BlockSpec((tm, D_MODEL), lambda i: (i, 0)),
            pl.BlockSpec((None, 6, D_MODEL), lambda i: (i // tiles_per_seq, 0, 0)),
            _const_spec((1, D_MODEL)),
            _const_spec((D_MODEL, n_cols)),
            row_spec, row_spec,
            _const_spec((GRID_W, LANES)), _const_spec((GRID_W, LANES)),
            pl.BlockSpec((pl.Element(g_rows), pl.Element(2 * D_MODEL)),
                         lambda i: (i * g_rows, n_cols)),
            slab(pa_rows, D_MODEL), slab(pf_rows, D_MODEL), slab(out_rows, D_MODEL),
        ],
        out_specs=[
            pl.BlockSpec((tm, Q_W), lambda i: (i, 0)),
            pl.BlockSpec((tm, KV_W), lambda i: (i, 0)),
            pl.BlockSpec((KV_W, tm), lambda i: (0, i)),
            pl.BlockSpec((tm, FOURIER_W), lambda i: (i, 0)),
            slab(g_rows, 2 * D_MODEL), slab(pa_rows, D_MODEL), slab(pf_rows, D_MODEL), slab(out_rows, D_MODEL),
        ],
        out_shape=[
            jax.ShapeDtypeStruct((n, Q_W), BF16),
            jax.ShapeDtypeStruct((n, KV_W), BF16),
            jax.ShapeDtypeStruct((KV_W, n), BF16),
            jax.ShapeDtypeStruct((n, FOURIER_W), F32),
            jax.ShapeDtypeStruct((D_MODEL, 2 * D_MODEL), BF16),
            jax.ShapeDtypeStruct(w_pa.shape, BF16),
            jax.ShapeDtypeStruct(w_pf.shape, BF16),
            jax.ShapeDtypeStruct(w_out.shape, BF16),
        ],
        scratch_shapes=[pltpu.VMEM((D_MODEL, n_cols), BF16)],
        compiler_params=_params(1),
        name="inproj",
    )(x2, mods, g_pre, w_in, *rope_tabs, w_in, w_pa, w_pf, w_out)


def _ctxkv_kernel(x_ref, mod_ref, g_ref, w_ref, k_ref, v_ref):
    h = _norm_modulate(x_ref[...], g_ref[...], mod_ref[0:1, :], mod_ref[1:2, :])
    p = jnp.dot(h.astype(BF16), w_ref[...].astype(BF16), preferred_element_type=F32)
    k_ref[...] = p[:, :KV_W].astype(BF16)
    v_ref[...] = p[:, KV_W:].T.astype(BF16)


def _ctxkv(ctx2, mods_ctx, g_pre, w_in, ctx_len):
    n = ctx2.shape[0]
    assert Q_W % (2 * KV_W) == 0
    return pl.pallas_call(
        _ctxkv_kernel,
        grid=(n // ctx_len,),
        in_specs=[
            pl.BlockSpec((ctx_len, D_MODEL), lambda i: (i, 0)),
            _const_spec((6, D_MODEL)),
            _const_spec((1, D_MODEL)),
            pl.BlockSpec((D_MODEL, 2 * KV_W), lambda i: (0, Q_W // (2 * KV_W)),
                         pipeline_mode=pl.Buffered(1)),
        ],
        out_specs=[
            pl.BlockSpec((ctx_len, KV_W), lambda i: (i, 0)),
            pl.BlockSpec((KV_W, ctx_len), lambda i: (0, i)),
        ],
        out_shape=[
            jax.ShapeDtypeStruct((n, KV_W), BF16),
            jax.ShapeDtypeStruct((KV_W, n), BF16),
        ],
        compiler_params=_params(1),
        name="ctxkv",
    )(ctx2, mods_ctx, g_pre, w_in)


def _attend_block(sink_ref, q_blk, k_parts, v_parts, bias_prev, bias_next):
    kcat = jnp.concatenate(k_parts, axis=0)
    vcat_t = jnp.concatenate(v_parts, axis=1)
    n_keys = kcat.shape[0]
    low = lax.broadcasted_iota(jnp.int32, (BLOCK, LANES), 1) < HEAD_DIM
    top = lax.broadcasted_iota(jnp.int32, (KV_W, n_keys), 0) < HEAD_DIM
    zero = jnp.zeros((), BF16)
    one = jnp.ones((), BF16)
    q_rows = [jnp.where(low if h == 0 else jnp.logical_not(low), q_blk[:, g * LANES:(g + 1) * LANES], zero)
              for h in range(N_KV_HEADS) for g in range(GROUP)]
    s_all = lax.dot_general(kcat, jnp.concatenate(q_rows, axis=0), (((1,), (1,)), ((), ())),
                            preferred_element_type=F32).astype(BF16)
    outs = []
    for h in range(N_KV_HEADS):
        v_h = jnp.where(top if h == 0 else jnp.logical_not(top), vcat_t, one)
        for g0 in range(0, GROUP, 2):
            probs = []
            sinks = []
            for g in (g0, g0 + 1):
                head = h * GROUP + g
                s = s_all[:, head * LANES:(head + 1) * LANES]
                parts = [s[0:BLOCK] + bias_prev,
                         s[BLOCK:2 * BLOCK],
                         s[2 * BLOCK:3 * BLOCK] + bias_next]
                parts += [s[r:r + BLOCK] for r in range(3 * BLOCK, n_keys, BLOCK)]
                mx = parts[0]
                for part in parts[1:]:
                    mx = jnp.maximum(mx, part)
                sink = sink_ref[head] * LOG2E
                m = jnp.maximum(jnp.max(mx, axis=0, keepdims=True), sink.astype(BF16))
                probs.append(jnp.concatenate([jnp.exp2(part - m) for part in parts], axis=0))
                sinks.append(jnp.exp2(sink - m.astype(F32)))
            o2 = jnp.dot(v_h, jnp.concatenate(probs, axis=1), preferred_element_type=F32)
            for idx in range(2):
                o_t = o2[:, idx * LANES:(idx + 1) * LANES]
                num = o_t[0:HEAD_DIM] if h == 0 else o_t[HEAD_DIM:]
                den = (o_t[HEAD_DIM:] if h == 0 else o_t[0:HEAD_DIM]) + sinks[idx]
                outs.append(num / den)
    return outs


def _attention_steps(sink_ref, q_ref, kp_ref, kc_ref, kn_ref, kx_ref, vp_ref, vc_ref, vn_ref, vx_ref,
                     bias_ref, o_ref):
    i = pl.program_id(1)
    n_sub = q_ref.shape[0] // BLOCK
    first_var = jnp.where(i == 0, 0, 1)
    last_var = jnp.where(i == pl.num_programs(1) - 1, 2, 1)

    def block_step(t):
        own = slice(t * BLOCK, (t + 1) * BLOCK)
        before = slice((t - 1) * BLOCK, t * BLOCK)
        after = slice((t + 1) * BLOCK, (t + 2) * BLOCK)
        k_parts = [kp_ref[...] if t == 0 else kc_ref[before, :], kc_ref[own, :],
                   kn_ref[...] if t == n_sub - 1 else kc_ref[after, :], kx_ref[...]]
        v_parts = [vp_ref[...] if t == 0 else vc_ref[:, before], vc_ref[:, own],
                   vn_ref[...] if t == n_sub - 1 else vc_ref[:, after], vx_ref[...]]
        bias_prev = bias_ref[first_var if t == 0 else 1, 0:BLOCK, :]
        bias_next = bias_ref[last_var if t == n_sub - 1 else 1, BLOCK:2 * BLOCK, :]
        outs = _attend_block(sink_ref, q_ref[own, :], k_parts, v_parts, bias_prev, bias_next)
        for j in range(Q_W // LANES):
            o_t = jnp.concatenate([outs[2 * j], outs[2 * j + 1]], axis=0)
            o_ref[own, j * LANES:(j + 1) * LANES] = o_t.T.astype(BF16)

    return [functools.partial(block_step, t) for t in range(n_sub)]


def _attention_specs(batch, seq, ctx_len, bias_shape):
    nb = seq // BLOCK
    sub = ATTN_BLOCKS_PER_STEP
    steps = nb // sub

    def cur(b, i):
        return (b * steps + i, 0)

    def prev(b, i):
        return (b * nb + jnp.maximum(i * sub - 1, 0), 0)

    def nxt(b, i):
        return (b * nb + jnp.minimum((i + 1) * sub, nb - 1), 0)

    def swap(f):
        return lambda b, i: f(b, i)[::-1]

    in_specs = [
        pl.BlockSpec(memory_space=pltpu.SMEM),
        pl.BlockSpec((sub * BLOCK, Q_W), cur),
        pl.BlockSpec((BLOCK, KV_W), prev),
        pl.BlockSpec((sub * BLOCK, KV_W), cur),
        pl.BlockSpec((BLOCK, KV_W), nxt),
        pl.BlockSpec((ctx_len, KV_W), lambda b, i: (b, 0)),
        pl.BlockSpec((KV_W, BLOCK), swap(prev)),
        pl.BlockSpec((KV_W, sub * BLOCK), swap(cur)),
        pl.BlockSpec((KV_W, BLOCK), swap(nxt)),
        pl.BlockSpec((KV_W, ctx_len), lambda b, i: (0, b)),
        _const_spec(bias_shape),
    ]
    return steps, in_specs, pl.BlockSpec((sub * BLOCK, Q_W), cur)


def _band_bias():
    qi = np.arange(BLOCK)[None, :]
    kj = np.arange(3 * BLOCK)[:, None]
    rel = kj - BLOCK - qi
    in_window = np.abs(rel) <= WINDOW
    variants = []
    for has_prev, has_next in ((False, True), (True, True), (True, False)):
        ok = in_window.copy()
        if not has_prev:
            ok &= kj >= BLOCK
        if not has_next:
            ok &= kj < 2 * BLOCK
        mask = np.where(ok, 0.0, NEG).astype(np.float32)
        variants.append(np.concatenate([mask[:BLOCK], mask[2 * BLOCK:]], axis=0))
    return jnp.asarray(np.stack(variants)).astype(BF16)


def _minor_dft_terms():
    assert DFT_MINOR == 8
    plan = []
    for k2 in range(DFT_MINOR):
        unit, half = [], []
        for s2 in range(DFT_MINOR):
            ang = -2.0 * math.pi * ((s2 * k2) % DFT_MINOR) / DFT_MINOR
            for part, coef in ((0, math.cos(ang)), (1, -math.sin(ang))):
                if abs(coef) < 1e-9:
                    continue
                target = unit if abs(abs(coef) - 1.0) < 1e-9 else half
                assert target is unit or abs(abs(coef) - math.sqrt(0.5)) < 1e-9
                target.append((s2, part, 1 if coef > 0 else -1))
        plan.append((unit, half))
    return plan


def _signed_sum(terms, pick):
    pos = [pick(s2, part) for s2, part, sign in terms if sign > 0]
    neg = [pick(s2, part) for s2, part, sign in terms if sign < 0]
    total = None
    for v in pos:
        total = v if total is None else total + v
    for v in neg:
        total = -v if total is None else total - v
    return total


def _fourier_steps(f_ref, f1_ref, tr_ref, ti_ref, cm_ref, o_ref):
    n_major = f_ref.shape[0] // DFT_MINOR
    pair_w = 2 * LANES
    n_pairs = DFT_MINOR // 2
    row_chunk = n_major // n_pairs
    twiddled = [None] * n_pairs
    spectra = [None] * DFT_MINOR
    plan = _minor_dft_terms()

    def stage1(c):
        y = jnp.concatenate(
            [f_ref[pl.ds(s2, n_major, stride=DFT_MINOR), :].astype(BF16) for s2 in (2 * c, 2 * c + 1)],
            axis=1)
        a = jnp.dot(f1_ref[...], y, preferred_element_type=F32)
        ar = a[:n_major]
        ai = a[n_major:]
        tr = tr_ref[:, c * pair_w:(c + 1) * pair_w]
        ti = ti_ref[:, c * pair_w:(c + 1) * pair_w]
        twiddled[c] = ((ar * tr - ai * ti).astype(BF16), (ar * ti + ai * tr).astype(BF16))

    def channels(c):
        apr, api = twiddled[c]
        for t in range(2):
            lanes = slice(t * LANES, (t + 1) * LANES)
            ap = jnp.concatenate([apr[:, lanes], api[:, lanes]], axis=1)
            spectra[2 * c + t] = jnp.dot(ap, cm_ref[...], preferred_element_type=F32)

    def stage2(r):
        rows = slice(r * row_chunk, (r + 1) * row_chunk)

        def pick(s2, part):
            return spectra[s2][rows, part * LANES:(part + 1) * LANES]

        for k2, (unit, half) in enumerate(plan):
            res = _signed_sum(unit, pick)
            if half:
                scaled = _signed_sum(half, pick) * math.sqrt(0.5)
                res = scaled if res is None else res + scaled
            o_ref[pl.ds(k2 * n_major + r * row_chunk, row_chunk), :] = res.astype(BF16)

    steps = [functools.partial(stage1, 0)]
    for c in range(1, n_pairs):
        steps += [functools.partial(stage1, c), functools.partial(channels, c - 1)]
    steps.append(functools.partial(channels, n_pairs - 1))
    return steps + [functools.partial(stage2, r) for r in range(n_pairs)]


def _cis(num, den):
    ang = (num % den).astype(F32) * F32(-2.0 * math.pi / den)
    return jnp.cos(ang), jnp.sin(ang)


def _dft_tables(seq):
    n_major = seq // DFT_MINOR
    ch = FOURIER_GROUP_CH
    root = int(round(math.sqrt(n_major)))
    assert root * root == n_major
    s1 = jnp.arange(n_major, dtype=jnp.int32)[None, :]
    hi = jnp.arange(root, dtype=jnp.int32)[:, None] * root
    lo = jnp.arange(root, dtype=jnp.int32)[:, None]
    hr, hi_ = _cis(hi * s1, n_major)
    lr, li = _cis(lo * s1, n_major)
    f1r = (hr[:, None, :] * lr[None, :, :] - hi_[:, None, :] * li[None, :, :]).reshape(n_major, n_major)
    f1i = (hr[:, None, :] * li[None, :, :] + hi_[:, None, :] * lr[None, :, :]).reshape(n_major, n_major)
    f1 = jnp.concatenate([f1r, f1i], axis=0).astype(BF16)
    k1 = jnp.arange(n_major, dtype=jnp.int32)[:, None]
    s2 = jnp.arange(DFT_MINOR, dtype=jnp.int32)[None, :]
    tr, ti = _cis(k1 * s2, seq)
    tr = jnp.broadcast_to(tr[:, :, None], (n_major, DFT_MINOR, ch)).reshape(n_major, DFT_MINOR * ch)
    ti = jnp.broadcast_to(ti[:, :, None], (n_major, DFT_MINOR, ch)).reshape(n_major, DFT_MINOR * ch)
    c128 = jnp.arange(ch, dtype=jnp.int32)
    cr, ci = _cis(c128[:, None] * c128[None, :], ch)
    norm = F32(1.0 / math.sqrt(seq * ch))
    cm = (jnp.concatenate([jnp.concatenate([cr, ci], axis=1),
                           jnp.concatenate([-ci, cr], axis=1)], axis=0) * norm).astype(BF16)
    return f1, tr, ti, cm


def _mix_kernel(sink_ref, q_ref, kp_ref, kc_ref, kn_ref, kx_ref, vp_ref, vc_ref, vn_ref, vx_ref, bias_ref,
                f_ref, f1_ref, tr_ref, ti_ref, cm_ref, attn_ref, four_ref):
    attn_steps = _attention_steps(sink_ref, q_ref, kp_ref, kc_ref, kn_ref, kx_ref,
                                  vp_ref, vc_ref, vn_ref, vx_ref, bias_ref, attn_ref)
    four_steps = _fourier_steps(f_ref, f1_ref, tr_ref, ti_ref, cm_ref, four_ref)
    done = 0
    for n, four_step in enumerate(four_steps):
        upto = (n + 1) * len(attn_steps) // len(four_steps)
        for attn_step in attn_steps[done:upto]:
            attn_step()
        done = upto
        four_step()


def _mix(sink, q, k, v_t, kx, vx_t, bias, f3, tables, ctx_len):
    batch, seq, _ = f3.shape
    f1, tr, ti, cm = tables
    ch = FOURIER_GROUP_CH
    steps, attn_in, attn_out = _attention_specs(batch, seq, ctx_len, bias.shape)
    assert steps == N_FOURIER_GROUPS
    four_spec = pl.BlockSpec((None, seq, ch), lambda b, g: (b, 0, g))
    return pl.pallas_call(
        _mix_kernel,
        grid=(batch, steps),
        in_specs=attn_in + [four_spec, _const_spec(f1.shape), _const_spec(tr.shape),
                            _const_spec(ti.shape), _const_spec(cm.shape)],
        out_specs=[attn_out, four_spec],
        out_shape=[jax.ShapeDtypeStruct((batch * seq, Q_W), BF16),
                   jax.ShapeDtypeStruct((batch, seq, FOURIER_W), BF16)],
        compiler_params=_params(2),
        name="mix",
    )(sink, q, k, k, k, kx, v_t, v_t, v_t, vx_t, bias, f3, f1, tr, ti, cm)


def _merge_kernel(x_ref, a_ref, f_ref, mod_ref, gpre_ref, gpost_ref,
                  wg_b, wpa_b, wpf_b, wout_b, wup_ref, wdown_ref,
                  o_ref, wup_o, wdown_o):
    wup_o[...] = wup_ref[...].astype(BF16)
    wdown_o[...] = wdown_ref[...].astype(BF16)

    x = x_ref[...]
    h = _norm_modulate(x, gpre_ref[...], mod_ref[0:1, :], mod_ref[1:2, :]).astype(BF16)
    a = a_ref[...]
    f = f_ref[...]
    y = None
    for c in range(0, D_MODEL, MXU_DIM):
        cols = slice(c, c + MXU_DIM)
        gate_a = jnp.dot(h, wg_b[:, cols], preferred_element_type=F32)
        gate_f = jnp.dot(h, wg_b[:, D_MODEL + c:D_MODEL + c + MXU_DIM], preferred_element_type=F32)
        pa = jnp.dot(a, wpa_b[:, cols], preferred_element_type=F32)
        pf = jnp.dot(f, wpf_b[:, cols], preferred_element_type=F32)
        m = (jax.nn.sigmoid(gate_a) * pa + jax.nn.sigmoid(gate_f) * pf).astype(BF16)
        part = jnp.dot(m, wout_b[cols, :], preferred_element_type=F32)
        y = part if y is None else y + part
    o_ref[...] = x + mod_ref[2:3, :] * _post_norm(y, gpost_ref[...])


def _merge(x2, attn, four, mods, g_pre, g_post, w_g, w_pa, w_pf, w_out, w_up, w_down, seq):
    n = x2.shape[0]
    tm = TOKEN_TILE
    n_steps = n // tm
    tiles_per_seq = seq // tm
    row = lambda i: (i, 0)
    up_rows = w_up.shape[0] // n_steps
    down_rows = w_down.shape[0] // BF16_ROWS
    n_down = w_down.shape[0] // down_rows
    assert up_rows * n_steps == w_up.shape[0] and up_rows % BF16_ROWS == 0
    assert down_rows % BF16_ROWS == 0 and n_down <= n_steps
    up_spec = pl.BlockSpec((up_rows, w_up.shape[1]), row)
    down_spec = pl.BlockSpec((down_rows, w_down.shape[1]), lambda i: (jnp.minimum(i, n_down - 1), 0))
    return pl.pallas_call(
        _merge_kernel,
        grid=(n // tm,),
        in_specs=[
            pl.BlockSpec((tm, D_MODEL), row),
            pl.BlockSpec((tm, Q_W), row),
            pl.BlockSpec((tm, FOURIER_W), row),
            pl.BlockSpec((None, 6, D_MODEL), lambda i: (i // tiles_per_seq, 0, 0)),
            _const_spec((1, D_MODEL)),
            _const_spec((1, D_MODEL)),
            _const_spec(w_g.shape),
            _const_spec(w_pa.shape),
            _const_spec(w_pf.shape),
            _const_spec(w_out.shape),
            up_spec,
            down_spec,
        ],
        out_specs=[pl.BlockSpec((tm, D_MODEL), row), up_spec, down_spec],
        out_shape=[jax.ShapeDtypeStruct((n, D_MODEL), F32),
                   jax.ShapeDtypeStruct(w_up.shape, BF16),
                   jax.ShapeDtypeStruct(w_down.shape, BF16)],
        compiler_params=_params(1),
        name="merge",
    )(x2, attn, four, mods, g_pre, g_post, w_g, w_pa, w_pf, w_out, w_up, w_down)


def _convffn_kernel(x_ref, xp_ref, xn_ref, mod_ref, gpre_ref, gpost_ref,
                    wu_ref, wgate_ref, cw_ref, cb_ref, wd_ref, o_ref, act_ref, *, tiles_per_seq):
    i = pl.program_id(0)
    tm = x_ref.shape[0]
    halo = SUBLANES
    has_prev = (i % tiles_per_seq != 0).astype(F32)
    has_next = (i % tiles_per_seq != tiles_per_seq - 1).astype(F32)
    x = x_ref[...]
    shift = mod_ref[3:4, :]
    scale = mod_ref[4:5, :]
    gain = gpre_ref[...]
    h = _norm_modulate(x, gain, shift, scale)
    hp = _norm_modulate(xp_ref[...], gain, shift, scale) * has_prev
    hn = _norm_modulate(xn_ref[...], gain, shift, scale) * has_next
    h_ext = jnp.concatenate([hp, h, hn], axis=0).astype(BF16)
    h_mid = h.astype(BF16)
    n_ext = tm + 2 * halo
    for c in range(D_FF // FFN_CHUNK):
        cols = slice(c * FFN_CHUNK, (c + 1) * FFN_CHUNK)
        u = jnp.dot(h_ext, wu_ref[:, cols], preferred_element_type=F32)
        gate = jnp.dot(h_mid, wgate_ref[:, cols], preferred_element_type=F32)
        u_prev = pltpu.roll(u, 1, 0)[halo:halo + tm]
        u_next = pltpu.roll(u, n_ext - 1, 0)[halo:halo + tm]
        conv = (u_prev * cw_ref[0:1, cols] + u[halo:halo + tm] * cw_ref[1:2, cols]
                + u_next * cw_ref[2:3, cols] + cb_ref[:, cols])
        act_ref[:, cols] = (conv * jax.nn.sigmoid(conv) * gate).astype(BF16)
    y = jnp.dot(act_ref[...], wd_ref[...], preferred_element_type=F32)
    o_ref[...] = x + mod_ref[5:6, :] * _post_norm(y, gpost_ref[...])


def _convffn(x1, mods, g_pre, g_post, w_up, conv_w, conv_b, w_down, seq):
    n = x1.shape[0]
    tm = TOKEN_TILE
    tiles_per_seq = seq // tm
    halo_blocks_per_tile = tm // SUBLANES
    n_halo_blocks = n // SUBLANES
    return pl.pallas_call(
        functools.partial(_convffn_kernel, tiles_per_seq=tiles_per_seq),
        grid=(n // tm,),
        in_specs=[
            pl.BlockSpec((tm, D_MODEL), lambda i: (i, 0)),
            pl.BlockSpec((SUBLANES, D_MODEL),
                         lambda i: (jnp.maximum(i * halo_blocks_per_tile - 1, 0), 0)),
            pl.BlockSpec((SUBLANES, D_MODEL),
                         lambda i: (jnp.minimum((i + 1) * halo_blocks_per_tile, n_halo_blocks - 1), 0)),
            pl.BlockSpec((None, 6, D_MODEL), lambda i: (i // tiles_per_seq, 0, 0)),
            _const_spec((1, D_MODEL)),
            _const_spec((1, D_MODEL)),
            pl.BlockSpec((D_MODEL, D_FF), lambda i: (0, 0), pipeline_mode=pl.Buffered(1)),
            pl.BlockSpec((D_MODEL, D_FF), lambda i: (0, 1), pipeline_mode=pl.Buffered(1)),
            _const_spec(conv_w.shape),
            _const_spec((1, D_FF)),
            _const_spec(w_down.shape),
        ],
        out_specs=pl.BlockSpec((tm, D_MODEL), lambda i: (i, 0)),
        out_shape=jax.ShapeDtypeStruct((n, D_MODEL), F32),
        scratch_shapes=[pltpu.VMEM((tm, D_FF), BF16)],
        compiler_params=_params(1),
        name="convffn",
    )(x1, x1, x1, mods, g_pre, g_post, w_up, w_up, conv_w, conv_b.reshape(1, D_FF), w_down)


def _rope_tables(seq):
    half = HEAD_DIM // 2
    inv_freq = ROPE_THETA ** (-jnp.arange(0, half, 2, dtype=F32) / half)
    ang_r = jnp.arange(seq // GRID_W).astype(F32)[:, None] * inv_freq
    ang_c = jnp.arange(GRID_W).astype(F32)[:, None] * inv_freq
    reps = LANES // HEAD_DIM

    def lanes(row_part, col_part):
        return jnp.tile(jnp.concatenate([row_part, col_part], axis=-1), (1, reps))

    zr = jnp.zeros((ang_r.shape[0], half), F32)
    zc = jnp.zeros((ang_c.shape[0], half), F32)
    rcos = lanes(jnp.concatenate([jnp.cos(ang_r), jnp.cos(ang_r)], axis=-1), zr)
    rsin = lanes(jnp.concatenate([-jnp.sin(ang_r), jnp.sin(ang_r)], axis=-1), zr)
    ccos = lanes(zc, jnp.concatenate([jnp.cos(ang_c), jnp.cos(ang_c)], axis=-1))
    csin = lanes(zc, jnp.concatenate([-jnp.sin(ang_c), jnp.sin(ang_c)], axis=-1))
    return rcos, rsin, ccos, csin


def kernel(x, c, ctx, c_ctx, w_mod, b_mod, g_pre1, g_post1, g_pre2, g_post2,
           w_in, sink, w_pa, w_pf, w_out, w_up, conv_w, conv_b, w_down):
    batch, seq, d = x.shape
    ctx_len = ctx.shape[1]
    depth = w_mod.shape[0]
    assert depth == 1 and d == D_MODEL and batch + 1 <= SUBLANES
    assert seq % TOKEN_TILE == 0 and TOKEN_TILE % GRID_W == 0
    assert seq % (DFT_MINOR * SUBLANES) == 0
    assert seq % (ATTN_BLOCKS_PER_STEP * BLOCK) == 0
    n = batch * seq
    rope_tabs = _rope_tables(seq)
    tables = _dft_tables(seq)
    bias = _band_bias()

    l = 0
    cvecs = jnp.zeros((SUBLANES, d), F32).at[:batch].set(c).at[batch].set(c_ctx)
    mod_all = _adaln(cvecs, w_mod[l], b_mod[l])
    mods = mod_all[:batch].reshape(batch, 6, d)
    mods_ctx = mod_all[batch].reshape(6, d)

    g_pre1_row = g_pre1[l].reshape(1, d)
    g_post1_row = g_post1[l].reshape(1, d)
    g_pre2_row = g_pre2[l].reshape(1, d)
    g_post2_row = g_post2[l].reshape(1, d)

    x2 = x.reshape(n, d)
    q, k, v_t, f, w_g_b, w_pa_b, w_pf_b, w_out_b = _inproj(x2, mods, g_pre1_row, w_in[l], w_pa[l], w_pf[l], w_out[l],
                                                          rope_tabs, seq)
    kx, vx_t = _ctxkv(ctx.reshape(batch * ctx_len, d), mods_ctx, g_pre1_row, w_in[l], ctx_len)
    attn, four = _mix(sink[l], q, k, v_t, kx, vx_t, bias, f.reshape(batch, seq, FOURIER_W), tables, ctx_len)
    four = four.reshape(n, FOURIER_W)
    x1, w_up_b, w_down_b = _merge(x2, attn, four, mods, g_pre1_row, g_post1_row, w_g_b, w_pa_b, w_pf_b, w_out_b,
                                  w_up[l], w_down[l], seq)
    out = _convffn(x1, mods, g_pre2_row, g_post2_row, w_up_b, conv_w[l], conv_b[l], w_down_b, seq)
    return out.reshape(batch, seq, d)
```

```python
import functools
import math

import numpy as np
import jax
import jax.numpy as jnp
from jax import lax
from jax.experimental import pallas as pl
from jax.experimental.pallas import tpu as pltpu

F32 = jnp.float32
BF16 = jnp.bfloat16

D_MODEL = 1024
GRID_W = 64
HEAD_DIM = 64
N_Q_HEADS = 8
N_KV_HEADS = 2
GROUP = N_Q_HEADS // N_KV_HEADS
WINDOW = 128
BLOCK = 128
ROPE_THETA = 10000.0
N_FOURIER_GROUPS = 4
FOURIER_GROUP_CH = 128
FOURIER_W = N_FOURIER_GROUPS * FOURIER_GROUP_CH
Q_W = N_Q_HEADS * HEAD_DIM
KV_W = N_KV_HEADS * HEAD_DIM
D_FF = 2816
EPS = 1e-6
NEG = -1e30
LOG2E = math.log2(math.e)

LANES = 128
SUBLANES = 8
BF16_ROWS = 16
MXU_DIM = 256
VMEM_LIMIT_BYTES = 56 * 1024 * 1024

TOKEN_TILE = 1024
FFN_CHUNK = MXU_DIM
DFT_MINOR = SUBLANES
ATTN_BLOCKS_PER_STEP = 16


def _const_spec(shape):
    nd = len(shape)
    return pl.BlockSpec(shape, lambda *_: (0,) * nd, pipeline_mode=pl.Buffered(1))


def _params(n_axes):
    return pltpu.CompilerParams(
        dimension_semantics=("arbitrary",) * n_axes,
        vmem_limit_bytes=VMEM_LIMIT_BYTES,
    )


def _norm_modulate(x, gain, shift, scale):
    ms = jnp.mean(x * x, axis=-1, keepdims=True)
    return (x * lax.rsqrt(ms + EPS)) * (gain * (1.0 + scale)) + shift


def _mod_rows(mod_ref, row, first, count):
    return [mod_ref[pl.ds(row, 1), (first + j) * D_MODEL:(first + j + 1) * D_MODEL] for j in range(count)]


def _post_norm(y, gain):
    ms = jnp.mean(y * y, axis=-1, keepdims=True)
    return (y * lax.rsqrt(ms + EPS)) * gain


def _adaln_kernel(c_ref, cctx_ref, w_ref, b_ref, o_ref, rows_ref):
    batch = c_ref.shape[0]
    rows_ref[...] = jnp.zeros(rows_ref.shape, F32)
    rows_ref[0:batch, :] = c_ref[...]
    rows_ref[batch:batch + 1, :] = cctx_ref[...]
    c = rows_ref[...]
    s = c * jax.nn.sigmoid(c)
    o_ref[...] = jnp.dot(s, w_ref[...], preferred_element_type=F32) + b_ref[...]


def _adaln(c, c_ctx, w_mod, b_mod):
    n_out = w_mod.shape[1]
    tn = 1536
    return pl.pallas_call(
        _adaln_kernel,
        grid=(n_out // tn,),
        in_specs=[
            pl.BlockSpec(c.shape, lambda j: (0, 0)),
            pl.BlockSpec(c_ctx.shape, lambda j: (0, 0)),
            pl.BlockSpec((D_MODEL, tn), lambda j: (0, j)),
            pl.BlockSpec((1, tn), lambda j: (0, j)),
        ],
        out_specs=pl.BlockSpec((SUBLANES, tn), lambda j: (0, j)),
        out_shape=jax.ShapeDtypeStruct((SUBLANES, n_out), F32),
        scratch_shapes=[pltpu.VMEM((SUBLANES, D_MODEL), F32)],
        compiler_params=_params(1),
        name="adaln",
    )(c, c_ctx, w_mod, b_mod.reshape(1, n_out))


def _rope(xb, cos_t, sin_t, first_half):
    sw = jnp.where(first_half, pltpu.roll(xb, LANES - 16, 1), pltpu.roll(xb, 16, 1))
    return xb * cos_t + sw * sin_t


def _token_table(row_tab, col_tab):
    n_rows = row_tab.shape[0]
    by_row = jnp.concatenate(
        [jnp.broadcast_to(row_tab[r:r + 1, :], (GRID_W, LANES)) for r in range(n_rows)], axis=0)
    return by_row + jnp.concatenate([col_tab] * n_rows, axis=0)


def _inproj_kernel(x_ref, mod_ref, g_ref, w_ref, rcos_ref, rsin_ref, ccos_ref, csin_ref,
                   wg_ref, wpa_ref, wpf_ref, wout_ref,
                   q_ref, k_ref, v_ref, f_ref, wg_o, wpa_o, wpf_o, wout_o, wb_ref, *, tiles_per_seq):
    wg_o[...] = wg_ref[...].astype(BF16)
    wpa_o[...] = wpa_ref[...].astype(BF16)
    wpf_o[...] = wpf_ref[...].astype(BF16)
    wout_o[...] = wout_ref[...].astype(BF16)

    @pl.when(pl.program_id(0) == 0)
    def _():
        low = lax.broadcasted_iota(jnp.int32, (D_MODEL, LANES), 1) < HEAD_DIM
        for j in range(Q_W // LANES):
            src_a = (j // 2) * LANES
            src_b = (GROUP // 2 + j // 2) * LANES
            a = w_ref[:, src_a:src_a + LANES]
            b = w_ref[:, src_b:src_b + LANES]
            if j % 2 == 1:
                a = pltpu.roll(a, HEAD_DIM, 1)
            else:
                b = pltpu.roll(b, HEAD_DIM, 1)
            wb_ref[:, j * LANES:(j + 1) * LANES] = jnp.where(low, a, b).astype(BF16)
        wb_ref[:, Q_W:] = w_ref[:, Q_W:].astype(BF16)

    shift, scale_mod = _mod_rows(mod_ref, pl.program_id(0) // tiles_per_seq, 0, 2)
    h = _norm_modulate(x_ref[...], g_ref[...], shift, scale_mod).astype(BF16)
    cos_t = _token_table(rcos_ref[...], ccos_ref[...])
    sin_t = _token_table(rsin_ref[...], csin_ref[...])
    lane = lax.broadcasted_iota(jnp.int32, cos_t.shape, 1)
    first_half = (lane % 32) < 16
    scale = HEAD_DIM ** -0.5 * LOG2E

    def project(col0, width):
        return jnp.dot(h, wb_ref[:, col0:col0 + width], preferred_element_type=F32)

    for c in range(0, Q_W, MXU_DIM):
        p = project(c, MXU_DIM)
        for b in range(MXU_DIM // LANES):
            blk = p[:, b * LANES:(b + 1) * LANES]
            q_ref[:, c + b * LANES:c + (b + 1) * LANES] = (
                _rope(blk, cos_t, sin_t, first_half) * scale).astype(BF16)
    p = project(Q_W, 2 * KV_W)
    k_ref[...] = _rope(p[:, :KV_W], cos_t, sin_t, first_half).astype(BF16)
    v_ref[...] = p[:, KV_W:].T.astype(BF16)
    for c in range(0, FOURIER_W, MXU_DIM):
        f_ref[:, c:c + MXU_DIM] = project(Q_W + 2 * KV_W + c, MXU_DIM)


def _row_slab_rows(n_rows, n_steps):
    rows = n_rows // n_steps
    assert rows * n_steps == n_rows and rows % BF16_ROWS == 0
    return rows


def _inproj(x2, mods, g_pre, w_in, w_pa, w_pf, w_out, rope_tabs, seq):
    n = x2.shape[0]
    tm = TOKEN_TILE
    n_steps = n // tm
    tiles_per_seq = seq // tm
    rows_per_tile = tm // GRID_W
    n_cols = Q_W + 2 * KV_W + FOURIER_W
    row_spec = pl.BlockSpec((rows_per_tile, LANES), lambda i: (i % tiles_per_seq, 0))
    g_rows = _row_slab_rows(D_MODEL, n_steps)
    pa_rows = _row_slab_rows(w_pa.shape[0], n_steps)
    pf_rows = _row_slab_rows(w_pf.shape[0], n_steps)
    out_rows = _row_slab_rows(w_out.shape[0], n_steps)
    slab = lambda rows, cols: pl.BlockSpec((rows, cols), lambda i: (i, 0))
    return pl.pallas_call(
        functools.partial(_inproj_kernel, tiles_per_seq=tiles_per_seq),
        grid=(n // tm,),
        in_specs=[
            pl.BlockSpec((tm, D_MODEL), lambda i: (i, 0)),
            _const_spec(mods.shape),
            _const_spec((1, D_MODEL)),
            _const_spec((D_MODEL, n_cols)),
            row_spec, row_spec,
            _const_spec((GRID_W, LANES)), _const_spec((GRID_W, LANES)),
            pl.BlockSpec((pl.Element(g_rows), pl.Element(2 * D_MODEL)),
                         lambda i: (i * g_rows, n_cols)),
            slab(pa_rows, D_MODEL), slab(pf_rows, D_MODEL), slab(out_rows, D_MODEL),
        ],
        out_specs=[
            pl.BlockSpec((tm, Q_W), lambda i: (i, 0)),
            pl.BlockSpec((tm, KV_W), lambda i: (i, 0)),
            pl.BlockSpec((KV_W, tm), lambda i: (0, i)),
            pl.BlockSpec((tm, FOURIER_W), lambda i: (i, 0)),
            slab(g_rows, 2 * D_MODEL), slab(pa_rows, D_MODEL), slab(pf_rows, D_MODEL), slab(out_rows, D_MODEL),
        ],
        out_shape=[
            jax.ShapeDtypeStruct((n, Q_W), BF16),
            jax.ShapeDtypeStruct((n, KV_W), BF16),
            jax.ShapeDtypeStruct((KV_W, n), BF16),
            jax.ShapeDtypeStruct((n, FOURIER_W), F32),
            jax.ShapeDtypeStruct((D_MODEL, 2 * D_MODEL), BF16),
            jax.ShapeDtypeStruct(w_pa.shape, BF16),
            jax.ShapeDtypeStruct(w_pf.shape, BF16),
            jax.ShapeDtypeStruct(w_out.shape, BF16),
        ],
        scratch_shapes=[pltpu.VMEM((D_MODEL, n_cols), BF16)],
        compiler_params=_params(1),
        name="inproj",
    )(x2, mods, g_pre, w_in, *rope_tabs, w_in, w_pa, w_pf, w_out)


def _ctxkv_kernel(x_ref, mod_ref, g_ref, w_ref, k_ref, v_ref, *, row):
    shift, scale_mod = _mod_rows(mod_ref, row, 0, 2)
    h = _norm_modulate(x_ref[...], g_ref[...], shift, scale_mod)
    p = jnp.dot(h.astype(BF16), w_ref[...].astype(BF16), preferred_element_type=F32)
    k_ref[...] = p[:, :KV_W].astype(BF16)
    v_ref[...] = p[:, KV_W:].T.astype(BF16)


def _ctxkv(ctx2, mods, ctx_row, g_pre, w_in, ctx_len):
    n = ctx2.shape[0]
    assert Q_W % (2 * KV_W) == 0
    return pl.pallas_call(
        functools.partial(_ctxkv_kernel, row=ctx_row),
        grid=(n // ctx_len,),
        in_specs=[
            pl.BlockSpec((ctx_len, D_MODEL), lambda i: (i, 0)),
            _const_spec(mods.shape),
            _const_spec((1, D_MODEL)),
            pl.BlockSpec((D_MODEL, 2 * KV_W), lambda i: (0, Q_W // (2 * KV_W)),
                         pipeline_mode=pl.Buffered(1)),
        ],
        out_specs=[
            pl.BlockSpec((ctx_len, KV_W), lambda i: (i, 0)),
            pl.BlockSpec((KV_W, ctx_len), lambda i: (0, i)),
        ],
        out_shape=[
            jax.ShapeDtypeStruct((n, KV_W), BF16),
            jax.ShapeDtypeStruct((KV_W, n), BF16),
        ],
        compiler_params=_params(1),
        name="ctxkv",
    )(ctx2, mods, g_pre, w_in)


def _attend_block(sink_ref, q_blk, k_parts, v_parts, bias_prev, bias_next):
    kcat = jnp.concatenate(k_parts, axis=0)
    vcat_t = jnp.concatenate(v_parts, axis=1)
    n_keys = kcat.shape[0]
    low = lax.broadcasted_iota(jnp.int32, (BLOCK, LANES), 1) < HEAD_DIM
    top = lax.broadcasted_iota(jnp.int32, (KV_W, n_keys), 0) < HEAD_DIM
    zero = jnp.zeros((), BF16)
    one = jnp.ones((), BF16)
    q_rows = [jnp.where(low if h == 0 else jnp.logical_not(low), q_blk[:, g * LANES:(g + 1) * LANES], zero)
              for h in range(N_KV_HEADS) for g in range(GROUP)]
    s_all = lax.dot_general(kcat, jnp.concatenate(q_rows, axis=0), (((1,), (1,)), ((), ())),
                            preferred_element_type=F32).astype(BF16)
    outs = []
    for h in range(N_KV_HEADS):
        v_h = jnp.where(top if h == 0 else jnp.logical_not(top), vcat_t, one)
        for g0 in range(0, GROUP, 2):
            probs = []
            sinks = []
            for g in (g0, g0 + 1):
                head = h * GROUP + g
                s = s_all[:, head * LANES:(head + 1) * LANES]
                parts = [s[0:BLOCK] + bias_prev,
                         s[BLOCK:2 * BLOCK],
                         s[2 * BLOCK:3 * BLOCK] + bias_next]
                parts += [s[r:r + BLOCK] for r in range(3 * BLOCK, n_keys, BLOCK)]
                mx = parts[0]
                for part in parts[1:]:
                    mx = jnp.maximum(mx, part)
                sink = sink_ref[head] * LOG2E
                m = jnp.maximum(jnp.max(mx, axis=0, keepdims=True), sink.astype(BF16))
                probs.append(jnp.concatenate([jnp.exp2(part - m) for part in parts], axis=0))
                sinks.append(jnp.exp2(sink - m.astype(F32)))
            o2 = jnp.dot(v_h, jnp.concatenate(probs, axis=1), preferred_element_type=F32)
            for idx in range(2):
                o_t = o2[:, idx * LANES:(idx + 1) * LANES]
                num = o_t[0:HEAD_DIM] if h == 0 else o_t[HEAD_DIM:]
                den = (o_t[HEAD_DIM:] if h == 0 else o_t[0:HEAD_DIM]) + sinks[idx]
                outs.append(num / den)
    return outs


def _attention_steps(sink_ref, q_ref, kp_ref, kc_ref, kn_ref, kx_ref, vp_ref, vc_ref, vn_ref, vx_ref,
                     bias_ref, o_ref):
    i = pl.program_id(1)
    n_sub = q_ref.shape[0] // BLOCK
    first_var = jnp.where(i == 0, 0, 1)
    last_var = jnp.where(i == pl.num_programs(1) - 1, 2, 1)

    def block_step(t):
        own = slice(t * BLOCK, (t + 1) * BLOCK)
        before = slice((t - 1) * BLOCK, t * BLOCK)
        after = slice((t + 1) * BLOCK, (t + 2) * BLOCK)
        k_parts = [kp_ref[...] if t == 0 else kc_ref[before, :], kc_ref[own, :],
                   kn_ref[...] if t == n_sub - 1 else kc_ref[after, :], kx_ref[...]]
        v_parts = [vp_ref[...] if t == 0 else vc_ref[:, before], vc_ref[:, own],
                   vn_ref[...] if t == n_sub - 1 else vc_ref[:, after], vx_ref[...]]
        bias_prev = bias_ref[first_var if t == 0 else 1, 0:BLOCK, :]
        bias_next = bias_ref[last_var if t == n_sub - 1 else 1, BLOCK:2 * BLOCK, :]
        outs = _attend_block(sink_ref, q_ref[own, :], k_parts, v_parts, bias_prev, bias_next)
        for j in range(Q_W // LANES):
            o_t = jnp.concatenate([outs[2 * j], outs[2 * j + 1]], axis=0)
            o_ref[own, j * LANES:(j + 1) * LANES] = o_t.T.astype(BF16)

    return [functools.partial(block_step, t) for t in range(n_sub)]


def _attention_specs(batch, seq, ctx_len, bias_shape):
    nb = seq // BLOCK
    sub = ATTN_BLOCKS_PER_STEP
    steps = nb // sub

    def cur(b, i):
        return (b * steps + i, 0)

    def prev(b, i):
        return (b * nb + jnp.maximum(i * sub - 1, 0), 0)

    def nxt(b, i):
        return (b * nb + jnp.minimum((i + 1) * sub, nb - 1), 0)

    def swap(f):
        return lambda b, i: f(b, i)[::-1]

    in_specs = [
        pl.BlockSpec(memory_space=pltpu.SMEM),
        pl.BlockSpec((sub * BLOCK, Q_W), cur),
        pl.BlockSpec((BLOCK, KV_W), prev),
        pl.BlockSpec((sub * BLOCK, KV_W), cur),
        pl.BlockSpec((BLOCK, KV_W), nxt),
        pl.BlockSpec((ctx_len, KV_W), lambda b, i: (b, 0)),
        pl.BlockSpec((KV_W, BLOCK), swap(prev)),
        pl.BlockSpec((KV_W, sub * BLOCK), swap(cur)),
        pl.BlockSpec((KV_W, BLOCK), swap(nxt)),
        pl.BlockSpec((KV_W, ctx_len), lambda b, i: (0, b)),
        _const_spec(bias_shape),
    ]
    return steps, in_specs, pl.BlockSpec((sub * BLOCK, Q_W), cur)


def _band_bias():
    qi = np.arange(BLOCK)[None, :]
    kj = np.arange(3 * BLOCK)[:, None]
    rel = kj - BLOCK - qi
    in_window = np.abs(rel) <= WINDOW
    variants = []
    for has_prev, has_next in ((False, True), (True, True), (True, False)):
        ok = in_window.copy()
        if not has_prev:
            ok &= kj >= BLOCK
        if not has_next:
            ok &= kj < 2 * BLOCK
        mask = np.where(ok, 0.0, NEG).astype(np.float32)
        variants.append(np.concatenate([mask[:BLOCK], mask[2 * BLOCK:]], axis=0))
    return jnp.asarray(np.stack(variants)).astype(BF16)


def _minor_dft_terms():
    assert DFT_MINOR == 8
    plan = []
    for k2 in range(DFT_MINOR):
        unit, half = [], []
        for s2 in range(DFT_MINOR):
            ang = -2.0 * math.pi * ((s2 * k2) % DFT_MINOR) / DFT_MINOR
            for part, coef in ((0, math.cos(ang)), (1, -math.sin(ang))):
                if abs(coef) < 1e-9:
                    continue
                target = unit if abs(abs(coef) - 1.0) < 1e-9 else half
                assert target is unit or abs(abs(coef) - math.sqrt(0.5)) < 1e-9
                target.append((s2, part, 1 if coef > 0 else -1))
        plan.append((unit, half))
    return plan


def _signed_sum(terms, pick):
    pos = [pick(s2, part) for s2, part, sign in terms if sign > 0]
    neg = [pick(s2, part) for s2, part, sign in terms if sign < 0]
    total = None
    for v in pos:
        total = v if total is None else total + v
    for v in neg:
        total = -v if total is None else total - v
    return total


def _fourier_steps(f_ref, f1_ref, tr_ref, ti_ref, cm_ref, o_ref):
    n_major = f_ref.shape[0] // DFT_MINOR
    pair_w = 2 * LANES
    n_pairs = DFT_MINOR // 2
    row_chunk = n_major // n_pairs
    twiddled = [None] * n_pairs
    spectra = [None] * DFT_MINOR
    plan = _minor_dft_terms()

    def stage1(c):
        y = jnp.concatenate(
            [f_ref[pl.ds(s2, n_major, stride=DFT_MINOR), :].astype(BF16) for s2 in (2 * c, 2 * c + 1)],
            axis=1)
        a = jnp.dot(f1_ref[...], y, preferred_element_type=F32)
        ar = a[:n_major]
        ai = a[n_major:]
        tr = tr_ref[:, c * pair_w:(c + 1) * pair_w]
        ti = ti_ref[:, c * pair_w:(c + 1) * pair_w]
        twiddled[c] = ((ar * tr - ai * ti).astype(BF16), (ar * ti + ai * tr).astype(BF16))

    def channels(c):
        apr, api = twiddled[c]
        for t in range(2):
            lanes = slice(t * LANES, (t + 1) * LANES)
            ap = jnp.concatenate([apr[:, lanes], api[:, lanes]], axis=1)
            spectra[2 * c + t] = jnp.dot(ap, cm_ref[...], preferred_element_type=F32)

    def stage2(r):
        rows = slice(r * row_chunk, (r + 1) * row_chunk)

        def pick(s2, part):
            return spectra[s2][rows, part * LANES:(part + 1) * LANES]

        for k2, (unit, half) in enumerate(plan):
            res = _signed_sum(unit, pick)
            if half:
                scaled = _signed_sum(half, pick) * math.sqrt(0.5)
                res = scaled if res is None else res + scaled
            o_ref[pl.ds(k2 * n_major + r * row_chunk, row_chunk), :] = res.astype(BF16)

    steps = [functools.partial(stage1, 0)]
    for c in range(1, n_pairs):
        steps += [functools.partial(stage1, c), functools.partial(channels, c - 1)]
    steps.append(functools.partial(channels, n_pairs - 1))
    return steps + [functools.partial(stage2, r) for r in range(n_pairs)]


def _cis(num, den):
    ang = (num % den).astype(F32) * F32(-2.0 * math.pi / den)
    return jnp.cos(ang), jnp.sin(ang)


def _dft_tables(seq):
    n_major = seq // DFT_MINOR
    ch = FOURIER_GROUP_CH
    root = int(round(math.sqrt(n_major)))
    assert root * root == n_major
    s1 = jnp.arange(n_major, dtype=jnp.int32)[None, :]
    hi = jnp.arange(root, dtype=jnp.int32)[:, None] * root
    lo = jnp.arange(root, dtype=jnp.int32)[:, None]
    coarse = _cis(hi * s1, n_major) + _cis(lo * s1, n_major)
    k1 = jnp.arange(n_major, dtype=jnp.int32)[:, None]
    s2 = jnp.arange(DFT_MINOR, dtype=jnp.int32)[None, :]
    twiddle = _cis(k1 * s2, seq)
    c128 = jnp.arange(ch, dtype=jnp.int32)
    chan = _cis(c128[:, None] * c128[None, :], ch)
    return coarse + twiddle + chan


def _expand_dft_tables(hr_ref, hi_ref, lr_ref, li_ref, twr_ref, twi_ref, cr_ref, ci_ref,
                       f1_ref, tr_ref, ti_ref, cm_ref):
    root, n_major = hr_ref.shape
    lr = lr_ref[...]
    li = li_ref[...]
    for a in range(root):
        har = hr_ref[a:a + 1, :]
        hai = hi_ref[a:a + 1, :]
        f1_ref[a * root:(a + 1) * root, :] = (har * lr - hai * li).astype(BF16)
        f1_ref[n_major + a * root:n_major + (a + 1) * root, :] = (har * li + hai * lr).astype(BF16)
    for s2 in range(DFT_MINOR):
        lanes = slice(s2 * LANES, (s2 + 1) * LANES)
        tr_ref[:, lanes] = jnp.broadcast_to(twr_ref[:, s2:s2 + 1], (n_major, LANES))
        ti_ref[:, lanes] = jnp.broadcast_to(twi_ref[:, s2:s2 + 1], (n_major, LANES))
    ch = cr_ref.shape[0]
    norm = 1.0 / math.sqrt(n_major * DFT_MINOR * ch)
    cr = cr_ref[...] * norm
    ci = ci_ref[...] * norm
    cm_ref[0:ch, 0:ch] = cr.astype(BF16)
    cm_ref[0:ch, ch:2 * ch] = ci.astype(BF16)
    cm_ref[ch:2 * ch, 0:ch] = (-ci).astype(BF16)
    cm_ref[ch:2 * ch, ch:2 * ch] = cr.astype(BF16)


def _mix_kernel(sink_ref, q_ref, kp_ref, kc_ref, kn_ref, kx_ref, vp_ref, vc_ref, vn_ref, vx_ref, bias_ref,
                f_ref, hr_ref, hi_ref, lr_ref, li_ref, twr_ref, twi_ref, cr_ref, ci_ref,
                attn_ref, four_ref, f1_ref, tr_ref, ti_ref, cm_ref):
    @pl.when((pl.program_id(0) == 0) & (pl.program_id(1) == 0))
    def _():
        _expand_dft_tables(hr_ref, hi_ref, lr_ref, li_ref, twr_ref, twi_ref, cr_ref, ci_ref,
                           f1_ref, tr_ref, ti_ref, cm_ref)

    attn_steps = _attention_steps(sink_ref, q_ref, kp_ref, kc_ref, kn_ref, kx_ref,
                                  vp_ref, vc_ref, vn_ref, vx_ref, bias_ref, attn_ref)
    four_steps = _fourier_steps(f_ref, f1_ref, tr_ref, ti_ref, cm_ref, four_ref)
    done = 0
    for n, four_step in enumerate(four_steps):
        upto = (n + 1) * len(attn_steps) // len(four_steps)
        for attn_step in attn_steps[done:upto]:
            attn_step()
        done = upto
        four_step()


def _mix(sink, q, k, v_t, kx, vx_t, bias, f3, tables, ctx_len):
    batch, seq, _ = f3.shape
    ch = FOURIER_GROUP_CH
    n_major = seq // DFT_MINOR
    steps, attn_in, attn_out = _attention_specs(batch, seq, ctx_len, bias.shape)
    assert steps == N_FOURIER_GROUPS
    four_spec = pl.BlockSpec((None, seq, ch), lambda b, g: (b, 0, g))
    return pl.pallas_call(
        _mix_kernel,
        grid=(batch, steps),
        in_specs=attn_in + [four_spec] + [_const_spec(t.shape) for t in tables],
        out_specs=[attn_out, four_spec],
        out_shape=[jax.ShapeDtypeStruct((batch * seq, Q_W), BF16),
                   jax.ShapeDtypeStruct((batch, seq, FOURIER_W), BF16)],
        scratch_shapes=[pltpu.VMEM((2 * n_major, n_major), BF16),
                        pltpu.VMEM((n_major, DFT_MINOR * ch), F32),
                        pltpu.VMEM((n_major, DFT_MINOR * ch), F32),
                        pltpu.VMEM((2 * ch, 2 * ch), BF16)],
        compiler_params=_params(2),
        name="mix",
    )(sink, q, k, k, k, kx, v_t, v_t, v_t, vx_t, bias, f3, *tables)


def _merge_kernel(x_ref, a_ref, f_ref, mod_ref, gpre_ref, gpost_ref,
                  wg_b, wpa_b, wpf_b, wout_b, wup_ref, wdown_ref,
                  o_ref, wup_o, wdown_o, *, tiles_per_seq):
    wup_o[...] = wup_ref[...].astype(BF16)
    wdown_o[...] = wdown_ref[...].astype(BF16)

    x = x_ref[...]
    shift, scale_mod, out_gate = _mod_rows(mod_ref, pl.program_id(0) // tiles_per_seq, 0, 3)
    h = _norm_modulate(x, gpre_ref[...], shift, scale_mod).astype(BF16)
    a = a_ref[...]
    f = f_ref[...]
    y = None
    for c in range(0, D_MODEL, MXU_DIM):
        cols = slice(c, c + MXU_DIM)
        gate_a = jnp.dot(h, wg_b[:, cols], preferred_element_type=F32)
        gate_f = jnp.dot(h, wg_b[:, D_MODEL + c:D_MODEL + c + MXU_DIM], preferred_element_type=F32)
        pa = jnp.dot(a, wpa_b[:, cols], preferred_element_type=F32)
        pf = jnp.dot(f, wpf_b[:, cols], preferred_element_type=F32)
        m = (jax.nn.sigmoid(gate_a) * pa + jax.nn.sigmoid(gate_f) * pf).astype(BF16)
        part = jnp.dot(m, wout_b[cols, :], preferred_element_type=F32)
        y = part if y is None else y + part
    o_ref[...] = x + out_gate * _post_norm(y, gpost_ref[...])


def _merge(x2, attn, four, mods, g_pre, g_post, w_g, w_pa, w_pf, w_out, w_up, w_down, seq):
    n = x2.shape[0]
    tm = TOKEN_TILE
    n_steps = n // tm
    tiles_per_seq = seq // tm
    row = lambda i: (i, 0)
    up_rows = w_up.shape[0] // n_steps
    down_rows = w_down.shape[0] // BF16_ROWS
    n_down = w_down.shape[0] // down_rows
    assert up_rows * n_steps == w_up.shape[0] and up_rows % BF16_ROWS == 0
    assert down_rows % BF16_ROWS == 0 and n_down <= n_steps
    up_spec = pl.BlockSpec((up_rows, w_up.shape[1]), row)
    down_spec = pl.BlockSpec((down_rows, w_down.shape[1]), lambda i: (jnp.minimum(i, n_down - 1), 0))
    return pl.pallas_call(
        functools.partial(_merge_kernel, tiles_per_seq=tiles_per_seq),
        grid=(n // tm,),
        in_specs=[
            pl.BlockSpec((tm, D_MODEL), row),
            pl.BlockSpec((tm, Q_W), row),
            pl.BlockSpec((tm, FOURIER_W), row),
            _const_spec(mods.shape),
            _const_spec((1, D_MODEL)),
            _const_spec((1, D_MODEL)),
            _const_spec(w_g.shape),
            _const_spec(w_pa.shape),
            _const_spec(w_pf.shape),
            _const_spec(w_out.shape),
            up_spec,
            down_spec,
        ],
        out_specs=[pl.BlockSpec((tm, D_MODEL), row), up_spec, down_spec],
        out_shape=[jax.ShapeDtypeStruct((n, D_MODEL), F32),
                   jax.ShapeDtypeStruct(w_up.shape, BF16),
                   jax.ShapeDtypeStruct(w_down.shape, BF16)],
        compiler_params=_params(1),
        name="merge",
    )(x2, attn, four, mods, g_pre, g_post, w_g, w_pa, w_pf, w_out, w_up, w_down)


def _convffn_kernel(x_ref, xp_ref, xn_ref, mod_ref, gpre_ref, gpost_ref,
                    wu_ref, wgate_ref, cw_ref, cb_ref, wd_ref, o_ref, act_ref, *, tiles_per_seq):
    i = pl.program_id(0)
    tm = x_ref.shape[0]
    halo = SUBLANES
    has_prev = (i % tiles_per_seq != 0).astype(F32)
    has_next = (i % tiles_per_seq != tiles_per_seq - 1).astype(F32)
    x = x_ref[...]
    shift, scale, out_gate = _mod_rows(mod_ref, i // tiles_per_seq, 3, 3)
    gain = gpre_ref[...]
    h = _norm_modulate(x, gain, shift, scale)
    hp = _norm_modulate(xp_ref[...], gain, shift, scale) * has_prev
    hn = _norm_modulate(xn_ref[...], gain, shift, scale) * has_next
    h_ext = jnp.concatenate([hp, h, hn], axis=0).astype(BF16)
    h_mid = h.astype(BF16)
    n_ext = tm + 2 * halo
    for c in range(D_FF // FFN_CHUNK):
        cols = slice(c * FFN_CHUNK, (c + 1) * FFN_CHUNK)
        u = jnp.dot(h_ext, wu_ref[:, cols], preferred_element_type=F32)
        gate = jnp.dot(h_mid, wgate_ref[:, cols], preferred_element_type=F32)
        u_prev = pltpu.roll(u, 1, 0)[halo:halo + tm]
        u_next = pltpu.roll(u, n_ext - 1, 0)[halo:halo + tm]
        conv = (u_prev * cw_ref[0:1, cols] + u[halo:halo + tm] * cw_ref[1:2, cols]
                + u_next * cw_ref[2:3, cols] + cb_ref[:, cols])
        act_ref[:, cols] = (conv * jax.nn.sigmoid(conv) * gate).astype(BF16)
    y = jnp.dot(act_ref[...], wd_ref[...], preferred_element_type=F32)
    o_ref[...] = x + out_gate * _post_norm(y, gpost_ref[...])


def _convffn(x1, mods, g_pre, g_post, w_up, conv_w, conv_b, w_down, seq):
    n = x1.shape[0]
    tm = TOKEN_TILE
    tiles_per_seq = seq // tm
    halo_blocks_per_tile = tm // SUBLANES
    n_halo_blocks = n // SUBLANES
    return pl.pallas_call(
        functools.partial(_convffn_kernel, tiles_per_seq=tiles_per_seq),
        grid=(n // tm,),
        in_specs=[
            pl.BlockSpec((tm, D_MODEL), lambda i: (i, 0)),
            pl.BlockSpec((SUBLANES, D_MODEL),
                         lambda i: (jnp.maximum(i * halo_blocks_per_tile - 1, 0), 0)),
            pl.BlockSpec((SUBLANES, D_MODEL),
                         lambda i: (jnp.minimum((i + 1) * halo_blocks_per_tile, n_halo_blocks - 1), 0)),
            _const_spec(mods.shape),
            _const_spec((1, D_MODEL)),
            _const_spec((1, D_MODEL)),
            pl.BlockSpec((D_MODEL, D_FF), lambda i: (0, 0), pipeline_mode=pl.Buffered(1)),
            pl.BlockSpec((D_MODEL, D_FF), lambda i: (0, 1), pipeline_mode=pl.Buffered(1)),
            _const_spec(conv_w.shape),
            _const_spec((1, D_FF)),
            _const_spec(w_down.shape),
        ],
        out_specs=pl.BlockSpec((tm, D_MODEL), lambda i: (i, 0)),
        out_shape=jax.ShapeDtypeStruct((n, D_MODEL), F32),
        scratch_shapes=[pltpu.VMEM((tm, D_FF), BF16)],
        compiler_params=_params(1),
        name="convffn",
    )(x1, x1, x1, mods, g_pre, g_post, w_up, w_up, conv_w, conv_b.reshape(1, D_FF), w_down)


def _rope_tables(seq):
    half = HEAD_DIM // 2
    inv_freq = ROPE_THETA ** (-jnp.arange(0, half, 2, dtype=F32) / half)
    ang_r = jnp.arange(seq // GRID_W).astype(F32)[:, None] * inv_freq
    ang_c = jnp.arange(GRID_W).astype(F32)[:, None] * inv_freq
    reps = LANES // HEAD_DIM

    def lanes(row_part, col_part):
        return jnp.tile(jnp.concatenate([row_part, col_part], axis=-1), (1, reps))

    zr = jnp.zeros((ang_r.shape[0], half), F32)
    zc = jnp.zeros((ang_c.shape[0], half), F32)
    rcos = lanes(jnp.concatenate([jnp.cos(ang_r), jnp.cos(ang_r)], axis=-1), zr)
    rsin = lanes(jnp.concatenate([-jnp.sin(ang_r), jnp.sin(ang_r)], axis=-1), zr)
    ccos = lanes(zc, jnp.concatenate([jnp.cos(ang_c), jnp.cos(ang_c)], axis=-1))
    csin = lanes(zc, jnp.concatenate([-jnp.sin(ang_c), jnp.sin(ang_c)], axis=-1))
    return rcos, rsin, ccos, csin


def kernel(x, c, ctx, c_ctx, w_mod, b_mod, g_pre1, g_post1, g_pre2, g_post2,
           w_in, sink, w_pa, w_pf, w_out, w_up, conv_w, conv_b, w_down):
    batch, seq, d = x.shape
    ctx_len = ctx.shape[1]
    depth = w_mod.shape[0]
    assert depth == 1 and d == D_MODEL and batch + 1 <= SUBLANES
    assert seq % TOKEN_TILE == 0 and TOKEN_TILE % GRID_W == 0
    assert seq % (DFT_MINOR * SUBLANES) == 0
    assert seq % (ATTN_BLOCKS_PER_STEP * BLOCK) == 0
    n = batch * seq
    rope_tabs = _rope_tables(seq)
    tables = _dft_tables(seq)
    bias = _band_bias()

    l = 0
    mods = _adaln(c, c_ctx.reshape(1, d), w_mod[l], b_mod[l])

    g_pre1_row = g_pre1[l].reshape(1, d)
    g_post1_row = g_post1[l].reshape(1, d)
    g_pre2_row = g_pre2[l].reshape(1, d)
    g_post2_row = g_post2[l].reshape(1, d)

    x2 = x.reshape(n, d)
    q, k, v_t, f, w_g_b, w_pa_b, w_pf_b, w_out_b = _inproj(x2, mods, g_pre1_row, w_in[l], w_pa[l], w_pf[l], w_out[l],
                                                          rope_tabs, seq)
    kx, vx_t = _ctxkv(ctx.reshape(batch * ctx_len, d), mods, batch, g_pre1_row, w_in[l], ctx_len)
    attn, four = _mix(sink[l], q, k, v_t, kx, vx_t, bias, f.reshape(batch, seq, FOURIER_W), tables, ctx_len)
    four = four.reshape(n, FOURIER_W)
    x1, w_up_b, w_down_b = _merge(x2, attn, four, mods, g_pre1_row, g_post1_row, w_g_b, w_pa_b, w_pf_b, w_out_b,
                                  w_up[l], w_down[l], seq)
    out = _convffn(x1, mods, g_pre2_row, g_post2_row, w_up_b, conv_w[l], conv_b[l], w_down_b, seq)
    return out.reshape(batch, seq, d)
```

```python
import functools
import math

import numpy as np
import jax
import jax.numpy as jnp
from jax import lax
from jax.experimental import pallas as pl
from jax.experimental.pallas import tpu as pltpu

F32 = jnp.float32
BF16 = jnp.bfloat16

D_MODEL = 1024
GRID_W = 64
HEAD_DIM = 64
N_Q_HEADS = 8
N_KV_HEADS = 2
GROUP = N_Q_HEADS // N_KV_HEADS
WINDOW = 128
BLOCK = 128
ROPE_THETA = 10000.0
N_FOURIER_GROUPS = 4
FOURIER_GROUP_CH = 128
FOURIER_W = N_FOURIER_GROUPS * FOURIER_GROUP_CH
Q_W = N_Q_HEADS * HEAD_DIM
KV_W = N_KV_HEADS * HEAD_DIM
D_FF = 2816
EPS = 1e-6
NEG = -1e30
LOG2E = math.log2(math.e)

LANES = 128
SUBLANES = 8
BF16_ROWS = 16
MXU_DIM = 256
VMEM_LIMIT_BYTES = 56 * 1024 * 1024

TOKEN_TILE = 1024
FFN_CHUNK = MXU_DIM
DFT_MINOR = SUBLANES
ATTN_BLOCKS_PER_STEP = 16


def _const_spec(shape):
    nd = len(shape)
    return pl.BlockSpec(shape, lambda *_: (0,) * nd, pipeline_mode=pl.Buffered(1))


def _params(n_axes):
    return pltpu.CompilerParams(
        dimension_semantics=("arbitrary",) * n_axes,
        vmem_limit_bytes=VMEM_LIMIT_BYTES,
    )


def _norm_modulate(x, gain, shift, scale):
    ms = jnp.mean(x * x, axis=-1, keepdims=True)
    return (x * lax.rsqrt(ms + EPS)) * (gain * (1.0 + scale)) + shift


def _mod_rows(mod_ref, row, first, count):
    return [mod_ref[pl.ds(row, 1), (first + j) * D_MODEL:(first + j + 1) * D_MODEL] for j in range(count)]


def _post_norm(y, gain):
    ms = jnp.mean(y * y, axis=-1, keepdims=True)
    return (y * lax.rsqrt(ms + EPS)) * gain


def _adaln_kernel(c_ref, cctx_ref, w_ref, b_ref, o_ref, rows_ref):
    batch = c_ref.shape[0]
    rows_ref[...] = jnp.zeros(rows_ref.shape, F32)
    rows_ref[0:batch, :] = c_ref[...]
    rows_ref[batch:batch + 1, :] = cctx_ref[...]
    c = rows_ref[...]
    s = c * jax.nn.sigmoid(c)
    o_ref[...] = jnp.dot(s, w_ref[...], preferred_element_type=F32) + b_ref[...]


def _adaln(c, c_ctx, w_mod, b_mod):
    n_out = w_mod.shape[1]
    tn = 1536
    return pl.pallas_call(
        _adaln_kernel,
        grid=(n_out // tn,),
        in_specs=[
            pl.BlockSpec(c.shape, lambda j: (0, 0)),
            pl.BlockSpec(c_ctx.shape, lambda j: (0, 0)),
            pl.BlockSpec((D_MODEL, tn), lambda j: (0, j)),
            pl.BlockSpec((1, tn), lambda j: (0, j)),
        ],
        out_specs=pl.BlockSpec((SUBLANES, tn), lambda j: (0, j)),
        out_shape=jax.ShapeDtypeStruct((SUBLANES, n_out), F32),
        scratch_shapes=[pltpu.VMEM((SUBLANES, D_MODEL), F32)],
        compiler_params=_params(1),
        name="adaln",
    )(c, c_ctx, w_mod, b_mod.reshape(1, n_out))


def _rope(xb, cos_t, sin_t, first_half):
    sw = jnp.where(first_half, pltpu.roll(xb, LANES - 16, 1), pltpu.roll(xb, 16, 1))
    return xb * cos_t + sw * sin_t


def _token_table(row_tab, col_tab):
    n_rows = row_tab.shape[0]
    by_row = jnp.concatenate(
        [jnp.broadcast_to(row_tab[r:r + 1, :], (GRID_W, LANES)) for r in range(n_rows)], axis=0)
    return by_row + jnp.concatenate([col_tab] * n_rows, axis=0)


def _inproj_kernel(x_ref, mod_ref, g_ref, w_ref, rcos_ref, rsin_ref, ccos_ref, csin_ref,
                   wg_ref, wpa_ref, wpf_ref, wout_ref,
                   q_ref, k_ref, v_ref, f_ref, wg_o, wpa_o, wpf_o, wout_o, wb_ref, *, tiles_per_seq):
    wg_o[...] = wg_ref[...].astype(BF16)
    wpa_o[...] = wpa_ref[...].astype(BF16)
    wpf_o[...] = wpf_ref[...].astype(BF16)
    wout_o[...] = wout_ref[...].astype(BF16)

    @pl.when(pl.program_id(0) == 0)
    def _():
        low = lax.broadcasted_iota(jnp.int32, (D_MODEL, LANES), 1) < HEAD_DIM
        for j in range(Q_W // LANES):
            src_a = (j // 2) * LANES
            src_b = (GROUP // 2 + j // 2) * LANES
            a = w_ref[:, src_a:src_a + LANES]
            b = w_ref[:, src_b:src_b + LANES]
            if j % 2 == 1:
                a = pltpu.roll(a, HEAD_DIM, 1)
            else:
                b = pltpu.roll(b, HEAD_DIM, 1)
            wb_ref[:, j * LANES:(j + 1) * LANES] = jnp.where(low, a, b).astype(BF16)
        wb_ref[:, Q_W:] = w_ref[:, Q_W:].astype(BF16)

    shift, scale_mod = _mod_rows(mod_ref, pl.program_id(0) // tiles_per_seq, 0, 2)
    h = _norm_modulate(x_ref[...], g_ref[...], shift, scale_mod).astype(BF16)
    cos_t = _token_table(rcos_ref[...], ccos_ref[...])
    sin_t = _token_table(rsin_ref[...], csin_ref[...])
    lane = lax.broadcasted_iota(jnp.int32, cos_t.shape, 1)
    first_half = (lane % 32) < 16
    scale = HEAD_DIM ** -0.5 * LOG2E

    def project(col0, width):
        return jnp.dot(h, wb_ref[:, col0:col0 + width], preferred_element_type=F32)

    for c in range(0, Q_W, MXU_DIM):
        p = project(c, MXU_DIM)
        for b in range(MXU_DIM // LANES):
            blk = p[:, b * LANES:(b + 1) * LANES]
            q_ref[:, c + b * LANES:c + (b + 1) * LANES] = (
                _rope(blk, cos_t, sin_t, first_half) * scale).astype(BF16)
    p = project(Q_W, 2 * KV_W)
    k_ref[...] = _rope(p[:, :KV_W], cos_t, sin_t, first_half).astype(BF16)
    v_ref[...] = p[:, KV_W:].T.astype(BF16)
    for c in range(0, FOURIER_W, MXU_DIM):
        f_ref[:, c:c + MXU_DIM] = project(Q_W + 2 * KV_W + c, MXU_DIM)


def _row_slab_rows(n_rows, n_steps):
    rows = n_rows // n_steps
    assert rows * n_steps == n_rows and rows % BF16_ROWS == 0
    return rows


def _inproj(x2, mods, g_pre, w_in, w_pa, w_pf, w_out, rope_tabs, seq):
    n = x2.shape[0]
    tm = TOKEN_TILE
    n_steps = n // tm
    tiles_per_seq = seq // tm
    rows_per_tile = tm // GRID_W
    n_cols = Q_W + 2 * KV_W + FOURIER_W
    row_spec = pl.BlockSpec((rows_per_tile, LANES), lambda i: (i % tiles_per_seq, 0))
    g_rows = _row_slab_rows(D_MODEL, n_steps)
    pa_rows = _row_slab_rows(w_pa.shape[0], n_steps)
    pf_rows = _row_slab_rows(w_pf.shape[0], n_steps)
    out_rows = _row_slab_rows(w_out.shape[0], n_steps)
    slab = lambda rows, cols: pl.BlockSpec((rows, cols), lambda i: (i, 0))
    return pl.pallas_call(
        functools.partial(_inproj_kernel, tiles_per_seq=tiles_per_seq),
        grid=(n // tm,),
        in_specs=[
            pl.BlockSpec((tm, D_MODEL), lambda i: (i, 0)),
            _const_spec(mods.shape),
            _const_spec((1, D_MODEL)),
            _const_spec((D_MODEL, n_cols)),
            row_spec, row_spec,
            _const_spec((GRID_W, LANES)), _const_spec((GRID_W, LANES)),
            pl.BlockSpec((pl.Element(g_rows), pl.Element(2 * D_MODEL)),
                         lambda i: (i * g_rows, n_cols)),
            slab(pa_rows, D_MODEL), slab(pf_rows, D_MODEL), slab(out_rows, D_MODEL),
        ],
        out_specs=[
            pl.BlockSpec((tm, Q_W), lambda i: (i, 0)),
            pl.BlockSpec((tm, KV_W), lambda i: (i, 0)),
            pl.BlockSpec((KV_W, tm), lambda i: (0, i)),
            pl.BlockSpec((tm, FOURIER_W), lambda i: (i, 0)),
            slab(g_rows, 2 * D_MODEL), slab(pa_rows, D_MODEL), slab(pf_rows, D_MODEL), slab(out_rows, D_MODEL),
        ],
        out_shape=[
            jax.ShapeDtypeStruct((n, Q_W), BF16),
            jax.ShapeDtypeStruct((n, KV_W), BF16),
            jax.ShapeDtypeStruct((KV_W, n), BF16),
            jax.ShapeDtypeStruct((n, FOURIER_W), F32),
            jax.ShapeDtypeStruct((D_MODEL, 2 * D_MODEL), BF16),
            jax.ShapeDtypeStruct(w_pa.shape, BF16),
            jax.ShapeDtypeStruct(w_pf.shape, BF16),
            jax.ShapeDtypeStruct(w_out.shape, BF16),
        ],
        scratch_shapes=[pltpu.VMEM((D_MODEL, n_cols), BF16)],
        compiler_params=_params(1),
        name="inproj",
    )(x2, mods, g_pre, w_in, *rope_tabs, w_in, w_pa, w_pf, w_out)


def _ctxkv_kernel(x_ref, mod_ref, g_ref, w_ref, k_ref, v_ref, *, row):
    shift, scale_mod = _mod_rows(mod_ref, row, 0, 2)
    h = _norm_modulate(x_ref[...], g_ref[...], shift, scale_mod)
    p = jnp.dot(h.astype(BF16), w_ref[...].astype(BF16), preferred_element_type=F32)
    k_ref[...] = p[:, :KV_W].astype(BF16)
    v_ref[...] = p[:, KV_W:].T.astype(BF16)


def _ctxkv(ctx2, mods, ctx_row, g_pre, w_in, ctx_len):
    n = ctx2.shape[0]
    assert Q_W % (2 * KV_W) == 0
    return pl.pallas_call(
        functools.partial(_ctxkv_kernel, row=ctx_row),
        grid=(n // ctx_len,),
        in_specs=[
            pl.BlockSpec((ctx_len, D_MODEL), lambda i: (i, 0)),
            _const_spec(mods.shape),
            _const_spec((1, D_MODEL)),
            pl.BlockSpec((D_MODEL, 2 * KV_W), lambda i: (0, Q_W // (2 * KV_W)),
                         pipeline_mode=pl.Buffered(1)),
        ],
        out_specs=[
            pl.BlockSpec((ctx_len, KV_W), lambda i: (i, 0)),
            pl.BlockSpec((KV_W, ctx_len), lambda i: (0, i)),
        ],
        out_shape=[
            jax.ShapeDtypeStruct((n, KV_W), BF16),
            jax.ShapeDtypeStruct((KV_W, n), BF16),
        ],
        compiler_params=_params(1),
        name="ctxkv",
    )(ctx2, mods, g_pre, w_in)


def _attend_block(sink_ref, q_blk, k_parts, v_parts, bias_prev, bias_next):
    kcat = jnp.concatenate(k_parts, axis=0)
    vcat_t = jnp.concatenate(v_parts, axis=1)
    n_keys = kcat.shape[0]
    low = lax.broadcasted_iota(jnp.int32, (BLOCK, LANES), 1) < HEAD_DIM
    top = lax.broadcasted_iota(jnp.int32, (KV_W, n_keys), 0) < HEAD_DIM
    zero = jnp.zeros((), BF16)
    one = jnp.ones((), BF16)
    q_rows = [jnp.where(low if h == 0 else jnp.logical_not(low), q_blk[:, g * LANES:(g + 1) * LANES], zero)
              for h in range(N_KV_HEADS) for g in range(GROUP)]
    s_all = lax.dot_general(kcat, jnp.concatenate(q_rows, axis=0), (((1,), (1,)), ((), ())),
                            preferred_element_type=F32).astype(BF16)
    outs = []
    for h in range(N_KV_HEADS):
        v_h = jnp.where(top if h == 0 else jnp.logical_not(top), vcat_t, one)
        for g0 in range(0, GROUP, 2):
            probs = []
            sinks = []
            for g in (g0, g0 + 1):
                head = h * GROUP + g
                s = s_all[:, head * LANES:(head + 1) * LANES]
                parts = [s[0:BLOCK] + bias_prev,
                         s[BLOCK:2 * BLOCK],
                         s[2 * BLOCK:3 * BLOCK] + bias_next]
                parts += [s[r:r + BLOCK] for r in range(3 * BLOCK, n_keys, BLOCK)]
                mx = parts[0]
                for part in parts[1:]:
                    mx = jnp.maximum(mx, part)
                sink = sink_ref[head] * LOG2E
                m = jnp.maximum(jnp.max(mx, axis=0, keepdims=True), sink.astype(BF16))
                probs.append(jnp.concatenate([jnp.exp2(part - m) for part in parts], axis=0))
                sinks.append(jnp.exp2(sink - m.astype(F32)))
            o2 = jnp.dot(v_h, jnp.concatenate(probs, axis=1), preferred_element_type=F32)
            for idx in range(2):
                o_t = o2[:, idx * LANES:(idx + 1) * LANES]
                num = o_t[0:HEAD_DIM] if h == 0 else o_t[HEAD_DIM:]
                den = (o_t[HEAD_DIM:] if h == 0 else o_t[0:HEAD_DIM]) + sinks[idx]
                outs.append(num / den)
    return outs


def _attention_steps(sink_ref, q_ref, kp_ref, kc_ref, kn_ref, kx_ref, vp_ref, vc_ref, vn_ref, vx_ref,
                     bias_ref, o_ref):
    i = pl.program_id(1)
    n_sub = q_ref.shape[0] // BLOCK
    first_var = jnp.where(i == 0, 0, 1)
    last_var = jnp.where(i == pl.num_programs(1) - 1, 2, 1)

    def block_step(t):
        own = slice(t * BLOCK, (t + 1) * BLOCK)
        before = slice((t - 1) * BLOCK, t * BLOCK)
        after = slice((t + 1) * BLOCK, (t + 2) * BLOCK)
        k_parts = [kp_ref[...] if t == 0 else kc_ref[before, :], kc_ref[own, :],
                   kn_ref[...] if t == n_sub - 1 else kc_ref[after, :], kx_ref[...]]
        v_parts = [vp_ref[...] if t == 0 else vc_ref[:, before], vc_ref[:, own],
                   vn_ref[...] if t == n_sub - 1 else vc_ref[:, after], vx_ref[...]]
        bias_prev = bias_ref[first_var if t == 0 else 1, 0:BLOCK, :]
        bias_next = bias_ref[last_var if t == n_sub - 1 else 1, BLOCK:2 * BLOCK, :]
        outs = _attend_block(sink_ref, q_ref[own, :], k_parts, v_parts, bias_prev, bias_next)
        for j in range(Q_W // LANES):
            o_t = jnp.concatenate([outs[2 * j], outs[2 * j + 1]], axis=0)
            o_ref[own, j * LANES:(j + 1) * LANES] = o_t.T.astype(BF16)

    return [functools.partial(block_step, t) for t in range(n_sub)]


def _attention_specs(batch, seq, ctx_len, bias_shape):
    nb = seq // BLOCK
    sub = ATTN_BLOCKS_PER_STEP
    steps = nb // sub

    def cur(b, i):
        return (b * steps + i, 0)

    def prev(b, i):
        return (b * nb + jnp.maximum(i * sub - 1, 0), 0)

    def nxt(b, i):
        return (b * nb + jnp.minimum((i + 1) * sub, nb - 1), 0)

    def swap(f):
        return lambda b, i: f(b, i)[::-1]

    in_specs = [
        pl.BlockSpec(memory_space=pltpu.SMEM),
        pl.BlockSpec((sub * BLOCK, Q_W), cur),
        pl.BlockSpec((BLOCK, KV_W), prev),
        pl.BlockSpec((sub * BLOCK, KV_W), cur),
        pl.BlockSpec((BLOCK, KV_W), nxt),
        pl.BlockSpec((ctx_len, KV_W), lambda b, i: (b, 0)),
        pl.BlockSpec((KV_W, BLOCK), swap(prev)),
        pl.BlockSpec((KV_W, sub * BLOCK), swap(cur)),
        pl.BlockSpec((KV_W, BLOCK), swap(nxt)),
        pl.BlockSpec((KV_W, ctx_len), lambda b, i: (0, b)),
        _const_spec(bias_shape),
    ]
    return steps, in_specs, pl.BlockSpec((sub * BLOCK, Q_W), cur)


def _band_bias():
    qi = np.arange(BLOCK)[None, :]
    kj = np.arange(3 * BLOCK)[:, None]
    rel = kj - BLOCK - qi
    in_window = np.abs(rel) <= WINDOW
    variants = []
    for has_prev, has_next in ((False, True), (True, True), (True, False)):
        ok = in_window.copy()
        if not has_prev:
            ok &= kj >= BLOCK
        if not has_next:
            ok &= kj < 2 * BLOCK
        mask = np.where(ok, 0.0, NEG).astype(np.float32)
        variants.append(np.concatenate([mask[:BLOCK], mask[2 * BLOCK:]], axis=0))
    return jnp.asarray(np.stack(variants)).astype(BF16)


def _minor_dft_terms():
    assert DFT_MINOR == 8
    plan = []
    for k2 in range(DFT_MINOR):
        unit, half = [], []
        for s2 in range(DFT_MINOR):
            ang = -2.0 * math.pi * ((s2 * k2) % DFT_MINOR) / DFT_MINOR
            for part, coef in ((0, math.cos(ang)), (1, -math.sin(ang))):
                if abs(coef) < 1e-9:
                    continue
                target = unit if abs(abs(coef) - 1.0) < 1e-9 else half
                assert target is unit or abs(abs(coef) - math.sqrt(0.5)) < 1e-9
                target.append((s2, part, 1 if coef > 0 else -1))
        plan.append((unit, half))
    return plan


def _signed_sum(terms, pick):
    pos = [pick(s2, part) for s2, part, sign in terms if sign > 0]
    neg = [pick(s2, part) for s2, part, sign in terms if sign < 0]
    total = None
    for v in pos:
        total = v if total is None else total + v
    for v in neg:
        total = -v if total is None else total - v
    return total


def _fourier_steps(f_ref, f1_ref, tr_ref, ti_ref, cm_ref, o_ref):
    n_major = f_ref.shape[0] // DFT_MINOR
    pair_w = 2 * LANES
    n_pairs = DFT_MINOR // 2
    row_chunk = n_major // n_pairs
    twiddled = [None] * n_pairs
    spectra = [None] * DFT_MINOR
    plan = _minor_dft_terms()

    def stage1(c):
        y = jnp.concatenate(
            [f_ref[pl.ds(s2, n_major, stride=DFT_MINOR), :].astype(BF16) for s2 in (2 * c, 2 * c + 1)],
            axis=1)
        a = jnp.dot(f1_ref[...], y, preferred_element_type=F32)
        ar = a[:n_major]
        ai = a[n_major:]
        tr = tr_ref[:, c * pair_w:(c + 1) * pair_w]
        ti = ti_ref[:, c * pair_w:(c + 1) * pair_w]
        twiddled[c] = ((ar * tr - ai * ti).astype(BF16), (ar * ti + ai * tr).astype(BF16))

    def channels(c):
        apr, api = twiddled[c]
        for t in range(2):
            lanes = slice(t * LANES, (t + 1) * LANES)
            ap = jnp.concatenate([apr[:, lanes], api[:, lanes]], axis=1)
            spectra[2 * c + t] = jnp.dot(ap, cm_ref[...], preferred_element_type=F32)

    def stage2(r):
        rows = slice(r * row_chunk, (r + 1) * row_chunk)

        def pick(s2, part):
            return spectra[s2][rows, part * LANES:(part + 1) * LANES]

        for k2, (unit, half) in enumerate(plan):
            res = _signed_sum(unit, pick)
            if half:
                scaled = _signed_sum(half, pick) * math.sqrt(0.5)
                res = scaled if res is None else res + scaled
            o_ref[pl.ds(k2 * n_major + r * row_chunk, row_chunk), :] = res.astype(BF16)

    steps = [functools.partial(stage1, 0)]
    for c in range(1, n_pairs):
        steps += [functools.partial(stage1, c), functools.partial(channels, c - 1)]
    steps.append(functools.partial(channels, n_pairs - 1))
    return steps + [functools.partial(stage2, r) for r in range(n_pairs)]


def _cis(num, den):
    ang = (num % den).astype(F32) * F32(-2.0 * math.pi / den)
    return jnp.cos(ang), jnp.sin(ang)


def _dft_tables(seq):
    n_major = seq // DFT_MINOR
    ch = FOURIER_GROUP_CH
    root = int(round(math.sqrt(n_major)))
    assert root * root == n_major
    s1 = jnp.arange(n_major, dtype=jnp.int32)[None, :]
    hi = jnp.arange(root, dtype=jnp.int32)[:, None] * root
    lo = jnp.arange(root, dtype=jnp.int32)[:, None]
    hr, hi_ = _cis(hi * s1, n_major)
    lr, li = _cis(lo * s1, n_major)
    f1r = (hr[:, None, :] * lr[None, :, :] - hi_[:, None, :] * li[None, :, :]).reshape(n_major, n_major)
    f1i = (hr[:, None, :] * li[None, :, :] + hi_[:, None, :] * lr[None, :, :]).reshape(n_major, n_major)
    f1 = jnp.concatenate([f1r, f1i], axis=0).astype(BF16)
    k1 = jnp.arange(n_major, dtype=jnp.int32)[:, None]
    s2 = jnp.arange(DFT_MINOR, dtype=jnp.int32)[None, :]
    tr, ti = _cis(k1 * s2, seq)
    tr = jnp.broadcast_to(tr[:, :, None], (n_major, DFT_MINOR, ch)).reshape(n_major, DFT_MINOR * ch)
    ti = jnp.broadcast_to(ti[:, :, None], (n_major, DFT_MINOR, ch)).reshape(n_major, DFT_MINOR * ch)
    c128 = jnp.arange(ch, dtype=jnp.int32)
    cr, ci = _cis(c128[:, None] * c128[None, :], ch)
    norm = F32(1.0 / math.sqrt(seq * ch))
    cm = (jnp.concatenate([jnp.concatenate([cr, ci], axis=1),
                           jnp.concatenate([-ci, cr], axis=1)], axis=0) * norm).astype(BF16)
    return f1, tr, ti, cm


def _mix_kernel(sink_ref, q_ref, kp_ref, kc_ref, kn_ref, kx_ref, vp_ref, vc_ref, vn_ref, vx_ref, bias_ref,
                f_ref, f1_ref, tr_ref, ti_ref, cm_ref, attn_ref, four_ref):
    attn_steps = _attention_steps(sink_ref, q_ref, kp_ref, kc_ref, kn_ref, kx_ref,
                                  vp_ref, vc_ref, vn_ref, vx_ref, bias_ref, attn_ref)
    four_steps = _fourier_steps(f_ref, f1_ref, tr_ref, ti_ref, cm_ref, four_ref)
    done = 0
    for n, four_step in enumerate(four_steps):
        upto = (n + 1) * len(attn_steps) // len(four_steps)
        for attn_step in attn_steps[done:upto]:
            attn_step()
        done = upto
        four_step()


def _mix(sink, q, k, v_t, kx, vx_t, bias, f3, tables, ctx_len):
    batch, seq, _ = f3.shape
    f1, tr, ti, cm = tables
    ch = FOURIER_GROUP_CH
    steps, attn_in, attn_out = _attention_specs(batch, seq, ctx_len, bias.shape)
    assert steps == N_FOURIER_GROUPS
    four_spec = pl.BlockSpec((None, seq, ch), lambda b, g: (b, 0, g))
    return pl.pallas_call(
        _mix_kernel,
        grid=(batch, steps),
        in_specs=attn_in + [four_spec, _const_spec(f1.shape), _const_spec(tr.shape),
                            _const_spec(ti.shape), _const_spec(cm.shape)],
        out_specs=[attn_out, four_spec],
        out_shape=[jax.ShapeDtypeStruct((batch * seq, Q_W), BF16),
                   jax.ShapeDtypeStruct((batch, seq, FOURIER_W), BF16)],
        compiler_params=_params(2),
        name="mix",
    )(sink, q, k, k, k, kx, v_t, v_t, v_t, vx_t, bias, f3, f1, tr, ti, cm)


def _merge_kernel(x_ref, a_ref, f_ref, mod_ref, gpre_ref, gpost_ref,
                  wg_b, wpa_b, wpf_b, wout_b, wup_ref, wdown_ref,
                  o_ref, wup_o, wdown_o, *, tiles_per_seq):
    wup_o[...] = wup_ref[...].astype(BF16)
    wdown_o[...] = wdown_ref[...].astype(BF16)

    x = x_ref[...]
    shift, scale_mod, out_gate = _mod_rows(mod_ref, pl.program_id(0) // tiles_per_seq, 0, 3)
    h = _norm_modulate(x, gpre_ref[...], shift, scale_mod).astype(BF16)
    a = a_ref[...]
    f = f_ref[...]
    y = None
    for c in range(0, D_MODEL, MXU_DIM):
        cols = slice(c, c + MXU_DIM)
        gate_a = jnp.dot(h, wg_b[:, cols], preferred_element_type=F32)
        gate_f = jnp.dot(h, wg_b[:, D_MODEL + c:D_MODEL + c + MXU_DIM], preferred_element_type=F32)
        pa = jnp.dot(a, wpa_b[:, cols], preferred_element_type=F32)
        pf = jnp.dot(f, wpf_b[:, cols], preferred_element_type=F32)
        m = (jax.nn.sigmoid(gate_a) * pa + jax.nn.sigmoid(gate_f) * pf).astype(BF16)
        part = jnp.dot(m, wout_b[cols, :], preferred_element_type=F32)
        y = part if y is None else y + part
    o_ref[...] = x + out_gate * _post_norm(y, gpost_ref[...])


def _merge(x2, attn, four, mods, g_pre, g_post, w_g, w_pa, w_pf, w_out, w_up, w_down, seq):
    n = x2.shape[0]
    tm = TOKEN_TILE
    n_steps = n // tm
    tiles_per_seq = seq // tm
    row = lambda i: (i, 0)
    up_rows = w_up.shape[0] // n_steps
    down_rows = w_down.shape[0] // BF16_ROWS
    n_down = w_down.shape[0] // down_rows
    assert up_rows * n_steps == w_up.shape[0] and up_rows % BF16_ROWS == 0
    assert down_rows % BF16_ROWS == 0 and n_down <= n_steps
    up_spec = pl.BlockSpec((up_rows, w_up.shape[1]), row)
    down_spec = pl.BlockSpec((down_rows, w_down.shape[1]), lambda i: (jnp.minimum(i, n_down - 1), 0))
    return pl.pallas_call(
        functools.partial(_merge_kernel, tiles_per_seq=tiles_per_seq),
        grid=(n // tm,),
        in_specs=[
            pl.BlockSpec((tm, D_MODEL), row),
            pl.BlockSpec((tm, Q_W), row),
            pl.BlockSpec((tm, FOURIER_W), row),
            _const_spec(mods.shape),
            _const_spec((1, D_MODEL)),
            _const_spec((1, D_MODEL)),
            _const_spec(w_g.shape),
            _const_spec(w_pa.shape),
            _const_spec(w_pf.shape),
            _const_spec(w_out.shape),
            up_spec,
            down_spec,
        ],
        out_specs=[pl.BlockSpec((tm, D_MODEL), row), up_spec, down_spec],
        out_shape=[jax.ShapeDtypeStruct((n, D_MODEL), F32),
                   jax.ShapeDtypeStruct(w_up.shape, BF16),
                   jax.ShapeDtypeStruct(w_down.shape, BF16)],
        compiler_params=_params(1),
        name="merge",
    )(x2, attn, four, mods, g_pre, g_post, w_g, w_pa, w_pf, w_out, w_up, w_down)


def _convffn_kernel(x_ref, xp_ref, xn_ref, mod_ref, gpre_ref, gpost_ref,
                    wu_ref, wgate_ref, cw_ref, cb_ref, wd_ref, o_ref, act_ref, *, tiles_per_seq):
    i = pl.program_id(0)
    tm = x_ref.shape[0]
    halo = SUBLANES
    has_prev = (i % tiles_per_seq != 0).astype(F32)
    has_next = (i % tiles_per_seq != tiles_per_seq - 1).astype(F32)
    x = x_ref[...]
    shift, scale, out_gate = _mod_rows(mod_ref, i // tiles_per_seq, 3, 3)
    gain = gpre_ref[...]
    h = _norm_modulate(x, gain, shift, scale)
    hp = _norm_modulate(xp_ref[...], gain, shift, scale) * has_prev
    hn = _norm_modulate(xn_ref[...], gain, shift, scale) * has_next
    h_ext = jnp.concatenate([hp, h, hn], axis=0).astype(BF16)
    h_mid = h.astype(BF16)
    n_ext = tm + 2 * halo
    for c in range(D_FF // FFN_CHUNK):
        cols = slice(c * FFN_CHUNK, (c + 1) * FFN_CHUNK)
        u = jnp.dot(h_ext, wu_ref[:, cols], preferred_element_type=F32)
        gate = jnp.dot(h_mid, wgate_ref[:, cols], preferred_element_type=F32)
        u_prev = pltpu.roll(u, 1, 0)[halo:halo + tm]
        u_next = pltpu.roll(u, n_ext - 1, 0)[halo:halo + tm]
        conv = (u_prev * cw_ref[0:1, cols] + u[halo:halo + tm] * cw_ref[1:2, cols]
                + u_next * cw_ref[2:3, cols] + cb_ref[:, cols])
        act_ref[:, cols] = (conv * jax.nn.sigmoid(conv) * gate).astype(BF16)
    y = jnp.dot(act_ref[...], wd_ref[...], preferred_element_type=F32)
    o_ref[...] = x + out_gate * _post_norm(y, gpost_ref[...])


def _convffn(x1, mods, g_pre, g_post, w_up, conv_w, conv_b, w_down, seq):
    n = x1.shape[0]
    tm = TOKEN_TILE
    tiles_per_seq = seq // tm
    halo_blocks_per_tile = tm // SUBLANES
    n_halo_blocks = n // SUBLANES
    return pl.pallas_call(
        functools.partial(_convffn_kernel, tiles_per_seq=tiles_per_seq),
        grid=(n // tm,),
        in_specs=[
            pl.BlockSpec((tm, D_MODEL), lambda i: (i, 0)),
            pl.BlockSpec((SUBLANES, D_MODEL),
                         lambda i: (jnp.maximum(i * halo_blocks_per_tile - 1, 0), 0)),
            pl.BlockSpec((SUBLANES, D_MODEL),
                         lambda i: (jnp.minimum((i + 1) * halo_blocks_per_tile, n_halo_blocks - 1), 0)),
            _const_spec(mods.shape),
            _const_spec((1, D_MODEL)),
            _const_spec((1, D_MODEL)),
            pl.BlockSpec((D_MODEL, D_FF), lambda i: (0, 0), pipeline_mode=pl.Buffered(1)),
            pl.BlockSpec((D_MODEL, D_FF), lambda i: (0, 1), pipeline_mode=pl.Buffered(1)),
            _const_spec(conv_w.shape),
            _const_spec((1, D_FF)),
            _const_spec(w_down.shape),
        ],
        out_specs=pl.BlockSpec((tm, D_MODEL), lambda i: (i, 0)),
        out_shape=jax.ShapeDtypeStruct((n, D_MODEL), F32),
        scratch_shapes=[pltpu.VMEM((tm, D_FF), BF16)],
        compiler_params=_params(1),
        name="convffn",
    )(x1, x1, x1, mods, g_pre, g_post, w_up, w_up, conv_w, conv_b.reshape(1, D_FF), w_down)


def _rope_tables(seq):
    half = HEAD_DIM // 2
    inv_freq = ROPE_THETA ** (-jnp.arange(0, half, 2, dtype=F32) / half)
    ang_r = jnp.arange(seq // GRID_W).astype(F32)[:, None] * inv_freq
    ang_c = jnp.arange(GRID_W).astype(F32)[:, None] * inv_freq
    reps = LANES // HEAD_DIM

    def lanes(row_part, col_part):
        return jnp.tile(jnp.concatenate([row_part, col_part], axis=-1), (1, reps))

    zr = jnp.zeros((ang_r.shape[0], half), F32)
    zc = jnp.zeros((ang_c.shape[0], half), F32)
    rcos = lanes(jnp.concatenate([jnp.cos(ang_r), jnp.cos(ang_r)], axis=-1), zr)
    rsin = lanes(jnp.concatenate([-jnp.sin(ang_r), jnp.sin(ang_r)], axis=-1), zr)
    ccos = lanes(zc, jnp.concatenate([jnp.cos(ang_c), jnp.cos(ang_c)], axis=-1))
    csin = lanes(zc, jnp.concatenate([-jnp.sin(ang_c), jnp.sin(ang_c)], axis=-1))
    return rcos, rsin, ccos, csin


def kernel(x, c, ctx, c_ctx, w_mod, b_mod, g_pre1, g_post1, g_pre2, g_post2,
           w_in, sink, w_pa, w_pf, w_out, w_up, conv_w, conv_b, w_down):
    batch, seq, d = x.shape
    ctx_len = ctx.shape[1]
    depth = w_mod.shape[0]
    assert depth == 1 and d == D_MODEL and batch + 1 <= SUBLANES
    assert seq % TOKEN_TILE == 0 and TOKEN_TILE % GRID_W == 0
    assert seq % (DFT_MINOR * SUBLANES) == 0
    assert seq % (ATTN_BLOCKS_PER_STEP * BLOCK) == 0
    n = batch * seq
    rope_tabs = _rope_tables(seq)
    tables = _dft_tables(seq)
    bias = _band_bias()

    l = 0
    mods = _adaln(c, c_ctx.reshape(1, d), w_mod[l], b_mod[l])

    g_pre1_row = g_pre1[l].reshape(1, d)
    g_post1_row = g_post1[l].reshape(1, d)
    g_pre2_row = g_pre2[l].reshape(1, d)
    g_post2_row = g_post2[l].reshape(1, d)

    x2 = x.reshape(n, d)
    q, k, v_t, f, w_g_b, w_pa_b, w_pf_b, w_out_b = _inproj(x2, mods, g_pre1_row, w_in[l], w_pa[l], w_pf[l], w_out[l],
                                                          rope_tabs, seq)
    kx, vx_t = _ctxkv(ctx.reshape(batch * ctx_len, d), mods, batch, g_pre1_row, w_in[l], ctx_len)
    attn, four = _mix(sink[l], q, k, v_t, kx, vx_t, bias, f.reshape(batch, seq, FOURIER_W), tables, ctx_len)
    four = four.reshape(n, FOURIER_W)
    x1, w_up_b, w_down_b = _merge(x2, attn, four, mods, g_pre1_row, g_post1_row, w_g_b, w_pa_b, w_pf_b, w_out_b,
                                  w_up[l], w_down[l], seq)
    out = _convffn(x1, mods, g_pre2_row, g_post2_row, w_up_b, conv_w[l], conv_b[l], w_down_b, seq)
    return out.reshape(batch, seq, d)
```

```python
import functools
import math

import numpy as np
import jax
import jax.numpy as jnp
from jax import lax
from jax.experimental import pallas as pl
from jax.experimental.pallas import tpu as pltpu

F32 = jnp.float32
BF16 = jnp.bfloat16

D_MODEL = 1024
GRID_W = 64
HEAD_DIM = 64
N_Q_HEADS = 8
N_KV_HEADS = 2
GROUP = N_Q_HEADS // N_KV_HEADS
WINDOW = 128
BLOCK = 128
ROPE_THETA = 10000.0
N_FOURIER_GROUPS = 4
FOURIER_GROUP_CH = 128
FOURIER_W = N_FOURIER_GROUPS * FOURIER_GROUP_CH
Q_W = N_Q_HEADS * HEAD_DIM
KV_W = N_KV_HEADS * HEAD_DIM
D_FF = 2816
EPS = 1e-6
NEG = -1e30
LOG2E = math.log2(math.e)

LANES = 128
SUBLANES = 8
BF16_ROWS = 16
MXU_DIM = 256
VMEM_LIMIT_BYTES = 56 * 1024 * 1024

TOKEN_TILE = 1024
FFN_CHUNK = MXU_DIM
DFT_MINOR = SUBLANES
ATTN_BLOCKS_PER_STEP = 16


def _const_spec(shape):
    nd = len(shape)
    return pl.BlockSpec(shape, lambda *_: (0,) * nd, pipeline_mode=pl.Buffered(1))


def _params(n_axes):
    return pltpu.CompilerParams(
        dimension_semantics=("arbitrary",) * n_axes,
        vmem_limit_bytes=VMEM_LIMIT_BYTES,
    )


def _norm_modulate(x, gain, shift, scale):
    ms = jnp.mean(x * x, axis=-1, keepdims=True)
    return (x * lax.rsqrt(ms + EPS)) * (gain * (1.0 + scale)) + shift


def _mod_rows(mod_ref, row, first, count):
    return [mod_ref[pl.ds(row, 1), (first + j) * D_MODEL:(first + j + 1) * D_MODEL] for j in range(count)]


def _post_norm(y, gain):
    ms = jnp.mean(y * y, axis=-1, keepdims=True)
    return (y * lax.rsqrt(ms + EPS)) * gain


def _adaln_kernel(c_ref, cctx_ref, w_ref, b_ref, o_ref, rows_ref):
    batch = c_ref.shape[0]
    rows_ref[...] = jnp.zeros(rows_ref.shape, F32)
    rows_ref[0:batch, :] = c_ref[...]
    rows_ref[batch:batch + 1, :] = cctx_ref[...]
    c = rows_ref[...]
    s = c * jax.nn.sigmoid(c)
    o_ref[...] = jnp.dot(s, w_ref[...], preferred_element_type=F32) + b_ref[...]


def _adaln(c, c_ctx, w_mod, b_mod):
    n_out = w_mod.shape[1]
    tn = 1536
    return pl.pallas_call(
        _adaln_kernel,
        grid=(n_out // tn,),
        in_specs=[
            pl.BlockSpec(c.shape, lambda j: (0, 0)),
            pl.BlockSpec(c_ctx.shape, lambda j: (0, 0)),
            pl.BlockSpec((D_MODEL, tn), lambda j: (0, j)),
            pl.BlockSpec((1, tn), lambda j: (0, j)),
        ],
        out_specs=pl.BlockSpec((SUBLANES, tn), lambda j: (0, j)),
        out_shape=jax.ShapeDtypeStruct((SUBLANES, n_out), F32),
        scratch_shapes=[pltpu.VMEM((SUBLANES, D_MODEL), F32)],
        compiler_params=_params(1),
        name="adaln",
    )(c, c_ctx, w_mod, b_mod.reshape(1, n_out))


def _rope(xb, cos_t, sin_t, first_half):
    sw = jnp.where(first_half, pltpu.roll(xb, LANES - 16, 1), pltpu.roll(xb, 16, 1))
    return xb * cos_t + sw * sin_t


def _token_table(row_tab, col_tab):
    n_rows = row_tab.shape[0]
    by_row = jnp.concatenate(
        [jnp.broadcast_to(row_tab[r:r + 1, :], (GRID_W, LANES)) for r in range(n_rows)], axis=0)
    return by_row + jnp.concatenate([col_tab] * n_rows, axis=0)


def _inproj_kernel(x_ref, mod_ref, g_ref, w_ref, rcos_ref, rsin_ref, ccos_ref, csin_ref,
                   wg_ref, wpa_ref, wpf_ref, wout_ref,
                   q_ref, k_ref, v_ref, f_ref, wg_o, wpa_o, wpf_o, wout_o, wb_ref, *, tiles_per_seq):
    wg_o[...] = wg_ref[...].astype(BF16)
    wpa_o[...] = wpa_ref[...].astype(BF16)
    wpf_o[...] = wpf_ref[...].astype(BF16)
    wout_o[...] = wout_ref[...].astype(BF16)

    @pl.when(pl.program_id(0) == 0)
    def _():
        low = lax.broadcasted_iota(jnp.int32, (D_MODEL, LANES), 1) < HEAD_DIM
        for j in range(Q_W // LANES):
            src_a = (j // 2) * LANES
            src_b = (GROUP // 2 + j // 2) * LANES
            a = w_ref[:, src_a:src_a + LANES]
            b = w_ref[:, src_b:src_b + LANES]
            if j % 2 == 1:
                a = pltpu.roll(a, HEAD_DIM, 1)
            else:
                b = pltpu.roll(b, HEAD_DIM, 1)
            wb_ref[:, j * LANES:(j + 1) * LANES] = jnp.where(low, a, b).astype(BF16)
        wb_ref[:, Q_W:] = w_ref[:, Q_W:].astype(BF16)

    shift, scale_mod = _mod_rows(mod_ref, pl.program_id(0) // tiles_per_seq, 0, 2)
    h = _norm_modulate(x_ref[...], g_ref[...], shift, scale_mod).astype(BF16)
    cos_t = _token_table(rcos_ref[...], ccos_ref[...])
    sin_t = _token_table(rsin_ref[...], csin_ref[...])
    lane = lax.broadcasted_iota(jnp.int32, cos_t.shape, 1)
    first_half = (lane % 32) < 16
    scale = HEAD_DIM ** -0.5 * LOG2E

    def project(col0, width):
        return jnp.dot(h, wb_ref[:, col0:col0 + width], preferred_element_type=F32)

    for c in range(0, Q_W, MXU_DIM):
        p = project(c, MXU_DIM)
        for b in range(MXU_DIM // LANES):
            blk = p[:, b * LANES:(b + 1) * LANES]
            q_ref[:, c + b * LANES:c + (b + 1) * LANES] = (
                _rope(blk, cos_t, sin_t, first_half) * scale).astype(BF16)
    p = project(Q_W, 2 * KV_W)
    k_ref[...] = _rope(p[:, :KV_W], cos_t, sin_t, first_half).astype(BF16)
    v_ref[...] = p[:, KV_W:].T.astype(BF16)
    for c in range(0, FOURIER_W, MXU_DIM):
        f_ref[:, c:c + MXU_DIM] = project(Q_W + 2 * KV_W + c, MXU_DIM)


def _row_slab_rows(n_rows, n_steps):
    rows = n_rows // n_steps
    assert rows * n_steps == n_rows and rows % BF16_ROWS == 0
    return rows


def _inproj(x2, mods, g_pre, w_in, w_pa, w_pf, w_out, rope_tabs, seq):
    n = x2.shape[0]
    tm = TOKEN_TILE
    n_steps = n // tm
    tiles_per_seq = seq // tm
    rows_per_tile = tm // GRID_W
    n_cols = Q_W + 2 * KV_W + FOURIER_W
    row_spec = pl.BlockSpec((rows_per_tile, LANES), lambda i: (i % tiles_per_seq, 0))
    g_rows = _row_slab_rows(D_MODEL, n_steps)
    pa_rows = _row_slab_rows(w_pa.shape[0], n_steps)
    pf_rows = _row_slab_rows(w_pf.shape[0], n_steps)
    out_rows = _row_slab_rows(w_out.shape[0], n_steps)
    slab = lambda rows, cols: pl.BlockSpec((rows, cols), lambda i: (i, 0))
    return pl.pallas_call(
        functools.partial(_inproj_kernel, tiles_per_seq=tiles_per_seq),
        grid=(n // tm,),
        in_specs=[
            pl.BlockSpec((tm, D_MODEL), lambda i: (i, 0)),
            _const_spec(mods.shape),
            _const_spec((1, D_MODEL)),
            _const_spec((D_MODEL, n_cols)),
            row_spec, row_spec,
            _const_spec((GRID_W, LANES)), _const_spec((GRID_W, LANES)),
            pl.BlockSpec((pl.Element(g_rows), pl.Element(2 * D_MODEL)),
                         lambda i: (i * g_rows, n_cols)),
            slab(pa_rows, D_MODEL), slab(pf_rows, D_MODEL), slab(out_rows, D_MODEL),
        ],
        out_specs=[
            pl.BlockSpec((tm, Q_W), lambda i: (i, 0)),
            pl.BlockSpec((tm, KV_W), lambda i: (i, 0)),
            pl.BlockSpec((KV_W, tm), lambda i: (0, i)),
            pl.BlockSpec((tm, FOURIER_W), lambda i: (i, 0)),
            slab(g_rows, 2 * D_MODEL), slab(pa_rows, D_MODEL), slab(pf_rows, D_MODEL), slab(out_rows, D_MODEL),
        ],
        out_shape=[
            jax.ShapeDtypeStruct((n, Q_W), BF16),
            jax.ShapeDtypeStruct((n, KV_W), BF16),
            jax.ShapeDtypeStruct((KV_W, n), BF16),
            jax.ShapeDtypeStruct((n, FOURIER_W), F32),
            jax.ShapeDtypeStruct((D_MODEL, 2 * D_MODEL), BF16),
            jax.ShapeDtypeStruct(w_pa.shape, BF16),
            jax.ShapeDtypeStruct(w_pf.shape, BF16),
            jax.ShapeDtypeStruct(w_out.shape, BF16),
        ],
        scratch_shapes=[pltpu.VMEM((D_MODEL, n_cols), BF16)],
        compiler_params=_params(1),
        name="inproj",
    )(x2, mods, g_pre, w_in, *rope_tabs, w_in, w_pa, w_pf, w_out)


def _ctxkv_kernel(x_ref, mod_ref, g_ref, w_ref, k_ref, v_ref, *, row):
    shift, scale_mod = _mod_rows(mod_ref, row, 0, 2)
    h = _norm_modulate(x_ref[...], g_ref[...], shift, scale_mod)
    p = jnp.dot(h.astype(BF16), w_ref[...].astype(BF16), preferred_element_type=F32)
    k_ref[...] = p[:, :KV_W].astype(BF16)
    v_ref[...] = p[:, KV_W:].T.astype(BF16)


def _ctxkv(ctx2, mods, ctx_row, g_pre, w_in, ctx_len):
    n = ctx2.shape[0]
    assert Q_W % (2 * KV_W) == 0
    return pl.pallas_call(
        functools.partial(_ctxkv_kernel, row=ctx_row),
        grid=(n // ctx_len,),
        in_specs=[
            pl.BlockSpec((ctx_len, D_MODEL), lambda i: (i, 0)),
            _const_spec(mods.shape),
            _const_spec((1, D_MODEL)),
            pl.BlockSpec((D_MODEL, 2 * KV_W), lambda i: (0, Q_W // (2 * KV_W)),
                         pipeline_mode=pl.Buffered(1)),
        ],
        out_specs=[
            pl.BlockSpec((ctx_len, KV_W), lambda i: (i, 0)),
            pl.BlockSpec((KV_W, ctx_len), lambda i: (0, i)),
        ],
        out_shape=[
            jax.ShapeDtypeStruct((n, KV_W), BF16),
            jax.ShapeDtypeStruct((KV_W, n), BF16),
        ],
        compiler_params=_params(1),
        name="ctxkv",
    )(ctx2, mods, g_pre, w_in)


def _attend_block(sink_ref, q_blk, k_parts, v_parts, bias_prev, bias_next):
    kcat = jnp.concatenate(k_parts, axis=0)
    vcat_t = jnp.concatenate(v_parts, axis=1)
    n_keys = kcat.shape[0]
    low = lax.broadcasted_iota(jnp.int32, (BLOCK, LANES), 1) < HEAD_DIM
    top = lax.broadcasted_iota(jnp.int32, (KV_W, n_keys), 0) < HEAD_DIM
    zero = jnp.zeros((), BF16)
    one = jnp.ones((), BF16)
    q_rows = [jnp.where(low if h == 0 else jnp.logical_not(low), q_blk[:, g * LANES:(g + 1) * LANES], zero)
              for h in range(N_KV_HEADS) for g in range(GROUP)]
    s_all = lax.dot_general(kcat, jnp.concatenate(q_rows, axis=0), (((1,), (1,)), ((), ())),
                            preferred_element_type=F32).astype(BF16)
    outs = []
    for h in range(N_KV_HEADS):
        v_h = jnp.where(top if h == 0 else jnp.logical_not(top), vcat_t, one)
        for g0 in range(0, GROUP, 2):
            probs = []
            sinks = []
            for g in (g0, g0 + 1):
                head = h * GROUP + g
                s = s_all[:, head * LANES:(head + 1) * LANES]
                parts = [s[0:BLOCK] + bias_prev,
                         s[BLOCK:2 * BLOCK],
                         s[2 * BLOCK:3 * BLOCK] + bias_next]
                parts += [s[r:r + BLOCK] for r in range(3 * BLOCK, n_keys, BLOCK)]
                mx = parts[0]
                for part in parts[1:]:
                    mx = jnp.maximum(mx, part)
                sink = sink_ref[head] * LOG2E
                m = jnp.maximum(jnp.max(mx, axis=0, keepdims=True), sink.astype(BF16))
                probs.append(jnp.concatenate([jnp.exp2(part - m) for part in parts], axis=0))
                sinks.append(jnp.exp2(sink - m.astype(F32)))
            o2 = jnp.dot(v_h, jnp.concatenate(probs, axis=1), preferred_element_type=F32)
            for idx in range(2):
                o_t = o2[:, idx * LANES:(idx + 1) * LANES]
                num = o_t[0:HEAD_DIM] if h == 0 else o_t[HEAD_DIM:]
                den = (o_t[HEAD_DIM:] if h == 0 else o_t[0:HEAD_DIM]) + sinks[idx]
                outs.append(num / den)
    return outs


def _attention_steps(sink_ref, q_ref, kp_ref, kc_ref, kn_ref, kx_ref, vp_ref, vc_ref, vn_ref, vx_ref,
                     bias_ref, o_ref):
    i = pl.program_id(1)
    n_sub = q_ref.shape[0] // BLOCK
    first_var = jnp.where(i == 0, 0, 1)
    last_var = jnp.where(i == pl.num_programs(1) - 1, 2, 1)

    def block_step(t):
        own = slice(t * BLOCK, (t + 1) * BLOCK)
        before = slice((t - 1) * BLOCK, t * BLOCK)
        after = slice((t + 1) * BLOCK, (t + 2) * BLOCK)
        k_parts = [kp_ref[...] if t == 0 else kc_ref[before, :], kc_ref[own, :],
                   kn_ref[...] if t == n_sub - 1 else kc_ref[after, :], kx_ref[...]]
        v_parts = [vp_ref[...] if t == 0 else vc_ref[:, before], vc_ref[:, own],
                   vn_ref[...] if t == n_sub - 1 else vc_ref[:, after], vx_ref[...]]
        bias_prev = bias_ref[first_var if t == 0 else 1, 0:BLOCK, :]
        bias_next = bias_ref[last_var if t == n_sub - 1 else 1, BLOCK:2 * BLOCK, :]
        outs = _attend_block(sink_ref, q_ref[own, :], k_parts, v_parts, bias_prev, bias_next)
        for j in range(Q_W // LANES):
            o_t = jnp.concatenate([outs[2 * j], outs[2 * j + 1]], axis=0)
            o_ref[own, j * LANES:(j + 1) * LANES] = o_t.T.astype(BF16)

    return [functools.partial(block_step, t) for t in range(n_sub)]


def _attention_specs(batch, seq, ctx_len, bias_shape):
    nb = seq // BLOCK
    sub = ATTN_BLOCKS_PER_STEP
    steps = nb // sub

    def cur(b, i):
        return (b * steps + i, 0)

    def prev(b, i):
        return (b * nb + jnp.maximum(i * sub - 1, 0), 0)

    def nxt(b, i):
        return (b * nb + jnp.minimum((i + 1) * sub, nb - 1), 0)

    def swap(f):
        return lambda b, i: f(b, i)[::-1]

    in_specs = [
        pl.BlockSpec(memory_space=pltpu.SMEM),
        pl.BlockSpec((sub * BLOCK, Q_W), cur),
        pl.BlockSpec((BLOCK, KV_W), prev),
        pl.BlockSpec((sub * BLOCK, KV_W), cur),
        pl.BlockSpec((BLOCK, KV_W), nxt),
        pl.BlockSpec((ctx_len, KV_W), lambda b, i: (b, 0)),
        pl.BlockSpec((KV_W, BLOCK), swap(prev)),
        pl.BlockSpec((KV_W, sub * BLOCK), swap(cur)),
        pl.BlockSpec((KV_W, BLOCK), swap(nxt)),
        pl.BlockSpec((KV_W, ctx_len), lambda b, i: (0, b)),
        _const_spec(bias_shape),
    ]
    return steps, in_specs, pl.BlockSpec((sub * BLOCK, Q_W), cur)


def _band_bias():
    qi = np.arange(BLOCK)[None, :]
    kj = np.arange(3 * BLOCK)[:, None]
    rel = kj - BLOCK - qi
    in_window = np.abs(rel) <= WINDOW
    variants = []
    for has_prev, has_next in ((False, True), (True, True), (True, False)):
        ok = in_window.copy()
        if not has_prev:
            ok &= kj >= BLOCK
        if not has_next:
            ok &= kj < 2 * BLOCK
        mask = np.where(ok, 0.0, NEG).astype(np.float32)
        variants.append(np.concatenate([mask[:BLOCK], mask[2 * BLOCK:]], axis=0))
    return jnp.asarray(np.stack(variants)).astype(BF16)


def _minor_dft_terms():
    assert DFT_MINOR == 8
    plan = []
    for k2 in range(DFT_MINOR):
        unit, half = [], []
        for s2 in range(DFT_MINOR):
            ang = -2.0 * math.pi * ((s2 * k2) % DFT_MINOR) / DFT_MINOR
            for part, coef in ((0, math.cos(ang)), (1, -math.sin(ang))):
                if abs(coef) < 1e-9:
                    continue
                target = unit if abs(abs(coef) - 1.0) < 1e-9 else half
                assert target is unit or abs(abs(coef) - math.sqrt(0.5)) < 1e-9
                target.append((s2, part, 1 if coef > 0 else -1))
        plan.append((unit, half))
    return plan


def _signed_sum(terms, pick):
    pos = [pick(s2, part) for s2, part, sign in terms if sign > 0]
    neg = [pick(s2, part) for s2, part, sign in terms if sign < 0]
    total = None
    for v in pos:
        total = v if total is None else total + v
    for v in neg:
        total = -v if total is None else total - v
    return total


def _fourier_steps(f_ref, f1_ref, tr_ref, ti_ref, cm_ref, o_ref):
    n_major = f_ref.shape[0] // DFT_MINOR
    pair_w = 2 * LANES
    n_pairs = DFT_MINOR // 2
    row_chunk = n_major // n_pairs
    twiddled = [None] * n_pairs
    spectra = [None] * DFT_MINOR
    plan = _minor_dft_terms()

    def stage1(c):
        y = jnp.concatenate(
            [f_ref[pl.ds(s2, n_major, stride=DFT_MINOR), :].astype(BF16) for s2 in (2 * c, 2 * c + 1)],
            axis=1)
        a = jnp.dot(f1_ref[...], y, preferred_element_type=F32)
        ar = a[:n_major]
        ai = a[n_major:]
        tr = tr_ref[:, c * pair_w:(c + 1) * pair_w]
        ti = ti_ref[:, c * pair_w:(c + 1) * pair_w]
        twiddled[c] = ((ar * tr - ai * ti).astype(BF16), (ar * ti + ai * tr).astype(BF16))

    def channels(c):
        apr, api = twiddled[c]
        for t in range(2):
            lanes = slice(t * LANES, (t + 1) * LANES)
            ap = jnp.concatenate([apr[:, lanes], api[:, lanes]], axis=1)
            spectra[2 * c + t] = jnp.dot(ap, cm_ref[...], preferred_element_type=F32)

    def stage2(r):
        rows = slice(r * row_chunk, (r + 1) * row_chunk)

        def pick(s2, part):
            return spectra[s2][rows, part * LANES:(part + 1) * LANES]

        for k2, (unit, half) in enumerate(plan):
            res = _signed_sum(unit, pick)
            if half:
                scaled = _signed_sum(half, pick) * math.sqrt(0.5)
                res = scaled if res is None else res + scaled
            o_ref[pl.ds(k2 * n_major + r * row_chunk, row_chunk), :] = res.astype(BF16)

    s1 = lambda c: (functools.partial(stage1, c), 4)
    ch = lambda c: (functools.partial(channels, c), 3)
    steps = [s1(0)]
    for c in range(1, n_pairs):
        steps += [s1(c), ch(c - 1)]
    steps.append(ch(n_pairs - 1))
    return steps + [(functools.partial(stage2, r), 1) for r in range(n_pairs)]


def _cis(num, den):
    ang = (num % den).astype(F32) * F32(-2.0 * math.pi / den)
    return jnp.cos(ang), jnp.sin(ang)


def _dft_tables(seq):
    n_major = seq // DFT_MINOR
    ch = FOURIER_GROUP_CH
    root = int(round(math.sqrt(n_major)))
    assert root * root == n_major
    s1 = jnp.arange(n_major, dtype=jnp.int32)[None, :]
    hi = jnp.arange(root, dtype=jnp.int32)[:, None] * root
    lo = jnp.arange(root, dtype=jnp.int32)[:, None]
    hr, hi_ = _cis(hi * s1, n_major)
    lr, li = _cis(lo * s1, n_major)
    p1 = jnp.stack([hr, hr])[:, :, None, :]
    q1 = jnp.stack([lr, li])[:, None, :, :]
    p2 = jnp.stack([-hi_, hi_])[:, :, None, :]
    q2 = jnp.stack([li, lr])[:, None, :, :]
    f1 = (p1 * q1 + p2 * q2).astype(BF16).reshape(2 * n_major, n_major)
    k1 = jnp.arange(n_major, dtype=jnp.int32)[:, None]
    s2 = jnp.arange(DFT_MINOR, dtype=jnp.int32)[None, :]
    tr, ti = _cis(k1 * s2, seq)
    tr = jnp.broadcast_to(tr[:, :, None], (n_major, DFT_MINOR, ch)).reshape(n_major, DFT_MINOR * ch)
    ti = jnp.broadcast_to(ti[:, :, None], (n_major, DFT_MINOR, ch)).reshape(n_major, DFT_MINOR * ch)
    c128 = jnp.arange(ch, dtype=jnp.int32)
    cr, ci = _cis(c128[:, None] * c128[None, :], ch)
    norm = F32(1.0 / math.sqrt(seq * ch))
    cm = (jnp.concatenate([jnp.concatenate([cr, ci], axis=1),
                           jnp.concatenate([-ci, cr], axis=1)], axis=0) * norm).astype(BF16)
    return f1, tr, ti, cm


def _mix_kernel(sink_ref, q_ref, kp_ref, kc_ref, kn_ref, kx_ref, vp_ref, vc_ref, vn_ref, vx_ref, bias_ref,
                f_ref, f1_ref, tr_ref, ti_ref, cm_ref, attn_ref, four_ref):
    attn_steps = _attention_steps(sink_ref, q_ref, kp_ref, kc_ref, kn_ref, kx_ref,
                                  vp_ref, vc_ref, vn_ref, vx_ref, bias_ref, attn_ref)
    four_steps = _fourier_steps(f_ref, f1_ref, tr_ref, ti_ref, cm_ref, four_ref)
    total = sum(weight for _, weight in four_steps)
    done = 0
    seen = 0
    for four_step, weight in four_steps:
        seen += weight
        upto = seen * len(attn_steps) // total
        for attn_step in attn_steps[done:upto]:
            attn_step()
        done = upto
        four_step()
    assert done == len(attn_steps)


def _mix(sink, q, k, v_t, kx, vx_t, bias, f3, tables, ctx_len):
    batch, seq, _ = f3.shape
    f1, tr, ti, cm = tables
    ch = FOURIER_GROUP_CH
    steps, attn_in, attn_out = _attention_specs(batch, seq, ctx_len, bias.shape)
    assert steps == N_FOURIER_GROUPS
    four_spec = pl.BlockSpec((None, seq, ch), lambda b, g: (b, 0, g))
    return pl.pallas_call(
        _mix_kernel,
        grid=(batch, steps),
        in_specs=attn_in + [four_spec, _const_spec(f1.shape), _const_spec(tr.shape),
                            _const_spec(ti.shape), _const_spec(cm.shape)],
        out_specs=[attn_out, four_spec],
        out_shape=[jax.ShapeDtypeStruct((batch * seq, Q_W), BF16),
                   jax.ShapeDtypeStruct((batch, seq, FOURIER_W), BF16)],
        compiler_params=_params(2),
        name="mix",
    )(sink, q, k, k, k, kx, v_t, v_t, v_t, vx_t, bias, f3, f1, tr, ti, cm)


def _merge_kernel(x_ref, a_ref, f_ref, mod_ref, gpre_ref, gpost_ref,
                  wg_b, wpa_b, wpf_b, wout_b, wup_ref, wdown_ref,
                  o_ref, wup_o, wdown_o, *, tiles_per_seq):
    wup_o[...] = wup_ref[...].astype(BF16)
    wdown_o[...] = wdown_ref[...].astype(BF16)

    x = x_ref[...]
    shift, scale_mod, out_gate = _mod_rows(mod_ref, pl.program_id(0) // tiles_per_seq, 0, 3)
    h = _norm_modulate(x, gpre_ref[...], shift, scale_mod).astype(BF16)
    a = a_ref[...]
    f = f_ref[...]
    y = None
    for c in range(0, D_MODEL, MXU_DIM):
        cols = slice(c, c + MXU_DIM)
        gate_a = jnp.dot(h, wg_b[:, cols], preferred_element_type=F32)
        gate_f = jnp.dot(h, wg_b[:, D_MODEL + c:D_MODEL + c + MXU_DIM], preferred_element_type=F32)
        pa = jnp.dot(a, wpa_b[:, cols], preferred_element_type=F32)
        pf = jnp.dot(f, wpf_b[:, cols], preferred_element_type=F32)
        m = (jax.nn.sigmoid(gate_a) * pa + jax.nn.sigmoid(gate_f) * pf).astype(BF16)
        part = jnp.dot(m, wout_b[cols, :], preferred_element_type=F32)
        y = part if y is None else y + part
    o_ref[...] = x + out_gate * _post_norm(y, gpost_ref[...])


def _merge(x2, attn, four, mods, g_pre, g_post, w_g, w_pa, w_pf, w_out, w_up, w_down, seq):
    n = x2.shape[0]
    tm = TOKEN_TILE
    n_steps = n // tm
    tiles_per_seq = seq // tm
    row = lambda i: (i, 0)
    up_rows = w_up.shape[0] // n_steps
    down_rows = w_down.shape[0] // BF16_ROWS
    n_down = w_down.shape[0] // down_rows
    assert up_rows * n_steps == w_up.shape[0] and up_rows % BF16_ROWS == 0
    assert down_rows % BF16_ROWS == 0 and n_down <= n_steps
    up_spec = pl.BlockSpec((up_rows, w_up.shape[1]), row)
    down_spec = pl.BlockSpec((down_rows, w_down.shape[1]), lambda i: (jnp.minimum(i, n_down - 1), 0))
    return pl.pallas_call(
        functools.partial(_merge_kernel, tiles_per_seq=tiles_per_seq),
        grid=(n // tm,),
        in_specs=[
            pl.BlockSpec((tm, D_MODEL), row),
            pl.BlockSpec((tm, Q_W), row),
            pl.BlockSpec((tm, FOURIER_W), row),
            _const_spec(mods.shape),
            _const_spec((1, D_MODEL)),
            _const_spec((1, D_MODEL)),
            _const_spec(w_g.shape),
            _const_spec(w_pa.shape),
            _const_spec(w_pf.shape),
            _const_spec(w_out.shape),
            up_spec,
            down_spec,
        ],
        out_specs=[pl.BlockSpec((tm, D_MODEL), row), up_spec, down_spec],
        out_shape=[jax.ShapeDtypeStruct((n, D_MODEL), F32),
                   jax.ShapeDtypeStruct(w_up.shape, BF16),
                   jax.ShapeDtypeStruct(w_down.shape, BF16)],
        compiler_params=_params(1),
        name="merge",
    )(x2, attn, four, mods, g_pre, g_post, w_g, w_pa, w_pf, w_out, w_up, w_down)


def _convffn_kernel(x_ref, xp_ref, xn_ref, mod_ref, gpre_ref, gpost_ref,
                    wu_ref, wgate_ref, cw_ref, cb_ref, wd_ref, o_ref, act_ref, *, tiles_per_seq):
    i = pl.program_id(0)
    tm = x_ref.shape[0]
    halo = SUBLANES
    has_prev = (i % tiles_per_seq != 0).astype(F32)
    has_next = (i % tiles_per_seq != tiles_per_seq - 1).astype(F32)
    x = x_ref[...]
    shift, scale, out_gate = _mod_rows(mod_ref, i // tiles_per_seq, 3, 3)
    gain = gpre_ref[...]
    h = _norm_modulate(x, gain, shift, scale)
    hp = _norm_modulate(xp_ref[...], gain, shift, scale) * has_prev
    hn = _norm_modulate(xn_ref[...], gain, shift, scale) * has_next
    h_ext = jnp.concatenate([hp, h, hn], axis=0).astype(BF16)
    h_mid = h.astype(BF16)
    n_ext = tm + 2 * halo
    for c in range(D_FF // FFN_CHUNK):
        cols = slice(c * FFN_CHUNK, (c + 1) * FFN_CHUNK)
        u = jnp.dot(h_ext, wu_ref[:, cols], preferred_element_type=F32)
        gate = jnp.dot(h_mid, wgate_ref[:, cols], preferred_element_type=F32)
        u_prev = pltpu.roll(u, 1, 0)[halo:halo + tm]
        u_next = pltpu.roll(u, n_ext - 1, 0)[halo:halo + tm]
        conv = (u_prev * cw_ref[0:1, cols] + u[halo:halo + tm] * cw_ref[1:2, cols]
                + u_next * cw_ref[2:3, cols] + cb_ref[:, cols])
        act_ref[:, cols] = (conv * jax.nn.sigmoid(conv) * gate).astype(BF16)
    y = jnp.dot(act_ref[...], wd_ref[...], preferred_element_type=F32)
    o_ref[...] = x + out_gate * _post_norm(y, gpost_ref[...])


def _convffn(x1, mods, g_pre, g_post, w_up, conv_w, conv_b, w_down, seq):
    n = x1.shape[0]
    tm = TOKEN_TILE
    tiles_per_seq = seq // tm
    halo_blocks_per_tile = tm // SUBLANES
    n_halo_blocks = n // SUBLANES
    return pl.pallas_call(
        functools.partial(_convffn_kernel, tiles_per_seq=tiles_per_seq),
        grid=(n // tm,),
        in_specs=[
            pl.BlockSpec((tm, D_MODEL), lambda i: (i, 0)),
            pl.BlockSpec((SUBLANES, D_MODEL),
                         lambda i: (jnp.maximum(i * halo_blocks_per_tile - 1, 0), 0)),
            pl.BlockSpec((SUBLANES, D_MODEL),
                         lambda i: (jnp.minimum((i + 1) * halo_blocks_per_tile, n_halo_blocks - 1), 0)),
            _const_spec(mods.shape),
            _const_spec((1, D_MODEL)),
            _const_spec((1, D_MODEL)),
            pl.BlockSpec((D_MODEL, D_FF), lambda i: (0, 0), pipeline_mode=pl.Buffered(1)),
            pl.BlockSpec((D_MODEL, D_FF), lambda i: (0, 1), pipeline_mode=pl.Buffered(1)),
            _const_spec(conv_w.shape),
            _const_spec((1, D_FF)),
            _const_spec(w_down.shape),
        ],
        out_specs=pl.BlockSpec((tm, D_MODEL), lambda i: (i, 0)),
        out_shape=jax.ShapeDtypeStruct((n, D_MODEL), F32),
        scratch_shapes=[pltpu.VMEM((tm, D_FF), BF16)],
        compiler_params=_params(1),
        name="convffn",
    )(x1, x1, x1, mods, g_pre, g_post, w_up, w_up, conv_w, conv_b.reshape(1, D_FF), w_down)


def _rope_tables(seq):
    half = HEAD_DIM // 2
    inv_freq = ROPE_THETA ** (-jnp.arange(0, half, 2, dtype=F32) / half)
    ang_r = jnp.arange(seq // GRID_W).astype(F32)[:, None] * inv_freq
    ang_c = jnp.arange(GRID_W).astype(F32)[:, None] * inv_freq
    reps = LANES // HEAD_DIM

    def lanes(row_part, col_part):
        return jnp.tile(jnp.concatenate([row_part, col_part], axis=-1), (1, reps))

    zr = jnp.zeros((ang_r.shape[0], half), F32)
    zc = jnp.zeros((ang_c.shape[0], half), F32)
    rcos = lanes(jnp.concatenate([jnp.cos(ang_r), jnp.cos(ang_r)], axis=-1), zr)
    rsin = lanes(jnp.concatenate([-jnp.sin(ang_r), jnp.sin(ang_r)], axis=-1), zr)
    ccos = lanes(zc, jnp.concatenate([jnp.cos(ang_c), jnp.cos(ang_c)], axis=-1))
    csin = lanes(zc, jnp.concatenate([-jnp.sin(ang_c), jnp.sin(ang_c)], axis=-1))
    return rcos, rsin, ccos, csin


def kernel(x, c, ctx, c_ctx, w_mod, b_mod, g_pre1, g_post1, g_pre2, g_post2,
           w_in, sink, w_pa, w_pf, w_out, w_up, conv_w, conv_b, w_down):
    batch, seq, d = x.shape
    ctx_len = ctx.shape[1]
    depth = w_mod.shape[0]
    assert depth == 1 and d == D_MODEL and batch + 1 <= SUBLANES
    assert seq % TOKEN_TILE == 0 and TOKEN_TILE % GRID_W == 0
    assert seq % (DFT_MINOR * SUBLANES) == 0
    assert seq % (ATTN_BLOCKS_PER_STEP * BLOCK) == 0
    n = batch * seq
    rope_tabs = _rope_tables(seq)
    tables = _dft_tables(seq)
    bias = _band_bias()

    l = 0
    mods = _adaln(c, c_ctx.reshape(1, d), w_mod[l], b_mod[l])

    g_pre1_row = g_pre1[l].reshape(1, d)
    g_post1_row = g_post1[l].reshape(1, d)
    g_pre2_row = g_pre2[l].reshape(1, d)
    g_post2_row = g_post2[l].reshape(1, d)

    x2 = x.reshape(n, d)
    q, k, v_t, f, w_g_b, w_pa_b, w_pf_b, w_out_b = _inproj(x2, mods, g_pre1_row, w_in[l], w_pa[l], w_pf[l], w_out[l],
                                                          rope_tabs, seq)
    kx, vx_t = _ctxkv(ctx.reshape(batch * ctx_len, d), mods, batch, g_pre1_row, w_in[l], ctx_len)
    attn, four = _mix(sink[l], q, k, v_t, kx, vx_t, bias, f.reshape(batch, seq, FOURIER_W), tables, ctx_len)
    four = four.reshape(n, FOURIER_W)
    x1, w_up_b, w_down_b = _merge(x2, attn, four, mods, g_pre1_row, g_post1_row, w_g_b, w_pa_b, w_pf_b, w_out_b,
                                  w_up[l], w_down[l], seq)
    out = _convffn(x1, mods, g_pre2_row, g_post2_row, w_up_b, conv_w[l], conv_b[l], w_down_b, seq)
    return out.reshape(batch, seq, d)
```

```python
import functools
import math

import numpy as np
import jax
import jax.numpy as jnp
from jax import lax
from jax.experimental import pallas as pl
from jax.experimental.pallas import tpu as pltpu

F32 = jnp.float32
BF16 = jnp.bfloat16

D_MODEL = 1024
GRID_W = 64
HEAD_DIM = 64
N_Q_HEADS = 8
N_KV_HEADS = 2
GROUP = N_Q_HEADS // N_KV_HEADS
WINDOW = 128
BLOCK = 128
ROPE_THETA = 10000.0
N_FOURIER_GROUPS = 4
FOURIER_GROUP_CH = 128
FOURIER_W = N_FOURIER_GROUPS * FOURIER_GROUP_CH
Q_W = N_Q_HEADS * HEAD_DIM
KV_W = N_KV_HEADS * HEAD_DIM
D_FF = 2816
EPS = 1e-6
NEG = -1e30
LOG2E = math.log2(math.e)

LANES = 128
SUBLANES = 8
BF16_ROWS = 16
MXU_DIM = 256
VMEM_LIMIT_BYTES = 56 * 1024 * 1024

TOKEN_TILE = 1024
ADALN_COLS = 512
FFN_CHUNK = MXU_DIM
DFT_MINOR = SUBLANES
ATTN_BLOCKS_PER_STEP = 16


def _const_spec(shape):
    nd = len(shape)
    return pl.BlockSpec(shape, lambda *_: (0,) * nd, pipeline_mode=pl.Buffered(1))


def _params(n_axes):
    return pltpu.CompilerParams(
        dimension_semantics=("arbitrary",) * n_axes,
        vmem_limit_bytes=VMEM_LIMIT_BYTES,
    )


def _norm_modulate(x, gain, shift, scale):
    ms = jnp.mean(x * x, axis=-1, keepdims=True)
    return (x * lax.rsqrt(ms + EPS)) * (gain * (1.0 + scale)) + shift


def _mod_rows(mod_ref, row, first, count):
    return [mod_ref[pl.ds(row, 1), (first + j) * D_MODEL:(first + j + 1) * D_MODEL] for j in range(count)]


def _post_norm(y, gain):
    ms = jnp.mean(y * y, axis=-1, keepdims=True)
    return (y * lax.rsqrt(ms + EPS)) * gain


def _adaln_kernel(c_ref, cctx_ref, w_ref, b_ref, o_ref, rows_ref):
    batch = c_ref.shape[0]
    rows_ref[...] = jnp.zeros(rows_ref.shape, F32)
    rows_ref[0:batch, :] = c_ref[...]
    rows_ref[batch:batch + 1, :] = cctx_ref[...]
    c = rows_ref[...]
    s = c * jax.nn.sigmoid(c)
    o_ref[...] = jnp.dot(s, w_ref[...], preferred_element_type=F32) + b_ref[...]


def _adaln(c, c_ctx, w_mod, b_mod):
    n_out = w_mod.shape[1]
    tn = ADALN_COLS
    return pl.pallas_call(
        _adaln_kernel,
        grid=(n_out // tn,),
        in_specs=[
            pl.BlockSpec(c.shape, lambda j: (0, 0)),
            pl.BlockSpec(c_ctx.shape, lambda j: (0, 0)),
            pl.BlockSpec((D_MODEL, tn), lambda j: (0, j)),
            pl.BlockSpec((1, tn), lambda j: (0, j)),
        ],
        out_specs=pl.BlockSpec((SUBLANES, tn), lambda j: (0, j)),
        out_shape=jax.ShapeDtypeStruct((SUBLANES, n_out), F32),
        scratch_shapes=[pltpu.VMEM((SUBLANES, D_MODEL), F32)],
        compiler_params=_params(1),
        name="adaln",
    )(c, c_ctx, w_mod, b_mod.reshape(1, n_out))


def _rope(xb, cos_t, sin_t, first_half):
    sw = jnp.where(first_half, pltpu.roll(xb, LANES - 16, 1), pltpu.roll(xb, 16, 1))
    return xb * cos_t + sw * sin_t


def _token_table(row_tab, col_tab):
    n_rows = row_tab.shape[0]
    by_row = jnp.concatenate(
        [jnp.broadcast_to(row_tab[r:r + 1, :], (GRID_W, LANES)) for r in range(n_rows)], axis=0)
    return by_row + jnp.concatenate([col_tab] * n_rows, axis=0)


def _inproj_kernel(x_ref, mod_ref, g_ref, w_ref, rcos_ref, rsin_ref, ccos_ref, csin_ref,
                   wg_ref, wpa_ref, wpf_ref, wout_ref,
                   q_ref, k_ref, v_ref, f_ref, wg_o, wpa_o, wpf_o, wout_o, wb_ref, *, tiles_per_seq):
    wg_o[...] = wg_ref[...].astype(BF16)
    wpa_o[...] = wpa_ref[...].astype(BF16)
    wpf_o[...] = wpf_ref[...].astype(BF16)
    wout_o[...] = wout_ref[...].astype(BF16)

    @pl.when(pl.program_id(0) == 0)
    def _():
        low = lax.broadcasted_iota(jnp.int32, (D_MODEL, LANES), 1) < HEAD_DIM
        for j in range(Q_W // LANES):
            src_a = (j // 2) * LANES
            src_b = (GROUP // 2 + j // 2) * LANES
            a = w_ref[:, src_a:src_a + LANES]
            b = w_ref[:, src_b:src_b + LANES]
            if j % 2 == 1:
                a = pltpu.roll(a, HEAD_DIM, 1)
            else:
                b = pltpu.roll(b, HEAD_DIM, 1)
            wb_ref[:, j * LANES:(j + 1) * LANES] = jnp.where(low, a, b).astype(BF16)
        wb_ref[:, Q_W:] = w_ref[:, Q_W:].astype(BF16)

    shift, scale_mod = _mod_rows(mod_ref, pl.program_id(0) // tiles_per_seq, 0, 2)
    h = _norm_modulate(x_ref[...], g_ref[...], shift, scale_mod).astype(BF16)
    cos_t = _token_table(rcos_ref[...], ccos_ref[...])
    sin_t = _token_table(rsin_ref[...], csin_ref[...])
    lane = lax.broadcasted_iota(jnp.int32, cos_t.shape, 1)
    first_half = (lane % 32) < 16
    scale = HEAD_DIM ** -0.5 * LOG2E

    def project(col0, width):
        return jnp.dot(h, wb_ref[:, col0:col0 + width], preferred_element_type=F32)

    for c in range(0, Q_W, MXU_DIM):
        p = project(c, MXU_DIM)
        for b in range(MXU_DIM // LANES):
            blk = p[:, b * LANES:(b + 1) * LANES]
            q_ref[:, c + b * LANES:c + (b + 1) * LANES] = (
                _rope(blk, cos_t, sin_t, first_half) * scale).astype(BF16)
    p = project(Q_W, 2 * KV_W)
    k_ref[...] = _rope(p[:, :KV_W], cos_t, sin_t, first_half).astype(BF16)
    v_ref[...] = p[:, KV_W:].T.astype(BF16)
    for c in range(0, FOURIER_W, MXU_DIM):
        f_ref[:, c:c + MXU_DIM] = project(Q_W + 2 * KV_W + c, MXU_DIM)


def _row_slab_rows(n_rows, n_steps):
    rows = n_rows // n_steps
    assert rows * n_steps == n_rows and rows % BF16_ROWS == 0
    return rows


def _inproj(x2, mods, g_pre, w_in, w_pa, w_pf, w_out, rope_tabs, seq):
    n = x2.shape[0]
    tm = TOKEN_TILE
    n_steps = n // tm
    tiles_per_seq = seq // tm
    rows_per_tile = tm // GRID_W
    n_cols = Q_W + 2 * KV_W + FOURIER_W
    row_spec = pl.BlockSpec((rows_per_tile, LANES), lambda i: (i % tiles_per_seq, 0))
    g_rows = _row_slab_rows(D_MODEL, n_steps)
    pa_rows = _row_slab_rows(w_pa.shape[0], n_steps)
    pf_rows = _row_slab_rows(w_pf.shape[0], n_steps)
    out_rows = _row_slab_rows(w_out.shape[0], n_steps)
    slab = lambda rows, cols: pl.BlockSpec((rows, cols), lambda i: (i, 0))
    return pl.pallas_call(
        functools.partial(_inproj_kernel, tiles_per_seq=tiles_per_seq),
        grid=(n // tm,),
        in_specs=[
            pl.BlockSpec((tm, D_MODEL), lambda i: (i, 0)),
            _const_spec(mods.shape),
            _const_spec((1, D_MODEL)),
            _const_spec((D_MODEL, n_cols)),
            row_spec, row_spec,
            _const_spec((GRID_W, LANES)), _const_spec((GRID_W, LANES)),
            pl.BlockSpec((pl.Element(g_rows), pl.Element(2 * D_MODEL)),
                         lambda i: (i * g_rows, n_cols)),
            slab(pa_rows, D_MODEL), slab(pf_rows, D_MODEL), slab(out_rows, D_MODEL),
        ],
        out_specs=[
            pl.BlockSpec((tm, Q_W), lambda i: (i, 0)),
            pl.BlockSpec((tm, KV_W), lambda i: (i, 0)),
            pl.BlockSpec((KV_W, tm), lambda i: (0, i)),
            pl.BlockSpec((tm, FOURIER_W), lambda i: (i, 0)),
            slab(g_rows, 2 * D_MODEL), slab(pa_rows, D_MODEL), slab(pf_rows, D_MODEL), slab(out_rows, D_MODEL),
        ],
        out_shape=[
            jax.ShapeDtypeStruct((n, Q_W), BF16),
            jax.ShapeDtypeStruct((n, KV_W), BF16),
            jax.ShapeDtypeStruct((KV_W, n), BF16),
            jax.ShapeDtypeStruct((n, FOURIER_W), F32),
            jax.ShapeDtypeStruct((D_MODEL, 2 * D_MODEL), BF16),
            jax.ShapeDtypeStruct(w_pa.shape, BF16),
            jax.ShapeDtypeStruct(w_pf.shape, BF16),
            jax.ShapeDtypeStruct(w_out.shape, BF16),
        ],
        scratch_shapes=[pltpu.VMEM((D_MODEL, n_cols), BF16)],
        compiler_params=_params(1),
        name="inproj",
    )(x2, mods, g_pre, w_in, *rope_tabs, w_in, w_pa, w_pf, w_out)


def _ctxkv_kernel(x_ref, mod_ref, g_ref, w_ref, k_ref, v_ref, *, row):
    shift, scale_mod = _mod_rows(mod_ref, row, 0, 2)
    h = _norm_modulate(x_ref[...], g_ref[...], shift, scale_mod)
    p = jnp.dot(h.astype(BF16), w_ref[...].astype(BF16), preferred_element_type=F32)
    k_ref[...] = p[:, :KV_W].astype(BF16)
    v_ref[...] = p[:, KV_W:].T.astype(BF16)


def _ctxkv(ctx2, mods, ctx_row, g_pre, w_in, ctx_len):
    n = ctx2.shape[0]
    assert Q_W % (2 * KV_W) == 0
    return pl.pallas_call(
        functools.partial(_ctxkv_kernel, row=ctx_row),
        grid=(n // ctx_len,),
        in_specs=[
            pl.BlockSpec((ctx_len, D_MODEL), lambda i: (i, 0)),
            _const_spec(mods.shape),
            _const_spec((1, D_MODEL)),
            pl.BlockSpec((D_MODEL, 2 * KV_W), lambda i: (0, Q_W // (2 * KV_W)),
                         pipeline_mode=pl.Buffered(1)),
        ],
        out_specs=[
            pl.BlockSpec((ctx_len, KV_W), lambda i: (i, 0)),
            pl.BlockSpec((KV_W, ctx_len), lambda i: (0, i)),
        ],
        out_shape=[
            jax.ShapeDtypeStruct((n, KV_W), BF16),
            jax.ShapeDtypeStruct((KV_W, n), BF16),
        ],
        compiler_params=_params(1),
        name="ctxkv",
    )(ctx2, mods, g_pre, w_in)


def _attend_block(sink_ref, q_blk, k_parts, v_parts, bias_prev, bias_next):
    kcat = jnp.concatenate(k_parts, axis=0)
    vcat_t = jnp.concatenate(v_parts, axis=1)
    n_keys = kcat.shape[0]
    low = lax.broadcasted_iota(jnp.int32, (BLOCK, LANES), 1) < HEAD_DIM
    top = lax.broadcasted_iota(jnp.int32, (KV_W, n_keys), 0) < HEAD_DIM
    zero = jnp.zeros((), BF16)
    one = jnp.ones((), BF16)
    q_rows = [jnp.where(low if h == 0 else jnp.logical_not(low), q_blk[:, g * LANES:(g + 1) * LANES], zero)
              for h in range(N_KV_HEADS) for g in range(GROUP)]
    s_all = lax.dot_general(kcat, jnp.concatenate(q_rows, axis=0), (((1,), (1,)), ((), ())),
                            preferred_element_type=F32).astype(BF16)
    outs = []
    for h in range(N_KV_HEADS):
        v_h = jnp.where(top if h == 0 else jnp.logical_not(top), vcat_t, one)
        for g0 in range(0, GROUP, 2):
            probs = []
            sinks = []
            for g in (g0, g0 + 1):
                head = h * GROUP + g
                s = s_all[:, head * LANES:(head + 1) * LANES]
                parts = [s[0:BLOCK] + bias_prev,
                         s[BLOCK:2 * BLOCK],
                         s[2 * BLOCK:3 * BLOCK] + bias_next]
                parts += [s[r:r + BLOCK] for r in range(3 * BLOCK, n_keys, BLOCK)]
                mx = parts[0]
                for part in parts[1:]:
                    mx = jnp.maximum(mx, part)
                sink = sink_ref[head] * LOG2E
                m = jnp.maximum(jnp.max(mx, axis=0, keepdims=True), sink.astype(BF16))
                probs.append(jnp.concatenate([jnp.exp2(part - m) for part in parts], axis=0))
                sinks.append(jnp.exp2(sink - m.astype(F32)))
            o2 = jnp.dot(v_h, jnp.concatenate(probs, axis=1), preferred_element_type=F32)
            for idx in range(2):
                o_t = o2[:, idx * LANES:(idx + 1) * LANES]
                num = o_t[0:HEAD_DIM] if h == 0 else o_t[HEAD_DIM:]
                den = (o_t[HEAD_DIM:] if h == 0 else o_t[0:HEAD_DIM]) + sinks[idx]
                outs.append(num / den)
    return outs


def _attention_steps(sink_ref, q_ref, kp_ref, kc_ref, kn_ref, kx_ref, vp_ref, vc_ref, vn_ref, vx_ref,
                     bias_ref, o_ref):
    i = pl.program_id(1)
    n_sub = q_ref.shape[0] // BLOCK
    first_var = jnp.where(i == 0, 0, 1)
    last_var = jnp.where(i == pl.num_programs(1) - 1, 2, 1)

    def block_step(t):
        own = slice(t * BLOCK, (t + 1) * BLOCK)
        before = slice((t - 1) * BLOCK, t * BLOCK)
        after = slice((t + 1) * BLOCK, (t + 2) * BLOCK)
        k_parts = [kp_ref[...] if t == 0 else kc_ref[before, :], kc_ref[own, :],
                   kn_ref[...] if t == n_sub - 1 else kc_ref[after, :], kx_ref[...]]
        v_parts = [vp_ref[...] if t == 0 else vc_ref[:, before], vc_ref[:, own],
                   vn_ref[...] if t == n_sub - 1 else vc_ref[:, after], vx_ref[...]]
        bias_prev = bias_ref[first_var if t == 0 else 1, 0:BLOCK, :]
        bias_next = bias_ref[last_var if t == n_sub - 1 else 1, BLOCK:2 * BLOCK, :]
        outs = _attend_block(sink_ref, q_ref[own, :], k_parts, v_parts, bias_prev, bias_next)
        for j in range(Q_W // LANES):
            o_t = jnp.concatenate([outs[2 * j], outs[2 * j + 1]], axis=0)
            o_ref[own, j * LANES:(j + 1) * LANES] = o_t.T.astype(BF16)

    return [functools.partial(block_step, t) for t in range(n_sub)]


def _attention_specs(batch, seq, ctx_len, bias_shape):
    nb = seq // BLOCK
    sub = ATTN_BLOCKS_PER_STEP
    steps = nb // sub

    def cur(b, i):
        return (b * steps + i, 0)

    def prev(b, i):
        return (b * nb + jnp.maximum(i * sub - 1, 0), 0)

    def nxt(b, i):
        return (b * nb + jnp.minimum((i + 1) * sub, nb - 1), 0)

    def swap(f):
        return lambda b, i: f(b, i)[::-1]

    in_specs = [
        pl.BlockSpec(memory_space=pltpu.SMEM),
        pl.BlockSpec((sub * BLOCK, Q_W), cur),
        pl.BlockSpec((BLOCK, KV_W), prev),
        pl.BlockSpec((sub * BLOCK, KV_W), cur),
        pl.BlockSpec((BLOCK, KV_W), nxt),
        pl.BlockSpec((ctx_len, KV_W), lambda b, i: (b, 0)),
        pl.BlockSpec((KV_W, BLOCK), swap(prev)),
        pl.BlockSpec((KV_W, sub * BLOCK), swap(cur)),
        pl.BlockSpec((KV_W, BLOCK), swap(nxt)),
        pl.BlockSpec((KV_W, ctx_len), lambda b, i: (0, b)),
        _const_spec(bias_shape),
    ]
    return steps, in_specs, pl.BlockSpec((sub * BLOCK, Q_W), cur)


def _band_bias():
    qi = np.arange(BLOCK)[None, :]
    kj = np.arange(3 * BLOCK)[:, None]
    rel = kj - BLOCK - qi
    in_window = np.abs(rel) <= WINDOW
    variants = []
    for has_prev, has_next in ((False, True), (True, True), (True, False)):
        ok = in_window.copy()
        if not has_prev:
            ok &= kj >= BLOCK
        if not has_next:
            ok &= kj < 2 * BLOCK
        mask = np.where(ok, 0.0, NEG).astype(np.float32)
        variants.append(np.concatenate([mask[:BLOCK], mask[2 * BLOCK:]], axis=0))
    return jnp.asarray(np.stack(variants)).astype(BF16)


def _minor_dft_terms():
    assert DFT_MINOR == 8
    plan = []
    for k2 in range(DFT_MINOR):
        unit, half = [], []
        for s2 in range(DFT_MINOR):
            ang = -2.0 * math.pi * ((s2 * k2) % DFT_MINOR) / DFT_MINOR
            for part, coef in ((0, math.cos(ang)), (1, -math.sin(ang))):
                if abs(coef) < 1e-9:
                    continue
                target = unit if abs(abs(coef) - 1.0) < 1e-9 else half
                assert target is unit or abs(abs(coef) - math.sqrt(0.5)) < 1e-9
                target.append((s2, part, 1 if coef > 0 else -1))
        plan.append((unit, half))
    return plan


def _signed_sum(terms, pick):
    pos = [pick(s2, part) for s2, part, sign in terms if sign > 0]
    neg = [pick(s2, part) for s2, part, sign in terms if sign < 0]
    total = None
    for v in pos:
        total = v if total is None else total + v
    for v in neg:
        total = -v if total is None else total - v
    return total


def _fourier_steps(f_ref, f1_ref, tr_ref, ti_ref, cm_ref, o_ref):
    n_major = f_ref.shape[0] // DFT_MINOR
    pair_w = 2 * LANES
    n_pairs = DFT_MINOR // 2
    row_chunk = n_major // n_pairs
    twiddled = [None] * n_pairs
    spectra = [None] * DFT_MINOR
    plan = _minor_dft_terms()

    def stage1(c):
        y = jnp.concatenate(
            [f_ref[pl.ds(s2, n_major, stride=DFT_MINOR), :].astype(BF16) for s2 in (2 * c, 2 * c + 1)],
            axis=1)
        a = jnp.dot(f1_ref[...], y, preferred_element_type=F32)
        ar = a[:n_major]
        ai = a[n_major:]
        tr = tr_ref[:, c * pair_w:(c + 1) * pair_w]
        ti = ti_ref[:, c * pair_w:(c + 1) * pair_w]
        twiddled[c] = ((ar * tr - ai * ti).astype(BF16), (ar * ti + ai * tr).astype(BF16))

    def channels(c):
        apr, api = twiddled[c]
        for t in range(2):
            lanes = slice(t * LANES, (t + 1) * LANES)
            ap = jnp.concatenate([apr[:, lanes], api[:, lanes]], axis=1)
            spectra[2 * c + t] = jnp.dot(ap, cm_ref[...], preferred_element_type=F32)

    def stage2(r):
        rows = slice(r * row_chunk, (r + 1) * row_chunk)

        def pick(s2, part):
            return spectra[s2][rows, part * LANES:(part + 1) * LANES]

        for k2, (unit, half) in enumerate(plan):
            res = _signed_sum(unit, pick)
            if half:
                scaled = _signed_sum(half, pick) * math.sqrt(0.5)
                res = scaled if res is None else res + scaled
            o_ref[pl.ds(k2 * n_major + r * row_chunk, row_chunk), :] = res.astype(BF16)

    s1 = lambda c: (functools.partial(stage1, c), 4)
    ch = lambda c: (functools.partial(channels, c), 3)
    steps = [s1(0)]
    for c in range(1, n_pairs):
        steps += [s1(c), ch(c - 1)]
    steps.append(ch(n_pairs - 1))
    return steps + [(functools.partial(stage2, r), 1) for r in range(n_pairs)]


def _cis(num, den):
    ang = (num % den).astype(F32) * F32(-2.0 * math.pi / den)
    return jnp.cos(ang), jnp.sin(ang)


def _dft_tables(seq):
    n_major = seq // DFT_MINOR
    ch = FOURIER_GROUP_CH
    root = int(round(math.sqrt(n_major)))
    assert root * root == n_major
    s1 = jnp.arange(n_major, dtype=jnp.int32)[None, :]
    hi = jnp.arange(root, dtype=jnp.int32)[:, None] * root
    lo = jnp.arange(root, dtype=jnp.int32)[:, None]
    hr, hi_ = _cis(hi * s1, n_major)
    lr, li = _cis(lo * s1, n_major)
    p1 = jnp.stack([hr, hr])[:, :, None, :]
    q1 = jnp.stack([lr, li])[:, None, :, :]
    p2 = jnp.stack([-hi_, hi_])[:, :, None, :]
    q2 = jnp.stack([li, lr])[:, None, :, :]
    f1 = (p1 * q1 + p2 * q2).astype(BF16).reshape(2 * n_major, n_major)
    k1 = jnp.arange(n_major, dtype=jnp.int32)[:, None]
    s2 = jnp.arange(DFT_MINOR, dtype=jnp.int32)[None, :]
    tr, ti = _cis(k1 * s2, seq)
    tr = jnp.broadcast_to(tr[:, :, None], (n_major, DFT_MINOR, ch)).reshape(n_major, DFT_MINOR * ch)
    ti = jnp.broadcast_to(ti[:, :, None], (n_major, DFT_MINOR, ch)).reshape(n_major, DFT_MINOR * ch)
    c128 = jnp.arange(ch, dtype=jnp.int32)
    cr, ci = _cis(c128[:, None] * c128[None, :], ch)
    norm = F32(1.0 / math.sqrt(seq * ch))
    cm = (jnp.concatenate([jnp.concatenate([cr, ci], axis=1),
                           jnp.concatenate([-ci, cr], axis=1)], axis=0) * norm).astype(BF16)
    return f1, tr, ti, cm


def _mix_kernel(sink_ref, q_ref, kp_ref, kc_ref, kn_ref, kx_ref, vp_ref, vc_ref, vn_ref, vx_ref, bias_ref,
                f_ref, f1_ref, tr_ref, ti_ref, cm_ref, attn_ref, four_ref):
    attn_steps = _attention_steps(sink_ref, q_ref, kp_ref, kc_ref, kn_ref, kx_ref,
                                  vp_ref, vc_ref, vn_ref, vx_ref, bias_ref, attn_ref)
    four_steps = _fourier_steps(f_ref, f1_ref, tr_ref, ti_ref, cm_ref, four_ref)
    total = sum(weight for _, weight in four_steps)
    done = 0
    seen = 0
    for four_step, weight in four_steps:
        seen += weight
        upto = seen * len(attn_steps) // total
        for attn_step in attn_steps[done:upto]:
            attn_step()
        done = upto
        four_step()
    assert done == len(attn_steps)


def _mix(sink, q, k, v_t, kx, vx_t, bias, f3, tables, ctx_len):
    batch, seq, _ = f3.shape
    f1, tr, ti, cm = tables
    ch = FOURIER_GROUP_CH
    steps, attn_in, attn_out = _attention_specs(batch, seq, ctx_len, bias.shape)
    assert steps == N_FOURIER_GROUPS
    four_spec = pl.BlockSpec((None, seq, ch), lambda b, g: (b, 0, g))
    return pl.pallas_call(
        _mix_kernel,
        grid=(batch, steps),
        in_specs=attn_in + [four_spec, _const_spec(f1.shape), _const_spec(tr.shape),
                            _const_spec(ti.shape), _const_spec(cm.shape)],
        out_specs=[attn_out, four_spec],
        out_shape=[jax.ShapeDtypeStruct((batch * seq, Q_W), BF16),
                   jax.ShapeDtypeStruct((batch, seq, FOURIER_W), BF16)],
        compiler_params=_params(2),
        name="mix",
    )(sink, q, k, k, k, kx, v_t, v_t, v_t, vx_t, bias, f3, f1, tr, ti, cm)


def _merge_kernel(x_ref, a_ref, f_ref, mod_ref, gpre_ref, gpost_ref,
                  wg_b, wpa_b, wpf_b, wout_b, wup_ref, wdown_ref,
                  o_ref, wup_o, wdown_o, *, tiles_per_seq):
    wup_o[...] = wup_ref[...].astype(BF16)
    wdown_o[...] = wdown_ref[...].astype(BF16)

    x = x_ref[...]
    shift, scale_mod, out_gate = _mod_rows(mod_ref, pl.program_id(0) // tiles_per_seq, 0, 3)
    h = _norm_modulate(x, gpre_ref[...], shift, scale_mod).astype(BF16)
    a = a_ref[...]
    f = f_ref[...]
    y = None
    for c in range(0, D_MODEL, MXU_DIM):
        cols = slice(c, c + MXU_DIM)
        gate_a = jnp.dot(h, wg_b[:, cols], preferred_element_type=F32)
        gate_f = jnp.dot(h, wg_b[:, D_MODEL + c:D_MODEL + c + MXU_DIM], preferred_element_type=F32)
        pa = jnp.dot(a, wpa_b[:, cols], preferred_element_type=F32)
        pf = jnp.dot(f, wpf_b[:, cols], preferred_element_type=F32)
        m = (jax.nn.sigmoid(gate_a) * pa + jax.nn.sigmoid(gate_f) * pf).astype(BF16)
        part = jnp.dot(m, wout_b[cols, :], preferred_element_type=F32)
        y = part if y is None else y + part
    o_ref[...] = x + out_gate * _post_norm(y, gpost_ref[...])


def _merge(x2, attn, four, mods, g_pre, g_post, w_g, w_pa, w_pf, w_out, w_up, w_down, seq):
    n = x2.shape[0]
    tm = TOKEN_TILE
    n_steps = n // tm
    tiles_per_seq = seq // tm
    row = lambda i: (i, 0)
    up_rows = w_up.shape[0] // n_steps
    down_rows = w_down.shape[0] // BF16_ROWS
    n_down = w_down.shape[0] // down_rows
    assert up_rows * n_steps == w_up.shape[0] and up_rows % BF16_ROWS == 0
    assert down_rows % BF16_ROWS == 0 and n_down <= n_steps
    up_spec = pl.BlockSpec((up_rows, w_up.shape[1]), row)
    down_spec = pl.BlockSpec((down_rows, w_down.shape[1]), lambda i: (jnp.minimum(i, n_down - 1), 0))
    return pl.pallas_call(
        functools.partial(_merge_kernel, tiles_per_seq=tiles_per_seq),
        grid=(n // tm,),
        in_specs=[
            pl.BlockSpec((tm, D_MODEL), row),
            pl.BlockSpec((tm, Q_W), row),
            pl.BlockSpec((tm, FOURIER_W), row),
            _const_spec(mods.shape),
            _const_spec((1, D_MODEL)),
            _const_spec((1, D_MODEL)),
            _const_spec(w_g.shape),
            _const_spec(w_pa.shape),
            _const_spec(w_pf.shape),
            _const_spec(w_out.shape),
            up_spec,
            down_spec,
        ],
        out_specs=[pl.BlockSpec((tm, D_MODEL), row), up_spec, down_spec],
        out_shape=[jax.ShapeDtypeStruct((n, D_MODEL), F32),
                   jax.ShapeDtypeStruct(w_up.shape, BF16),
                   jax.ShapeDtypeStruct(w_down.shape, BF16)],
        compiler_params=_params(1),
        name="merge",
    )(x2, attn, four, mods, g_pre, g_post, w_g, w_pa, w_pf, w_out, w_up, w_down)


def _convffn_kernel(x_ref, xp_ref, xn_ref, mod_ref, gpre_ref, gpost_ref,
                    wu_ref, wgate_ref, cw_ref, cb_ref, wd_ref, o_ref, act_ref, *, tiles_per_seq):
    i = pl.program_id(0)
    tm = x_ref.shape[0]
    halo = SUBLANES
    has_prev = (i % tiles_per_seq != 0).astype(F32)
    has_next = (i % tiles_per_seq != tiles_per_seq - 1).astype(F32)
    x = x_ref[...]
    shift, scale, out_gate = _mod_rows(mod_ref, i // tiles_per_seq, 3, 3)
    gain = gpre_ref[...]
    h = _norm_modulate(x, gain, shift, scale)
    hp = _norm_modulate(xp_ref[...], gain, shift, scale) * has_prev
    hn = _norm_modulate(xn_ref[...], gain, shift, scale) * has_next
    h_ext = jnp.concatenate([hp, h, hn], axis=0).astype(BF16)
    h_mid = h.astype(BF16)
    n_ext = tm + 2 * halo
    for c in range(D_FF // FFN_CHUNK):
        cols = slice(c * FFN_CHUNK, (c + 1) * FFN_CHUNK)
        u = jnp.dot(h_ext, wu_ref[:, cols], preferred_element_type=F32)
        gate = jnp.dot(h_mid, wgate_ref[:, cols], preferred_element_type=F32)
        u_prev = pltpu.roll(u, 1, 0)[halo:halo + tm]
        u_next = pltpu.roll(u, n_ext - 1, 0)[halo:halo + tm]
        conv = (u_prev * cw_ref[0:1, cols] + u[halo:halo + tm] * cw_ref[1:2, cols]
                + u_next * cw_ref[2:3, cols] + cb_ref[:, cols])
        act_ref[:, cols] = (conv * jax.nn.sigmoid(conv) * gate).astype(BF16)
    y = jnp.dot(act_ref[...], wd_ref[...], preferred_element_type=F32)
    o_ref[...] = x + out_gate * _post_norm(y, gpost_ref[...])


def _convffn(x1, mods, g_pre, g_post, w_up, conv_w, conv_b, w_down, seq):
    n = x1.shape[0]
    tm = TOKEN_TILE
    tiles_per_seq = seq // tm
    halo_blocks_per_tile = tm // SUBLANES
    n_halo_blocks = n // SUBLANES
    return pl.pallas_call(
        functools.partial(_convffn_kernel, tiles_per_seq=tiles_per_seq),
        grid=(n // tm,),
        in_specs=[
            pl.BlockSpec((tm, D_MODEL), lambda i: (i, 0)),
            pl.BlockSpec((SUBLANES, D_MODEL),
                         lambda i: (jnp.maximum(i * halo_blocks_per_tile - 1, 0), 0)),
            pl.BlockSpec((SUBLANES, D_MODEL),
                         lambda i: (jnp.minimum((i + 1) * halo_blocks_per_tile, n_halo_blocks - 1), 0)),
            _const_spec(mods.shape),
            _const_spec((1, D_MODEL)),
            _const_spec((1, D_MODEL)),
            pl.BlockSpec((D_MODEL, D_FF), lambda i: (0, 0), pipeline_mode=pl.Buffered(1)),
            pl.BlockSpec((D_MODEL, D_FF), lambda i: (0, 1), pipeline_mode=pl.Buffered(1)),
            _const_spec(conv_w.shape),
            _const_spec((1, D_FF)),
            _const_spec(w_down.shape),
        ],
        out_specs=pl.BlockSpec((tm, D_MODEL), lambda i: (i, 0)),
        out_shape=jax.ShapeDtypeStruct((n, D_MODEL), F32),
        scratch_shapes=[pltpu.VMEM((tm, D_FF), BF16)],
        compiler_params=_params(1),
        name="convffn",
    )(x1, x1, x1, mods, g_pre, g_post, w_up, w_up, conv_w, conv_b.reshape(1, D_FF), w_down)


def _rope_tables(seq):
    half = HEAD_DIM // 2
    inv_freq = ROPE_THETA ** (-jnp.arange(0, half, 2, dtype=F32) / half)
    ang_r = jnp.arange(seq // GRID_W).astype(F32)[:, None] * inv_freq
    ang_c = jnp.arange(GRID_W).astype(F32)[:, None] * inv_freq
    reps = LANES // HEAD_DIM

    def lanes(row_part, col_part):
        return jnp.tile(jnp.concatenate([row_part, col_part], axis=-1), (1, reps))

    zr = jnp.zeros((ang_r.shape[0], half), F32)
    zc = jnp.zeros((ang_c.shape[0], half), F32)
    rcos = lanes(jnp.concatenate([jnp.cos(ang_r), jnp.cos(ang_r)], axis=-1), zr)
    rsin = lanes(jnp.concatenate([-jnp.sin(ang_r), jnp.sin(ang_r)], axis=-1), zr)
    ccos = lanes(zc, jnp.concatenate([jnp.cos(ang_c), jnp.cos(ang_c)], axis=-1))
    csin = lanes(zc, jnp.concatenate([-jnp.sin(ang_c), jnp.sin(ang_c)], axis=-1))
    return rcos, rsin, ccos, csin


def kernel(x, c, ctx, c_ctx, w_mod, b_mod, g_pre1, g_post1, g_pre2, g_post2,
           w_in, sink, w_pa, w_pf, w_out, w_up, conv_w, conv_b, w_down):
    batch, seq, d = x.shape
    ctx_len = ctx.shape[1]
    depth = w_mod.shape[0]
    assert depth == 1 and d == D_MODEL and batch + 1 <= SUBLANES
    assert seq % TOKEN_TILE == 0 and TOKEN_TILE % GRID_W == 0
    assert seq % (DFT_MINOR * SUBLANES) == 0
    assert seq % (ATTN_BLOCKS_PER_STEP * BLOCK) == 0
    n = batch * seq
    rope_tabs = _rope_tables(seq)
    tables = _dft_tables(seq)
    bias = _band_bias()

    l = 0
    mods = _adaln(c, c_ctx.reshape(1, d), w_mod[l], b_mod[l])

    g_pre1_row = g_pre1[l].reshape(1, d)
    g_post1_row = g_post1[l].reshape(1, d)
    g_pre2_row = g_pre2[l].reshape(1, d)
    g_post2_row = g_post2[l].reshape(1, d)

    x2 = x.reshape(n, d)
    q, k, v_t, f, w_g_b, w_pa_b, w_pf_b, w_out_b = _inproj(x2, mods, g_pre1_row, w_in[l], w_pa[l], w_pf[l], w_out[l],
                                                          rope_tabs, seq)
    kx, vx_t = _ctxkv(ctx.reshape(batch * ctx_len, d), mods, batch, g_pre1_row, w_in[l], ctx_len)
    attn, four = _mix(sink[l], q, k, v_t, kx, vx_t, bias, f.reshape(batch, seq, FOURIER_W), tables, ctx_len)
    four = four.reshape(n, FOURIER_W)
    x1, w_up_b, w_down_b = _merge(x2, attn, four, mods, g_pre1_row, g_post1_row, w_g_b, w_pa_b, w_pf_b, w_out_b,
                                  w_up[l], w_down[l], seq)
    out = _convffn(x1, mods, g_pre2_row, g_post2_row, w_up_b, conv_w[l], conv_b[l], w_down_b, seq)
    return out.reshape(batch, seq, d)
```

```python
import functools
import math

import numpy as np
import jax
import jax.numpy as jnp
from jax import lax
from jax.experimental import pallas as pl
from jax.experimental.pallas import tpu as pltpu

F32 = jnp.float32
BF16 = jnp.bfloat16

D_MODEL = 1024
GRID_W = 64
HEAD_DIM = 64
N_Q_HEADS = 8
N_KV_HEADS = 2
GROUP = N_Q_HEADS // N_KV_HEADS
WINDOW = 128
BLOCK = 128
ROPE_THETA = 10000.0
N_FOURIER_GROUPS = 4
FOURIER_GROUP_CH = 128
FOURIER_W = N_FOURIER_GROUPS * FOURIER_GROUP_CH
Q_W = N_Q_HEADS * HEAD_DIM
KV_W = N_KV_HEADS * HEAD_DIM
D_FF = 2816
EPS = 1e-6
NEG = -1e30
LOG2E = math.log2(math.e)

LANES = 128
SUBLANES = 8
BF16_ROWS = 16
MXU_DIM = 256
VMEM_LIMIT_BYTES = 56 * 1024 * 1024

TOKEN_TILE = 1024
FFN_CHUNK = MXU_DIM
DFT_MINOR = SUBLANES
ATTN_BLOCKS_PER_STEP = 16


def _const_spec(shape):
    nd = len(shape)
    return pl.BlockSpec(shape, lambda *_: (0,) * nd, pipeline_mode=pl.Buffered(1))


def _params(n_axes):
    return pltpu.CompilerParams(
        dimension_semantics=("arbitrary",) * n_axes,
        vmem_limit_bytes=VMEM_LIMIT_BYTES,
    )


def _norm_modulate(x, gain, shift, scale):
    ms = jnp.mean(x * x, axis=-1, keepdims=True)
    return (x * lax.rsqrt(ms + EPS)) * (gain * (1.0 + scale)) + shift


def _mod_rows(mod_ref, row, first, count):
    return [mod_ref[pl.ds(row, 1), (first + j) * D_MODEL:(first + j + 1) * D_MODEL] for j in range(count)]


def _post_norm(y, gain):
    ms = jnp.mean(y * y, axis=-1, keepdims=True)
    return (y * lax.rsqrt(ms + EPS)) * gain


def _adaln_kernel(c_ref, cctx_ref, w_ref, b_ref, o_ref, rows_ref):
    batch = c_ref.shape[0]
    rows_ref[...] = jnp.zeros(rows_ref.shape, F32)
    rows_ref[0:batch, :] = c_ref[...]
    rows_ref[batch:batch + 1, :] = cctx_ref[...]
    c = rows_ref[...]
    s = c * jax.nn.sigmoid(c)
    o_ref[...] = jnp.dot(s, w_ref[...], preferred_element_type=F32) + b_ref[...]


def _adaln(c, c_ctx, w_mod, b_mod):
    n_out = w_mod.shape[1]
    tn = 1536
    return pl.pallas_call(
        _adaln_kernel,
        grid=(n_out // tn,),
        in_specs=[
            pl.BlockSpec(c.shape, lambda j: (0, 0)),
            pl.BlockSpec(c_ctx.shape, lambda j: (0, 0)),
            pl.BlockSpec((D_MODEL, tn), lambda j: (0, j)),
            pl.BlockSpec((1, tn), lambda j: (0, j)),
        ],
        out_specs=pl.BlockSpec((SUBLANES, tn), lambda j: (0, j)),
        out_shape=jax.ShapeDtypeStruct((SUBLANES, n_out), F32),
        scratch_shapes=[pltpu.VMEM((SUBLANES, D_MODEL), F32)],
        compiler_params=_params(1),
        name="adaln",
    )(c, c_ctx, w_mod, b_mod.reshape(1, n_out))


def _rope(xb, cos_t, sin_t, first_half):
    sw = jnp.where(first_half, pltpu.roll(xb, LANES - 16, 1), pltpu.roll(xb, 16, 1))
    return xb * cos_t + sw * sin_t


def _token_table(row_tab, col_tab):
    n_rows = row_tab.shape[0]
    by_row = jnp.concatenate(
        [jnp.broadcast_to(row_tab[r:r + 1, :], (GRID_W, LANES)) for r in range(n_rows)], axis=0)
    return by_row + jnp.concatenate([col_tab] * n_rows, axis=0)


def _inproj_kernel(x_ref, mod_ref, g_ref, w_ref, rcos_ref, rsin_ref, ccos_ref, csin_ref,
                   wg_ref, wpa_ref, wpf_ref, wout_ref,
                   q_ref, k_ref, v_ref, f_ref, wg_o, wpa_o, wpf_o, wout_o, wb_ref, *, tiles_per_seq):
    wg_o[...] = wg_ref[...].astype(BF16)
    wpa_o[...] = wpa_ref[...].astype(BF16)
    wpf_o[...] = wpf_ref[...].astype(BF16)
    wout_o[...] = wout_ref[...].astype(BF16)

    @pl.when(pl.program_id(0) == 0)
    def _():
        low = lax.broadcasted_iota(jnp.int32, (D_MODEL, LANES), 1) < HEAD_DIM
        for j in range(Q_W // LANES):
            src_a = (j // 2) * LANES
            src_b = (GROUP // 2 + j // 2) * LANES
            a = w_ref[:, src_a:src_a + LANES]
            b = w_ref[:, src_b:src_b + LANES]
            if j % 2 == 1:
                a = pltpu.roll(a, HEAD_DIM, 1)
            else:
                b = pltpu.roll(b, HEAD_DIM, 1)
            wb_ref[:, j * LANES:(j + 1) * LANES] = jnp.where(low, a, b).astype(BF16)
        wb_ref[:, Q_W:] = w_ref[:, Q_W:].astype(BF16)

    shift, scale_mod = _mod_rows(mod_ref, pl.program_id(0) // tiles_per_seq, 0, 2)
    h = _norm_modulate(x_ref[...], g_ref[...], shift, scale_mod).astype(BF16)
    cos_t = _token_table(rcos_ref[...], ccos_ref[...])
    sin_t = _token_table(rsin_ref[...], csin_ref[...])
    lane = lax.broadcasted_iota(jnp.int32, cos_t.shape, 1)
    first_half = (lane % 32) < 16
    scale = HEAD_DIM ** -0.5 * LOG2E

    def project(col0, width):
        return jnp.dot(h, wb_ref[:, col0:col0 + width], preferred_element_type=F32)

    for c in range(0, Q_W, MXU_DIM):
        p = project(c, MXU_DIM)
        for b in range(MXU_DIM // LANES):
            blk = p[:, b * LANES:(b + 1) * LANES]
            q_ref[:, c + b * LANES:c + (b + 1) * LANES] = (
                _rope(blk, cos_t, sin_t, first_half) * scale).astype(BF16)
    p = project(Q_W, 2 * KV_W)
    k_ref[...] = _rope(p[:, :KV_W], cos_t, sin_t, first_half).astype(BF16)
    v_ref[...] = p[:, KV_W:].T.astype(BF16)
    for c in range(0, FOURIER_W, MXU_DIM):
        f_ref[:, c:c + MXU_DIM] = project(Q_W + 2 * KV_W + c, MXU_DIM)


def _row_slab_rows(n_rows, n_steps):
    rows = n_rows // n_steps
    assert rows * n_steps == n_rows and rows % BF16_ROWS == 0
    return rows


def _inproj(x2, mods, g_pre, w_in, w_pa, w_pf, w_out, rope_tabs, seq):
    n = x2.shape[0]
    tm = TOKEN_TILE
    n_steps = n // tm
    tiles_per_seq = seq // tm
    rows_per_tile = tm // GRID_W
    n_cols = Q_W + 2 * KV_W + FOURIER_W
    row_spec = pl.BlockSpec((rows_per_tile, LANES), lambda i: (i % tiles_per_seq, 0))
    g_rows = _row_slab_rows(D_MODEL, n_steps)
    pa_rows = _row_slab_rows(w_pa.shape[0], n_steps)
    pf_rows = _row_slab_rows(w_pf.shape[0], n_steps)
    out_rows = _row_slab_rows(w_out.shape[0], n_steps)
    slab = lambda rows, cols: pl.BlockSpec((rows, cols), lambda i: (i, 0))
    return pl.pallas_call(
        functools.partial(_inproj_kernel, tiles_per_seq=tiles_per_seq),
        grid=(n // tm,),
        in_specs=[
            pl.BlockSpec((tm, D_MODEL), lambda i: (i, 0)),
            _const_spec(mods.shape),
            _const_spec((1, D_MODEL)),
            _const_spec((D_MODEL, n_cols)),
            row_spec, row_spec,
            _const_spec((GRID_W, LANES)), _const_spec((GRID_W, LANES)),
            pl.BlockSpec((pl.Element(g_rows), pl.Element(2 * D_MODEL)),
                         lambda i: (i * g_rows, n_cols)),
            slab(pa_rows, D_MODEL), slab(pf_rows, D_MODEL), slab(out_rows, D_MODEL),
        ],
        out_specs=[
            pl.BlockSpec((tm, Q_W), lambda i: (i, 0)),
            pl.BlockSpec((tm, KV_W), lambda i: (i, 0)),
            pl.BlockSpec((KV_W, tm), lambda i: (0, i)),
            pl.BlockSpec((tm, FOURIER_W), lambda i: (i, 0)),
            slab(g_rows, 2 * D_MODEL), slab(pa_rows, D_MODEL), slab(pf_rows, D_MODEL), slab(out_rows, D_MODEL),
        ],
        out_shape=[
            jax.ShapeDtypeStruct((n, Q_W), BF16),
            jax.ShapeDtypeStruct((n, KV_W), BF16),
            jax.ShapeDtypeStruct((KV_W, n), BF16),
            jax.ShapeDtypeStruct((n, FOURIER_W), F32),
            jax.ShapeDtypeStruct((D_MODEL, 2 * D_MODEL), BF16),
            jax.ShapeDtypeStruct(w_pa.shape, BF16),
            jax.ShapeDtypeStruct(w_pf.shape, BF16),
            jax.ShapeDtypeStruct(w_out.shape, BF16),
        ],
        scratch_shapes=[pltpu.VMEM((D_MODEL, n_cols), BF16)],
        compiler_params=_params(1),
        name="inproj",
    )(x2, mods, g_pre, w_in, *rope_tabs, w_in, w_pa, w_pf, w_out)


def _ctxkv_kernel(x_ref, mod_ref, g_ref, w_ref, k_ref, v_ref, *, row):
    shift, scale_mod = _mod_rows(mod_ref, row, 0, 2)
    h = _norm_modulate(x_ref[...], g_ref[...], shift, scale_mod)
    p = jnp.dot(h.astype(BF16), w_ref[...].astype(BF16), preferred_element_type=F32)
    k_ref[...] = p[:, :KV_W].astype(BF16)
    v_ref[...] = p[:, KV_W:].T.astype(BF16)


def _ctxkv(ctx2, mods, ctx_row, g_pre, w_in, ctx_len):
    n = ctx2.shape[0]
    assert Q_W % (2 * KV_W) == 0
    return pl.pallas_call(
        functools.partial(_ctxkv_kernel, row=ctx_row),
        grid=(n // ctx_len,),
        in_specs=[
            pl.BlockSpec((ctx_len, D_MODEL), lambda i: (i, 0)),
            _const_spec(mods.shape),
            _const_spec((1, D_MODEL)),
            pl.BlockSpec((D_MODEL, 2 * KV_W), lambda i: (0, Q_W // (2 * KV_W)),
                         pipeline_mode=pl.Buffered(1)),
        ],
        out_specs=[
            pl.BlockSpec((ctx_len, KV_W), lambda i: (i, 0)),
            pl.BlockSpec((KV_W, ctx_len), lambda i: (0, i)),
        ],
        out_shape=[
            jax.ShapeDtypeStruct((n, KV_W), BF16),
            jax.ShapeDtypeStruct((KV_W, n), BF16),
        ],
        compiler_params=_params(1),
        name="ctxkv",
    )(ctx2, mods, g_pre, w_in)


def _attend_block(sink_ref, q_blk, k_parts, v_parts, bias_prev, bias_next):
    kcat = jnp.concatenate(k_parts, axis=0)
    vcat_t = jnp.concatenate(v_parts, axis=1)
    n_keys = kcat.shape[0]
    low = lax.broadcasted_iota(jnp.int32, (BLOCK, LANES), 1) < HEAD_DIM
    top = lax.broadcasted_iota(jnp.int32, (KV_W, n_keys), 0) < HEAD_DIM
    zero = jnp.zeros((), BF16)
    one = jnp.ones((), BF16)
    q_rows = [jnp.where(low if h == 0 else jnp.logical_not(low), q_blk[:, g * LANES:(g + 1) * LANES], zero)
              for h in range(N_KV_HEADS) for g in range(GROUP)]
    s_all = lax.dot_general(kcat, jnp.concatenate(q_rows, axis=0), (((1,), (1,)), ((), ())),
                            preferred_element_type=F32).astype(BF16)
    outs = []
    for h in range(N_KV_HEADS):
        v_h = jnp.where(top if h == 0 else jnp.logical_not(top), vcat_t, one)
        for g0 in range(0, GROUP, 2):
            probs = []
            sinks = []
            for g in (g0, g0 + 1):
                head = h * GROUP + g
                s = s_all[:, head * LANES:(head + 1) * LANES]
                parts = [s[0:BLOCK] + bias_prev,
                         s[BLOCK:2 * BLOCK],
                         s[2 * BLOCK:3 * BLOCK] + bias_next]
                parts += [s[r:r + BLOCK] for r in range(3 * BLOCK, n_keys, BLOCK)]
                mx = parts[0]
                for part in parts[1:]:
                    mx = jnp.maximum(mx, part)
                sink = sink_ref[head] * LOG2E
                m = jnp.maximum(jnp.max(mx, axis=0, keepdims=True), sink.astype(BF16))
                probs.append(jnp.concatenate([jnp.exp2(part - m) for part in parts], axis=0))
                sinks.append(jnp.exp2(sink - m.astype(F32)))
            o2 = jnp.dot(v_h, jnp.concatenate(probs, axis=1), preferred_element_type=F32)
            for idx in range(2):
                o_t = o2[:, idx * LANES:(idx + 1) * LANES]
                num = o_t[0:HEAD_DIM] if h == 0 else o_t[HEAD_DIM:]
                den = (o_t[HEAD_DIM:] if h == 0 else o_t[0:HEAD_DIM]) + sinks[idx]
                outs.append(num / den)
    return outs


def _attention_steps(sink_ref, q_ref, kp_ref, kc_ref, kn_ref, kx_ref, vp_ref, vc_ref, vn_ref, vx_ref,
                     bias_ref, o_ref):
    i = pl.program_id(1)
    n_sub = q_ref.shape[0] // BLOCK
    first_var = jnp.where(i == 0, 0, 1)
    last_var = jnp.where(i == pl.num_programs(1) - 1, 2, 1)

    def block_step(t):
        own = slice(t * BLOCK, (t + 1) * BLOCK)
        before = slice((t - 1) * BLOCK, t * BLOCK)
        after = slice((t + 1) * BLOCK, (t + 2) * BLOCK)
        k_parts = [kp_ref[...] if t == 0 else kc_ref[before, :], kc_ref[own, :],
                   kn_ref[...] if t == n_sub - 1 else kc_ref[after, :], kx_ref[...]]
        v_parts = [vp_ref[...] if t == 0 else vc_ref[:, before], vc_ref[:, own],
                   vn_ref[...] if t == n_sub - 1 else vc_ref[:, after], vx_ref[...]]
        bias_prev = bias_ref[first_var if t == 0 else 1, 0:BLOCK, :]
        bias_next = bias_ref[last_var if t == n_sub - 1 else 1, BLOCK:2 * BLOCK, :]
        outs = _attend_block(sink_ref, q_ref[own, :], k_parts, v_parts, bias_prev, bias_next)
        for j in range(Q_W // LANES):
            o_t = jnp.concatenate([outs[2 * j], outs[2 * j + 1]], axis=0)
            o_ref[own, j * LANES:(j + 1) * LANES] = o_t.T.astype(BF16)

    return [functools.partial(block_step, t) for t in range(n_sub)]


def _attention_specs(batch, seq, ctx_len, bias_shape):
    nb = seq // BLOCK
    sub = ATTN_BLOCKS_PER_STEP
    steps = nb // sub

    def cur(b, i):
        return (b * steps + i, 0)

    def prev(b, i):
        return (b * nb + jnp.maximum(i * sub - 1, 0), 0)

    def nxt(b, i):
        return (b * nb + jnp.minimum((i + 1) * sub, nb - 1), 0)

    def swap(f):
        return lambda b, i: f(b, i)[::-1]

    in_specs = [
        pl.BlockSpec(memory_space=pltpu.SMEM),
        pl.BlockSpec((sub * BLOCK, Q_W), cur),
        pl.BlockSpec((BLOCK, KV_W), prev),
        pl.BlockSpec((sub * BLOCK, KV_W), cur),
        pl.BlockSpec((BLOCK, KV_W), nxt),
        pl.BlockSpec((ctx_len, KV_W), lambda b, i: (b, 0)),
        pl.BlockSpec((KV_W, BLOCK), swap(prev)),
        pl.BlockSpec((KV_W, sub * BLOCK), swap(cur)),
        pl.BlockSpec((KV_W, BLOCK), swap(nxt)),
        pl.BlockSpec((KV_W, ctx_len), lambda b, i: (0, b)),
        _const_spec(bias_shape),
    ]
    return steps, in_specs, pl.BlockSpec((sub * BLOCK, Q_W), cur)


def _band_bias():
    qi = np.arange(BLOCK)[None, :]
    kj = np.arange(3 * BLOCK)[:, None]
    rel = kj - BLOCK - qi
    in_window = np.abs(rel) <= WINDOW
    variants = []
    for has_prev, has_next in ((False, True), (True, True), (True, False)):
        ok = in_window.copy()
        if not has_prev:
            ok &= kj >= BLOCK
        if not has_next:
            ok &= kj < 2 * BLOCK
        mask = np.where(ok, 0.0, NEG).astype(np.float32)
        variants.append(np.concatenate([mask[:BLOCK], mask[2 * BLOCK:]], axis=0))
    return jnp.asarray(np.stack(variants)).astype(BF16)


def _minor_dft_terms():
    assert DFT_MINOR == 8
    plan = []
    for k2 in range(DFT_MINOR):
        unit, half = [], []
        for s2 in range(DFT_MINOR):
            ang = -2.0 * math.pi * ((s2 * k2) % DFT_MINOR) / DFT_MINOR
            for part, coef in ((0, math.cos(ang)), (1, -math.sin(ang))):
                if abs(coef) < 1e-9:
                    continue
                target = unit if abs(abs(coef) - 1.0) < 1e-9 else half
                assert target is unit or abs(abs(coef) - math.sqrt(0.5)) < 1e-9
                target.append((s2, part, 1 if coef > 0 else -1))
        plan.append((unit, half))
    return plan


def _signed_sum(terms, pick):
    pos = [pick(s2, part) for s2, part, sign in terms if sign > 0]
    neg = [pick(s2, part) for s2, part, sign in terms if sign < 0]
    total = None
    for v in pos:
        total = v if total is None else total + v
    for v in neg:
        total = -v if total is None else total - v
    return total


def _fourier_steps(f_ref, f1_ref, tr_ref, ti_ref, cm_ref, o_ref):
    n_major = f_ref.shape[0] // DFT_MINOR
    pair_w = 2 * LANES
    n_pairs = DFT_MINOR // 2
    row_chunk = n_major // n_pairs
    twiddled = [None] * n_pairs
    spectra = [None] * DFT_MINOR
    plan = _minor_dft_terms()

    def stage1(c):
        y = jnp.concatenate(
            [f_ref[pl.ds(s2, n_major, stride=DFT_MINOR), :].astype(BF16) for s2 in (2 * c, 2 * c + 1)],
            axis=1)
        a = jnp.dot(f1_ref[...], y, preferred_element_type=F32)
        ar = a[:n_major]
        ai = a[n_major:]
        tr = tr_ref[:, c * pair_w:(c + 1) * pair_w]
        ti = ti_ref[:, c * pair_w:(c + 1) * pair_w]
        twiddled[c] = ((ar * tr - ai * ti).astype(BF16), (ar * ti + ai * tr).astype(BF16))

    def channels(c):
        apr, api = twiddled[c]
        for t in range(2):
            lanes = slice(t * LANES, (t + 1) * LANES)
            ap = jnp.concatenate([apr[:, lanes], api[:, lanes]], axis=1)
            spectra[2 * c + t] = jnp.dot(ap, cm_ref[...], preferred_element_type=F32)

    def stage2(r):
        rows = slice(r * row_chunk, (r + 1) * row_chunk)

        def pick(s2, part):
            return spectra[s2][rows, part * LANES:(part + 1) * LANES]

        for k2, (unit, half) in enumerate(plan):
            res = _signed_sum(unit, pick)
            if half:
                scaled = _signed_sum(half, pick) * math.sqrt(0.5)
                res = scaled if res is None else res + scaled
            o_ref[pl.ds(k2 * n_major + r * row_chunk, row_chunk), :] = res.astype(BF16)

    s1 = lambda c: (functools.partial(stage1, c), 4)
    ch = lambda c: (functools.partial(channels, c), 3)
    steps = [s1(0)]
    for c in range(1, n_pairs):
        steps += [s1(c), ch(c - 1)]
    steps.append(ch(n_pairs - 1))
    return steps + [(functools.partial(stage2, r), 1) for r in range(n_pairs)]


def _cis(num, den):
    ang = (num % den).astype(F32) * F32(-2.0 * math.pi / den)
    return jnp.cos(ang), jnp.sin(ang)


def _dft_tables(seq):
    n_major = seq // DFT_MINOR
    ch = FOURIER_GROUP_CH
    root = int(round(math.sqrt(n_major)))
    assert root * root == n_major
    s1 = jnp.arange(n_major, dtype=jnp.int32)[None, :]
    hi = jnp.arange(root, dtype=jnp.int32)[:, None] * root
    lo = jnp.arange(root, dtype=jnp.int32)[:, None]
    hr, hi_ = _cis(hi * s1, n_major)
    lr, li = _cis(lo * s1, n_major)
    p1 = jnp.stack([hr, hr])[:, :, None, :]
    q1 = jnp.stack([lr, li])[:, None, :, :]
    p2 = jnp.stack([-hi_, hi_])[:, :, None, :]
    q2 = jnp.stack([li, lr])[:, None, :, :]
    f1 = (p1 * q1 + p2 * q2).astype(BF16).reshape(2 * n_major, n_major)
    k1 = jnp.arange(n_major, dtype=jnp.int32)[:, None]
    s2 = jnp.arange(DFT_MINOR, dtype=jnp.int32)[None, :]
    tr, ti = _cis(k1 * s2, seq)
    tr = jnp.broadcast_to(tr[:, :, None], (n_major, DFT_MINOR, ch)).reshape(n_major, DFT_MINOR * ch)
    ti = jnp.broadcast_to(ti[:, :, None], (n_major, DFT_MINOR, ch)).reshape(n_major, DFT_MINOR * ch)
    c128 = jnp.arange(ch, dtype=jnp.int32)
    cr, ci = _cis(c128[:, None] * c128[None, :], ch)
    norm = F32(1.0 / math.sqrt(seq * ch))
    cm = (jnp.concatenate([jnp.concatenate([cr, ci], axis=1),
                           jnp.concatenate([-ci, cr], axis=1)], axis=0) * norm).astype(BF16)
    return f1, tr, ti, cm


def _mix_kernel(sink_ref, q_ref, kp_ref, kc_ref, kn_ref, kx_ref, vp_ref, vc_ref, vn_ref, vx_ref, bias_ref,
                f_ref, f1_ref, tr_ref, ti_ref, cm_ref, attn_ref, four_ref):
    attn_steps = _attention_steps(sink_ref, q_ref, kp_ref, kc_ref, kn_ref, kx_ref,
                                  vp_ref, vc_ref, vn_ref, vx_ref, bias_ref, attn_ref)
    four_steps = _fourier_steps(f_ref, f1_ref, tr_ref, ti_ref, cm_ref, four_ref)
    total = sum(weight for _, weight in four_steps)
    done = 0
    seen = 0
    for four_step, weight in four_steps:
        seen += weight
        upto = seen * len(attn_steps) // total
        for attn_step in attn_steps[done:upto]:
            attn_step()
        done = upto
        four_step()
    assert done == len(attn_steps)


def _mix(sink, q, k, v_t, kx, vx_t, bias, f3, tables, ctx_len):
    batch, seq, _ = f3.shape
    f1, tr, ti, cm = tables
    ch = FOURIER_GROUP_CH
    steps, attn_in, attn_out = _attention_specs(batch, seq, ctx_len, bias.shape)
    assert steps == N_FOURIER_GROUPS
    four_spec = pl.BlockSpec((None, seq, ch), lambda b, g: (b, 0, g))
    return pl.pallas_call(
        _mix_kernel,
        grid=(batch, steps),
        in_specs=attn_in + [four_spec, _const_spec(f1.shape), _const_spec(tr.shape),
                            _const_spec(ti.shape), _const_spec(cm.shape)],
        out_specs=[attn_out, four_spec],
        out_shape=[jax.ShapeDtypeStruct((batch * seq, Q_W), BF16),
                   jax.ShapeDtypeStruct((batch, seq, FOURIER_W), BF16)],
        compiler_params=_params(2),
        name="mix",
    )(sink, q, k, k, k, kx, v_t, v_t, v_t, vx_t, bias, f3, f1, tr, ti, cm)


def _merge_kernel(x_ref, a_ref, f_ref, mod_ref, gpre_ref, gpost_ref,
                  wg_b, wpa_b, wpf_b, wout_b, wup_ref, wdown_ref,
                  o_ref, wup_o, wdown_o, *, tiles_per_seq):
    wup_o[...] = wup_ref[...].astype(BF16)
    wdown_o[...] = wdown_ref[...].astype(BF16)

    shift, scale_mod, out_gate = _mod_rows(mod_ref, pl.program_id(0) // tiles_per_seq, 0, 3)
    gain = gpre_ref[...]
    post_gain = gpost_ref[...]
    n_chunks = D_MODEL // MXU_DIM
    half = x_ref.shape[0] // 2
    piece = half // n_chunks
    normed = {}
    y = [None, None]

    def pre_norm(r0, rows):
        normed[r0] = _norm_modulate(x_ref[r0:r0 + rows, :], gain, shift, scale_mod).astype(BF16)

    def matmul_chunk(part, c):
        rows = slice(part * half, (part + 1) * half)
        pieces = [normed[r] for r in sorted(normed) if part * half <= r < (part + 1) * half]
        h = jnp.concatenate(pieces, axis=0) if len(pieces) > 1 else pieces[0]
        cols = slice(c * MXU_DIM, (c + 1) * MXU_DIM)
        gate_a = jnp.dot(h, wg_b[:, cols], preferred_element_type=F32)
        gate_f = jnp.dot(h, wg_b[:, D_MODEL + c * MXU_DIM:D_MODEL + (c + 1) * MXU_DIM],
                         preferred_element_type=F32)
        pa = jnp.dot(a_ref[rows, :], wpa_b[:, cols], preferred_element_type=F32)
        pf = jnp.dot(f_ref[rows, :], wpf_b[:, cols], preferred_element_type=F32)
        m = (jax.nn.sigmoid(gate_a) * pa + jax.nn.sigmoid(gate_f) * pf).astype(BF16)
        contrib = jnp.dot(m, wout_b[cols, :], preferred_element_type=F32)
        y[part] = contrib if y[part] is None else y[part] + contrib

    def finish(part, r0, rows):
        local = slice(r0 - part * half, r0 - part * half + rows)
        o_ref[r0:r0 + rows, :] = x_ref[r0:r0 + rows, :] + out_gate * _post_norm(y[part][local], post_gain)

    pre_norm(0, half)
    for c in range(n_chunks):
        matmul_chunk(0, c)
        pre_norm(half + c * piece, piece)
    for c in range(n_chunks):
        matmul_chunk(1, c)
        finish(0, c * piece, piece)
    finish(1, half, half)


def _merge(x2, attn, four, mods, g_pre, g_post, w_g, w_pa, w_pf, w_out, w_up, w_down, seq):
    n = x2.shape[0]
    tm = TOKEN_TILE
    n_steps = n // tm
    tiles_per_seq = seq // tm
    row = lambda i: (i, 0)
    up_rows = w_up.shape[0] // n_steps
    down_rows = w_down.shape[0] // BF16_ROWS
    n_down = w_down.shape[0] // down_rows
    assert up_rows * n_steps == w_up.shape[0] and up_rows % BF16_ROWS == 0
    assert down_rows % BF16_ROWS == 0 and n_down <= n_steps
    up_spec = pl.BlockSpec((up_rows, w_up.shape[1]), row)
    down_spec = pl.BlockSpec((down_rows, w_down.shape[1]), lambda i: (jnp.minimum(i, n_down - 1), 0))
    return pl.pallas_call(
        functools.partial(_merge_kernel, tiles_per_seq=tiles_per_seq),
        grid=(n // tm,),
        in_specs=[
            pl.BlockSpec((tm, D_MODEL), row),
            pl.BlockSpec((tm, Q_W), row),
            pl.BlockSpec((tm, FOURIER_W), row),
            _const_spec(mods.shape),
            _const_spec((1, D_MODEL)),
            _const_spec((1, D_MODEL)),
            _const_spec(w_g.shape),
            _const_spec(w_pa.shape),
            _const_spec(w_pf.shape),
            _const_spec(w_out.shape),
            up_spec,
            down_spec,
        ],
        out_specs=[pl.BlockSpec((tm, D_MODEL), row), up_spec, down_spec],
        out_shape=[jax.ShapeDtypeStruct((n, D_MODEL), F32),
                   jax.ShapeDtypeStruct(w_up.shape, BF16),
                   jax.ShapeDtypeStruct(w_down.shape, BF16)],
        compiler_params=_params(1),
        name="merge",
    )(x2, attn, four, mods, g_pre, g_post, w_g, w_pa, w_pf, w_out, w_up, w_down)


def _convffn_kernel(x_ref, xp_ref, xn_ref, mod_ref, gpre_ref, gpost_ref,
                    wu_ref, wgate_ref, cw_ref, cb_ref, wd_ref, o_ref, act_ref, *, tiles_per_seq):
    i = pl.program_id(0)
    tm = x_ref.shape[0]
    halo = SUBLANES
    has_prev = (i % tiles_per_seq != 0).astype(F32)
    has_next = (i % tiles_per_seq != tiles_per_seq - 1).astype(F32)
    x = x_ref[...]
    shift, scale, out_gate = _mod_rows(mod_ref, i // tiles_per_seq, 3, 3)
    gain = gpre_ref[...]
    h = _norm_modulate(x, gain, shift, scale)
    hp = _norm_modulate(xp_ref[...], gain, shift, scale) * has_prev
    hn = _norm_modulate(xn_ref[...], gain, shift, scale) * has_next
    h_ext = jnp.concatenate([hp, h, hn], axis=0).astype(BF16)
    h_mid = h.astype(BF16)
    n_ext = tm + 2 * halo
    for c in range(D_FF // FFN_CHUNK):
        cols = slice(c * FFN_CHUNK, (c + 1) * FFN_CHUNK)
        u = jnp.dot(h_ext, wu_ref[:, cols], preferred_element_type=F32)
        gate = jnp.dot(h_mid, wgate_ref[:, cols], preferred_element_type=F32)
        u_prev = pltpu.roll(u, 1, 0)[halo:halo + tm]
        u_next = pltpu.roll(u, n_ext - 1, 0)[halo:halo + tm]
        conv = (u_prev * cw_ref[0:1, cols] + u[halo:halo + tm] * cw_ref[1:2, cols]
                + u_next * cw_ref[2:3, cols] + cb_ref[:, cols])
        act_ref[:, cols] = (conv * jax.nn.sigmoid(conv) * gate).astype(BF16)
    y = jnp.dot(act_ref[...], wd_ref[...], preferred_element_type=F32)
    o_ref[...] = x + out_gate * _post_norm(y, gpost_ref[...])


def _convffn(x1, mods, g_pre, g_post, w_up, conv_w, conv_b, w_down, seq):
    n = x1.shape[0]
    tm = TOKEN_TILE
    tiles_per_seq = seq // tm
    halo_blocks_per_tile = tm // SUBLANES
    n_halo_blocks = n // SUBLANES
    return pl.pallas_call(
        functools.partial(_convffn_kernel, tiles_per_seq=tiles_per_seq),
        grid=(n // tm,),
        in_specs=[
            pl.BlockSpec((tm, D_MODEL), lambda i: (i, 0)),
            pl.BlockSpec((SUBLANES, D_MODEL),
                         lambda i: (jnp.maximum(i * halo_blocks_per_tile - 1, 0), 0)),
            pl.BlockSpec((SUBLANES, D_MODEL),
                         lambda i: (jnp.minimum((i + 1) * halo_blocks_per_tile, n_halo_blocks - 1), 0)),
            _const_spec(mods.shape),
            _const_spec((1, D_MODEL)),
            _const_spec((1, D_MODEL)),
            pl.BlockSpec((D_MODEL, D_FF), lambda i: (0, 0), pipeline_mode=pl.Buffered(1)),
            pl.BlockSpec((D_MODEL, D_FF), lambda i: (0, 1), pipeline_mode=pl.Buffered(1)),
            _const_spec(conv_w.shape),
            _const_spec((1, D_FF)),
            _const_spec(w_down.shape),
        ],
        out_specs=pl.BlockSpec((tm, D_MODEL), lambda i: (i, 0)),
        out_shape=jax.ShapeDtypeStruct((n, D_MODEL), F32),
        scratch_shapes=[pltpu.VMEM((tm, D_FF), BF16)],
        compiler_params=_params(1),
        name="convffn",
    )(x1, x1, x1, mods, g_pre, g_post, w_up, w_up, conv_w, conv_b.reshape(1, D_FF), w_down)


def _rope_tables(seq):
    half = HEAD_DIM // 2
    inv_freq = ROPE_THETA ** (-jnp.arange(0, half, 2, dtype=F32) / half)
    ang_r = jnp.arange(seq // GRID_W).astype(F32)[:, None] * inv_freq
    ang_c = jnp.arange(GRID_W).astype(F32)[:, None] * inv_freq
    reps = LANES // HEAD_DIM

    def lanes(row_part, col_part):
        return jnp.tile(jnp.concatenate([row_part, col_part], axis=-1), (1, reps))

    zr = jnp.zeros((ang_r.shape[0], half), F32)
    zc = jnp.zeros((ang_c.shape[0], half), F32)
    rcos = lanes(jnp.concatenate([jnp.cos(ang_r), jnp.cos(ang_r)], axis=-1), zr)
    rsin = lanes(jnp.concatenate([-jnp.sin(ang_r), jnp.sin(ang_r)], axis=-1), zr)
    ccos = lanes(zc, jnp.concatenate([jnp.cos(ang_c), jnp.cos(ang_c)], axis=-1))
    csin = lanes(zc, jnp.concatenate([-jnp.sin(ang_c), jnp.sin(ang_c)], axis=-1))
    return rcos, rsin, ccos, csin


def kernel(x, c, ctx, c_ctx, w_mod, b_mod, g_pre1, g_post1, g_pre2, g_post2,
           w_in, sink, w_pa, w_pf, w_out, w_up, conv_w, conv_b, w_down):
    batch, seq, d = x.shape
    ctx_len = ctx.shape[1]
    depth = w_mod.shape[0]
    assert depth == 1 and d == D_MODEL and batch + 1 <= SUBLANES
    assert seq % TOKEN_TILE == 0 and TOKEN_TILE % GRID_W == 0
    assert seq % (DFT_MINOR * SUBLANES) == 0
    assert seq % (ATTN_BLOCKS_PER_STEP * BLOCK) == 0
    n = batch * seq
    rope_tabs = _rope_tables(seq)
    tables = _dft_tables(seq)
    bias = _band_bias()

    l = 0
    mods = _adaln(c, c_ctx.reshape(1, d), w_mod[l], b_mod[l])

    g_pre1_row = g_pre1[l].reshape(1, d)
    g_post1_row = g_post1[l].reshape(1, d)
    g_pre2_row = g_pre2[l].reshape(1, d)
    g_post2_row = g_post2[l].reshape(1, d)

    x2 = x.reshape(n, d)
    q, k, v_t, f, w_g_b, w_pa_b, w_pf_b, w_out_b = _inproj(x2, mods, g_pre1_row, w_in[l], w_pa[l], w_pf[l], w_out[l],
                                                          rope_tabs, seq)
    kx, vx_t = _ctxkv(ctx.reshape(batch * ctx_len, d), mods, batch, g_pre1_row, w_in[l], ctx_len)
    attn, four = _mix(sink[l], q, k, v_t, kx, vx_t, bias, f.reshape(batch, seq, FOURIER_W), tables, ctx_len)
    four = four.reshape(n, FOURIER_W)
    x1, w_up_b, w_down_b = _merge(x2, attn, four, mods, g_pre1_row, g_post1_row, w_g_b, w_pa_b, w_pf_b, w_out_b,
                                  w_up[l], w_down[l], seq)
    out = _convffn(x1, mods, g_pre2_row, g_post2_row, w_up_b, conv_w[l], conv_b[l], w_down_b, seq)
    return out.reshape(batch, seq, d)
```

```python
import functools
import math

import numpy as np
import jax
import jax.numpy as jnp
from jax import lax
from jax.experimental import pallas as pl
from jax.experimental.pallas import tpu as pltpu

F32 = jnp.float32
BF16 = jnp.bfloat16

D_MODEL = 1024
GRID_W = 64
HEAD_DIM = 64
N_Q_HEADS = 8
N_KV_HEADS = 2
GROUP = N_Q_HEADS // N_KV_HEADS
WINDOW = 128
BLOCK = 128
ROPE_THETA = 10000.0
N_FOURIER_GROUPS = 4
FOURIER_GROUP_CH = 128
FOURIER_W = N_FOURIER_GROUPS * FOURIER_GROUP_CH
Q_W = N_Q_HEADS * HEAD_DIM
KV_W = N_KV_HEADS * HEAD_DIM
D_FF = 2816
EPS = 1e-6
NEG = -1e30
LOG2E = math.log2(math.e)

LANES = 128
SUBLANES = 8
BF16_ROWS = 16
MXU_DIM = 256
MIB = 1024 * 1024
VMEM_LIMIT_BYTES = 56 * MIB

TOKEN_TILE = 1024
FFN_CHUNK = MXU_DIM
DFT_MINOR = SUBLANES
ATTN_BLOCKS_PER_STEP = 16


def _const_spec(shape):
    nd = len(shape)
    return pl.BlockSpec(shape, lambda *_: (0,) * nd, pipeline_mode=pl.Buffered(1))


def _params(n_axes, vmem_bytes=VMEM_LIMIT_BYTES):
    return pltpu.CompilerParams(
        dimension_semantics=("arbitrary",) * n_axes,
        vmem_limit_bytes=min(int(vmem_bytes), VMEM_LIMIT_BYTES),
    )


def _norm_modulate(x, gain, shift, scale):
    ms = jnp.mean(x * x, axis=-1, keepdims=True)
    return (x * lax.rsqrt(ms + EPS)) * (gain * (1.0 + scale)) + shift


def _mod_rows(mod_ref, row, first, count):
    return [mod_ref[pl.ds(row, 1), (first + j) * D_MODEL:(first + j + 1) * D_MODEL] for j in range(count)]


def _post_norm(y, gain):
    ms = jnp.mean(y * y, axis=-1, keepdims=True)
    return (y * lax.rsqrt(ms + EPS)) * gain


def _adaln_kernel(c_ref, cctx_ref, w_ref, b_ref, o_ref, rows_ref):
    batch = c_ref.shape[0]
    rows_ref[...] = jnp.zeros(rows_ref.shape, F32)
    rows_ref[0:batch, :] = c_ref[...]
    rows_ref[batch:batch + 1, :] = cctx_ref[...]
    c = rows_ref[...]
    s = c * jax.nn.sigmoid(c)
    o_ref[...] = jnp.dot(s, w_ref[...], preferred_element_type=F32) + b_ref[...]


def _adaln(c, c_ctx, w_mod, b_mod):
    n_out = w_mod.shape[1]
    tn = 1536
    return pl.pallas_call(
        _adaln_kernel,
        grid=(n_out // tn,),
        in_specs=[
            pl.BlockSpec(c.shape, lambda j: (0, 0)),
            pl.BlockSpec(c_ctx.shape, lambda j: (0, 0)),
            pl.BlockSpec((D_MODEL, tn), lambda j: (0, j)),
            pl.BlockSpec((1, tn), lambda j: (0, j)),
        ],
        out_specs=pl.BlockSpec((SUBLANES, tn), lambda j: (0, j)),
        out_shape=jax.ShapeDtypeStruct((SUBLANES, n_out), F32),
        scratch_shapes=[pltpu.VMEM((SUBLANES, D_MODEL), F32)],
        compiler_params=_params(1, 2 * D_MODEL * tn * 4 + 8 * MIB),
        name="adaln",
    )(c, c_ctx, w_mod, b_mod.reshape(1, n_out))


def _rope(xb, cos_t, sin_t, first_half):
    sw = jnp.where(first_half, pltpu.roll(xb, LANES - 16, 1), pltpu.roll(xb, 16, 1))
    return xb * cos_t + sw * sin_t


def _token_table(row_tab, col_tab):
    n_rows = row_tab.shape[0]
    by_row = jnp.concatenate(
        [jnp.broadcast_to(row_tab[r:r + 1, :], (GRID_W, LANES)) for r in range(n_rows)], axis=0)
    return by_row + jnp.concatenate([col_tab] * n_rows, axis=0)


def _inproj_kernel(x_ref, mod_ref, g_ref, w_ref, rcos_ref, rsin_ref, ccos_ref, csin_ref,
                   wg_ref, wpa_ref, wpf_ref, wout_ref,
                   q_ref, k_ref, v_ref, f_ref, wg_o, wpa_o, wpf_o, wout_o, wb_ref, *, tiles_per_seq):
    wg_o[...] = wg_ref[...].astype(BF16)
    wpa_o[...] = wpa_ref[...].astype(BF16)
    wpf_o[...] = wpf_ref[...].astype(BF16)
    wout_o[...] = wout_ref[...].astype(BF16)

    @pl.when(pl.program_id(0) == 0)
    def _():
        low = lax.broadcasted_iota(jnp.int32, (D_MODEL, LANES), 1) < HEAD_DIM
        for j in range(Q_W // LANES):
            src_a = (j // 2) * LANES
            src_b = (GROUP // 2 + j // 2) * LANES
            a = w_ref[:, src_a:src_a + LANES]
            b = w_ref[:, src_b:src_b + LANES]
            if j % 2 == 1:
                a = pltpu.roll(a, HEAD_DIM, 1)
            else:
                b = pltpu.roll(b, HEAD_DIM, 1)
            wb_ref[:, j * LANES:(j + 1) * LANES] = jnp.where(low, a, b).astype(BF16)
        wb_ref[:, Q_W:] = w_ref[:, Q_W:].astype(BF16)

    shift, scale_mod = _mod_rows(mod_ref, pl.program_id(0) // tiles_per_seq, 0, 2)
    h = _norm_modulate(x_ref[...], g_ref[...], shift, scale_mod).astype(BF16)
    cos_t = _token_table(rcos_ref[...], ccos_ref[...])
    sin_t = _token_table(rsin_ref[...], csin_ref[...])
    lane = lax.broadcasted_iota(jnp.int32, cos_t.shape, 1)
    first_half = (lane % 32) < 16
    scale = HEAD_DIM ** -0.5 * LOG2E

    def project(col0, width):
        return jnp.dot(h, wb_ref[:, col0:col0 + width], preferred_element_type=F32)

    for c in range(0, Q_W, MXU_DIM):
        p = project(c, MXU_DIM)
        for b in range(MXU_DIM // LANES):
            blk = p[:, b * LANES:(b + 1) * LANES]
            q_ref[:, c + b * LANES:c + (b + 1) * LANES] = (
                _rope(blk, cos_t, sin_t, first_half) * scale).astype(BF16)
    p = project(Q_W, 2 * KV_W)
    k_ref[...] = _rope(p[:, :KV_W], cos_t, sin_t, first_half).astype(BF16)
    v_ref[...] = p[:, KV_W:].T.astype(BF16)
    for c in range(0, FOURIER_W, MXU_DIM):
        f_ref[:, c:c + MXU_DIM] = project(Q_W + 2 * KV_W + c, MXU_DIM)


def _row_slab_rows(n_rows, n_steps):
    rows = n_rows // n_steps
    assert rows * n_steps == n_rows and rows % BF16_ROWS == 0
    return rows


def _inproj(x2, mods, g_pre, w_in, w_pa, w_pf, w_out, rope_tabs, seq):
    n = x2.shape[0]
    tm = TOKEN_TILE
    n_steps = n // tm
    tiles_per_seq = seq // tm
    rows_per_tile = tm // GRID_W
    n_cols = Q_W + 2 * KV_W + FOURIER_W
    row_spec = pl.BlockSpec((rows_per_tile, LANES), lambda i: (i % tiles_per_seq, 0))
    g_rows = _row_slab_rows(D_MODEL, n_steps)
    pa_rows = _row_slab_rows(w_pa.shape[0], n_steps)
    pf_rows = _row_slab_rows(w_pf.shape[0], n_steps)
    out_rows = _row_slab_rows(w_out.shape[0], n_steps)
    slab = lambda rows, cols: pl.BlockSpec((rows, cols), lambda i: (i, 0))
    return pl.pallas_call(
        functools.partial(_inproj_kernel, tiles_per_seq=tiles_per_seq),
        grid=(n // tm,),
        in_specs=[
            pl.BlockSpec((tm, D_MODEL), lambda i: (i, 0)),
            _const_spec(mods.shape),
            _const_spec((1, D_MODEL)),
            _const_spec((D_MODEL, n_cols)),
            row_spec, row_spec,
            _const_spec((GRID_W, LANES)), _const_spec((GRID_W, LANES)),
            pl.BlockSpec((pl.Element(g_rows), pl.Element(2 * D_MODEL)),
                         lambda i: (i * g_rows, n_cols)),
            slab(pa_rows, D_MODEL), slab(pf_rows, D_MODEL), slab(out_rows, D_MODEL),
        ],
        out_specs=[
            pl.BlockSpec((tm, Q_W), lambda i: (i, 0)),
            pl.BlockSpec((tm, KV_W), lambda i: (i, 0)),
            pl.BlockSpec((KV_W, tm), lambda i: (0, i)),
            pl.BlockSpec((tm, FOURIER_W), lambda i: (i, 0)),
            slab(g_rows, 2 * D_MODEL), slab(pa_rows, D_MODEL), slab(pf_rows, D_MODEL), slab(out_rows, D_MODEL),
        ],
        out_shape=[
            jax.ShapeDtypeStruct((n, Q_W), BF16),
            jax.ShapeDtypeStruct((n, KV_W), BF16),
            jax.ShapeDtypeStruct((KV_W, n), BF16),
            jax.ShapeDtypeStruct((n, FOURIER_W), F32),
            jax.ShapeDtypeStruct((D_MODEL, 2 * D_MODEL), BF16),
            jax.ShapeDtypeStruct(w_pa.shape, BF16),
            jax.ShapeDtypeStruct(w_pf.shape, BF16),
            jax.ShapeDtypeStruct(w_out.shape, BF16),
        ],
        scratch_shapes=[pltpu.VMEM((D_MODEL, n_cols), BF16)],
        compiler_params=_params(1, 2 * tm * (D_MODEL * 4 + Q_W * 2 + 2 * KV_W * 2 + FOURIER_W * 4)
                                + D_MODEL * n_cols * 6 + 2 * tm * D_MODEL * 4 + 6 * MIB),
        name="inproj",
    )(x2, mods, g_pre, w_in, *rope_tabs, w_in, w_pa, w_pf, w_out)


def _ctxkv_kernel(x_ref, mod_ref, g_ref, w_ref, k_ref, v_ref, *, row):
    shift, scale_mod = _mod_rows(mod_ref, row, 0, 2)
    h = _norm_modulate(x_ref[...], g_ref[...], shift, scale_mod)
    p = jnp.dot(h.astype(BF16), w_ref[...].astype(BF16), preferred_element_type=F32)
    k_ref[...] = p[:, :KV_W].astype(BF16)
    v_ref[...] = p[:, KV_W:].T.astype(BF16)


def _ctxkv(ctx2, mods, ctx_row, g_pre, w_in, ctx_len):
    n = ctx2.shape[0]
    assert Q_W % (2 * KV_W) == 0
    return pl.pallas_call(
        functools.partial(_ctxkv_kernel, row=ctx_row),
        grid=(n // ctx_len,),
        in_specs=[
            pl.BlockSpec((ctx_len, D_MODEL), lambda i: (i, 0)),
            _const_spec(mods.shape),
            _const_spec((1, D_MODEL)),
            pl.BlockSpec((D_MODEL, 2 * KV_W), lambda i: (0, Q_W // (2 * KV_W)),
                         pipeline_mode=pl.Buffered(1)),
        ],
        out_specs=[
            pl.BlockSpec((ctx_len, KV_W), lambda i: (i, 0)),
            pl.BlockSpec((KV_W, ctx_len), lambda i: (0, i)),
        ],
        out_shape=[
            jax.ShapeDtypeStruct((n, KV_W), BF16),
            jax.ShapeDtypeStruct((KV_W, n), BF16),
        ],
        compiler_params=_params(1, 2 * ctx_len * D_MODEL * 4 + D_MODEL * 2 * KV_W * 4 + 8 * MIB),
        name="ctxkv",
    )(ctx2, mods, g_pre, w_in)


def _attend_block(sink_ref, q_blk, k_parts, v_parts, bias_prev, bias_next):
    kcat = jnp.concatenate(k_parts, axis=0)
    vcat_t = jnp.concatenate(v_parts, axis=1)
    n_keys = kcat.shape[0]
    low = lax.broadcasted_iota(jnp.int32, (BLOCK, LANES), 1) < HEAD_DIM
    top = lax.broadcasted_iota(jnp.int32, (KV_W, n_keys), 0) < HEAD_DIM
    zero = jnp.zeros((), BF16)
    one = jnp.ones((), BF16)
    q_rows = [jnp.where(low if h == 0 else jnp.logical_not(low), q_blk[:, g * LANES:(g + 1) * LANES], zero)
              for h in range(N_KV_HEADS) for g in range(GROUP)]
    s_all = lax.dot_general(kcat, jnp.concatenate(q_rows, axis=0), (((1,), (1,)), ((), ())),
                            preferred_element_type=F32).astype(BF16)
    outs = []
    for h in range(N_KV_HEADS):
        v_h = jnp.where(top if h == 0 else jnp.logical_not(top), vcat_t, one)
        for g0 in range(0, GROUP, 2):
            probs = []
            sinks = []
            for g in (g0, g0 + 1):
                head = h * GROUP + g
                s = s_all[:, head * LANES:(head + 1) * LANES]
                parts = [s[0:BLOCK] + bias_prev,
                         s[BLOCK:2 * BLOCK],
                         s[2 * BLOCK:3 * BLOCK] + bias_next]
                parts += [s[r:r + BLOCK] for r in range(3 * BLOCK, n_keys, BLOCK)]
                mx = parts[0]
                for part in parts[1:]:
                    mx = jnp.maximum(mx, part)
                sink = sink_ref[head] * LOG2E
                m = jnp.maximum(jnp.max(mx, axis=0, keepdims=True), sink.astype(BF16))
                probs.append(jnp.concatenate([jnp.exp2(part - m) for part in parts], axis=0))
                sinks.append(jnp.exp2(sink - m.astype(F32)))
            o2 = jnp.dot(v_h, jnp.concatenate(probs, axis=1), preferred_element_type=F32)
            for idx in range(2):
                o_t = o2[:, idx * LANES:(idx + 1) * LANES]
                num = o_t[0:HEAD_DIM] if h == 0 else o_t[HEAD_DIM:]
                den = (o_t[HEAD_DIM:] if h == 0 else o_t[0:HEAD_DIM]) + sinks[idx]
                outs.append(num / den)
    return outs


def _attention_steps(sink_ref, q_ref, kp_ref, kc_ref, kn_ref, kx_ref, vp_ref, vc_ref, vn_ref, vx_ref,
                     bias_ref, o_ref):
    i = pl.program_id(1)
    n_sub = q_ref.shape[0] // BLOCK
    first_var = jnp.where(i == 0, 0, 1)
    last_var = jnp.where(i == pl.num_programs(1) - 1, 2, 1)

    def block_step(t):
        own = slice(t * BLOCK, (t + 1) * BLOCK)
        before = slice((t - 1) * BLOCK, t * BLOCK)
        after = slice((t + 1) * BLOCK, (t + 2) * BLOCK)
        k_parts = [kp_ref[...] if t == 0 else kc_ref[before, :], kc_ref[own, :],
                   kn_ref[...] if t == n_sub - 1 else kc_ref[after, :], kx_ref[...]]
        v_parts = [vp_ref[...] if t == 0 else vc_ref[:, before], vc_ref[:, own],
                   vn_ref[...] if t == n_sub - 1 else vc_ref[:, after], vx_ref[...]]
        bias_prev = bias_ref[first_var if t == 0 else 1, 0:BLOCK, :]
        bias_next = bias_ref[last_var if t == n_sub - 1 else 1, BLOCK:2 * BLOCK, :]
        outs = _attend_block(sink_ref, q_ref[own, :], k_parts, v_parts, bias_prev, bias_next)
        for j in range(Q_W // LANES):
            o_t = jnp.concatenate([outs[2 * j], outs[2 * j + 1]], axis=0)
            o_ref[own, j * LANES:(j + 1) * LANES] = o_t.T.astype(BF16)

    return [functools.partial(block_step, t) for t in range(n_sub)]


def _attention_specs(batch, seq, ctx_len, bias_shape):
    nb = seq // BLOCK
    sub = ATTN_BLOCKS_PER_STEP
    steps = nb // sub

    def cur(b, i):
        return (b * steps + i, 0)

    def prev(b, i):
        return (b * nb + jnp.maximum(i * sub - 1, 0), 0)

    def nxt(b, i):
        return (b * nb + jnp.minimum((i + 1) * sub, nb - 1), 0)

    def swap(f):
        return lambda b, i: f(b, i)[::-1]

    in_specs = [
        pl.BlockSpec(memory_space=pltpu.SMEM),
        pl.BlockSpec((sub * BLOCK, Q_W), cur),
        pl.BlockSpec((BLOCK, KV_W), prev),
        pl.BlockSpec((sub * BLOCK, KV_W), cur),
        pl.BlockSpec((BLOCK, KV_W), nxt),
        pl.BlockSpec((ctx_len, KV_W), lambda b, i: (b, 0)),
        pl.BlockSpec((KV_W, BLOCK), swap(prev)),
        pl.BlockSpec((KV_W, sub * BLOCK), swap(cur)),
        pl.BlockSpec((KV_W, BLOCK), swap(nxt)),
        pl.BlockSpec((KV_W, ctx_len), lambda b, i: (0, b)),
        _const_spec(bias_shape),
    ]
    return steps, in_specs, pl.BlockSpec((sub * BLOCK, Q_W), cur)


def _band_bias():
    qi = np.arange(BLOCK)[None, :]
    kj = np.arange(3 * BLOCK)[:, None]
    rel = kj - BLOCK - qi
    in_window = np.abs(rel) <= WINDOW
    variants = []
    for has_prev, has_next in ((False, True), (True, True), (True, False)):
        ok = in_window.copy()
        if not has_prev:
            ok &= kj >= BLOCK
        if not has_next:
            ok &= kj < 2 * BLOCK
        mask = np.where(ok, 0.0, NEG).astype(np.float32)
        variants.append(np.concatenate([mask[:BLOCK], mask[2 * BLOCK:]], axis=0))
    return jnp.asarray(np.stack(variants)).astype(BF16)


def _minor_dft_terms():
    assert DFT_MINOR == 8
    plan = []
    for k2 in range(DFT_MINOR):
        unit, half = [], []
        for s2 in range(DFT_MINOR):
            ang = -2.0 * math.pi * ((s2 * k2) % DFT_MINOR) / DFT_MINOR
            for part, coef in ((0, math.cos(ang)), (1, -math.sin(ang))):
                if abs(coef) < 1e-9:
                    continue
                target = unit if abs(abs(coef) - 1.0) < 1e-9 else half
                assert target is unit or abs(abs(coef) - math.sqrt(0.5)) < 1e-9
                target.append((s2, part, 1 if coef > 0 else -1))
        plan.append((unit, half))
    return plan


def _signed_sum(terms, pick):
    pos = [pick(s2, part) for s2, part, sign in terms if sign > 0]
    neg = [pick(s2, part) for s2, part, sign in terms if sign < 0]
    total = None
    for v in pos:
        total = v if total is None else total + v
    for v in neg:
        total = -v if total is None else total - v
    return total


def _fourier_steps(f_ref, f1_ref, tr_ref, ti_ref, cm_ref, o_ref):
    n_major = f_ref.shape[0] // DFT_MINOR
    pair_w = 2 * LANES
    n_pairs = DFT_MINOR // 2
    row_chunk = n_major // n_pairs
    twiddled = [None] * n_pairs
    spectra = [None] * DFT_MINOR
    plan = _minor_dft_terms()

    def stage1(c):
        y = jnp.concatenate(
            [f_ref[pl.ds(s2, n_major, stride=DFT_MINOR), :].astype(BF16) for s2 in (2 * c, 2 * c + 1)],
            axis=1)
        a = jnp.dot(f1_ref[...], y, preferred_element_type=F32)
        ar = a[:n_major]
        ai = a[n_major:]
        tr = tr_ref[:, c * pair_w:(c + 1) * pair_w]
        ti = ti_ref[:, c * pair_w:(c + 1) * pair_w]
        twiddled[c] = ((ar * tr - ai * ti).astype(BF16), (ar * ti + ai * tr).astype(BF16))

    def channels(c):
        apr, api = twiddled[c]
        for t in range(2):
            lanes = slice(t * LANES, (t + 1) * LANES)
            ap = jnp.concatenate([apr[:, lanes], api[:, lanes]], axis=1)
            spectra[2 * c + t] = jnp.dot(ap, cm_ref[...], preferred_element_type=F32)

    def stage2(r):
        rows = slice(r * row_chunk, (r + 1) * row_chunk)

        def pick(s2, part):
            return spectra[s2][rows, part * LANES:(part + 1) * LANES]

        for k2, (unit, half) in enumerate(plan):
            res = _signed_sum(unit, pick)
            if half:
                scaled = _signed_sum(half, pick) * math.sqrt(0.5)
                res = scaled if res is None else res + scaled
            o_ref[pl.ds(k2 * n_major + r * row_chunk, row_chunk), :] = res.astype(BF16)

    s1 = lambda c: (functools.partial(stage1, c), 4)
    ch = lambda c: (functools.partial(channels, c), 3)
    steps = [s1(0)]
    for c in range(1, n_pairs):
        steps += [s1(c), ch(c - 1)]
    steps.append(ch(n_pairs - 1))
    return steps + [(functools.partial(stage2, r), 1) for r in range(n_pairs)]


def _cis(num, den):
    ang = (num % den).astype(F32) * F32(-2.0 * math.pi / den)
    return jnp.cos(ang), jnp.sin(ang)


def _dft_tables(seq):
    n_major = seq // DFT_MINOR
    ch = FOURIER_GROUP_CH
    root = int(round(math.sqrt(n_major)))
    assert root * root == n_major
    s1 = jnp.arange(n_major, dtype=jnp.int32)[None, :]
    hi = jnp.arange(root, dtype=jnp.int32)[:, None] * root
    lo = jnp.arange(root, dtype=jnp.int32)[:, None]
    hr, hi_ = _cis(hi * s1, n_major)
    lr, li = _cis(lo * s1, n_major)
    p1 = jnp.stack([hr, hr])[:, :, None, :]
    q1 = jnp.stack([lr, li])[:, None, :, :]
    p2 = jnp.stack([-hi_, hi_])[:, :, None, :]
    q2 = jnp.stack([li, lr])[:, None, :, :]
    f1 = (p1 * q1 + p2 * q2).astype(BF16).reshape(2 * n_major, n_major)
    k1 = jnp.arange(n_major, dtype=jnp.int32)[:, None]
    s2 = jnp.arange(DFT_MINOR, dtype=jnp.int32)[None, :]
    tr, ti = _cis(k1 * s2, seq)
    tr = jnp.broadcast_to(tr[:, :, None], (n_major, DFT_MINOR, ch)).reshape(n_major, DFT_MINOR * ch)
    ti = jnp.broadcast_to(ti[:, :, None], (n_major, DFT_MINOR, ch)).reshape(n_major, DFT_MINOR * ch)
    c128 = jnp.arange(ch, dtype=jnp.int32)
    cr, ci = _cis(c128[:, None] * c128[None, :], ch)
    norm = F32(1.0 / math.sqrt(seq * ch))
    cm = (jnp.concatenate([jnp.concatenate([cr, ci], axis=1),
                           jnp.concatenate([-ci, cr], axis=1)], axis=0) * norm).astype(BF16)
    return f1, tr, ti, cm


def _mix_kernel(sink_ref, q_ref, kp_ref, kc_ref, kn_ref, kx_ref, vp_ref, vc_ref, vn_ref, vx_ref, bias_ref,
                f_ref, f1_ref, tr_ref, ti_ref, cm_ref, attn_ref, four_ref):
    attn_steps = _attention_steps(sink_ref, q_ref, kp_ref, kc_ref, kn_ref, kx_ref,
                                  vp_ref, vc_ref, vn_ref, vx_ref, bias_ref, attn_ref)
    four_steps = _fourier_steps(f_ref, f1_ref, tr_ref, ti_ref, cm_ref, four_ref)
    total = sum(weight for _, weight in four_steps)
    done = 0
    seen = 0
    for four_step, weight in four_steps:
        seen += weight
        upto = seen * len(attn_steps) // total
        for attn_step in attn_steps[done:upto]:
            attn_step()
        done = upto
        four_step()
    assert done == len(attn_steps)


def _mix(sink, q, k, v_t, kx, vx_t, bias, f3, tables, ctx_len):
    batch, seq, _ = f3.shape
    f1, tr, ti, cm = tables
    ch = FOURIER_GROUP_CH
    steps, attn_in, attn_out = _attention_specs(batch, seq, ctx_len, bias.shape)
    assert steps == N_FOURIER_GROUPS
    four_spec = pl.BlockSpec((None, seq, ch), lambda b, g: (b, 0, g))
    return pl.pallas_call(
        _mix_kernel,
        grid=(batch, steps),
        in_specs=attn_in + [four_spec, _const_spec(f1.shape), _const_spec(tr.shape),
                            _const_spec(ti.shape), _const_spec(cm.shape)],
        out_specs=[attn_out, four_spec],
        out_shape=[jax.ShapeDtypeStruct((batch * seq, Q_W), BF16),
                   jax.ShapeDtypeStruct((batch, seq, FOURIER_W), BF16)],
        compiler_params=_params(2),
        name="mix",
    )(sink, q, k, k, k, kx, v_t, v_t, v_t, vx_t, bias, f3, f1, tr, ti, cm)


def _merge_kernel(x_ref, a_ref, f_ref, mod_ref, gpre_ref, gpost_ref,
                  wg_b, wpa_b, wpf_b, wout_b, wup_ref, wdown_ref,
                  o_ref, wup_o, wdown_o, *, tiles_per_seq):
    wup_o[...] = wup_ref[...].astype(BF16)
    wdown_o[...] = wdown_ref[...].astype(BF16)

    x = x_ref[...]
    shift, scale_mod, out_gate = _mod_rows(mod_ref, pl.program_id(0) // tiles_per_seq, 0, 3)
    h = _norm_modulate(x, gpre_ref[...], shift, scale_mod).astype(BF16)
    a = a_ref[...]
    f = f_ref[...]
    y = None
    for c in range(0, D_MODEL, MXU_DIM):
        cols = slice(c, c + MXU_DIM)
        gate_a = jnp.dot(h, wg_b[:, cols], preferred_element_type=F32)
        gate_f = jnp.dot(h, wg_b[:, D_MODEL + c:D_MODEL + c + MXU_DIM], preferred_element_type=F32)
        pa = jnp.dot(a, wpa_b[:, cols], preferred_element_type=F32)
        pf = jnp.dot(f, wpf_b[:, cols], preferred_element_type=F32)
        m = (jax.nn.sigmoid(gate_a) * pa + jax.nn.sigmoid(gate_f) * pf).astype(BF16)
        part = jnp.dot(m, wout_b[cols, :], preferred_element_type=F32)
        y = part if y is None else y + part
    o_ref[...] = x + out_gate * _post_norm(y, gpost_ref[...])


def _merge(x2, attn, four, mods, g_pre, g_post, w_g, w_pa, w_pf, w_out, w_up, w_down, seq):
    n = x2.shape[0]
    tm = TOKEN_TILE
    n_steps = n // tm
    tiles_per_seq = seq // tm
    row = lambda i: (i, 0)
    up_rows = w_up.shape[0] // n_steps
    down_rows = w_down.shape[0] // BF16_ROWS
    n_down = w_down.shape[0] // down_rows
    assert up_rows * n_steps == w_up.shape[0] and up_rows % BF16_ROWS == 0
    assert down_rows % BF16_ROWS == 0 and n_down <= n_steps
    up_spec = pl.BlockSpec((up_rows, w_up.shape[1]), row)
    down_spec = pl.BlockSpec((down_rows, w_down.shape[1]), lambda i: (jnp.minimum(i, n_down - 1), 0))
    return pl.pallas_call(
        functools.partial(_merge_kernel, tiles_per_seq=tiles_per_seq),
        grid=(n // tm,),
        in_specs=[
            pl.BlockSpec((tm, D_MODEL), row),
            pl.BlockSpec((tm, Q_W), row),
            pl.BlockSpec((tm, FOURIER_W), row),
            _const_spec(mods.shape),
            _const_spec((1, D_MODEL)),
            _const_spec((1, D_MODEL)),
            _const_spec(w_g.shape),
            _const_spec(w_pa.shape),
            _const_spec(w_pf.shape),
            _const_spec(w_out.shape),
            up_spec,
            down_spec,
        ],
        out_specs=[pl.BlockSpec((tm, D_MODEL), row), up_spec, down_spec],
        out_shape=[jax.ShapeDtypeStruct((n, D_MODEL), F32),
                   jax.ShapeDtypeStruct(w_up.shape, BF16),
                   jax.ShapeDtypeStruct(w_down.shape, BF16)],
        compiler_params=_params(1),
        name="merge",
    )(x2, attn, four, mods, g_pre, g_post, w_g, w_pa, w_pf, w_out, w_up, w_down)


def _convffn_kernel(x_ref, xp_ref, xn_ref, mod_ref, gpre_ref, gpost_ref,
                    wu_ref, wgate_ref, cw_ref, cb_ref, wd_ref, o_ref, act_ref, *, tiles_per_seq):
    i = pl.program_id(0)
    tm = x_ref.shape[0]
    halo = SUBLANES
    has_prev = (i % tiles_per_seq != 0).astype(F32)
    has_next = (i % tiles_per_seq != tiles_per_seq - 1).astype(F32)
    x = x_ref[...]
    shift, scale, out_gate = _mod_rows(mod_ref, i // tiles_per_seq, 3, 3)
    gain = gpre_ref[...]
    h = _norm_modulate(x, gain, shift, scale)
    hp = _norm_modulate(xp_ref[...], gain, shift, scale) * has_prev
    hn = _norm_modulate(xn_ref[...], gain, shift, scale) * has_next
    h_ext = jnp.concatenate([hp, h, hn], axis=0).astype(BF16)
    h_mid = h.astype(BF16)
    n_ext = tm + 2 * halo
    for c in range(D_FF // FFN_CHUNK):
        cols = slice(c * FFN_CHUNK, (c + 1) * FFN_CHUNK)
        u = jnp.dot(h_ext, wu_ref[:, cols], preferred_element_type=F32)
        gate = jnp.dot(h_mid, wgate_ref[:, cols], preferred_element_type=F32)
        u_prev = pltpu.roll(u, 1, 0)[halo:halo + tm]
        u_next = pltpu.roll(u, n_ext - 1, 0)[halo:halo + tm]
        conv = (u_prev * cw_ref[0:1, cols] + u[halo:halo + tm] * cw_ref[1:2, cols]
                + u_next * cw_ref[2:3, cols] + cb_ref[:, cols])
        act_ref[:, cols] = (conv * jax.nn.sigmoid(conv) * gate).astype(BF16)
    y = jnp.dot(act_ref[...], wd_ref[...], preferred_element_type=F32)
    o_ref[...] = x + out_gate * _post_norm(y, gpost_ref[...])


def _convffn(x1, mods, g_pre, g_post, w_up, conv_w, conv_b, w_down, seq):
    n = x1.shape[0]
    tm = TOKEN_TILE
    tiles_per_seq = seq // tm
    halo_blocks_per_tile = tm // SUBLANES
    n_halo_blocks = n // SUBLANES
    return pl.pallas_call(
        functools.partial(_convffn_kernel, tiles_per_seq=tiles_per_seq),
        grid=(n // tm,),
        in_specs=[
            pl.BlockSpec((tm, D_MODEL), lambda i: (i, 0)),
            pl.BlockSpec((SUBLANES, D_MODEL),
                         lambda i: (jnp.maximum(i * halo_blocks_per_tile - 1, 0), 0)),
            pl.BlockSpec((SUBLANES, D_MODEL),
                         lambda i: (jnp.minimum((i + 1) * halo_blocks_per_tile, n_halo_blocks - 1), 0)),
            _const_spec(mods.shape),
            _const_spec((1, D_MODEL)),
            _const_spec((1, D_MODEL)),
            pl.BlockSpec((D_MODEL, D_FF), lambda i: (0, 0), pipeline_mode=pl.Buffered(1)),
            pl.BlockSpec((D_MODEL, D_FF), lambda i: (0, 1), pipeline_mode=pl.Buffered(1)),
            _const_spec(conv_w.shape),
            _const_spec((1, D_FF)),
            _const_spec(w_down.shape),
        ],
        out_specs=pl.BlockSpec((tm, D_MODEL), lambda i: (i, 0)),
        out_shape=jax.ShapeDtypeStruct((n, D_MODEL), F32),
        scratch_shapes=[pltpu.VMEM((tm, D_FF), BF16)],
        compiler_params=_params(1),
        name="convffn",
    )(x1, x1, x1, mods, g_pre, g_post, w_up, w_up, conv_w, conv_b.reshape(1, D_FF), w_down)


def _rope_tables(seq):
    half = HEAD_DIM // 2
    inv_freq = ROPE_THETA ** (-jnp.arange(0, half, 2, dtype=F32) / half)
    ang_r = jnp.arange(seq // GRID_W).astype(F32)[:, None] * inv_freq
    ang_c = jnp.arange(GRID_W).astype(F32)[:, None] * inv_freq
    reps = LANES // HEAD_DIM

    def lanes(row_part, col_part):
        return jnp.tile(jnp.concatenate([row_part, col_part], axis=-1), (1, reps))

    zr = jnp.zeros((ang_r.shape[0], half), F32)
    zc = jnp.zeros((ang_c.shape[0], half), F32)
    rcos = lanes(jnp.concatenate([jnp.cos(ang_r), jnp.cos(ang_r)], axis=-1), zr)
    rsin = lanes(jnp.concatenate([-jnp.sin(ang_r), jnp.sin(ang_r)], axis=-1), zr)
    ccos = lanes(zc, jnp.concatenate([jnp.cos(ang_c), jnp.cos(ang_c)], axis=-1))
    csin = lanes(zc, jnp.concatenate([-jnp.sin(ang_c), jnp.sin(ang_c)], axis=-1))
    return rcos, rsin, ccos, csin


def kernel(x, c, ctx, c_ctx, w_mod, b_mod, g_pre1, g_post1, g_pre2, g_post2,
           w_in, sink, w_pa, w_pf, w_out, w_up, conv_w, conv_b, w_down):
    batch, seq, d = x.shape
    ctx_len = ctx.shape[1]
    depth = w_mod.shape[0]
    assert depth == 1 and d == D_MODEL and batch + 1 <= SUBLANES
    assert seq % TOKEN_TILE == 0 and TOKEN_TILE % GRID_W == 0
    assert seq % (DFT_MINOR * SUBLANES) == 0
    assert seq % (ATTN_BLOCKS_PER_STEP * BLOCK) == 0
    n = batch * seq
    rope_tabs = _rope_tables(seq)
    tables = _dft_tables(seq)
    bias = _band_bias()

    l = 0
    mods = _adaln(c, c_ctx.reshape(1, d), w_mod[l], b_mod[l])

    g_pre1_row = g_pre1[l].reshape(1, d)
    g_post1_row = g_post1[l].reshape(1, d)
    g_pre2_row = g_pre2[l].reshape(1, d)
    g_post2_row = g_post2[l].reshape(1, d)

    x2 = x.reshape(n, d)
    q, k, v_t, f, w_g_b, w_pa_b, w_pf_b, w_out_b = _inproj(x2, mods, g_pre1_row, w_in[l], w_pa[l], w_pf[l], w_out[l],
                                                          rope_tabs, seq)
    kx, vx_t = _ctxkv(ctx.reshape(batch * ctx_len, d), mods, batch, g_pre1_row, w_in[l], ctx_len)
    attn, four = _mix(sink[l], q, k, v_t, kx, vx_t, bias, f.reshape(batch, seq, FOURIER_W), tables, ctx_len)
    four = four.reshape(n, FOURIER_W)
    x1, w_up_b, w_down_b = _merge(x2, attn, four, mods, g_pre1_row, g_post1_row, w_g_b, w_pa_b, w_pf_b, w_out_b,
                                  w_up[l], w_down[l], seq)
    out = _convffn(x1, mods, g_pre2_row, g_post2_row, w_up_b, conv_w[l], conv_b[l], w_down_b, seq)
    return out.reshape(batch, seq, d)
```

```python
import functools
import math

import numpy as np
import jax
import jax.numpy as jnp
from jax import lax
from jax.experimental import pallas as pl
from jax.experimental.pallas import tpu as pltpu

F32 = jnp.float32
BF16 = jnp.bfloat16

D_MODEL = 1024
GRID_W = 64
HEAD_DIM = 64
N_Q_HEADS = 8
N_KV_HEADS = 2
GROUP = N_Q_HEADS // N_KV_HEADS
WINDOW = 128
BLOCK = 128
ROPE_THETA = 10000.0
N_FOURIER_GROUPS = 4
FOURIER_GROUP_CH = 128
FOURIER_W = N_FOURIER_GROUPS * FOURIER_GROUP_CH
Q_W = N_Q_HEADS * HEAD_DIM
KV_W = N_KV_HEADS * HEAD_DIM
D_FF = 2816
EPS = 1e-6
NEG = -1e30
LOG2E = math.log2(math.e)

LANES = 128
SUBLANES = 8
BF16_ROWS = 16
MXU_DIM = 256
MIB = 1024 * 1024
VMEM_LIMIT_BYTES = 56 * MIB

TOKEN_TILE = 1024
FFN_CHUNK = MXU_DIM
DFT_MINOR = SUBLANES
ATTN_BLOCKS_PER_STEP = 16


def _const_spec(shape):
    nd = len(shape)
    return pl.BlockSpec(shape, lambda *_: (0,) * nd, pipeline_mode=pl.Buffered(1))


def _params(n_axes, vmem_bytes=VMEM_LIMIT_BYTES):
    return pltpu.CompilerParams(
        dimension_semantics=("arbitrary",) * n_axes,
        vmem_limit_bytes=min(int(vmem_bytes), VMEM_LIMIT_BYTES),
    )


def _norm_modulate(x, gain, shift, scale):
    ms = jnp.mean(x * x, axis=-1, keepdims=True)
    return (x * lax.rsqrt(ms + EPS)) * (gain * (1.0 + scale)) + shift


def _mod_rows(mod_ref, row, first, count):
    return [mod_ref[pl.ds(row, 1), (first + j) * D_MODEL:(first + j + 1) * D_MODEL] for j in range(count)]


def _post_norm(y, gain):
    ms = jnp.mean(y * y, axis=-1, keepdims=True)
    return (y * lax.rsqrt(ms + EPS)) * gain


def _adaln_kernel(c_ref, cctx_ref, w_ref, b_ref, o_ref, rows_ref):
    batch = c_ref.shape[0]
    rows_ref[...] = jnp.zeros(rows_ref.shape, F32)
    rows_ref[0:batch, :] = c_ref[...]
    rows_ref[batch:batch + 1, :] = cctx_ref[...]
    c = rows_ref[...]
    s = c * jax.nn.sigmoid(c)
    o_ref[...] = jnp.dot(s, w_ref[...], preferred_element_type=F32) + b_ref[...]


def _adaln(c, c_ctx, w_mod, b_mod):
    n_out = w_mod.shape[1]
    tn = 1536
    return pl.pallas_call(
        _adaln_kernel,
        grid=(n_out // tn,),
        in_specs=[
            pl.BlockSpec(c.shape, lambda j: (0, 0)),
            pl.BlockSpec(c_ctx.shape, lambda j: (0, 0)),
            pl.BlockSpec((D_MODEL, tn), lambda j: (0, j)),
            pl.BlockSpec((1, tn), lambda j: (0, j)),
        ],
        out_specs=pl.BlockSpec((SUBLANES, tn), lambda j: (0, j)),
        out_shape=jax.ShapeDtypeStruct((SUBLANES, n_out), F32),
        scratch_shapes=[pltpu.VMEM((SUBLANES, D_MODEL), F32)],
        compiler_params=_params(1),
        name="adaln",
    )(c, c_ctx, w_mod, b_mod.reshape(1, n_out))


def _rope(xb, cos_t, sin_t, first_half):
    sw = jnp.where(first_half, pltpu.roll(xb, LANES - 16, 1), pltpu.roll(xb, 16, 1))
    return xb * cos_t + sw * sin_t


def _token_table(row_tab, col_tab):
    n_rows = row_tab.shape[0]
    by_row = jnp.concatenate(
        [jnp.broadcast_to(row_tab[r:r + 1, :], (GRID_W, LANES)) for r in range(n_rows)], axis=0)
    return by_row + jnp.concatenate([col_tab] * n_rows, axis=0)


def _inproj_kernel(x_ref, mod_ref, g_ref, w_ref, rcos_ref, rsin_ref, ccos_ref, csin_ref,
                   wg_ref, wpa_ref, wpf_ref, wout_ref,
                   q_ref, k_ref, v_ref, f_ref, wg_o, wpa_o, wpf_o, wout_o, wb_ref, *, tiles_per_seq):
    wg_o[...] = wg_ref[...].astype(BF16)
    wpa_o[...] = wpa_ref[...].astype(BF16)
    wpf_o[...] = wpf_ref[...].astype(BF16)
    wout_o[...] = wout_ref[...].astype(BF16)

    @pl.when(pl.program_id(0) == 0)
    def _():
        low = lax.broadcasted_iota(jnp.int32, (D_MODEL, LANES), 1) < HEAD_DIM
        for j in range(Q_W // LANES):
            src_a = (j // 2) * LANES
            src_b = (GROUP // 2 + j // 2) * LANES
            a = w_ref[:, src_a:src_a + LANES]
            b = w_ref[:, src_b:src_b + LANES]
            if j % 2 == 1:
                a = pltpu.roll(a, HEAD_DIM, 1)
            else:
                b = pltpu.roll(b, HEAD_DIM, 1)
            wb_ref[:, j * LANES:(j + 1) * LANES] = jnp.where(low, a, b).astype(BF16)
        wb_ref[:, Q_W:] = w_ref[:, Q_W:].astype(BF16)

    shift, scale_mod = _mod_rows(mod_ref, pl.program_id(0) // tiles_per_seq, 0, 2)
    h = _norm_modulate(x_ref[...], g_ref[...], shift, scale_mod).astype(BF16)
    cos_t = _token_table(rcos_ref[...], ccos_ref[...])
    sin_t = _token_table(rsin_ref[...], csin_ref[...])
    lane = lax.broadcasted_iota(jnp.int32, cos_t.shape, 1)
    first_half = (lane % 32) < 16
    scale = HEAD_DIM ** -0.5 * LOG2E

    def project(col0, width):
        return jnp.dot(h, wb_ref[:, col0:col0 + width], preferred_element_type=F32)

    for c in range(0, Q_W, MXU_DIM):
        p = project(c, MXU_DIM)
        for b in range(MXU_DIM // LANES):
            blk = p[:, b * LANES:(b + 1) * LANES]
            q_ref[:, c + b * LANES:c + (b + 1) * LANES] = (
                _rope(blk, cos_t, sin_t, first_half) * scale).astype(BF16)
    p = project(Q_W, 2 * KV_W)
    k_ref[...] = _rope(p[:, :KV_W], cos_t, sin_t, first_half).astype(BF16)
    v_ref[...] = p[:, KV_W:].T.astype(BF16)
    for c in range(0, FOURIER_W, MXU_DIM):
        f_ref[:, c:c + MXU_DIM] = project(Q_W + 2 * KV_W + c, MXU_DIM)


def _row_slab_rows(n_rows, n_steps):
    rows = n_rows // n_steps
    assert rows * n_steps == n_rows and rows % BF16_ROWS == 0
    return rows


def _inproj(x2, mods, g_pre, w_in, w_pa, w_pf, w_out, rope_tabs, seq):
    n = x2.shape[0]
    tm = TOKEN_TILE
    n_steps = n // tm
    tiles_per_seq = seq // tm
    rows_per_tile = tm // GRID_W
    n_cols = Q_W + 2 * KV_W + FOURIER_W
    row_spec = pl.BlockSpec((rows_per_tile, LANES), lambda i: (i % tiles_per_seq, 0))
    g_rows = _row_slab_rows(D_MODEL, n_steps)
    pa_rows = _row_slab_rows(w_pa.shape[0], n_steps)
    pf_rows = _row_slab_rows(w_pf.shape[0], n_steps)
    out_rows = _row_slab_rows(w_out.shape[0], n_steps)
    slab = lambda rows, cols: pl.BlockSpec((rows, cols), lambda i: (i, 0))
    return pl.pallas_call(
        functools.partial(_inproj_kernel, tiles_per_seq=tiles_per_seq),
        grid=(n // tm,),
        in_specs=[
            pl.BlockSpec((tm, D_MODEL), lambda i: (i, 0)),
            _const_spec(mods.shape),
            _const_spec((1, D_MODEL)),
            _const_spec((D_MODEL, n_cols)),
            row_spec, row_spec,
            _const_spec((GRID_W, LANES)), _const_spec((GRID_W, LANES)),
            pl.BlockSpec((pl.Element(g_rows), pl.Element(2 * D_MODEL)),
                         lambda i: (i * g_rows, n_cols)),
            slab(pa_rows, D_MODEL), slab(pf_rows, D_MODEL), slab(out_rows, D_MODEL),
        ],
        out_specs=[
            pl.BlockSpec((tm, Q_W), lambda i: (i, 0)),
            pl.BlockSpec((tm, KV_W), lambda i: (i, 0)),
            pl.BlockSpec((KV_W, tm), lambda i: (0, i)),
            pl.BlockSpec((tm, FOURIER_W), lambda i: (i, 0)),
            slab(g_rows, 2 * D_MODEL), slab(pa_rows, D_MODEL), slab(pf_rows, D_MODEL), slab(out_rows, D_MODEL),
        ],
        out_shape=[
            jax.ShapeDtypeStruct((n, Q_W), BF16),
            jax.ShapeDtypeStruct((n, KV_W), BF16),
            jax.ShapeDtypeStruct((KV_W, n), BF16),
            jax.ShapeDtypeStruct((n, FOURIER_W), F32),
            jax.ShapeDtypeStruct((D_MODEL, 2 * D_MODEL), BF16),
            jax.ShapeDtypeStruct(w_pa.shape, BF16),
            jax.ShapeDtypeStruct(w_pf.shape, BF16),
            jax.ShapeDtypeStruct(w_out.shape, BF16),
        ],
        scratch_shapes=[pltpu.VMEM((D_MODEL, n_cols), BF16)],
        compiler_params=_params(1, 2 * tm * (D_MODEL * 4 + Q_W * 2 + 2 * KV_W * 2 + FOURIER_W * 4)
                                + D_MODEL * n_cols * 6 + 2 * tm * D_MODEL * 4 + 6 * MIB),
        name="inproj",
    )(x2, mods, g_pre, w_in, *rope_tabs, w_in, w_pa, w_pf, w_out)


def _ctxkv_kernel(x_ref, mod_ref, g_ref, w_ref, k_ref, v_ref, *, row):
    shift, scale_mod = _mod_rows(mod_ref, row, 0, 2)
    h = _norm_modulate(x_ref[...], g_ref[...], shift, scale_mod)
    p = jnp.dot(h.astype(BF16), w_ref[...].astype(BF16), preferred_element_type=F32)
    k_ref[...] = p[:, :KV_W].astype(BF16)
    v_ref[...] = p[:, KV_W:].T.astype(BF16)


def _ctxkv(ctx2, mods, ctx_row, g_pre, w_in, ctx_len):
    n = ctx2.shape[0]
    assert Q_W % (2 * KV_W) == 0
    return pl.pallas_call(
        functools.partial(_ctxkv_kernel, row=ctx_row),
        grid=(n // ctx_len,),
        in_specs=[
            pl.BlockSpec((ctx_len, D_MODEL), lambda i: (i, 0)),
            _const_spec(mods.shape),
            _const_spec((1, D_MODEL)),
            pl.BlockSpec((D_MODEL, 2 * KV_W), lambda i: (0, Q_W // (2 * KV_W)),
                         pipeline_mode=pl.Buffered(1)),
        ],
        out_specs=[
            pl.BlockSpec((ctx_len, KV_W), lambda i: (i, 0)),
            pl.BlockSpec((KV_W, ctx_len), lambda i: (0, i)),
        ],
        out_shape=[
            jax.ShapeDtypeStruct((n, KV_W), BF16),
            jax.ShapeDtypeStruct((KV_W, n), BF16),
        ],
        compiler_params=_params(1, 2 * ctx_len * D_MODEL * 4 + D_MODEL * 2 * KV_W * 4 + 8 * MIB),
        name="ctxkv",
    )(ctx2, mods, g_pre, w_in)


def _attend_block(sink_ref, q_blk, k_parts, v_parts, bias_prev, bias_next):
    kcat = jnp.concatenate(k_parts, axis=0)
    vcat_t = jnp.concatenate(v_parts, axis=1)
    n_keys = kcat.shape[0]
    low = lax.broadcasted_iota(jnp.int32, (BLOCK, LANES), 1) < HEAD_DIM
    top = lax.broadcasted_iota(jnp.int32, (KV_W, n_keys), 0) < HEAD_DIM
    zero = jnp.zeros((), BF16)
    one = jnp.ones((), BF16)
    q_rows = [jnp.where(low if h == 0 else jnp.logical_not(low), q_blk[:, g * LANES:(g + 1) * LANES], zero)
              for h in range(N_KV_HEADS) for g in range(GROUP)]
    s_all = lax.dot_general(kcat, jnp.concatenate(q_rows, axis=0), (((1,), (1,)), ((), ())),
                            preferred_element_type=F32).astype(BF16)
    outs = []
    for h in range(N_KV_HEADS):
        v_h = jnp.where(top if h == 0 else jnp.logical_not(top), vcat_t, one)
        for g0 in range(0, GROUP, 2):
            probs = []
            sinks = []
            for g in (g0, g0 + 1):
                head = h * GROUP + g
                s = s_all[:, head * LANES:(head + 1) * LANES]
                parts = [s[0:BLOCK] + bias_prev,
                         s[BLOCK:2 * BLOCK],
                         s[2 * BLOCK:3 * BLOCK] + bias_next]
                parts += [s[r:r + BLOCK] for r in range(3 * BLOCK, n_keys, BLOCK)]
                mx = parts[0]
                for part in parts[1:]:
                    mx = jnp.maximum(mx, part)
                sink = sink_ref[head] * LOG2E
                m = jnp.maximum(jnp.max(mx, axis=0, keepdims=True), sink.astype(BF16))
                probs.append(jnp.concatenate([jnp.exp2(part - m) for part in parts], axis=0))
                sinks.append(jnp.exp2(sink - m.astype(F32)))
            o2 = jnp.dot(v_h, jnp.concatenate(probs, axis=1), preferred_element_type=F32)
            for idx in range(2):
                o_t = o2[:, idx * LANES:(idx + 1) * LANES]
                num = o_t[0:HEAD_DIM] if h == 0 else o_t[HEAD_DIM:]
                den = (o_t[HEAD_DIM:] if h == 0 else o_t[0:HEAD_DIM]) + sinks[idx]
                outs.append(num / den)
    return outs


def _attention_steps(sink_ref, q_ref, kp_ref, kc_ref, kn_ref, kx_ref, vp_ref, vc_ref, vn_ref, vx_ref,
                     bias_ref, o_ref):
    i = pl.program_id(1)
    n_sub = q_ref.shape[0] // BLOCK
    first_var = jnp.where(i == 0, 0, 1)
    last_var = jnp.where(i == pl.num_programs(1) - 1, 2, 1)

    def block_step(t):
        own = slice(t * BLOCK, (t + 1) * BLOCK)
        before = slice((t - 1) * BLOCK, t * BLOCK)
        after = slice((t + 1) * BLOCK, (t + 2) * BLOCK)
        k_parts = [kp_ref[...] if t == 0 else kc_ref[before, :], kc_ref[own, :],
                   kn_ref[...] if t == n_sub - 1 else kc_ref[after, :], kx_ref[...]]
        v_parts = [vp_ref[...] if t == 0 else vc_ref[:, before], vc_ref[:, own],
                   vn_ref[...] if t == n_sub - 1 else vc_ref[:, after], vx_ref[...]]
        bias_prev = bias_ref[first_var if t == 0 else 1, 0:BLOCK, :]
        bias_next = bias_ref[last_var if t == n_sub - 1 else 1, BLOCK:2 * BLOCK, :]
        outs = _attend_block(sink_ref, q_ref[own, :], k_parts, v_parts, bias_prev, bias_next)
        for j in range(Q_W // LANES):
            o_t = jnp.concatenate([outs[2 * j], outs[2 * j + 1]], axis=0)
            o_ref[own, j * LANES:(j + 1) * LANES] = o_t.T.astype(BF16)

    return [functools.partial(block_step, t) for t in range(n_sub)]


def _attention_specs(batch, seq, ctx_len, bias_shape):
    nb = seq // BLOCK
    sub = ATTN_BLOCKS_PER_STEP
    steps = nb // sub

    def cur(b, i):
        return (b * steps + i, 0)

    def prev(b, i):
        return (b * nb + jnp.maximum(i * sub - 1, 0), 0)

    def nxt(b, i):
        return (b * nb + jnp.minimum((i + 1) * sub, nb - 1), 0)

    def swap(f):
        return lambda b, i: f(b, i)[::-1]

    in_specs = [
        pl.BlockSpec(memory_space=pltpu.SMEM),
        pl.BlockSpec((sub * BLOCK, Q_W), cur),
        pl.BlockSpec((BLOCK, KV_W), prev),
        pl.BlockSpec((sub * BLOCK, KV_W), cur),
        pl.BlockSpec((BLOCK, KV_W), nxt),
        pl.BlockSpec((ctx_len, KV_W), lambda b, i: (b, 0)),
        pl.BlockSpec((KV_W, BLOCK), swap(prev)),
        pl.BlockSpec((KV_W, sub * BLOCK), swap(cur)),
        pl.BlockSpec((KV_W, BLOCK), swap(nxt)),
        pl.BlockSpec((KV_W, ctx_len), lambda b, i: (0, b)),
        _const_spec(bias_shape),
    ]
    return steps, in_specs, pl.BlockSpec((sub * BLOCK, Q_W), cur)


def _band_bias():
    qi = np.arange(BLOCK)[None, :]
    kj = np.arange(3 * BLOCK)[:, None]
    rel = kj - BLOCK - qi
    in_window = np.abs(rel) <= WINDOW
    variants = []
    for has_prev, has_next in ((False, True), (True, True), (True, False)):
        ok = in_window.copy()
        if not has_prev:
            ok &= kj >= BLOCK
        if not has_next:
            ok &= kj < 2 * BLOCK
        mask = np.where(ok, 0.0, NEG).astype(np.float32)
        variants.append(np.concatenate([mask[:BLOCK], mask[2 * BLOCK:]], axis=0))
    return jnp.asarray(np.stack(variants)).astype(BF16)


def _minor_dft_terms():
    assert DFT_MINOR == 8
    plan = []
    for k2 in range(DFT_MINOR):
        unit, half = [], []
        for s2 in range(DFT_MINOR):
            ang = -2.0 * math.pi * ((s2 * k2) % DFT_MINOR) / DFT_MINOR
            for part, coef in ((0, math.cos(ang)), (1, -math.sin(ang))):
                if abs(coef) < 1e-9:
                    continue
                target = unit if abs(abs(coef) - 1.0) < 1e-9 else half
                assert target is unit or abs(abs(coef) - math.sqrt(0.5)) < 1e-9
                target.append((s2, part, 1 if coef > 0 else -1))
        plan.append((unit, half))
    return plan


def _signed_sum(terms, pick):
    pos = [pick(s2, part) for s2, part, sign in terms if sign > 0]
    neg = [pick(s2, part) for s2, part, sign in terms if sign < 0]
    total = None
    for v in pos:
        total = v if total is None else total + v
    for v in neg:
        total = -v if total is None else total - v
    return total


def _fourier_steps(f_ref, f1_ref, tr_ref, ti_ref, cm_ref, o_ref):
    n_major = f_ref.shape[0] // DFT_MINOR
    pair_w = 2 * LANES
    n_pairs = DFT_MINOR // 2
    row_chunk = n_major // n_pairs
    twiddled = [None] * n_pairs
    spectra = [None] * DFT_MINOR
    plan = _minor_dft_terms()

    def stage1(c):
        y = jnp.concatenate(
            [f_ref[pl.ds(s2, n_major, stride=DFT_MINOR), :].astype(BF16) for s2 in (2 * c, 2 * c + 1)],
            axis=1)
        a = jnp.dot(f1_ref[...], y, preferred_element_type=F32)
        ar = a[:n_major]
        ai = a[n_major:]
        tr = tr_ref[:, c * pair_w:(c + 1) * pair_w]
        ti = ti_ref[:, c * pair_w:(c + 1) * pair_w]
        twiddled[c] = ((ar * tr - ai * ti).astype(BF16), (ar * ti + ai * tr).astype(BF16))

    def channels(c):
        apr, api = twiddled[c]
        for t in range(2):
            lanes = slice(t * LANES, (t + 1) * LANES)
            ap = jnp.concatenate([apr[:, lanes], api[:, lanes]], axis=1)
            spectra[2 * c + t] = jnp.dot(ap, cm_ref[...], preferred_element_type=F32)

    def stage2(r):
        rows = slice(r * row_chunk, (r + 1) * row_chunk)

        def pick(s2, part):
            return spectra[s2][rows, part * LANES:(part + 1) * LANES]

        for k2, (unit, half) in enumerate(plan):
            res = _signed_sum(unit, pick)
            if half:
                scaled = _signed_sum(half, pick) * math.sqrt(0.5)
                res = scaled if res is None else res + scaled
            o_ref[pl.ds(k2 * n_major + r * row_chunk, row_chunk), :] = res.astype(BF16)

    s1 = lambda c: (functools.partial(stage1, c), 4)
    ch = lambda c: (functools.partial(channels, c), 3)
    steps = [s1(0)]
    for c in range(1, n_pairs):
        steps += [s1(c), ch(c - 1)]
    steps.append(ch(n_pairs - 1))
    return steps + [(functools.partial(stage2, r), 1) for r in range(n_pairs)]


def _cis(num, den):
    ang = (num % den).astype(F32) * F32(-2.0 * math.pi / den)
    return jnp.cos(ang), jnp.sin(ang)


def _dft_tables(seq):
    n_major = seq // DFT_MINOR
    ch = FOURIER_GROUP_CH
    root = int(round(math.sqrt(n_major)))
    assert root * root == n_major
    s1 = jnp.arange(n_major, dtype=jnp.int32)[None, :]
    hi = jnp.arange(root, dtype=jnp.int32)[:, None] * root
    lo = jnp.arange(root, dtype=jnp.int32)[:, None]
    hr, hi_ = _cis(hi * s1, n_major)
    lr, li = _cis(lo * s1, n_major)
    p1 = jnp.stack([hr, hr])[:, :, None, :]
    q1 = jnp.stack([lr, li])[:, None, :, :]
    p2 = jnp.stack([-hi_, hi_])[:, :, None, :]
    q2 = jnp.stack([li, lr])[:, None, :, :]
    f1 = (p1 * q1 + p2 * q2).astype(BF16).reshape(2 * n_major, n_major)
    k1 = jnp.arange(n_major, dtype=jnp.int32)[:, None]
    s2 = jnp.arange(DFT_MINOR, dtype=jnp.int32)[None, :]
    tr, ti = _cis(k1 * s2, seq)
    tr = jnp.broadcast_to(tr[:, :, None], (n_major, DFT_MINOR, ch)).reshape(n_major, DFT_MINOR * ch)
    ti = jnp.broadcast_to(ti[:, :, None], (n_major, DFT_MINOR, ch)).reshape(n_major, DFT_MINOR * ch)
    c128 = jnp.arange(ch, dtype=jnp.int32)
    cr, ci = _cis(c128[:, None] * c128[None, :], ch)
    norm = F32(1.0 / math.sqrt(seq * ch))
    cm = (jnp.concatenate([jnp.concatenate([cr, ci], axis=1),
                           jnp.concatenate([-ci, cr], axis=1)], axis=0) * norm).astype(BF16)
    return f1, tr, ti, cm


def _mix_kernel(sink_ref, q_ref, kp_ref, kc_ref, kn_ref, kx_ref, vp_ref, vc_ref, vn_ref, vx_ref, bias_ref,
                f_ref, f1_ref, tr_ref, ti_ref, cm_ref, attn_ref, four_ref):
    attn_steps = _attention_steps(sink_ref, q_ref, kp_ref, kc_ref, kn_ref, kx_ref,
                                  vp_ref, vc_ref, vn_ref, vx_ref, bias_ref, attn_ref)
    four_steps = _fourier_steps(f_ref, f1_ref, tr_ref, ti_ref, cm_ref, four_ref)
    total = sum(weight for _, weight in four_steps)
    done = 0
    seen = 0
    for four_step, weight in four_steps:
        seen += weight
        upto = seen * len(attn_steps) // total
        for attn_step in attn_steps[done:upto]:
            attn_step()
        done = upto
        four_step()
    assert done == len(attn_steps)


def _mix(sink, q, k, v_t, kx, vx_t, bias, f3, tables, ctx_len):
    batch, seq, _ = f3.shape
    f1, tr, ti, cm = tables
    ch = FOURIER_GROUP_CH
    steps, attn_in, attn_out = _attention_specs(batch, seq, ctx_len, bias.shape)
    assert steps == N_FOURIER_GROUPS
    four_spec = pl.BlockSpec((None, seq, ch), lambda b, g: (b, 0, g))
    return pl.pallas_call(
        _mix_kernel,
        grid=(batch, steps),
        in_specs=attn_in + [four_spec, _const_spec(f1.shape), _const_spec(tr.shape),
                            _const_spec(ti.shape), _const_spec(cm.shape)],
        out_specs=[attn_out, four_spec],
        out_shape=[jax.ShapeDtypeStruct((batch * seq, Q_W), BF16),
                   jax.ShapeDtypeStruct((batch, seq, FOURIER_W), BF16)],
        compiler_params=_params(2),
        name="mix",
    )(sink, q, k, k, k, kx, v_t, v_t, v_t, vx_t, bias, f3, f1, tr, ti, cm)


def _merge_kernel(x_ref, a_ref, f_ref, mod_ref, gpre_ref, gpost_ref,
                  wg_b, wpa_b, wpf_b, wout_b, wup_ref, wdown_ref,
                  o_ref, wup_o, wdown_o, *, tiles_per_seq):
    wup_o[...] = wup_ref[...].astype(BF16)
    wdown_o[...] = wdown_ref[...].astype(BF16)

    x = x_ref[...]
    shift, scale_mod, out_gate = _mod_rows(mod_ref, pl.program_id(0) // tiles_per_seq, 0, 3)
    h = _norm_modulate(x, gpre_ref[...], shift, scale_mod).astype(BF16)
    a = a_ref[...]
    f = f_ref[...]
    y = None
    for c in range(0, D_MODEL, MXU_DIM):
        cols = slice(c, c + MXU_DIM)
        gate_a = jnp.dot(h, wg_b[:, cols], preferred_element_type=F32)
        gate_f = jnp.dot(h, wg_b[:, D_MODEL + c:D_MODEL + c + MXU_DIM], preferred_element_type=F32)
        pa = jnp.dot(a, wpa_b[:, cols], preferred_element_type=F32)
        pf = jnp.dot(f, wpf_b[:, cols], preferred_element_type=F32)
        m = (jax.nn.sigmoid(gate_a) * pa + jax.nn.sigmoid(gate_f) * pf).astype(BF16)
        part = jnp.dot(m, wout_b[cols, :], preferred_element_type=F32)
        y = part if y is None else y + part
    o_ref[...] = x + out_gate * _post_norm(y, gpost_ref[...])


def _merge(x2, attn, four, mods, g_pre, g_post, w_g, w_pa, w_pf, w_out, w_up, w_down, seq):
    n = x2.shape[0]
    tm = TOKEN_TILE
    n_steps = n // tm
    tiles_per_seq = seq // tm
    row = lambda i: (i, 0)
    up_rows = w_up.shape[0] // n_steps
    down_rows = w_down.shape[0] // BF16_ROWS
    n_down = w_down.shape[0] // down_rows
    assert up_rows * n_steps == w_up.shape[0] and up_rows % BF16_ROWS == 0
    assert down_rows % BF16_ROWS == 0 and n_down <= n_steps
    up_spec = pl.BlockSpec((up_rows, w_up.shape[1]), row)
    down_spec = pl.BlockSpec((down_rows, w_down.shape[1]), lambda i: (jnp.minimum(i, n_down - 1), 0))
    return pl.pallas_call(
        functools.partial(_merge_kernel, tiles_per_seq=tiles_per_seq),
        grid=(n // tm,),
        in_specs=[
            pl.BlockSpec((tm, D_MODEL), row),
            pl.BlockSpec((tm, Q_W), row),
            pl.BlockSpec((tm, FOURIER_W), row),
            _const_spec(mods.shape),
            _const_spec((1, D_MODEL)),
            _const_spec((1, D_MODEL)),
            _const_spec(w_g.shape),
            _const_spec(w_pa.shape),
            _const_spec(w_pf.shape),
            _const_spec(w_out.shape),
            up_spec,
            down_spec,
        ],
        out_specs=[pl.BlockSpec((tm, D_MODEL), row), up_spec, down_spec],
        out_shape=[jax.ShapeDtypeStruct((n, D_MODEL), F32),
                   jax.ShapeDtypeStruct(w_up.shape, BF16),
                   jax.ShapeDtypeStruct(w_down.shape, BF16)],
        compiler_params=_params(1),
        name="merge",
    )(x2, attn, four, mods, g_pre, g_post, w_g, w_pa, w_pf, w_out, w_up, w_down)


def _convffn_kernel(x_ref, xp_ref, xn_ref, mod_ref, gpre_ref, gpost_ref,
                    wu_ref, wgate_ref, cw_ref, cb_ref, wd_ref, o_ref, act_ref, *, tiles_per_seq):
    i = pl.program_id(0)
    tm = x_ref.shape[0]
    halo = SUBLANES
    has_prev = (i % tiles_per_seq != 0).astype(F32)
    has_next = (i % tiles_per_seq != tiles_per_seq - 1).astype(F32)
    x = x_ref[...]
    shift, scale, out_gate = _mod_rows(mod_ref, i // tiles_per_seq, 3, 3)
    gain = gpre_ref[...]
    h = _norm_modulate(x, gain, shift, scale)
    hp = _norm_modulate(xp_ref[...], gain, shift, scale) * has_prev
    hn = _norm_modulate(xn_ref[...], gain, shift, scale) * has_next
    h_ext = jnp.concatenate([hp, h, hn], axis=0).astype(BF16)
    h_mid = h.astype(BF16)
    n_ext = tm + 2 * halo
    for c in range(D_FF // FFN_CHUNK):
        cols = slice(c * FFN_CHUNK, (c + 1) * FFN_CHUNK)
        u = jnp.dot(h_ext, wu_ref[:, cols], preferred_element_type=F32)
        gate = jnp.dot(h_mid, wgate_ref[:, cols], preferred_element_type=F32)
        u_prev = pltpu.roll(u, 1, 0)[halo:halo + tm]
        u_next = pltpu.roll(u, n_ext - 1, 0)[halo:halo + tm]
        conv = (u_prev * cw_ref[0:1, cols] + u[halo:halo + tm] * cw_ref[1:2, cols]
                + u_next * cw_ref[2:3, cols] + cb_ref[:, cols])
        act_ref[:, cols] = (conv * jax.nn.sigmoid(conv) * gate).astype(BF16)
    y = jnp.dot(act_ref[...], wd_ref[...], preferred_element_type=F32)
    o_ref[...] = x + out_gate * _post_norm(y, gpost_ref[...])


def _convffn(x1, mods, g_pre, g_post, w_up, conv_w, conv_b, w_down, seq):
    n = x1.shape[0]
    tm = TOKEN_TILE
    tiles_per_seq = seq // tm
    halo_blocks_per_tile = tm // SUBLANES
    n_halo_blocks = n // SUBLANES
    return pl.pallas_call(
        functools.partial(_convffn_kernel, tiles_per_seq=tiles_per_seq),
        grid=(n // tm,),
        in_specs=[
            pl.BlockSpec((tm, D_MODEL), lambda i: (i, 0)),
            pl.BlockSpec((SUBLANES, D_MODEL),
                         lambda i: (jnp.maximum(i * halo_blocks_per_tile - 1, 0), 0)),
            pl.BlockSpec((SUBLANES, D_MODEL),
                         lambda i: (jnp.minimum((i + 1) * halo_blocks_per_tile, n_halo_blocks - 1), 0)),
            _const_spec(mods.shape),
            _const_spec((1, D_MODEL)),
            _const_spec((1, D_MODEL)),
            pl.BlockSpec((D_MODEL, D_FF), lambda i: (0, 0), pipeline_mode=pl.Buffered(1)),
            pl.BlockSpec((D_MODEL, D_FF), lambda i: (0, 1), pipeline_mode=pl.Buffered(1)),
            _const_spec(conv_w.shape),
            _const_spec((1, D_FF)),
            _const_spec(w_down.shape),
        ],
        out_specs=pl.BlockSpec((tm, D_MODEL), lambda i: (i, 0)),
        out_shape=jax.ShapeDtypeStruct((n, D_MODEL), F32),
        scratch_shapes=[pltpu.VMEM((tm, D_FF), BF16)],
        compiler_params=_params(1),
        name="convffn",
    )(x1, x1, x1, mods, g_pre, g_post, w_up, w_up, conv_w, conv_b.reshape(1, D_FF), w_down)


def _rope_tables(seq):
    half = HEAD_DIM // 2
    inv_freq = ROPE_THETA ** (-jnp.arange(0, half, 2, dtype=F32) / half)
    ang_r = jnp.arange(seq // GRID_W).astype(F32)[:, None] * inv_freq
    ang_c = jnp.arange(GRID_W).astype(F32)[:, None] * inv_freq
    reps = LANES // HEAD_DIM

    def lanes(row_part, col_part):
        return jnp.tile(jnp.concatenate([row_part, col_part], axis=-1), (1, reps))

    zr = jnp.zeros((ang_r.shape[0], half), F32)
    zc = jnp.zeros((ang_c.shape[0], half), F32)
    rcos = lanes(jnp.concatenate([jnp.cos(ang_r), jnp.cos(ang_r)], axis=-1), zr)
    rsin = lanes(jnp.concatenate([-jnp.sin(ang_r), jnp.sin(ang_r)], axis=-1), zr)
    ccos = lanes(zc, jnp.concatenate([jnp.cos(ang_c), jnp.cos(ang_c)], axis=-1))
    csin = lanes(zc, jnp.concatenate([-jnp.sin(ang_c), jnp.sin(ang_c)], axis=-1))
    return rcos, rsin, ccos, csin


def kernel(x, c, ctx, c_ctx, w_mod, b_mod, g_pre1, g_post1, g_pre2, g_post2,
           w_in, sink, w_pa, w_pf, w_out, w_up, conv_w, conv_b, w_down):
    batch, seq, d = x.shape
    ctx_len = ctx.shape[1]
    depth = w_mod.shape[0]
    assert depth == 1 and d == D_MODEL and batch + 1 <= SUBLANES
    assert seq % TOKEN_TILE == 0 and TOKEN_TILE % GRID_W == 0
    assert seq % (DFT_MINOR * SUBLANES) == 0
    assert seq % (ATTN_BLOCKS_PER_STEP * BLOCK) == 0
    n = batch * seq
    rope_tabs = _rope_tables(seq)
    tables = _dft_tables(seq)
    bias = _band_bias()

    l = 0
    mods = _adaln(c, c_ctx.reshape(1, d), w_mod[l], b_mod[l])

    g_pre1_row = g_pre1[l].reshape(1, d)
    g_post1_row = g_post1[l].reshape(1, d)
    g_pre2_row = g_pre2[l].reshape(1, d)
    g_post2_row = g_post2[l].reshape(1, d)

    x2 = x.reshape(n, d)
    q, k, v_t, f, w_g_b, w_pa_b, w_pf_b, w_out_b = _inproj(x2, mods, g_pre1_row, w_in[l], w_pa[l], w_pf[l], w_out[l],
                                                          rope_tabs, seq)
    kx, vx_t = _ctxkv(ctx.reshape(batch * ctx_len, d), mods, batch, g_pre1_row, w_in[l], ctx_len)
    attn, four = _mix(sink[l], q, k, v_t, kx, vx_t, bias, f.reshape(batch, seq, FOURIER_W), tables, ctx_len)
    four = four.reshape(n, FOURIER_W)
    x1, w_up_b, w_down_b = _merge(x2, attn, four, mods, g_pre1_row, g_post1_row, w_g_b, w_pa_b, w_pf_b, w_out_b,
                                  w_up[l], w_down[l], seq)
    out = _convffn(x1, mods, g_pre2_row, g_post2_row, w_up_b, conv_w[l], conv_b[l], w_down_b, seq)
    return out.reshape(batch, seq, d)
```

```python
import functools
import math

import numpy as np
import jax
import jax.numpy as jnp
from jax import lax
from jax.experimental import pallas as pl
from jax.experimental.pallas import tpu as pltpu

F32 = jnp.float32
BF16 = jnp.bfloat16

D_MODEL = 1024
GRID_W = 64
HEAD_DIM = 64
N_Q_HEADS = 8
N_KV_HEADS = 2
GROUP = N_Q_HEADS // N_KV_HEADS
WINDOW = 128
BLOCK = 128
ROPE_THETA = 10000.0
N_FOURIER_GROUPS = 4
FOURIER_GROUP_CH = 128
FOURIER_W = N_FOURIER_GROUPS * FOURIER_GROUP_CH
Q_W = N_Q_HEADS * HEAD_DIM
KV_W = N_KV_HEADS * HEAD_DIM
D_FF = 2816
EPS = 1e-6
NEG = -1e30
LOG2E = math.log2(math.e)

LANES = 128
SUBLANES = 8
BF16_ROWS = 16
MXU_DIM = 256
VMEM_LIMIT_BYTES = 56 * 1024 * 1024

TOKEN_TILE = 1024
FFN_CHUNK = MXU_DIM
DFT_MINOR = SUBLANES
ATTN_BLOCKS_PER_STEP = 16


def _const_spec(shape):
    nd = len(shape)
    return pl.BlockSpec(shape, lambda *_: (0,) * nd, pipeline_mode=pl.Buffered(1))


def _params(n_axes):
    return pltpu.CompilerParams(
        dimension_semantics=("arbitrary",) * n_axes,
        vmem_limit_bytes=VMEM_LIMIT_BYTES,
    )


def _norm_modulate(x, gain, shift, scale):
    ms = jnp.mean(x * x, axis=-1, keepdims=True)
    return (x * lax.rsqrt(ms + EPS)) * (gain * (1.0 + scale)) + shift


def _mod_rows(mod_ref, row, first, count):
    return [mod_ref[pl.ds(row, 1), (first + j) * D_MODEL:(first + j + 1) * D_MODEL] for j in range(count)]


def _post_norm(y, gain):
    ms = jnp.mean(y * y, axis=-1, keepdims=True)
    return (y * lax.rsqrt(ms + EPS)) * gain


def _adaln_kernel(c_ref, cctx_ref, w_ref, b_ref, o_ref, rows_ref):
    batch = c_ref.shape[0]
    rows_ref[...] = jnp.zeros(rows_ref.shape, F32)
    rows_ref[0:batch, :] = c_ref[...]
    rows_ref[batch:batch + 1, :] = cctx_ref[...]
    c = rows_ref[...]
    s = c * jax.nn.sigmoid(c)
    o_ref[...] = jnp.dot(s, w_ref[...], preferred_element_type=F32) + b_ref[...]


def _adaln(c, c_ctx, w_mod, b_mod):
    n_out = w_mod.shape[1]
    tn = 1536
    return pl.pallas_call(
        _adaln_kernel,
        grid=(n_out // tn,),
        in_specs=[
            pl.BlockSpec(c.shape, lambda j: (0, 0)),
            pl.BlockSpec(c_ctx.shape, lambda j: (0, 0)),
            pl.BlockSpec((D_MODEL, tn), lambda j: (0, j)),
            pl.BlockSpec((1, tn), lambda j: (0, j)),
        ],
        out_specs=pl.BlockSpec((SUBLANES, tn), lambda j: (0, j)),
        out_shape=jax.ShapeDtypeStruct((SUBLANES, n_out), F32),
        scratch_shapes=[pltpu.VMEM((SUBLANES, D_MODEL), F32)],
        compiler_params=_params(1),
        name="adaln",
    )(c, c_ctx, w_mod, b_mod.reshape(1, n_out))


def _rope(xb, cos_t, sin_t, first_half):
    sw = jnp.where(first_half, pltpu.roll(xb, LANES - 16, 1), pltpu.roll(xb, 16, 1))
    return xb * cos_t + sw * sin_t


def _token_table(row_tab, col_tab):
    n_rows = row_tab.shape[0]
    by_row = jnp.concatenate(
        [jnp.broadcast_to(row_tab[r:r + 1, :], (GRID_W, LANES)) for r in range(n_rows)], axis=0)
    return by_row + jnp.concatenate([col_tab] * n_rows, axis=0)


def _inproj_kernel(x_ref, mod_ref, g_ref, w_ref, rcos_ref, rsin_ref, ccos_ref, csin_ref,
                   wg_ref, wpa_ref, wpf_ref, wout_ref,
                   q_ref, k_ref, v_ref, f_ref, wg_o, wpa_o, wpf_o, wout_o, wb_ref, *, tiles_per_seq):
    wg_o[...] = wg_ref[...].astype(BF16)
    wpa_o[...] = wpa_ref[...].astype(BF16)
    wpf_o[...] = wpf_ref[...].astype(BF16)
    wout_o[...] = wout_ref[...].astype(BF16)

    @pl.when(pl.program_id(0) == 0)
    def _():
        low = lax.broadcasted_iota(jnp.int32, (D_MODEL, LANES), 1) < HEAD_DIM
        for j in range(Q_W // LANES):
            src_a = (j // 2) * LANES
            src_b = (GROUP // 2 + j // 2) * LANES
            a = w_ref[:, src_a:src_a + LANES]
            b = w_ref[:, src_b:src_b + LANES]
            if j % 2 == 1:
                a = pltpu.roll(a, HEAD_DIM, 1)
            else:
                b = pltpu.roll(b, HEAD_DIM, 1)
            wb_ref[:, j * LANES:(j + 1) * LANES] = jnp.where(low, a, b).astype(BF16)
        wb_ref[:, Q_W:] = w_ref[:, Q_W:].astype(BF16)

    shift, scale_mod = _mod_rows(mod_ref, pl.program_id(0) // tiles_per_seq, 0, 2)
    h = _norm_modulate(x_ref[...], g_ref[...], shift, scale_mod).astype(BF16)
    cos_t = _token_table(rcos_ref[...], ccos_ref[...])
    sin_t = _token_table(rsin_ref[...], csin_ref[...])
    lane = lax.broadcasted_iota(jnp.int32, cos_t.shape, 1)
    first_half = (lane % 32) < 16
    scale = HEAD_DIM ** -0.5 * LOG2E

    def project(col0, width):
        return jnp.dot(h, wb_ref[:, col0:col0 + width], preferred_element_type=F32)

    for c in range(0, Q_W, MXU_DIM):
        p = project(c, MXU_DIM)
        for b in range(MXU_DIM // LANES):
            blk = p[:, b * LANES:(b + 1) * LANES]
            q_ref[:, c + b * LANES:c + (b + 1) * LANES] = (
                _rope(blk, cos_t, sin_t, first_half) * scale).astype(BF16)
    p = project(Q_W, 2 * KV_W)
    k_ref[...] = _rope(p[:, :KV_W], cos_t, sin_t, first_half).astype(BF16)
    v_ref[...] = p[:, KV_W:].T.astype(BF16)
    for c in range(0, FOURIER_W, MXU_DIM):
        f_ref[:, c:c + MXU_DIM] = project(Q_W + 2 * KV_W + c, MXU_DIM)


def _row_slab_rows(n_rows, n_steps):
    rows = n_rows // n_steps
    assert rows * n_steps == n_rows and rows % BF16_ROWS == 0
    return rows


def _inproj(x2, mods, g_pre, w_in, w_pa, w_pf, w_out, rope_tabs, seq):
    n = x2.shape[0]
    tm = TOKEN_TILE
    n_steps = n // tm
    tiles_per_seq = seq // tm
    rows_per_tile = tm // GRID_W
    n_cols = Q_W + 2 * KV_W + FOURIER_W
    row_spec = pl.BlockSpec((rows_per_tile, LANES), lambda i: (i % tiles_per_seq, 0))
    g_rows = _row_slab_rows(D_MODEL, n_steps)
    pa_rows = _row_slab_rows(w_pa.shape[0], n_steps)
    pf_rows = _row_slab_rows(w_pf.shape[0], n_steps)
    out_rows = _row_slab_rows(w_out.shape[0], n_steps)
    slab = lambda rows, cols: pl.BlockSpec((rows, cols), lambda i: (i, 0))
    return pl.pallas_call(
        functools.partial(_inproj_kernel, tiles_per_seq=tiles_per_seq),
        grid=(n // tm,),
        in_specs=[
            pl.BlockSpec((tm, D_MODEL), lambda i: (i, 0)),
            _const_spec(mods.shape),
            _const_spec((1, D_MODEL)),
            _const_spec((D_MODEL, n_cols)),
            row_spec, row_spec,
            _const_spec((GRID_W, LANES)), _const_spec((GRID_W, LANES)),
            pl.BlockSpec((pl.Element(g_rows), pl.Element(2 * D_MODEL)),
                         lambda i: (i * g_rows, n_cols)),
            slab(pa_rows, D_MODEL), slab(pf_rows, D_MODEL), slab(out_rows, D_MODEL),
        ],
        out_specs=[
            pl.BlockSpec((tm, Q_W), lambda i: (i, 0)),
            pl.BlockSpec((tm, KV_W), lambda i: (i, 0)),
            pl.BlockSpec((KV_W, tm), lambda i: (0, i)),
            pl.BlockSpec((tm, FOURIER_W), lambda i: (i, 0)),
            slab(g_rows, 2 * D_MODEL), slab(pa_rows, D_MODEL), slab(pf_rows, D_MODEL), slab(out_rows, D_MODEL),
        ],
        out_shape=[
            jax.ShapeDtypeStruct((n, Q_W), BF16),
            jax.ShapeDtypeStruct((n, KV_W), BF16),
            jax.ShapeDtypeStruct((KV_W, n), BF16),
            jax.ShapeDtypeStruct((n, FOURIER_W), F32),
            jax.ShapeDtypeStruct((D_MODEL, 2 * D_MODEL), BF16),
            jax.ShapeDtypeStruct(w_pa.shape, BF16),
            jax.ShapeDtypeStruct(w_pf.shape, BF16),
            jax.ShapeDtypeStruct(w_out.shape, BF16),
        ],
        scratch_shapes=[pltpu.VMEM((D_MODEL, n_cols), BF16)],
        compiler_params=_params(1),
        name="inproj",
    )(x2, mods, g_pre, w_in, *rope_tabs, w_in, w_pa, w_pf, w_out)


def _ctxkv_kernel(x_ref, mod_ref, g_ref, w_ref, k_ref, v_ref, *, row):
    shift, scale_mod = _mod_rows(mod_ref, row, 0, 2)
    h = _norm_modulate(x_ref[...], g_ref[...], shift, scale_mod)
    p = jnp.dot(h.astype(BF16), w_ref[...].astype(BF16), preferred_element_type=F32)
    k_ref[...] = p[:, :KV_W].astype(BF16)
    v_ref[...] = p[:, KV_W:].T.astype(BF16)


def _ctxkv(ctx2, mods, ctx_row, g_pre, w_in, ctx_len):
    n = ctx2.shape[0]
    assert Q_W % (2 * KV_W) == 0
    return pl.pallas_call(
        functools.partial(_ctxkv_kernel, row=ctx_row),
        grid=(n // ctx_len,),
        in_specs=[
            pl.BlockSpec((ctx_len, D_MODEL), lambda i: (i, 0)),
            _const_spec(mods.shape),
            _const_spec((1, D_MODEL)),
            pl.BlockSpec((D_MODEL, 2 * KV_W), lambda i: (0, Q_W // (2 * KV_W)),
                         pipeline_mode=pl.Buffered(1)),
        ],
        out_specs=[
            pl.BlockSpec((ctx_len, KV_W), lambda i: (i, 0)),
            pl.BlockSpec((KV_W, ctx_len), lambda i: (0, i)),
        ],
        out_shape=[
            jax.ShapeDtypeStruct((n, KV_W), BF16),
            jax.ShapeDtypeStruct((KV_W, n), BF16),
        ],
        compiler_params=_params(1),
        name="ctxkv",
    )(ctx2, mods, g_pre, w_in)


def _attend_block(sink_ref, q_blk, k_parts, v_parts, bias_prev, bias_next):
    kcat = jnp.concatenate(k_parts, axis=0)
    vcat_t = jnp.concatenate(v_parts, axis=1)
    n_keys = kcat.shape[0]
    low = lax.broadcasted_iota(jnp.int32, (BLOCK, LANES), 1) < HEAD_DIM
    top = lax.broadcasted_iota(jnp.int32, (KV_W, n_keys), 0) < HEAD_DIM
    zero = jnp.zeros((), BF16)
    one = jnp.ones((), BF16)
    q_rows = [jnp.where(low if h == 0 else jnp.logical_not(low), q_blk[:, g * LANES:(g + 1) * LANES], zero)
              for h in range(N_KV_HEADS) for g in range(GROUP)]
    s_all = lax.dot_general(kcat, jnp.concatenate(q_rows, axis=0), (((1,), (1,)), ((), ())),
                            preferred_element_type=F32).astype(BF16)
    outs = []
    for h in range(N_KV_HEADS):
        v_h = jnp.where(top if h == 0 else jnp.logical_not(top), vcat_t, one)
        for g0 in range(0, GROUP, 2):
            probs = []
            sinks = []
            for g in (g0, g0 + 1):
                head = h * GROUP + g
                s = s_all[:, head * LANES:(head + 1) * LANES]
                parts = [s[0:BLOCK] + bias_prev,
                         s[BLOCK:2 * BLOCK],
                         s[2 * BLOCK:3 * BLOCK] + bias_next]
                parts += [s[r:r + BLOCK] for r in range(3 * BLOCK, n_keys, BLOCK)]
                mx = parts[0]
                for part in parts[1:]:
                    mx = jnp.maximum(mx, part)
                sink = sink_ref[head] * LOG2E
                m = jnp.maximum(jnp.max(mx, axis=0, keepdims=True), sink.astype(BF16))
                probs.append(jnp.concatenate([jnp.exp2(part - m) for part in parts], axis=0))
                sinks.append(jnp.exp2(sink - m.astype(F32)))
            o2 = jnp.dot(v_h, jnp.concatenate(probs, axis=1), preferred_element_type=F32)
            for idx in range(2):
                o_t = o2[:, idx * LANES:(idx + 1) * LANES]
                num = o_t[0:HEAD_DIM] if h == 0 else o_t[HEAD_DIM:]
                den = (o_t[HEAD_DIM:] if h == 0 else o_t[0:HEAD_DIM]) + sinks[idx]
                outs.append(num / den)
    return outs


def _attention_steps(sink_ref, q_ref, kp_ref, kc_ref, kn_ref, kx_ref, vp_ref, vc_ref, vn_ref, vx_ref,
                     bias_ref, o_ref):
    i = pl.program_id(1)
    n_sub = q_ref.shape[0] // BLOCK
    first_var = jnp.where(i == 0, 0, 1)
    last_var = jnp.where(i == pl.num_programs(1) - 1, 2, 1)

    def block_step(t):
        own = slice(t * BLOCK, (t + 1) * BLOCK)
        before = slice((t - 1) * BLOCK, t * BLOCK)
        after = slice((t + 1) * BLOCK, (t + 2) * BLOCK)
        k_parts = [kp_ref[...] if t == 0 else kc_ref[before, :], kc_ref[own, :],
                   kn_ref[...] if t == n_sub - 1 else kc_ref[after, :], kx_ref[...]]
        v_parts = [vp_ref[...] if t == 0 else vc_ref[:, before], vc_ref[:, own],
                   vn_ref[...] if t == n_sub - 1 else vc_ref[:, after], vx_ref[...]]
        bias_prev = bias_ref[first_var if t == 0 else 1, 0:BLOCK, :]
        bias_next = bias_ref[last_var if t == n_sub - 1 else 1, BLOCK:2 * BLOCK, :]
        outs = _attend_block(sink_ref, q_ref[own, :], k_parts, v_parts, bias_prev, bias_next)
        for j in range(Q_W // LANES):
            o_t = jnp.concatenate([outs[2 * j], outs[2 * j + 1]], axis=0)
            o_ref[own, j * LANES:(j + 1) * LANES] = o_t.T.astype(BF16)

    return [functools.partial(block_step, t) for t in range(n_sub)]


def _attention_specs(batch, seq, ctx_len, bias_shape):
    nb = seq // BLOCK
    sub = ATTN_BLOCKS_PER_STEP
    steps = nb // sub

    def cur(b, i):
        return (b * steps + i, 0)

    def prev(b, i):
        return (b * nb + jnp.maximum(i * sub - 1, 0), 0)

    def nxt(b, i):
        return (b * nb + jnp.minimum((i + 1) * sub, nb - 1), 0)

    def swap(f):
        return lambda b, i: f(b, i)[::-1]

    in_specs = [
        pl.BlockSpec(memory_space=pltpu.SMEM),
        pl.BlockSpec((sub * BLOCK, Q_W), cur),
        pl.BlockSpec((BLOCK, KV_W), prev),
        pl.BlockSpec((sub * BLOCK, KV_W), cur),
        pl.BlockSpec((BLOCK, KV_W), nxt),
        pl.BlockSpec((ctx_len, KV_W), lambda b, i: (b, 0)),
        pl.BlockSpec((KV_W, BLOCK), swap(prev)),
        pl.BlockSpec((KV_W, sub * BLOCK), swap(cur)),
        pl.BlockSpec((KV_W, BLOCK), swap(nxt)),
        pl.BlockSpec((KV_W, ctx_len), lambda b, i: (0, b)),
        _const_spec(bias_shape),
    ]
    return steps, in_specs, pl.BlockSpec((sub * BLOCK, Q_W), cur)


def _band_bias():
    qi = np.arange(BLOCK)[None, :]
    kj = np.arange(3 * BLOCK)[:, None]
    rel = kj - BLOCK - qi
    in_window = np.abs(rel) <= WINDOW
    variants = []
    for has_prev, has_next in ((False, True), (True, True), (True, False)):
        ok = in_window.copy()
        if not has_prev:
            ok &= kj >= BLOCK
        if not has_next:
            ok &= kj < 2 * BLOCK
        mask = np.where(ok, 0.0, NEG).astype(np.float32)
        variants.append(np.concatenate([mask[:BLOCK], mask[2 * BLOCK:]], axis=0))
    return jnp.asarray(np.stack(variants)).astype(BF16)


def _minor_dft_terms():
    assert DFT_MINOR == 8
    plan = []
    for k2 in range(DFT_MINOR):
        unit, half = [], []
        for s2 in range(DFT_MINOR):
            ang = -2.0 * math.pi * ((s2 * k2) % DFT_MINOR) / DFT_MINOR
            for part, coef in ((0, math.cos(ang)), (1, -math.sin(ang))):
                if abs(coef) < 1e-9:
                    continue
                target = unit if abs(abs(coef) - 1.0) < 1e-9 else half
                assert target is unit or abs(abs(coef) - math.sqrt(0.5)) < 1e-9
                target.append((s2, part, 1 if coef > 0 else -1))
        plan.append((unit, half))
    return plan


def _signed_sum(terms, pick):
    pos = [pick(s2, part) for s2, part, sign in terms if sign > 0]
    neg = [pick(s2, part) for s2, part, sign in terms if sign < 0]
    total = None
    for v in pos:
        total = v if total is None else total + v
    for v in neg:
        total = -v if total is None else total - v
    return total


def _fourier_steps(f_ref, f1_ref, tr_ref, ti_ref, cm_ref, o_ref):
    n_major = f_ref.shape[0] // DFT_MINOR
    pair_w = 2 * LANES
    n_pairs = DFT_MINOR // 2
    row_chunk = n_major // n_pairs
    twiddled = [None] * n_pairs
    spectra = [None] * DFT_MINOR
    plan = _minor_dft_terms()

    def stage1(c):
        y = jnp.concatenate(
            [f_ref[pl.ds(s2, n_major, stride=DFT_MINOR), :].astype(BF16) for s2 in (2 * c, 2 * c + 1)],
            axis=1)
        a = jnp.dot(f1_ref[...], y, preferred_element_type=F32)
        ar = a[:n_major]
        ai = a[n_major:]
        tr = tr_ref[:, c * pair_w:(c + 1) * pair_w]
        ti = ti_ref[:, c * pair_w:(c + 1) * pair_w]
        twiddled[c] = ((ar * tr - ai * ti).astype(BF16), (ar * ti + ai * tr).astype(BF16))

    def channels(c):
        apr, api = twiddled[c]
        for t in range(2):
            lanes = slice(t * LANES, (t + 1) * LANES)
            ap = jnp.concatenate([apr[:, lanes], api[:, lanes]], axis=1)
            spectra[2 * c + t] = jnp.dot(ap, cm_ref[...], preferred_element_type=F32)

    def stage2(r):
        rows = slice(r * row_chunk, (r + 1) * row_chunk)

        def pick(s2, part):
            return spectra[s2][rows, part * LANES:(part + 1) * LANES]

        for k2, (unit, half) in enumerate(plan):
            res = _signed_sum(unit, pick)
            if half:
                scaled = _signed_sum(half, pick) * math.sqrt(0.5)
                res = scaled if res is None else res + scaled
            o_ref[pl.ds(k2 * n_major + r * row_chunk, row_chunk), :] = res.astype(BF16)

    s1 = lambda c: (functools.partial(stage1, c), 6)
    ch = lambda c: (functools.partial(channels, c), 3)
    steps = [s1(0)]
    for c in range(1, n_pairs):
        steps += [s1(c), ch(c - 1)]
    steps.append(ch(n_pairs - 1))
    return steps + [(functools.partial(stage2, r), 1) for r in range(n_pairs)]


def _cis(num, den):
    ang = (num % den).astype(F32) * F32(-2.0 * math.pi / den)
    return jnp.cos(ang), jnp.sin(ang)


def _dft_tables(seq):
    n_major = seq // DFT_MINOR
    ch = FOURIER_GROUP_CH
    root = int(round(math.sqrt(n_major)))
    assert root * root == n_major
    s1 = jnp.arange(n_major, dtype=jnp.int32)[None, :]
    hi = jnp.arange(root, dtype=jnp.int32)[:, None] * root
    lo = jnp.arange(root, dtype=jnp.int32)[:, None]
    hr, hi_ = _cis(hi * s1, n_major)
    lr, li = _cis(lo * s1, n_major)
    p1 = jnp.stack([hr, hr])[:, :, None, :]
    q1 = jnp.stack([lr, li])[:, None, :, :]
    p2 = jnp.stack([-hi_, hi_])[:, :, None, :]
    q2 = jnp.stack([li, lr])[:, None, :, :]
    f1 = (p1 * q1 + p2 * q2).astype(BF16).reshape(2 * n_major, n_major)
    k1 = jnp.arange(n_major, dtype=jnp.int32)[:, None]
    s2 = jnp.arange(DFT_MINOR, dtype=jnp.int32)[None, :]
    tr, ti = _cis(k1 * s2, seq)
    tr = jnp.broadcast_to(tr[:, :, None], (n_major, DFT_MINOR, ch)).reshape(n_major, DFT_MINOR * ch)
    ti = jnp.broadcast_to(ti[:, :, None], (n_major, DFT_MINOR, ch)).reshape(n_major, DFT_MINOR * ch)
    c128 = jnp.arange(ch, dtype=jnp.int32)
    cr, ci = _cis(c128[:, None] * c128[None, :], ch)
    norm = F32(1.0 / math.sqrt(seq * ch))
    cm = (jnp.concatenate([jnp.concatenate([cr, ci], axis=1),
                           jnp.concatenate([-ci, cr], axis=1)], axis=0) * norm).astype(BF16)
    return f1, tr, ti, cm


def _mix_kernel(sink_ref, q_ref, kp_ref, kc_ref, kn_ref, kx_ref, vp_ref, vc_ref, vn_ref, vx_ref, bias_ref,
                f_ref, f1_ref, tr_ref, ti_ref, cm_ref, attn_ref, four_ref):
    attn_steps = _attention_steps(sink_ref, q_ref, kp_ref, kc_ref, kn_ref, kx_ref,
                                  vp_ref, vc_ref, vn_ref, vx_ref, bias_ref, attn_ref)
    four_steps = _fourier_steps(f_ref, f1_ref, tr_ref, ti_ref, cm_ref, four_ref)
    total = sum(weight for _, weight in four_steps)
    done = 0
    seen = 0
    for four_step, weight in four_steps:
        seen += weight
        upto = seen * len(attn_steps) // total
        for attn_step in attn_steps[done:upto]:
            attn_step()
        done = upto
        four_step()
    assert done == len(attn_steps)


def _mix(sink, q, k, v_t, kx, vx_t, bias, f3, tables, ctx_len):
    batch, seq, _ = f3.shape
    f1, tr, ti, cm = tables
    ch = FOURIER_GROUP_CH
    steps, attn_in, attn_out = _attention_specs(batch, seq, ctx_len, bias.shape)
    assert steps == N_FOURIER_GROUPS
    four_spec = pl.BlockSpec((None, seq, ch), lambda b, g: (b, 0, g))
    return pl.pallas_call(
        _mix_kernel,
        grid=(batch, steps),
        in_specs=attn_in + [four_spec, _const_spec(f1.shape), _const_spec(tr.shape),
                            _const_spec(ti.shape), _const_spec(cm.shape)],
        out_specs=[attn_out, four_spec],
        out_shape=[jax.ShapeDtypeStruct((batch * seq, Q_W), BF16),
                   jax.ShapeDtypeStruct((batch, seq, FOURIER_W), BF16)],
        compiler_params=_params(2),
        name="mix",
    )(sink, q, k, k, k, kx, v_t, v_t, v_t, vx_t, bias, f3, f1, tr, ti, cm)


def _merge_kernel(x_ref, a_ref, f_ref, mod_ref, gpre_ref, gpost_ref,
                  wg_b, wpa_b, wpf_b, wout_b, wup_ref, wdown_ref,
                  o_ref, wup_o, wdown_o, *, tiles_per_seq):
    wup_o[...] = wup_ref[...].astype(BF16)
    wdown_o[...] = wdown_ref[...].astype(BF16)

    x = x_ref[...]
    shift, scale_mod, out_gate = _mod_rows(mod_ref, pl.program_id(0) // tiles_per_seq, 0, 3)
    h = _norm_modulate(x, gpre_ref[...], shift, scale_mod).astype(BF16)
    a = a_ref[...]
    f = f_ref[...]
    y = None
    for c in range(0, D_MODEL, MXU_DIM):
        cols = slice(c, c + MXU_DIM)
        gate_a = jnp.dot(h, wg_b[:, cols], preferred_element_type=F32)
        gate_f = jnp.dot(h, wg_b[:, D_MODEL + c:D_MODEL + c + MXU_DIM], preferred_element_type=F32)
        pa = jnp.dot(a, wpa_b[:, cols], preferred_element_type=F32)
        pf = jnp.dot(f, wpf_b[:, cols], preferred_element_type=F32)
        m = (jax.nn.sigmoid(gate_a) * pa + jax.nn.sigmoid(gate_f) * pf).astype(BF16)
        part = jnp.dot(m, wout_b[cols, :], preferred_element_type=F32)
        y = part if y is None else y + part
    o_ref[...] = x + out_gate * _post_norm(y, gpost_ref[...])


def _merge(x2, attn, four, mods, g_pre, g_post, w_g, w_pa, w_pf, w_out, w_up, w_down, seq):
    n = x2.shape[0]
    tm = TOKEN_TILE
    n_steps = n // tm
    tiles_per_seq = seq // tm
    row = lambda i: (i, 0)
    up_rows = w_up.shape[0] // n_steps
    down_rows = w_down.shape[0] // BF16_ROWS
    n_down = w_down.shape[0] // down_rows
    assert up_rows * n_steps == w_up.shape[0] and up_rows % BF16_ROWS == 0
    assert down_rows % BF16_ROWS == 0 and n_down <= n_steps
    up_spec = pl.BlockSpec((up_rows, w_up.shape[1]), row)
    down_spec = pl.BlockSpec((down_rows, w_down.shape[1]), lambda i: (jnp.minimum(i, n_down - 1), 0))
    return pl.pallas_call(
        functools.partial(_merge_kernel, tiles_per_seq=tiles_per_seq),
        grid=(n // tm,),
        in_specs=[
            pl.BlockSpec((tm, D_MODEL), row),
            pl.BlockSpec((tm, Q_W), row),
            pl.BlockSpec((tm, FOURIER_W), row),
            _const_spec(mods.shape),
            _const_spec((1, D_MODEL)),
            _const_spec((1, D_MODEL)),
            _const_spec(w_g.shape),
            _const_spec(w_pa.shape),
            _const_spec(w_pf.shape),
            _const_spec(w_out.shape),
            up_spec,
            down_spec,
        ],
        out_specs=[pl.BlockSpec((tm, D_MODEL), row), up_spec, down_spec],
        out_shape=[jax.ShapeDtypeStruct((n, D_MODEL), F32),
                   jax.ShapeDtypeStruct(w_up.shape, BF16),
                   jax.ShapeDtypeStruct(w_down.shape, BF16)],
        compiler_params=_params(1),
        name="merge",
    )(x2, attn, four, mods, g_pre, g_post, w_g, w_pa, w_pf, w_out, w_up, w_down)


def _convffn_kernel(x_ref, xp_ref, xn_ref, mod_ref, gpre_ref, gpost_ref,
                    wu_ref, wgate_ref, cw_ref, cb_ref, wd_ref, o_ref, act_ref, *, tiles_per_seq):
    i = pl.program_id(0)
    tm = x_ref.shape[0]
    halo = SUBLANES
    has_prev = (i % tiles_per_seq != 0).astype(F32)
    has_next = (i % tiles_per_seq != tiles_per_seq - 1).astype(F32)
    x = x_ref[...]
    shift, scale, out_gate = _mod_rows(mod_ref, i // tiles_per_seq, 3, 3)
    gain = gpre_ref[...]
    h = _norm_modulate(x, gain, shift, scale)
    hp = _norm_modulate(xp_ref[...], gain, shift, scale) * has_prev
    hn = _norm_modulate(xn_ref[...], gain, shift, scale) * has_next
    h_ext = jnp.concatenate([hp, h, hn], axis=0).astype(BF16)
    h_mid = h.astype(BF16)
    n_ext = tm + 2 * halo
    for c in range(D_FF // FFN_CHUNK):
        cols = slice(c * FFN_CHUNK, (c + 1) * FFN_CHUNK)
        u = jnp.dot(h_ext, wu_ref[:, cols], preferred_element_type=F32)
        gate = jnp.dot(h_mid, wgate_ref[:, cols], preferred_element_type=F32)
        u_prev = pltpu.roll(u, 1, 0)[halo:halo + tm]
        u_next = pltpu.roll(u, n_ext - 1, 0)[halo:halo + tm]
        conv = (u_prev * cw_ref[0:1, cols] + u[halo:halo + tm] * cw_ref[1:2, cols]
                + u_next * cw_ref[2:3, cols] + cb_ref[:, cols])
        act_ref[:, cols] = (conv * jax.nn.sigmoid(conv) * gate).astype(BF16)
    y = jnp.dot(act_ref[...], wd_ref[...], preferred_element_type=F32)
    o_ref[...] = x + out_gate * _post_norm(y, gpost_ref[...])


def _convffn(x1, mods, g_pre, g_post, w_up, conv_w, conv_b, w_down, seq):
    n = x1.shape[0]
    tm = TOKEN_TILE
    tiles_per_seq = seq // tm
    halo_blocks_per_tile = tm // SUBLANES
    n_halo_blocks = n // SUBLANES
    return pl.pallas_call(
        functools.partial(_convffn_kernel, tiles_per_seq=tiles_per_seq),
        grid=(n // tm,),
        in_specs=[
            pl.BlockSpec((tm, D_MODEL), lambda i: (i, 0)),
            pl.BlockSpec((SUBLANES, D_MODEL),
                         lambda i: (jnp.maximum(i * halo_blocks_per_tile - 1, 0), 0)),
            pl.BlockSpec((SUBLANES, D_MODEL),
                         lambda i: (jnp.minimum((i + 1) * halo_blocks_per_tile, n_halo_blocks - 1), 0)),
            _const_spec(mods.shape),
            _const_spec((1, D_MODEL)),
            _const_spec((1, D_MODEL)),
            pl.BlockSpec((D_MODEL, D_FF), lambda i: (0, 0), pipeline_mode=pl.Buffered(1)),
            pl.BlockSpec((D_MODEL, D_FF), lambda i: (0, 1), pipeline_mode=pl.Buffered(1)),
            _const_spec(conv_w.shape),
            _const_spec((1, D_FF)),
            _const_spec(w_down.shape),
        ],
        out_specs=pl.BlockSpec((tm, D_MODEL), lambda i: (i, 0)),
        out_shape=jax.ShapeDtypeStruct((n, D_MODEL), F32),
        scratch_shapes=[pltpu.VMEM((tm, D_FF), BF16)],
        compiler_params=_params(1),
        name="convffn",
    )(x1, x1, x1, mods, g_pre, g_post, w_up, w_up, conv_w, conv_b.reshape(1, D_FF), w_down)


def _rope_tables(seq):
    half = HEAD_DIM // 2
    inv_freq = ROPE_THETA ** (-jnp.arange(0, half, 2, dtype=F32) / half)
    ang_r = jnp.arange(seq // GRID_W).astype(F32)[:, None] * inv_freq
    ang_c = jnp.arange(GRID_W).astype(F32)[:, None] * inv_freq
    reps = LANES // HEAD_DIM

    def lanes(row_part, col_part):
        return jnp.tile(jnp.concatenate([row_part, col_part], axis=-1), (1, reps))

    zr = jnp.zeros((ang_r.shape[0], half), F32)
    zc = jnp.zeros((ang_c.shape[0], half), F32)
    rcos = lanes(jnp.concatenate([jnp.cos(ang_r), jnp.cos(ang_r)], axis=-1), zr)
    rsin = lanes(jnp.concatenate([-jnp.sin(ang_r), jnp.sin(ang_r)], axis=-1), zr)
    ccos = lanes(zc, jnp.concatenate([jnp.cos(ang_c), jnp.cos(ang_c)], axis=-1))
    csin = lanes(zc, jnp.concatenate([-jnp.sin(ang_c), jnp.sin(ang_c)], axis=-1))
    return rcos, rsin, ccos, csin


def kernel(x, c, ctx, c_ctx, w_mod, b_mod, g_pre1, g_post1, g_pre2, g_post2,
           w_in, sink, w_pa, w_pf, w_out, w_up, conv_w, conv_b, w_down):
    batch, seq, d = x.shape
    ctx_len = ctx.shape[1]
    depth = w_mod.shape[0]
    assert depth == 1 and d == D_MODEL and batch + 1 <= SUBLANES
    assert seq % TOKEN_TILE == 0 and TOKEN_TILE % GRID_W == 0
    assert seq % (DFT_MINOR * SUBLANES) == 0
    assert seq % (ATTN_BLOCKS_PER_STEP * BLOCK) == 0
    n = batch * seq
    rope_tabs = _rope_tables(seq)
    tables = _dft_tables(seq)
    bias = _band_bias()

    l = 0
    mods = _adaln(c, c_ctx.reshape(1, d), w_mod[l], b_mod[l])

    g_pre1_row = g_pre1[l].reshape(1, d)
    g_post1_row = g_post1[l].reshape(1, d)
    g_pre2_row = g_pre2[l].reshape(1, d)
    g_post2_row = g_post2[l].reshape(1, d)

    x2 = x.reshape(n, d)
    q, k, v_t, f, w_g_b, w_pa_b, w_pf_b, w_out_b = _inproj(x2, mods, g_pre1_row, w_in[l], w_pa[l], w_pf[l], w_out[l],
                                                          rope_tabs, seq)
    kx, vx_t = _ctxkv(ctx.reshape(batch * ctx_len, d), mods, batch, g_pre1_row, w_in[l], ctx_len)
    attn, four = _mix(sink[l], q, k, v_t, kx, vx_t, bias, f.reshape(batch, seq, FOURIER_W), tables, ctx_len)
    four = four.reshape(n, FOURIER_W)
    x1, w_up_b, w_down_b = _merge(x2, attn, four, mods, g_pre1_row, g_post1_row, w_g_b, w_pa_b, w_pf_b, w_out_b,
                                  w_up[l], w_down[l], seq)
    out = _convffn(x1, mods, g_pre2_row, g_post2_row, w_up_b, conv_w[l], conv_b[l], w_down_b, seq)
    return out.reshape(batch, seq, d)
```

```python
import functools
import math

import numpy as np
import jax
import jax.numpy as jnp
from jax import lax
from jax.experimental import pallas as pl
from jax.experimental.pallas import tpu as pltpu

F32 = jnp.float32
BF16 = jnp.bfloat16

D_MODEL = 1024
GRID_W = 64
HEAD_DIM = 64
N_Q_HEADS = 8
N_KV_HEADS = 2
GROUP = N_Q_HEADS // N_KV_HEADS
WINDOW = 128
BLOCK = 128
ROPE_THETA = 10000.0
N_FOURIER_GROUPS = 4
FOURIER_GROUP_CH = 128
FOURIER_W = N_FOURIER_GROUPS * FOURIER_GROUP_CH
Q_W = N_Q_HEADS * HEAD_DIM
KV_W = N_KV_HEADS * HEAD_DIM
D_FF = 2816
EPS = 1e-6
NEG = -1e30
LOG2E = math.log2(math.e)

LANES = 128
SUBLANES = 8
BF16_ROWS = 16
MXU_DIM = 256
VMEM_LIMIT_BYTES = 56 * 1024 * 1024

TOKEN_TILE = 1024
FFN_CHUNK = MXU_DIM
DFT_MINOR = SUBLANES
ATTN_BLOCKS_PER_STEP = 16


def _const_spec(shape):
    nd = len(shape)
    return pl.BlockSpec(shape, lambda *_: (0,) * nd, pipeline_mode=pl.Buffered(1))


def _params(n_axes):
    return pltpu.CompilerParams(
        dimension_semantics=("arbitrary",) * n_axes,
        vmem_limit_bytes=VMEM_LIMIT_BYTES,
    )


def _norm_modulate(x, gain, shift, scale):
    ms = jnp.mean(x * x, axis=-1, keepdims=True)
    return (x * lax.rsqrt(ms + EPS)) * (gain * (1.0 + scale)) + shift


def _mod_rows(mod_ref, row, first, count):
    return [mod_ref[pl.ds(row, 1), (first + j) * D_MODEL:(first + j + 1) * D_MODEL] for j in range(count)]


def _post_norm(y, gain):
    ms = jnp.mean(y * y, axis=-1, keepdims=True)
    return (y * lax.rsqrt(ms + EPS)) * gain


def _adaln_kernel(c_ref, cctx_ref, w_ref, b_ref, o_ref, rows_ref):
    batch = c_ref.shape[0]
    rows_ref[...] = jnp.zeros(rows_ref.shape, F32)
    rows_ref[0:batch, :] = c_ref[...]
    rows_ref[batch:batch + 1, :] = cctx_ref[...]
    c = rows_ref[...]
    s = c * jax.nn.sigmoid(c)
    o_ref[...] = jnp.dot(s, w_ref[...], preferred_element_type=F32) + b_ref[...]


def _adaln(c, c_ctx, w_mod, b_mod):
    n_out = w_mod.shape[1]
    tn = 1536
    return pl.pallas_call(
        _adaln_kernel,
        grid=(n_out // tn,),
        in_specs=[
            pl.BlockSpec(c.shape, lambda j: (0, 0)),
            pl.BlockSpec(c_ctx.shape, lambda j: (0, 0)),
            pl.BlockSpec((D_MODEL, tn), lambda j: (0, j)),
            pl.BlockSpec((1, tn), lambda j: (0, j)),
        ],
        out_specs=pl.BlockSpec((SUBLANES, tn), lambda j: (0, j)),
        out_shape=jax.ShapeDtypeStruct((SUBLANES, n_out), F32),
        scratch_shapes=[pltpu.VMEM((SUBLANES, D_MODEL), F32)],
        compiler_params=_params(1),
        name="adaln",
    )(c, c_ctx, w_mod, b_mod.reshape(1, n_out))


def _rope(xb, cos_t, sin_t, first_half):
    sw = jnp.where(first_half, pltpu.roll(xb, LANES - 16, 1), pltpu.roll(xb, 16, 1))
    return xb * cos_t + sw * sin_t


def _token_table(row_tab, col_tab):
    n_rows = row_tab.shape[0]
    by_row = jnp.concatenate(
        [jnp.broadcast_to(row_tab[r:r + 1, :], (GRID_W, LANES)) for r in range(n_rows)], axis=0)
    return by_row + jnp.concatenate([col_tab] * n_rows, axis=0)


def _inproj_kernel(x_ref, mod_ref, g_ref, w_ref, rcos_ref, rsin_ref, ccos_ref, csin_ref,
                   wg_ref, wpa_ref, wpf_ref, wout_ref,
                   q_ref, k_ref, v_ref, f_ref, wg_o, wpa_o, wpf_o, wout_o, wb_ref, *, tiles_per_seq):
    wg_o[...] = wg_ref[...].astype(BF16)
    wpa_o[...] = wpa_ref[...].astype(BF16)
    wpf_o[...] = wpf_ref[...].astype(BF16)
    wout_o[...] = wout_ref[...].astype(BF16)

    @pl.when(pl.program_id(0) == 0)
    def _():
        low = lax.broadcasted_iota(jnp.int32, (D_MODEL, LANES), 1) < HEAD_DIM
        for j in range(Q_W // LANES):
            src_a = (j // 2) * LANES
            src_b = (GROUP // 2 + j // 2) * LANES
            a = w_ref[:, src_a:src_a + LANES]
            b = w_ref[:, src_b:src_b + LANES]
            if j % 2 == 1:
                a = pltpu.roll(a, HEAD_DIM, 1)
            else:
                b = pltpu.roll(b, HEAD_DIM, 1)
            wb_ref[:, j * LANES:(j + 1) * LANES] = jnp.where(low, a, b).astype(BF16)
        wb_ref[:, Q_W:] = w_ref[:, Q_W:].astype(BF16)

    shift, scale_mod = _mod_rows(mod_ref, pl.program_id(0) // tiles_per_seq, 0, 2)
    h = _norm_modulate(x_ref[...], g_ref[...], shift, scale_mod).astype(BF16)
    cos_t = _token_table(rcos_ref[...], ccos_ref[...])
    sin_t = _token_table(rsin_ref[...], csin_ref[...])
    lane = lax.broadcasted_iota(jnp.int32, cos_t.shape, 1)
    first_half = (lane % 32) < 16
    scale = HEAD_DIM ** -0.5 * LOG2E

    def project(col0, width):
        return jnp.dot(h, wb_ref[:, col0:col0 + width], preferred_element_type=F32)

    for c in range(0, Q_W, MXU_DIM):
        p = project(c, MXU_DIM)
        for b in range(MXU_DIM // LANES):
            blk = p[:, b * LANES:(b + 1) * LANES]
            q_ref[:, c + b * LANES:c + (b + 1) * LANES] = (
                _rope(blk, cos_t, sin_t, first_half) * scale).astype(BF16)
    p = project(Q_W, 2 * KV_W)
    k_ref[...] = _rope(p[:, :KV_W], cos_t, sin_t, first_half).astype(BF16)
    v_ref[...] = p[:, KV_W:].T.astype(BF16)
    for c in range(0, FOURIER_W, MXU_DIM):
        f_ref[:, c:c + MXU_DIM] = project(Q_W + 2 * KV_W + c, MXU_DIM)


def _row_slab_rows(n_rows, n_steps):
    rows = n_rows // n_steps
    assert rows * n_steps == n_rows and rows % BF16_ROWS == 0
    return rows


def _inproj(x2, mods, g_pre, w_in, w_pa, w_pf, w_out, rope_tabs, seq):
    n = x2.shape[0]
    tm = TOKEN_TILE
    n_steps = n // tm
    tiles_per_seq = seq // tm
    rows_per_tile = tm // GRID_W
    n_cols = Q_W + 2 * KV_W + FOURIER_W
    row_spec = pl.BlockSpec((rows_per_tile, LANES), lambda i: (i % tiles_per_seq, 0))
    g_rows = _row_slab_rows(D_MODEL, n_steps)
    pa_rows = _row_slab_rows(w_pa.shape[0], n_steps)
    pf_rows = _row_slab_rows(w_pf.shape[0], n_steps)
    out_rows = _row_slab_rows(w_out.shape[0], n_steps)
    slab = lambda rows, cols: pl.BlockSpec((rows, cols), lambda i: (i, 0))
    return pl.pallas_call(
        functools.partial(_inproj_kernel, tiles_per_seq=tiles_per_seq),
        grid=(n // tm,),
        in_specs=[
            pl.BlockSpec((tm, D_MODEL), lambda i: (i, 0)),
            _const_spec(mods.shape),
            _const_spec((1, D_MODEL)),
            _const_spec((D_MODEL, n_cols)),
            row_spec, row_spec,
            _const_spec((GRID_W, LANES)), _const_spec((GRID_W, LANES)),
            pl.BlockSpec((pl.Element(g_rows), pl.Element(2 * D_MODEL)),
                         lambda i: (i * g_rows, n_cols)),
            slab(pa_rows, D_MODEL), slab(pf_rows, D_MODEL), slab(out_rows, D_MODEL),
        ],
        out_specs=[
            pl.BlockSpec((tm, Q_W), lambda i: (i, 0)),
            pl.BlockSpec((tm, KV_W), lambda i: (i, 0)),
            pl.BlockSpec((KV_W, tm), lambda i: (0, i)),
            pl.BlockSpec((tm, FOURIER_W), lambda i: (i, 0)),
            slab(g_rows, 2 * D_MODEL), slab(pa_rows, D_MODEL), slab(pf_rows, D_MODEL), slab(out_rows, D_MODEL),
        ],
        out_shape=[
            jax.ShapeDtypeStruct((n, Q_W), BF16),
            jax.ShapeDtypeStruct((n, KV_W), BF16),
            jax.ShapeDtypeStruct((KV_W, n), BF16),
            jax.ShapeDtypeStruct((n, FOURIER_W), F32),
            jax.ShapeDtypeStruct((D_MODEL, 2 * D_MODEL), BF16),
            jax.ShapeDtypeStruct(w_pa.shape, BF16),
            jax.ShapeDtypeStruct(w_pf.shape, BF16),
            jax.ShapeDtypeStruct(w_out.shape, BF16),
        ],
        scratch_shapes=[pltpu.VMEM((D_MODEL, n_cols), BF16)],
        compiler_params=_params(1),
        name="inproj",
    )(x2, mods, g_pre, w_in, *rope_tabs, w_in, w_pa, w_pf, w_out)


def _ctxkv_kernel(x_ref, mod_ref, g_ref, w_ref, k_ref, v_ref, *, row):
    shift, scale_mod = _mod_rows(mod_ref, row, 0, 2)
    h = _norm_modulate(x_ref[...], g_ref[...], shift, scale_mod)
    p = jnp.dot(h.astype(BF16), w_ref[...].astype(BF16), preferred_element_type=F32)
    k_ref[...] = p[:, :KV_W].astype(BF16)
    v_ref[...] = p[:, KV_W:].T.astype(BF16)


def _ctxkv(ctx2, mods, ctx_row, g_pre, w_in, ctx_len):
    n = ctx2.shape[0]
    assert Q_W % (2 * KV_W) == 0
    return pl.pallas_call(
        functools.partial(_ctxkv_kernel, row=ctx_row),
        grid=(n // ctx_len,),
        in_specs=[
            pl.BlockSpec((ctx_len, D_MODEL), lambda i: (i, 0)),
            _const_spec(mods.shape),
            _const_spec((1, D_MODEL)),
            pl.BlockSpec((D_MODEL, 2 * KV_W), lambda i: (0, Q_W // (2 * KV_W)),
                         pipeline_mode=pl.Buffered(1)),
        ],
        out_specs=[
            pl.BlockSpec((ctx_len, KV_W), lambda i: (i, 0)),
            pl.BlockSpec((KV_W, ctx_len), lambda i: (0, i)),
        ],
        out_shape=[
            jax.ShapeDtypeStruct((n, KV_W), BF16),
            jax.ShapeDtypeStruct((KV_W, n), BF16),
        ],
        compiler_params=_params(1),
        name="ctxkv",
    )(ctx2, mods, g_pre, w_in)


def _attend_block(sink_ref, q_blk, k_parts, v_parts, bias_prev, bias_next):
    kcat = jnp.concatenate(k_parts, axis=0)
    vcat_t = jnp.concatenate(v_parts, axis=1)
    n_keys = kcat.shape[0]
    low = lax.broadcasted_iota(jnp.int32, (BLOCK, LANES), 1) < HEAD_DIM
    top = lax.broadcasted_iota(jnp.int32, (KV_W, n_keys), 0) < HEAD_DIM
    zero = jnp.zeros((), BF16)
    one = jnp.ones((), BF16)
    q_rows = [jnp.where(low if h == 0 else jnp.logical_not(low), q_blk[:, g * LANES:(g + 1) * LANES], zero)
              for h in range(N_KV_HEADS) for g in range(GROUP)]
    s_all = lax.dot_general(kcat, jnp.concatenate(q_rows, axis=0), (((1,), (1,)), ((), ())),
                            preferred_element_type=F32).astype(BF16)
    outs = []
    for h in range(N_KV_HEADS):
        v_h = jnp.where(top if h == 0 else jnp.logical_not(top), vcat_t, one)
        for g0 in range(0, GROUP, 2):
            probs = []
            sinks = []
            for g in (g0, g0 + 1):
                head = h * GROUP + g
                s = s_all[:, head * LANES:(head + 1) * LANES]
                parts = [s[0:BLOCK] + bias_prev,
                         s[BLOCK:2 * BLOCK],
                         s[2 * BLOCK:3 * BLOCK] + bias_next]
                parts += [s[r:r + BLOCK] for r in range(3 * BLOCK, n_keys, BLOCK)]
                mx = parts[0]
                for part in parts[1:]:
                    mx = jnp.maximum(mx, part)
                sink = sink_ref[head] * LOG2E
                m = jnp.maximum(jnp.max(mx, axis=0, keepdims=True), sink.astype(BF16))
                probs.append(jnp.concatenate([jnp.exp2(part - m) for part in parts], axis=0))
                sinks.append(jnp.exp2(sink - m.astype(F32)))
            o2 = jnp.dot(v_h, jnp.concatenate(probs, axis=1), preferred_element_type=F32)
            for idx in range(2):
                o_t = o2[:, idx * LANES:(idx + 1) * LANES]
                num = o_t[0:HEAD_DIM] if h == 0 else o_t[HEAD_DIM:]
                den = (o_t[HEAD_DIM:] if h == 0 else o_t[0:HEAD_DIM]) + sinks[idx]
                outs.append(num / den)
    return outs


def _attention_steps(sink_ref, q_ref, kp_ref, kc_ref, kn_ref, kx_ref, vp_ref, vc_ref, vn_ref, vx_ref,
                     bias_ref, o_ref):
    i = pl.program_id(1)
    n_sub = q_ref.shape[0] // BLOCK
    first_var = jnp.where(i == 0, 0, 1)
    last_var = jnp.where(i == pl.num_programs(1) - 1, 2, 1)

    def block_step(t):
        own = slice(t * BLOCK, (t + 1) * BLOCK)
        before = slice((t - 1) * BLOCK, t * BLOCK)
        after = slice((t + 1) * BLOCK, (t + 2) * BLOCK)
        k_parts = [kp_ref[...] if t == 0 else kc_ref[before, :], kc_ref[own, :],
                   kn_ref[...] if t == n_sub - 1 else kc_ref[after, :], kx_ref[...]]
        v_parts = [vp_ref[...] if t == 0 else vc_ref[:, before], vc_ref[:, own],
                   vn_ref[...] if t == n_sub - 1 else vc_ref[:, after], vx_ref[...]]
        bias_prev = bias_ref[first_var if t == 0 else 1, 0:BLOCK, :]
        bias_next = bias_ref[last_var if t == n_sub - 1 else 1, BLOCK:2 * BLOCK, :]
        outs = _attend_block(sink_ref, q_ref[own, :], k_parts, v_parts, bias_prev, bias_next)
        for j in range(Q_W // LANES):
            o_t = jnp.concatenate([outs[2 * j], outs[2 * j + 1]], axis=0)
            o_ref[own, j * LANES:(j + 1) * LANES] = o_t.T.astype(BF16)

    return [functools.partial(block_step, t) for t in range(n_sub)]


def _attention_specs(batch, seq, ctx_len, bias_shape):
    nb = seq // BLOCK
    sub = ATTN_BLOCKS_PER_STEP
    steps = nb // sub

    def cur(b, i):
        return (b * steps + i, 0)

    def prev(b, i):
        return (b * nb + jnp.maximum(i * sub - 1, 0), 0)

    def nxt(b, i):
        return (b * nb + jnp.minimum((i + 1) * sub, nb - 1), 0)

    def swap(f):
        return lambda b, i: f(b, i)[::-1]

    in_specs = [
        pl.BlockSpec(memory_space=pltpu.SMEM),
        pl.BlockSpec((sub * BLOCK, Q_W), cur),
        pl.BlockSpec((BLOCK, KV_W), prev),
        pl.BlockSpec((sub * BLOCK, KV_W), cur),
        pl.BlockSpec((BLOCK, KV_W), nxt),
        pl.BlockSpec((ctx_len, KV_W), lambda b, i: (b, 0)),
        pl.BlockSpec((KV_W, BLOCK), swap(prev)),
        pl.BlockSpec((KV_W, sub * BLOCK), swap(cur)),
        pl.BlockSpec((KV_W, BLOCK), swap(nxt)),
        pl.BlockSpec((KV_W, ctx_len), lambda b, i: (0, b)),
        _const_spec(bias_shape),
    ]
    return steps, in_specs, pl.BlockSpec((sub * BLOCK, Q_W), cur)


def _band_bias():
    qi = np.arange(BLOCK)[None, :]
    kj = np.arange(3 * BLOCK)[:, None]
    rel = kj - BLOCK - qi
    in_window = np.abs(rel) <= WINDOW
    variants = []
    for has_prev, has_next in ((False, True), (True, True), (True, False)):
        ok = in_window.copy()
        if not has_prev:
            ok &= kj >= BLOCK
        if not has_next:
            ok &= kj < 2 * BLOCK
        mask = np.where(ok, 0.0, NEG).astype(np.float32)
        variants.append(np.concatenate([mask[:BLOCK], mask[2 * BLOCK:]], axis=0))
    return jnp.asarray(np.stack(variants)).astype(BF16)


def _minor_dft_terms():
    assert DFT_MINOR == 8
    plan = []
    for k2 in range(DFT_MINOR):
        unit, half = [], []
        for s2 in range(DFT_MINOR):
            ang = -2.0 * math.pi * ((s2 * k2) % DFT_MINOR) / DFT_MINOR
            for part, coef in ((0, math.cos(ang)), (1, -math.sin(ang))):
                if abs(coef) < 1e-9:
                    continue
                target = unit if abs(abs(coef) - 1.0) < 1e-9 else half
                assert target is unit or abs(abs(coef) - math.sqrt(0.5)) < 1e-9
                target.append((s2, part, 1 if coef > 0 else -1))
        plan.append((unit, half))
    return plan


def _signed_sum(terms, pick):
    pos = [pick(s2, part) for s2, part, sign in terms if sign > 0]
    neg = [pick(s2, part) for s2, part, sign in terms if sign < 0]
    total = None
    for v in pos:
        total = v if total is None else total + v
    for v in neg:
        total = -v if total is None else total - v
    return total


def _fourier_steps(f_ref, f1_ref, tr_ref, ti_ref, cm_ref, o_ref):
    n_major = f_ref.shape[0] // DFT_MINOR
    pair_w = 2 * LANES
    n_pairs = DFT_MINOR // 2
    row_chunk = n_major // n_pairs
    twiddled = [None] * n_pairs
    spectra = [None] * DFT_MINOR
    plan = _minor_dft_terms()

    def stage1(c):
        y = jnp.concatenate(
            [f_ref[pl.ds(s2, n_major, stride=DFT_MINOR), :].astype(BF16) for s2 in (2 * c, 2 * c + 1)],
            axis=1)
        a = jnp.dot(f1_ref[...], y, preferred_element_type=F32)
        ar = a[:n_major]
        ai = a[n_major:]
        tr = tr_ref[:, c * pair_w:(c + 1) * pair_w]
        ti = ti_ref[:, c * pair_w:(c + 1) * pair_w]
        twiddled[c] = ((ar * tr - ai * ti).astype(BF16), (ar * ti + ai * tr).astype(BF16))

    def channels(c):
        apr, api = twiddled[c]
        for t in range(2):
            lanes = slice(t * LANES, (t + 1) * LANES)
            ap = jnp.concatenate([apr[:, lanes], api[:, lanes]], axis=1)
            spectra[2 * c + t] = jnp.dot(ap, cm_ref[...], preferred_element_type=F32)

    def stage2(r):
        rows = slice(r * row_chunk, (r + 1) * row_chunk)

        def pick(s2, part):
            return spectra[s2][rows, part * LANES:(part + 1) * LANES]

        for k2, (unit, half) in enumerate(plan):
            res = _signed_sum(unit, pick)
            if half:
                scaled = _signed_sum(half, pick) * math.sqrt(0.5)
                res = scaled if res is None else res + scaled
            o_ref[pl.ds(k2 * n_major + r * row_chunk, row_chunk), :] = res.astype(BF16)

    s1 = lambda c: (functools.partial(stage1, c), 4)
    ch = lambda c: (functools.partial(channels, c), 3)
    steps = [s1(0)]
    for c in range(1, n_pairs):
        steps += [s1(c), ch(c - 1)]
    steps.append(ch(n_pairs - 1))
    return steps + [(functools.partial(stage2, r), 1) for r in range(n_pairs)]


def _cis(num, den):
    ang = (num % den).astype(F32) * F32(-2.0 * math.pi / den)
    return jnp.cos(ang), jnp.sin(ang)


def _dft_tables(seq):
    n_major = seq // DFT_MINOR
    ch = FOURIER_GROUP_CH
    root = int(round(math.sqrt(n_major)))
    assert root * root == n_major
    s1 = jnp.arange(n_major, dtype=jnp.int32)[None, :]
    hi = jnp.arange(root, dtype=jnp.int32)[:, None] * root
    lo = jnp.arange(root, dtype=jnp.int32)[:, None]
    hr, hi_ = _cis(hi * s1, n_major)
    lr, li = _cis(lo * s1, n_major)
    p1 = jnp.stack([hr, hr])[:, :, None, :]
    q1 = jnp.stack([lr, li])[:, None, :, :]
    p2 = jnp.stack([-hi_, hi_])[:, :, None, :]
    q2 = jnp.stack([li, lr])[:, None, :, :]
    f1 = (p1 * q1 + p2 * q2).astype(BF16).reshape(2 * n_major, n_major)
    k1 = jnp.arange(n_major, dtype=jnp.int32)[:, None]
    s2 = jnp.arange(DFT_MINOR, dtype=jnp.int32)[None, :]
    tr, ti = _cis(k1 * s2, seq)
    tr = jnp.broadcast_to(tr[:, :, None], (n_major, DFT_MINOR, ch)).reshape(n_major, DFT_MINOR * ch)
    ti = jnp.broadcast_to(ti[:, :, None], (n_major, DFT_MINOR, ch)).reshape(n_major, DFT_MINOR * ch)
    c128 = jnp.arange(ch, dtype=jnp.int32)
    cr, ci = _cis(c128[:, None] * c128[None, :], ch)
    norm = F32(1.0 / math.sqrt(seq * ch))
    cm = (jnp.concatenate([jnp.concatenate([cr, ci], axis=1),
                           jnp.concatenate([-ci, cr], axis=1)], axis=0) * norm).astype(BF16)
    return f1, tr, ti, cm


def _mix_kernel(sink_ref, q_ref, kp_ref, kc_ref, kn_ref, kx_ref, vp_ref, vc_ref, vn_ref, vx_ref, bias_ref,
                f_ref, f1_ref, tr_ref, ti_ref, cm_ref, attn_ref, four_ref):
    attn_steps = _attention_steps(sink_ref, q_ref, kp_ref, kc_ref, kn_ref, kx_ref,
                                  vp_ref, vc_ref, vn_ref, vx_ref, bias_ref, attn_ref)
    four_steps = _fourier_steps(f_ref, f1_ref, tr_ref, ti_ref, cm_ref, four_ref)
    total = sum(weight for _, weight in four_steps)
    done = 0
    seen = 0
    for four_step, weight in four_steps:
        seen += weight
        upto = seen * len(attn_steps) // total
        for attn_step in attn_steps[done:upto]:
            attn_step()
        done = upto
        four_step()
    assert done == len(attn_steps)


def _mix(sink, q, k, v_t, kx, vx_t, bias, f3, tables, ctx_len):
    batch, seq, _ = f3.shape
    f1, tr, ti, cm = tables
    ch = FOURIER_GROUP_CH
    steps, attn_in, attn_out = _attention_specs(batch, seq, ctx_len, bias.shape)
    assert steps == N_FOURIER_GROUPS
    four_spec = pl.BlockSpec((None, seq, ch), lambda b, g: (b, 0, g))
    return pl.pallas_call(
        _mix_kernel,
        grid=(batch, steps),
        in_specs=attn_in + [four_spec, _const_spec(f1.shape), _const_spec(tr.shape),
                            _const_spec(ti.shape), _const_spec(cm.shape)],
        out_specs=[attn_out, four_spec],
        out_shape=[jax.ShapeDtypeStruct((batch * seq, Q_W), BF16),
                   jax.ShapeDtypeStruct((batch, seq, FOURIER_W), BF16)],
        compiler_params=_params(2),
        name="mix",
    )(sink, q, k, k, k, kx, v_t, v_t, v_t, vx_t, bias, f3, f1, tr, ti, cm)


def _merge_kernel(x_ref, a_ref, f_ref, mod_ref, gpre_ref, gpost_ref,
                  wg_b, wpa_b, wpf_b, wout_b, wup_ref, wdown_ref,
                  o_ref, wup_o, wdown_o, *, tiles_per_seq):
    wup_o[...] = wup_ref[...].astype(BF16)
    wdown_o[...] = wdown_ref[...].astype(BF16)

    x = x_ref[...]
    shift, scale_mod, out_gate = _mod_rows(mod_ref, pl.program_id(0) // tiles_per_seq, 0, 3)
    h = _norm_modulate(x, gpre_ref[...], shift, scale_mod).astype(BF16)
    a = a_ref[...]
    f = f_ref[...]
    y = None
    for c in range(0, D_MODEL, MXU_DIM):
        cols = slice(c, c + MXU_DIM)
        gate_a = jnp.dot(h, wg_b[:, cols], preferred_element_type=F32)
        gate_f = jnp.dot(h, wg_b[:, D_MODEL + c:D_MODEL + c + MXU_DIM], preferred_element_type=F32)
        pa = jnp.dot(a, wpa_b[:, cols], preferred_element_type=F32)
        pf = jnp.dot(f, wpf_b[:, cols], preferred_element_type=F32)
        m = (jax.nn.sigmoid(gate_a) * pa + jax.nn.sigmoid(gate_f) * pf).astype(BF16)
        part = jnp.dot(m, wout_b[cols, :], preferred_element_type=F32)
        y = part if y is None else y + part
    o_ref[...] = x + out_gate * _post_norm(y, gpost_ref[...])


def _merge(x2, attn, four, mods, g_pre, g_post, w_g, w_pa, w_pf, w_out, w_up, w_down, seq):
    n = x2.shape[0]
    tm = TOKEN_TILE
    n_steps = n // tm
    tiles_per_seq = seq // tm
    row = lambda i: (i, 0)
    up_rows = w_up.shape[0] // n_steps
    down_rows = w_down.shape[0] // BF16_ROWS
    n_down = w_down.shape[0] // down_rows
    assert up_rows * n_steps == w_up.shape[0] and up_rows % BF16_ROWS == 0
    assert down_rows % BF16_ROWS == 0 and n_down <= n_steps
    up_spec = pl.BlockSpec((up_rows, w_up.shape[1]), row)
    down_spec = pl.BlockSpec((down_rows, w_down.shape[1]), lambda i: (jnp.minimum(i, n_down - 1), 0))
    return pl.pallas_call(
        functools.partial(_merge_kernel, tiles_per_seq=tiles_per_seq),
        grid=(n // tm,),
        in_specs=[
            pl.BlockSpec((tm, D_MODEL), row),
            pl.BlockSpec((tm, Q_W), row),
            pl.BlockSpec((tm, FOURIER_W), row),
            _const_spec(mods.shape),
            _const_spec((1, D_MODEL)),
            _const_spec((1, D_MODEL)),
            _const_spec(w_g.shape),
            _const_spec(w_pa.shape),
            _const_spec(w_pf.shape),
            _const_spec(w_out.shape),
            up_spec,
            down_spec,
        ],
        out_specs=[pl.BlockSpec((tm, D_MODEL), row), up_spec, down_spec],
        out_shape=[jax.ShapeDtypeStruct((n, D_MODEL), F32),
                   jax.ShapeDtypeStruct(w_up.shape, BF16),
                   jax.ShapeDtypeStruct(w_down.shape, BF16)],
        compiler_params=_params(1),
        name="merge",
    )(x2, attn, four, mods, g_pre, g_post, w_g, w_pa, w_pf, w_out, w_up, w_down)


def _convffn_kernel(x_ref, xp_ref, xn_ref, mod_ref, gpre_ref, gpost_ref,
                    wu_ref, wgate_ref, cw_ref, cb_ref, wd_ref, o_ref, act_ref, *, tiles_per_seq):
    i = pl.program_id(0)
    tm = x_ref.shape[0]
    halo = SUBLANES
    has_prev = (i % tiles_per_seq != 0).astype(F32)
    has_next = (i % tiles_per_seq != tiles_per_seq - 1).astype(F32)
    x = x_ref[...]
    shift, scale, out_gate = _mod_rows(mod_ref, i // tiles_per_seq, 3, 3)
    gain = gpre_ref[...]
    h = _norm_modulate(x, gain, shift, scale)
    hp = _norm_modulate(xp_ref[...], gain, shift, scale) * has_prev
    hn = _norm_modulate(xn_ref[...], gain, shift, scale) * has_next
    h_ext = jnp.concatenate([hp, h, hn], axis=0).astype(BF16)
    h_mid = h.astype(BF16)
    n_ext = tm + 2 * halo
    for c in range(D_FF // FFN_CHUNK):
        cols = slice(c * FFN_CHUNK, (c + 1) * FFN_CHUNK)
        u = jnp.dot(h_ext, wu_ref[:, cols], preferred_element_type=F32)
        gate = jnp.dot(h_mid, wgate_ref[:, cols], preferred_element_type=F32)
        u_prev = pltpu.roll(u, 1, 0)[halo:halo + tm]
        u_next = pltpu.roll(u, n_ext - 1, 0)[halo:halo + tm]
        conv = (u_prev * cw_ref[0:1, cols] + u[halo:halo + tm] * cw_ref[1:2, cols]
                + u_next * cw_ref[2:3, cols] + cb_ref[:, cols])
        act_ref[:, cols] = (conv * jax.nn.sigmoid(conv) * gate).astype(BF16)
    y = jnp.dot(act_ref[...], wd_ref[...], preferred_element_type=F32)
    o_ref[...] = x + out_gate * _post_norm(y, gpost_ref[...])


def _convffn(x1, mods, g_pre, g_post, w_up, conv_w, conv_b, w_down, seq):
    n = x1.shape[0]
    tm = TOKEN_TILE
    tiles_per_seq = seq // tm
    halo_blocks_per_tile = tm // SUBLANES
    n_halo_blocks = n // SUBLANES
    return pl.pallas_call(
        functools.partial(_convffn_kernel, tiles_per_seq=tiles_per_seq),
        grid=(n // tm,),
        in_specs=[
            pl.BlockSpec((tm, D_MODEL), lambda i: (i, 0)),
            pl.BlockSpec((SUBLANES, D_MODEL),
                         lambda i: (jnp.maximum(i * halo_blocks_per_tile - 1, 0), 0)),
            pl.BlockSpec((SUBLANES, D_MODEL),
                         lambda i: (jnp.minimum((i + 1) * halo_blocks_per_tile, n_halo_blocks - 1), 0)),
            _const_spec(mods.shape),
            _const_spec((1, D_MODEL)),
            _const_spec((1, D_MODEL)),
            pl.BlockSpec((D_MODEL, D_FF), lambda i: (0, 0), pipeline_mode=pl.Buffered(1)),
            pl.BlockSpec((D_MODEL, D_FF), lambda i: (0, 1), pipeline_mode=pl.Buffered(1)),
            _const_spec(conv_w.shape),
            _const_spec((1, D_FF)),
            _const_spec(w_down.shape),
        ],
        out_specs=pl.BlockSpec((tm, D_MODEL), lambda i: (i, 0)),
        out_shape=jax.ShapeDtypeStruct((n, D_MODEL), F32),
        scratch_shapes=[pltpu.VMEM((tm, D_FF), BF16)],
        compiler_params=_params(1),
        name="convffn",
    )(x1, x1, x1, mods, g_pre, g_post, w_up, w_up, conv_w, conv_b.reshape(1, D_FF), w_down)


def _rope_tables(seq):
    half = HEAD_DIM // 2
    inv_freq = ROPE_THETA ** (-jnp.arange(0, half, 2, dtype=F32) / half)
    ang_r = jnp.arange(seq // GRID_W).astype(F32)[:, None] * inv_freq
    ang_c = jnp.arange(GRID_W).astype(F32)[:, None] * inv_freq
    reps = LANES // HEAD_DIM

    def lanes(row_part, col_part):
        return jnp.tile(jnp.concatenate([row_part, col_part], axis=-1), (1, reps))

    zr = jnp.zeros((ang_r.shape[0], half), F32)
    zc = jnp.zeros((ang_c.shape[0], half), F32)
    rcos = lanes(jnp.concatenate([jnp.cos(ang_r), jnp.cos(ang_r)], axis=-1), zr)
    rsin = lanes(jnp.concatenate([-jnp.sin(ang_r), jnp.sin(ang_r)], axis=-1), zr)
    ccos = lanes(zc, jnp.concatenate([jnp.cos(ang_c), jnp.cos(ang_c)], axis=-1))
    csin = lanes(zc, jnp.concatenate([-jnp.sin(ang_c), jnp.sin(ang_c)], axis=-1))
    return rcos, rsin, ccos, csin


def kernel(x, c, ctx, c_ctx, w_mod, b_mod, g_pre1, g_post1, g_pre2, g_post2,
           w_in, sink, w_pa, w_pf, w_out, w_up, conv_w, conv_b, w_down):
    batch, seq, d = x.shape
    ctx_len = ctx.shape[1]
    depth = w_mod.shape[0]
    assert depth == 1 and d == D_MODEL and batch + 1 <= SUBLANES
    assert seq % TOKEN_TILE == 0 and TOKEN_TILE % GRID_W == 0
    assert seq % (DFT_MINOR * SUBLANES) == 0
    assert seq % (ATTN_BLOCKS_PER_STEP * BLOCK) == 0
    n = batch * seq
    rope_tabs = _rope_tables(seq)
    tables = _dft_tables(seq)
    bias = _band_bias()

    l = 0
    mods = _adaln(c, c_ctx.reshape(1, d), w_mod[l], b_mod[l])

    g_pre1_row = g_pre1[l].reshape(1, d)
    g_post1_row = g_post1[l].reshape(1, d)
    g_pre2_row = g_pre2[l].reshape(1, d)
    g_post2_row = g_post2[l].reshape(1, d)

    x2 = x.reshape(n, d)
    q, k, v_t, f, w_g_b, w_pa_b, w_pf_b, w_out_b = _inproj(x2, mods, g_pre1_row, w_in[l], w_pa[l], w_pf[l], w_out[l],
                                                          rope_tabs, seq)
    kx, vx_t = _ctxkv(ctx.reshape(batch * ctx_len, d), mods, batch, g_pre1_row, w_in[l], ctx_len)
    attn, four = _mix(sink[l], q, k, v_t, kx, vx_t, bias, f.reshape(batch, seq, FOURIER_W), tables, ctx_len)
    four = four.reshape(n, FOURIER_W)
    x1, w_up_b, w_down_b = _merge(x2, attn, four, mods, g_pre1_row, g_post1_row, w_g_b, w_pa_b, w_pf_b, w_out_b,
                                  w_up[l], w_down[l], seq)
    out = _convffn(x1, mods, g_pre2_row, g_post2_row, w_up_b, conv_w[l], conv_b[l], w_down_b, seq)
    return out.reshape(batch, seq, d)
```

```python
import functools
import math

import numpy as np
import jax
import jax.numpy as jnp
from jax import lax
from jax.experimental import pallas as pl
from jax.experimental.pallas import tpu as pltpu

F32 = jnp.float32
BF16 = jnp.bfloat16

D_MODEL = 1024
GRID_W = 64
HEAD_DIM = 64
N_Q_HEADS = 8
N_KV_HEADS = 2
GROUP = N_Q_HEADS // N_KV_HEADS
WINDOW = 128
BLOCK = 128
ROPE_THETA = 10000.0
N_FOURIER_GROUPS = 4
FOURIER_GROUP_CH = 128
FOURIER_W = N_FOURIER_GROUPS * FOURIER_GROUP_CH
Q_W = N_Q_HEADS * HEAD_DIM
KV_W = N_KV_HEADS * HEAD_DIM
D_FF = 2816
EPS = 1e-6
NEG = -1e30
LOG2E = math.log2(math.e)

LANES = 128
SUBLANES = 8
BF16_ROWS = 16
MXU_DIM = 256
VMEM_LIMIT_BYTES = 56 * 1024 * 1024

TOKEN_TILE = 1024
ADALN_BUFFERS = 3
FFN_CHUNK = MXU_DIM
DFT_MINOR = SUBLANES
ATTN_BLOCKS_PER_STEP = 16


def _const_spec(shape):
    nd = len(shape)
    return pl.BlockSpec(shape, lambda *_: (0,) * nd, pipeline_mode=pl.Buffered(1))


def _params(n_axes):
    return pltpu.CompilerParams(
        dimension_semantics=("arbitrary",) * n_axes,
        vmem_limit_bytes=VMEM_LIMIT_BYTES,
    )


def _norm_modulate(x, gain, shift, scale):
    ms = jnp.mean(x * x, axis=-1, keepdims=True)
    return (x * lax.rsqrt(ms + EPS)) * (gain * (1.0 + scale)) + shift


def _mod_rows(mod_ref, row, first, count):
    return [mod_ref[pl.ds(row, 1), (first + j) * D_MODEL:(first + j + 1) * D_MODEL] for j in range(count)]


def _post_norm(y, gain):
    ms = jnp.mean(y * y, axis=-1, keepdims=True)
    return (y * lax.rsqrt(ms + EPS)) * gain


def _adaln_kernel(c_ref, cctx_ref, w_hbm, b_hbm, o_hbm, rows_ref, *, tn):
    batch = c_ref.shape[0]
    rows_ref[...] = jnp.zeros(rows_ref.shape, F32)
    rows_ref[0:batch, :] = c_ref[...]
    rows_ref[batch:batch + 1, :] = cctx_ref[...]
    c = rows_ref[...]
    rows_ref[...] = c * jax.nn.sigmoid(c)

    def block(w_ref, b_ref, o_ref):
        o_ref[...] = jnp.dot(rows_ref[...], w_ref[...], preferred_element_type=F32) + b_ref[...]

    n_out = w_hbm.shape[1]
    pltpu.emit_pipeline(
        block,
        grid=(n_out // tn,),
        in_specs=[pl.BlockSpec((D_MODEL, tn), lambda j: (0, j), pipeline_mode=pl.Buffered(ADALN_BUFFERS)),
                  pl.BlockSpec((1, tn), lambda j: (0, j))],
        out_specs=[pl.BlockSpec((SUBLANES, tn), lambda j: (0, j))],
    )(w_hbm, b_hbm, o_hbm)


def _adaln(c, c_ctx, w_mod, b_mod):
    n_out = w_mod.shape[1]
    tn = 1536
    return pl.pallas_call(
        functools.partial(_adaln_kernel, tn=tn),
        in_specs=[
            pl.BlockSpec(memory_space=pltpu.VMEM),
            pl.BlockSpec(memory_space=pltpu.VMEM),
            pl.BlockSpec(memory_space=pl.ANY),
            pl.BlockSpec(memory_space=pl.ANY),
        ],
        out_specs=pl.BlockSpec(memory_space=pl.ANY),
        out_shape=jax.ShapeDtypeStruct((SUBLANES, n_out), F32),
        scratch_shapes=[pltpu.VMEM((SUBLANES, D_MODEL), F32)],
        compiler_params=pltpu.CompilerParams(vmem_limit_bytes=VMEM_LIMIT_BYTES),
        name="adaln",
    )(c, c_ctx, w_mod, b_mod.reshape(1, n_out))


def _rope(xb, cos_t, sin_t, first_half):
    sw = jnp.where(first_half, pltpu.roll(xb, LANES - 16, 1), pltpu.roll(xb, 16, 1))
    return xb * cos_t + sw * sin_t


def _token_table(row_tab, col_tab):
    n_rows = row_tab.shape[0]
    by_row = jnp.concatenate(
        [jnp.broadcast_to(row_tab[r:r + 1, :], (GRID_W, LANES)) for r in range(n_rows)], axis=0)
    return by_row + jnp.concatenate([col_tab] * n_rows, axis=0)


def _inproj_kernel(x_ref, mod_ref, g_ref, w_ref, rcos_ref, rsin_ref, ccos_ref, csin_ref,
                   wg_ref, wpa_ref, wpf_ref, wout_ref,
                   q_ref, k_ref, v_ref, f_ref, wg_o, wpa_o, wpf_o, wout_o, wb_ref, *, tiles_per_seq):
    wg_o[...] = wg_ref[...].astype(BF16)
    wpa_o[...] = wpa_ref[...].astype(BF16)
    wpf_o[...] = wpf_ref[...].astype(BF16)
    wout_o[...] = wout_ref[...].astype(BF16)

    @pl.when(pl.program_id(0) == 0)
    def _():
        low = lax.broadcasted_iota(jnp.int32, (D_MODEL, LANES), 1) < HEAD_DIM
        for j in range(Q_W // LANES):
            src_a = (j // 2) * LANES
            src_b = (GROUP // 2 + j // 2) * LANES
            a = w_ref[:, src_a:src_a + LANES]
            b = w_ref[:, src_b:src_b + LANES]
            if j % 2 == 1:
                a = pltpu.roll(a, HEAD_DIM, 1)
            else:
                b = pltpu.roll(b, HEAD_DIM, 1)
            wb_ref[:, j * LANES:(j + 1) * LANES] = jnp.where(low, a, b).astype(BF16)
        wb_ref[:, Q_W:] = w_ref[:, Q_W:].astype(BF16)

    shift, scale_mod = _mod_rows(mod_ref, pl.program_id(0) // tiles_per_seq, 0, 2)
    h = _norm_modulate(x_ref[...], g_ref[...], shift, scale_mod).astype(BF16)
    cos_t = _token_table(rcos_ref[...], ccos_ref[...])
    sin_t = _token_table(rsin_ref[...], csin_ref[...])
    lane = lax.broadcasted_iota(jnp.int32, cos_t.shape, 1)
    first_half = (lane % 32) < 16
    scale = HEAD_DIM ** -0.5 * LOG2E

    def project(col0, width):
        return jnp.dot(h, wb_ref[:, col0:col0 + width], preferred_element_type=F32)

    for c in range(0, Q_W, MXU_DIM):
        p = project(c, MXU_DIM)
        for b in range(MXU_DIM // LANES):
            blk = p[:, b * LANES:(b + 1) * LANES]
            q_ref[:, c + b * LANES:c + (b + 1) * LANES] = (
                _rope(blk, cos_t, sin_t, first_half) * scale).astype(BF16)
    p = project(Q_W, 2 * KV_W)
    k_ref[...] = _rope(p[:, :KV_W], cos_t, sin_t, first_half).astype(BF16)
    v_ref[...] = p[:, KV_W:].T.astype(BF16)
    for c in range(0, FOURIER_W, MXU_DIM):
        f_ref[:, c:c + MXU_DIM] = project(Q_W + 2 * KV_W + c, MXU_DIM)


def _row_slab_rows(n_rows, n_steps):
    rows = n_rows // n_steps
    assert rows * n_steps == n_rows and rows % BF16_ROWS == 0
    return rows


def _inproj(x2, mods, g_pre, w_in, w_pa, w_pf, w_out, rope_tabs, seq):
    n = x2.shape[0]
    tm = TOKEN_TILE
    n_steps = n // tm
    tiles_per_seq = seq // tm
    rows_per_tile = tm // GRID_W
    n_cols = Q_W + 2 * KV_W + FOURIER_W
    row_spec = pl.BlockSpec((rows_per_tile, LANES), lambda i: (i % tiles_per_seq, 0))
    g_rows = _row_slab_rows(D_MODEL, n_steps)
    pa_rows = _row_slab_rows(w_pa.shape[0], n_steps)
    pf_rows = _row_slab_rows(w_pf.shape[0], n_steps)
    out_rows = _row_slab_rows(w_out.shape[0], n_steps)
    slab = lambda rows, cols: pl.BlockSpec((rows, cols), lambda i: (i, 0))
    return pl.pallas_call(
        functools.partial(_inproj_kernel, tiles_per_seq=tiles_per_seq),
        grid=(n // tm,),
        in_specs=[
            pl.BlockSpec((tm, D_MODEL), lambda i: (i, 0)),
            _const_spec(mods.shape),
            _const_spec((1, D_MODEL)),
            _const_spec((D_MODEL, n_cols)),
            row_spec, row_spec,
            _const_spec((GRID_W, LANES)), _const_spec((GRID_W, LANES)),
            pl.BlockSpec((pl.Element(g_rows), pl.Element(2 * D_MODEL)),
                         lambda i: (i * g_rows, n_cols)),
            slab(pa_rows, D_MODEL), slab(pf_rows, D_MODEL), slab(out_rows, D_MODEL),
        ],
        out_specs=[
            pl.BlockSpec((tm, Q_W), lambda i: (i, 0)),
            pl.BlockSpec((tm, KV_W), lambda i: (i, 0)),
            pl.BlockSpec((KV_W, tm), lambda i: (0, i)),
            pl.BlockSpec((tm, FOURIER_W), lambda i: (i, 0)),
            slab(g_rows, 2 * D_MODEL), slab(pa_rows, D_MODEL), slab(pf_rows, D_MODEL), slab(out_rows, D_MODEL),
        ],
        out_shape=[
            jax.ShapeDtypeStruct((n, Q_W), BF16),
            jax.ShapeDtypeStruct((n, KV_W), BF16),
            jax.ShapeDtypeStruct((KV_W, n), BF16),
            jax.ShapeDtypeStruct((n, FOURIER_W), F32),
            jax.ShapeDtypeStruct((D_MODEL, 2 * D_MODEL), BF16),
            jax.ShapeDtypeStruct(w_pa.shape, BF16),
            jax.ShapeDtypeStruct(w_pf.shape, BF16),
            jax.ShapeDtypeStruct(w_out.shape, BF16),
        ],
        scratch_shapes=[pltpu.VMEM((D_MODEL, n_cols), BF16)],
        compiler_params=_params(1),
        name="inproj",
    )(x2, mods, g_pre, w_in, *rope_tabs, w_in, w_pa, w_pf, w_out)


def _ctxkv_kernel(x_ref, mod_ref, g_ref, w_ref, k_ref, v_ref, *, row):
    shift, scale_mod = _mod_rows(mod_ref, row, 0, 2)
    h = _norm_modulate(x_ref[...], g_ref[...], shift, scale_mod)
    p = jnp.dot(h.astype(BF16), w_ref[...].astype(BF16), preferred_element_type=F32)
    k_ref[...] = p[:, :KV_W].astype(BF16)
    v_ref[...] = p[:, KV_W:].T.astype(BF16)


def _ctxkv(ctx2, mods, ctx_row, g_pre, w_in, ctx_len):
    n = ctx2.shape[0]
    assert Q_W % (2 * KV_W) == 0
    return pl.pallas_call(
        functools.partial(_ctxkv_kernel, row=ctx_row),
        grid=(n // ctx_len,),
        in_specs=[
            pl.BlockSpec((ctx_len, D_MODEL), lambda i: (i, 0)),
            _const_spec(mods.shape),
            _const_spec((1, D_MODEL)),
            pl.BlockSpec((D_MODEL, 2 * KV_W), lambda i: (0, Q_W // (2 * KV_W)),
                         pipeline_mode=pl.Buffered(1)),
        ],
        out_specs=[
            pl.BlockSpec((ctx_len, KV_W), lambda i: (i, 0)),
            pl.BlockSpec((KV_W, ctx_len), lambda i: (0, i)),
        ],
        out_shape=[
            jax.ShapeDtypeStruct((n, KV_W), BF16),
            jax.ShapeDtypeStruct((KV_W, n), BF16),
        ],
        compiler_params=_params(1),
        name="ctxkv",
    )(ctx2, mods, g_pre, w_in)


def _attend_block(sink_ref, q_blk, k_parts, v_parts, bias_prev, bias_next):
    kcat = jnp.concatenate(k_parts, axis=0)
    vcat_t = jnp.concatenate(v_parts, axis=1)
    n_keys = kcat.shape[0]
    low = lax.broadcasted_iota(jnp.int32, (BLOCK, LANES), 1) < HEAD_DIM
    top = lax.broadcasted_iota(jnp.int32, (KV_W, n_keys), 0) < HEAD_DIM
    zero = jnp.zeros((), BF16)
    one = jnp.ones((), BF16)
    q_rows = [jnp.where(low if h == 0 else jnp.logical_not(low), q_blk[:, g * LANES:(g + 1) * LANES], zero)
              for h in range(N_KV_HEADS) for g in range(GROUP)]
    s_all = lax.dot_general(kcat, jnp.concatenate(q_rows, axis=0), (((1,), (1,)), ((), ())),
                            preferred_element_type=F32).astype(BF16)
    outs = []
    for h in range(N_KV_HEADS):
        v_h = jnp.where(top if h == 0 else jnp.logical_not(top), vcat_t, one)
        for g0 in range(0, GROUP, 2):
            probs = []
            sinks = []
            for g in (g0, g0 + 1):
                head = h * GROUP + g
                s = s_all[:, head * LANES:(head + 1) * LANES]
                parts = [s[0:BLOCK] + bias_prev,
                         s[BLOCK:2 * BLOCK],
                         s[2 * BLOCK:3 * BLOCK] + bias_next]
                parts += [s[r:r + BLOCK] for r in range(3 * BLOCK, n_keys, BLOCK)]
                mx = parts[0]
                for part in parts[1:]:
                    mx = jnp.maximum(mx, part)
                sink = sink_ref[head] * LOG2E
                m = jnp.maximum(jnp.max(mx, axis=0, keepdims=True), sink.astype(BF16))
                probs.append(jnp.concatenate([jnp.exp2(part - m) for part in parts], axis=0))
                sinks.append(jnp.exp2(sink - m.astype(F32)))
            o2 = jnp.dot(v_h, jnp.concatenate(probs, axis=1), preferred_element_type=F32)
            for idx in range(2):
                o_t = o2[:, idx * LANES:(idx + 1) * LANES]
                num = o_t[0:HEAD_DIM] if h == 0 else o_t[HEAD_DIM:]
                den = (o_t[HEAD_DIM:] if h == 0 else o_t[0:HEAD_DIM]) + sinks[idx]
                outs.append(num / den)
    return outs


def _attention_steps(sink_ref, q_ref, kp_ref, kc_ref, kn_ref, kx_ref, vp_ref, vc_ref, vn_ref, vx_ref,
                     bias_ref, o_ref):
    i = pl.program_id(1)
    n_sub = q_ref.shape[0] // BLOCK
    first_var = jnp.where(i == 0, 0, 1)
    last_var = jnp.where(i == pl.num_programs(1) - 1, 2, 1)

    def block_step(t):
        own = slice(t * BLOCK, (t + 1) * BLOCK)
        before = slice((t - 1) * BLOCK, t * BLOCK)
        after = slice((t + 1) * BLOCK, (t + 2) * BLOCK)
        k_parts = [kp_ref[...] if t == 0 else kc_ref[before, :], kc_ref[own, :],
                   kn_ref[...] if t == n_sub - 1 else kc_ref[after, :], kx_ref[...]]
        v_parts = [vp_ref[...] if t == 0 else vc_ref[:, before], vc_ref[:, own],
                   vn_ref[...] if t == n_sub - 1 else vc_ref[:, after], vx_ref[...]]
        bias_prev = bias_ref[first_var if t == 0 else 1, 0:BLOCK, :]
        bias_next = bias_ref[last_var if t == n_sub - 1 else 1, BLOCK:2 * BLOCK, :]
        outs = _attend_block(sink_ref, q_ref[own, :], k_parts, v_parts, bias_prev, bias_next)
        for j in range(Q_W // LANES):
            o_t = jnp.concatenate([outs[2 * j], outs[2 * j + 1]], axis=0)
            o_ref[own, j * LANES:(j + 1) * LANES] = o_t.T.astype(BF16)

    return [functools.partial(block_step, t) for t in range(n_sub)]


def _attention_specs(batch, seq, ctx_len, bias_shape):
    nb = seq // BLOCK
    sub = ATTN_BLOCKS_PER_STEP
    steps = nb // sub

    def cur(b, i):
        return (b * steps + i, 0)

    def prev(b, i):
        return (b * nb + jnp.maximum(i * sub - 1, 0), 0)

    def nxt(b, i):
        return (b * nb + jnp.minimum((i + 1) * sub, nb - 1), 0)

    def swap(f):
        return lambda b, i: f(b, i)[::-1]

    in_specs = [
        pl.BlockSpec(memory_space=pltpu.SMEM),
        pl.BlockSpec((sub * BLOCK, Q_W), cur),
        pl.BlockSpec((BLOCK, KV_W), prev),
        pl.BlockSpec((sub * BLOCK, KV_W), cur),
        pl.BlockSpec((BLOCK, KV_W), nxt),
        pl.BlockSpec((ctx_len, KV_W), lambda b, i: (b, 0)),
        pl.BlockSpec((KV_W, BLOCK), swap(prev)),
        pl.BlockSpec((KV_W, sub * BLOCK), swap(cur)),
        pl.BlockSpec((KV_W, BLOCK), swap(nxt)),
        pl.BlockSpec((KV_W, ctx_len), lambda b, i: (0, b)),
        _const_spec(bias_shape),
    ]
    return steps, in_specs, pl.BlockSpec((sub * BLOCK, Q_W), cur)


def _band_bias():
    qi = np.arange(BLOCK)[None, :]
    kj = np.arange(3 * BLOCK)[:, None]
    rel = kj - BLOCK - qi
    in_window = np.abs(rel) <= WINDOW
    variants = []
    for has_prev, has_next in ((False, True), (True, True), (True, False)):
        ok = in_window.copy()
        if not has_prev:
            ok &= kj >= BLOCK
        if not has_next:
            ok &= kj < 2 * BLOCK
        mask = np.where(ok, 0.0, NEG).astype(np.float32)
        variants.append(np.concatenate([mask[:BLOCK], mask[2 * BLOCK:]], axis=0))
    return jnp.asarray(np.stack(variants)).astype(BF16)


def _minor_dft_terms():
    assert DFT_MINOR == 8
    plan = []
    for k2 in range(DFT_MINOR):
        unit, half = [], []
        for s2 in range(DFT_MINOR):
            ang = -2.0 * math.pi * ((s2 * k2) % DFT_MINOR) / DFT_MINOR
            for part, coef in ((0, math.cos(ang)), (1, -math.sin(ang))):
                if abs(coef) < 1e-9:
                    continue
                target = unit if abs(abs(coef) - 1.0) < 1e-9 else half
                assert target is unit or abs(abs(coef) - math.sqrt(0.5)) < 1e-9
                target.append((s2, part, 1 if coef > 0 else -1))
        plan.append((unit, half))
    return plan


def _signed_sum(terms, pick):
    pos = [pick(s2, part) for s2, part, sign in terms if sign > 0]
    neg = [pick(s2, part) for s2, part, sign in terms if sign < 0]
    total = None
    for v in pos:
        total = v if total is None else total + v
    for v in neg:
        total = -v if total is None else total - v
    return total


def _fourier_steps(f_ref, f1_ref, tr_ref, ti_ref, cm_ref, o_ref):
    n_major = f_ref.shape[0] // DFT_MINOR
    pair_w = 2 * LANES
    n_pairs = DFT_MINOR // 2
    row_chunk = n_major // n_pairs
    twiddled = [None] * n_pairs
    spectra = [None] * DFT_MINOR
    plan = _minor_dft_terms()

    def stage1(c):
        y = jnp.concatenate(
            [f_ref[pl.ds(s2, n_major, stride=DFT_MINOR), :].astype(BF16) for s2 in (2 * c, 2 * c + 1)],
            axis=1)
        a = jnp.dot(f1_ref[...], y, preferred_element_type=F32)
        ar = a[:n_major]
        ai = a[n_major:]
        tr = tr_ref[:, c * pair_w:(c + 1) * pair_w]
        ti = ti_ref[:, c * pair_w:(c + 1) * pair_w]
        twiddled[c] = ((ar * tr - ai * ti).astype(BF16), (ar * ti + ai * tr).astype(BF16))

    def channels(c):
        apr, api = twiddled[c]
        for t in range(2):
            lanes = slice(t * LANES, (t + 1) * LANES)
            ap = jnp.concatenate([apr[:, lanes], api[:, lanes]], axis=1)
            spectra[2 * c + t] = jnp.dot(ap, cm_ref[...], preferred_element_type=F32)

    def stage2(r):
        rows = slice(r * row_chunk, (r + 1) * row_chunk)

        def pick(s2, part):
            return spectra[s2][rows, part * LANES:(part + 1) * LANES]

        for k2, (unit, half) in enumerate(plan):
            res = _signed_sum(unit, pick)
            if half:
                scaled = _signed_sum(half, pick) * math.sqrt(0.5)
                res = scaled if res is None else res + scaled
            o_ref[pl.ds(k2 * n_major + r * row_chunk, row_chunk), :] = res.astype(BF16)

    s1 = lambda c: (functools.partial(stage1, c), 4)
    ch = lambda c: (functools.partial(channels, c), 3)
    steps = [s1(0)]
    for c in range(1, n_pairs):
        steps += [s1(c), ch(c - 1)]
    steps.append(ch(n_pairs - 1))
    return steps + [(functools.partial(stage2, r), 1) for r in range(n_pairs)]


def _cis(num, den):
    ang = (num % den).astype(F32) * F32(-2.0 * math.pi / den)
    return jnp.cos(ang), jnp.sin(ang)


def _dft_tables(seq):
    n_major = seq // DFT_MINOR
    ch = FOURIER_GROUP_CH
    root = int(round(math.sqrt(n_major)))
    assert root * root == n_major
    s1 = jnp.arange(n_major, dtype=jnp.int32)[None, :]
    hi = jnp.arange(root, dtype=jnp.int32)[:, None] * root
    lo = jnp.arange(root, dtype=jnp.int32)[:, None]
    hr, hi_ = _cis(hi * s1, n_major)
    lr, li = _cis(lo * s1, n_major)
    p1 = jnp.stack([hr, hr])[:, :, None, :]
    q1 = jnp.stack([lr, li])[:, None, :, :]
    p2 = jnp.stack([-hi_, hi_])[:, :, None, :]
    q2 = jnp.stack([li, lr])[:, None, :, :]
    f1 = (p1 * q1 + p2 * q2).astype(BF16).reshape(2 * n_major, n_major)
    k1 = jnp.arange(n_major, dtype=jnp.int32)[:, None]
    s2 = jnp.arange(DFT_MINOR, dtype=jnp.int32)[None, :]
    tr, ti = _cis(k1 * s2, seq)
    tr = jnp.broadcast_to(tr[:, :, None], (n_major, DFT_MINOR, ch)).reshape(n_major, DFT_MINOR * ch)
    ti = jnp.broadcast_to(ti[:, :, None], (n_major, DFT_MINOR, ch)).reshape(n_major, DFT_MINOR * ch)
    c128 = jnp.arange(ch, dtype=jnp.int32)
    cr, ci = _cis(c128[:, None] * c128[None, :], ch)
    norm = F32(1.0 / math.sqrt(seq * ch))
    cm = (jnp.concatenate([jnp.concatenate([cr, ci], axis=1),
                           jnp.concatenate([-ci, cr], axis=1)], axis=0) * norm).astype(BF16)
    return f1, tr, ti, cm


def _mix_kernel(sink_ref, q_ref, kp_ref, kc_ref, kn_ref, kx_ref, vp_ref, vc_ref, vn_ref, vx_ref, bias_ref,
                f_ref, f1_ref, tr_ref, ti_ref, cm_ref, attn_ref, four_ref):
    attn_steps = _attention_steps(sink_ref, q_ref, kp_ref, kc_ref, kn_ref, kx_ref,
                                  vp_ref, vc_ref, vn_ref, vx_ref, bias_ref, attn_ref)
    four_steps = _fourier_steps(f_ref, f1_ref, tr_ref, ti_ref, cm_ref, four_ref)
    total = sum(weight for _, weight in four_steps)
    done = 0
    seen = 0
    for four_step, weight in four_steps:
        seen += weight
        upto = seen * len(attn_steps) // total
        for attn_step in attn_steps[done:upto]:
            attn_step()
        done = upto
        four_step()
    assert done == len(attn_steps)


def _mix(sink, q, k, v_t, kx, vx_t, bias, f3, tables, ctx_len):
    batch, seq, _ = f3.shape
    f1, tr, ti, cm = tables
    ch = FOURIER_GROUP_CH
    steps, attn_in, attn_out = _attention_specs(batch, seq, ctx_len, bias.shape)
    assert steps == N_FOURIER_GROUPS
    four_spec = pl.BlockSpec((None, seq, ch), lambda b, g: (b, 0, g))
    return pl.pallas_call(
        _mix_kernel,
        grid=(batch, steps),
        in_specs=attn_in + [four_spec, _const_spec(f1.shape), _const_spec(tr.shape),
                            _const_spec(ti.shape), _const_spec(cm.shape)],
        out_specs=[attn_out, four_spec],
        out_shape=[jax.ShapeDtypeStruct((batch * seq, Q_W), BF16),
                   jax.ShapeDtypeStruct((batch, seq, FOURIER_W), BF16)],
        compiler_params=_params(2),
        name="mix",
    )(sink, q, k, k, k, kx, v_t, v_t, v_t, vx_t, bias, f3, f1, tr, ti, cm)


def _merge_kernel(x_ref, a_ref, f_ref, mod_ref, gpre_ref, gpost_ref,
                  wg_b, wpa_b, wpf_b, wout_b, wup_ref, wdown_ref,
                  o_ref, wup_o, wdown_o, *, tiles_per_seq):
    wup_o[...] = wup_ref[...].astype(BF16)
    wdown_o[...] = wdown_ref[...].astype(BF16)

    x = x_ref[...]
    shift, scale_mod, out_gate = _mod_rows(mod_ref, pl.program_id(0) // tiles_per_seq, 0, 3)
    h = _norm_modulate(x, gpre_ref[...], shift, scale_mod).astype(BF16)
    a = a_ref[...]
    f = f_ref[...]
    y = None
    for c in range(0, D_MODEL, MXU_DIM):
        cols = slice(c, c + MXU_DIM)
        gate_a = jnp.dot(h, wg_b[:, cols], preferred_element_type=F32)
        gate_f = jnp.dot(h, wg_b[:, D_MODEL + c:D_MODEL + c + MXU_DIM], preferred_element_type=F32)
        pa = jnp.dot(a, wpa_b[:, cols], preferred_element_type=F32)
        pf = jnp.dot(f, wpf_b[:, cols], preferred_element_type=F32)
        m = (jax.nn.sigmoid(gate_a) * pa + jax.nn.sigmoid(gate_f) * pf).astype(BF16)
        part = jnp.dot(m, wout_b[cols, :], preferred_element_type=F32)
        y = part if y is None else y + part
    o_ref[...] = x + out_gate * _post_norm(y, gpost_ref[...])


def _merge(x2, attn, four, mods, g_pre, g_post, w_g, w_pa, w_pf, w_out, w_up, w_down, seq):
    n = x2.shape[0]
    tm = TOKEN_TILE
    n_steps = n // tm
    tiles_per_seq = seq // tm
    row = lambda i: (i, 0)
    up_rows = w_up.shape[0] // n_steps
    down_rows = w_down.shape[0] // BF16_ROWS
    n_down = w_down.shape[0] // down_rows
    assert up_rows * n_steps == w_up.shape[0] and up_rows % BF16_ROWS == 0
    assert down_rows % BF16_ROWS == 0 and n_down <= n_steps
    up_spec = pl.BlockSpec((up_rows, w_up.shape[1]), row)
    down_spec = pl.BlockSpec((down_rows, w_down.shape[1]), lambda i: (jnp.minimum(i, n_down - 1), 0))
    return pl.pallas_call(
        functools.partial(_merge_kernel, tiles_per_seq=tiles_per_seq),
        grid=(n // tm,),
        in_specs=[
            pl.BlockSpec((tm, D_MODEL), row),
            pl.BlockSpec((tm, Q_W), row),
            pl.BlockSpec((tm, FOURIER_W), row),
            _const_spec(mods.shape),
            _const_spec((1, D_MODEL)),
            _const_spec((1, D_MODEL)),
            _const_spec(w_g.shape),
            _const_spec(w_pa.shape),
            _const_spec(w_pf.shape),
            _const_spec(w_out.shape),
            up_spec,
            down_spec,
        ],
        out_specs=[pl.BlockSpec((tm, D_MODEL), row), up_spec, down_spec],
        out_shape=[jax.ShapeDtypeStruct((n, D_MODEL), F32),
                   jax.ShapeDtypeStruct(w_up.shape, BF16),
                   jax.ShapeDtypeStruct(w_down.shape, BF16)],
        compiler_params=_params(1),
        name="merge",
    )(x2, attn, four, mods, g_pre, g_post, w_g, w_pa, w_pf, w_out, w_up, w_down)


def _convffn_kernel(x_ref, xp_ref, xn_ref, mod_ref, gpre_ref, gpost_ref,
                    wu_ref, wgate_ref, cw_ref, cb_ref, wd_ref, o_ref, act_ref, *, tiles_per_seq):
    i = pl.program_id(0)
    tm = x_ref.shape[0]
    halo = SUBLANES
    has_prev = (i % tiles_per_seq != 0).astype(F32)
    has_next = (i % tiles_per_seq != tiles_per_seq - 1).astype(F32)
    x = x_ref[...]
    shift, scale, out_gate = _mod_rows(mod_ref, i // tiles_per_seq, 3, 3)
    gain = gpre_ref[...]
    h = _norm_modulate(x, gain, shift, scale)
    hp = _norm_modulate(xp_ref[...], gain, shift, scale) * has_prev
    hn = _norm_modulate(xn_ref[...], gain, shift, scale) * has_next
    h_ext = jnp.concatenate([hp, h, hn], axis=0).astype(BF16)
    h_mid = h.astype(BF16)
    n_ext = tm + 2 * halo
    for c in range(D_FF // FFN_CHUNK):
        cols = slice(c * FFN_CHUNK, (c + 1) * FFN_CHUNK)
        u = jnp.dot(h_ext, wu_ref[:, cols], preferred_element_type=F32)
        gate = jnp.dot(h_mid, wgate_ref[:, cols], preferred_element_type=F32)
        u_prev = pltpu.roll(u, 1, 0)[halo:halo + tm]
        u_next = pltpu.roll(u, n_ext - 1, 0)[halo:halo + tm]
        conv = (u_prev * cw_ref[0:1, cols] + u[halo:halo + tm] * cw_ref[1:2, cols]
                + u_next * cw_ref[2:3, cols] + cb_ref[:, cols])
        act_ref[:, cols] = (conv * jax.nn.sigmoid(conv) * gate).astype(BF16)
    y = jnp.dot(act_ref[...], wd_ref[...], preferred_element_type=F32)
    o_ref[...] = x + out_gate * _post_norm(y, gpost_ref[...])


def _convffn(x1, mods, g_pre, g_post, w_up, conv_w, conv_b, w_down, seq):
    n = x1.shape[0]
    tm = TOKEN_TILE
    tiles_per_seq = seq // tm
    halo_blocks_per_tile = tm // SUBLANES
    n_halo_blocks = n // SUBLANES
    return pl.pallas_call(
        functools.partial(_convffn_kernel, tiles_per_seq=tiles_per_seq),
        grid=(n // tm,),
        in_specs=[
            pl.BlockSpec((tm, D_MODEL), lambda i: (i, 0)),
            pl.BlockSpec((SUBLANES, D_MODEL),
                         lambda i: (jnp.maximum(i * halo_blocks_per_tile - 1, 0), 0)),
            pl.BlockSpec((SUBLANES, D_MODEL),
                         lambda i: (jnp.minimum((i + 1) * halo_blocks_per_tile, n_halo_blocks - 1), 0)),
            _const_spec(mods.shape),
            _const_spec((1, D_MODEL)),
            _const_spec((1, D_MODEL)),
            pl.BlockSpec((D_MODEL, D_FF), lambda i: (0, 0), pipeline_mode=pl.Buffered(1)),
            pl.BlockSpec((D_MODEL, D_FF), lambda i: (0, 1), pipeline_mode=pl.Buffered(1)),
            _const_spec(conv_w.shape),
            _const_spec((1, D_FF)),
            _const_spec(w_down.shape),
        ],
        out_specs=pl.BlockSpec((tm, D_MODEL), lambda i: (i, 0)),
        out_shape=jax.ShapeDtypeStruct((n, D_MODEL), F32),
        scratch_shapes=[pltpu.VMEM((tm, D_FF), BF16)],
        compiler_params=_params(1),
        name="convffn",
    )(x1, x1, x1, mods, g_pre, g_post, w_up, w_up, conv_w, conv_b.reshape(1, D_FF), w_down)


def _rope_tables(seq):
    half = HEAD_DIM // 2
    inv_freq = ROPE_THETA ** (-jnp.arange(0, half, 2, dtype=F32) / half)
    ang_r = jnp.arange(seq // GRID_W).astype(F32)[:, None] * inv_freq
    ang_c = jnp.arange(GRID_W).astype(F32)[:, None] * inv_freq
    reps = LANES // HEAD_DIM

    def lanes(row_part, col_part):
        return jnp.tile(jnp.concatenate([row_part, col_part], axis=-1), (1, reps))

    zr = jnp.zeros((ang_r.shape[0], half), F32)
    zc = jnp.zeros((ang_c.shape[0], half), F32)
    rcos = lanes(jnp.concatenate([jnp.cos(ang_r), jnp.cos(ang_r)], axis=-1), zr)
    rsin = lanes(jnp.concatenate([-jnp.sin(ang_r), jnp.sin(ang_r)], axis=-1), zr)
    ccos = lanes(zc, jnp.concatenate([jnp.cos(ang_c), jnp.cos(ang_c)], axis=-1))
    csin = lanes(zc, jnp.concatenate([-jnp.sin(ang_c), jnp.sin(ang_c)], axis=-1))
    return rcos, rsin, ccos, csin


def kernel(x, c, ctx, c_ctx, w_mod, b_mod, g_pre1, g_post1, g_pre2, g_post2,
           w_in, sink, w_pa, w_pf, w_out, w_up, conv_w, conv_b, w_down):
    batch, seq, d = x.shape
    ctx_len = ctx.shape[1]
    depth = w_mod.shape[0]
    assert depth == 1 and d == D_MODEL and batch + 1 <= SUBLANES
    assert seq % TOKEN_TILE == 0 and TOKEN_TILE % GRID_W == 0
    assert seq % (DFT_MINOR * SUBLANES) == 0
    assert seq % (ATTN_BLOCKS_PER_STEP * BLOCK) == 0
    n = batch * seq
    rope_tabs = _rope_tables(seq)
    tables = _dft_tables(seq)
    bias = _band_bias()

    l = 0
    mods = _adaln(c, c_ctx.reshape(1, d), w_mod[l], b_mod[l])

    g_pre1_row = g_pre1[l].reshape(1, d)
    g_post1_row = g_post1[l].reshape(1, d)
    g_pre2_row = g_pre2[l].reshape(1, d)
    g_post2_row = g_post2[l].reshape(1, d)

    x2 = x.reshape(n, d)
    q, k, v_t, f, w_g_b, w_pa_b, w_pf_b, w_out_b = _inproj(x2, mods, g_pre1_row, w_in[l], w_pa[l], w_pf[l], w_out[l],
                                                          rope_tabs, seq)
    kx, vx_t = _ctxkv(ctx.reshape(batch * ctx_len, d), mods, batch, g_pre1_row, w_in[l], ctx_len)
    attn, four = _mix(sink[l], q, k, v_t, kx, vx_t, bias, f.reshape(batch, seq, FOURIER_W), tables, ctx_len)
    four = four.reshape(n, FOURIER_W)
    x1, w_up_b, w_down_b = _merge(x2, attn, four, mods, g_pre1_row, g_post1_row, w_g_b, w_pa_b, w_pf_b, w_out_b,
                                  w_up[l], w_down[l], seq)
    out = _convffn(x1, mods, g_pre2_row, g_post2_row, w_up_b, conv_w[l], conv_b[l], w_down_b, seq)
    return out.reshape(batch, seq, d)
```
